```python
import jax
import jax.numpy as jnp
from jax import lax
import numpy as np

D_MODEL = 1024
BATCH = 4
SEQ = 8192
DEPTH = 2

HEAD_DIM = 64
ROT_DIM = HEAD_DIM // 4
ROPE_THETA = 500000.0
NORM_EPS = 1e-6
Q_BLOCK = 128
D_FF = 2816

SB_HEADS = 4
NSA_HEADS = 8
NSA_GROUPS = 2
NSA_HPG = NSA_HEADS // NSA_GROUPS
NSA_CMP_LEN = 32
NSA_CMP_STRIDE = 16
NSA_CMP_HIDDEN = 2 * HEAD_DIM
NSA_SEL_LEN = 64
NSA_SEL_N = 16
NSA_WINDOW = 512
NSA_FORCED_SCORE = 1.0e4
DSA_HEADS = 4
DSA_IDX_HEADS = 8
DSA_IDX_DIM = 32
DSA_IDX_ROT = DSA_IDX_DIM // 4
DSA_TOPK = 256

SB_W = SB_HEADS * HEAD_DIM
NSA_QW = NSA_HEADS * HEAD_DIM
NSA_KVW = NSA_GROUPS * HEAD_DIM
DSA_QW = DSA_HEADS * HEAD_DIM
IN_WIDTHS = (SB_W, SB_W, SB_W,
             NSA_QW, NSA_KVW, NSA_KVW, NSA_KVW, NSA_KVW, NSA_KVW, NSA_KVW, 3 * NSA_HEADS,
             DSA_QW, HEAD_DIM, HEAD_DIM, DSA_IDX_HEADS * DSA_IDX_DIM, DSA_IDX_DIM, DSA_IDX_HEADS)
D_IN = sum(IN_WIDTHS)

kernel_name = "hybrid_sb_nsa_dsa_macaron"


def rms_norm(x, g):
    xf = x.astype(jnp.float32)
    y = xf * lax.rsqrt(jnp.mean(xf * xf, axis=-1, keepdims=True) + NORM_EPS)
    return (y * g.astype(jnp.float32)).astype(x.dtype)


def rope_tables(pos, rot_dim):
    inv = ROPE_THETA ** (-jnp.arange(0, rot_dim, 2, dtype=jnp.float32) / rot_dim)
    ang = pos.astype(jnp.float32)[:, None] * inv[None, :]
    return jnp.cos(ang), jnp.sin(ang)


def apply_rope(x, cos, sin):
    half = cos.shape[-1]
    xf = x[..., :2 * half].astype(jnp.float32)
    x1, x2 = xf[..., :half], xf[..., half:]
    c, s = cos[:, None, :], sin[:, None, :]
    rot = jnp.concatenate([x1 * c - x2 * s, x2 * c + x1 * s], axis=-1).astype(x.dtype)
    return jnp.concatenate([rot, x[..., 2 * half:]], axis=-1)


def masked_softmax(s, mask):
    s = jnp.where(mask, s.astype(jnp.float32), -jnp.inf)
    m = jnp.max(s, axis=-1, keepdims=True)
    m = jnp.where(jnp.isfinite(m), m, 0.0)
    p = jnp.exp(s - m)
    den = jnp.sum(p, axis=-1, keepdims=True)
    return p / jnp.where(den > 0, den, 1.0)


def unblock(y):
    n, b, t = y.shape[:3]
    return jnp.moveaxis(y, 0, 1).reshape((b, n * t) + y.shape[3:])


def swiglu_ffn(x, g, w_gu, w_down):
    h = rms_norm(x, g)
    gate, up = jnp.split(h @ w_gu, 2, axis=-1)
    return (jax.nn.silu(gate) * up) @ w_down


def stick_breaking_attention(q, k, v):
    B, S, H, d = q.shape
    scale = d ** -0.5
    kh = k.transpose(0, 2, 1, 3)
    vh = v.transpose(0, 2, 1, 3)
    s_pos = jnp.arange(S)

    def block(i):
        q0 = i * Q_BLOCK
        qb = lax.dynamic_slice_in_dim(q, q0, Q_BLOCK, axis=1)
        z = jnp.einsum('bthd,bhsd->bhts', qb, kh).astype(jnp.float32) * scale
        t_pos = q0 + jnp.arange(Q_BLOCK)
        past = s_pos[None, :] < t_pos[:, None]
        log_1m = jnp.where(past, jax.nn.log_sigmoid(-z), 0.0)
        tail = lax.cumsum(log_1m, axis=3, reverse=True) - log_1m
        w = jnp.where(past, jnp.exp(jax.nn.log_sigmoid(z) + tail), 0.0)
        return jnp.einsum('bhts,bhsd->bthd', w.astype(v.dtype), vh)

    return unblock(lax.map(block, jnp.arange(S // Q_BLOCK)))


def nsa_compress(x, pe, w1, w2):
    B, S, G, d = x.shape
    nchunk = S // NSA_CMP_STRIDE
    r = NSA_CMP_LEN // NSA_CMP_STRIDE
    nc = nchunk - r + 1
    chunks = x.reshape(B, nchunk, NSA_CMP_STRIDE, G, d)
    blocks = jnp.concatenate([chunks[:, m:m + nc] for m in range(r)], axis=2)
    blocks = blocks + pe[None, None, :, None, :]
    flat = blocks.transpose(0, 1, 3, 2, 4).reshape(B, nc, G, NSA_CMP_LEN * d)
    return jax.nn.silu(flat @ w1) @ w2


def nsa_attention(q, k_cmp, v_cmp, k_sel, v_sel, k_win, v_win, gate_logits,
                  q_norm, k_norm, pe, w1, w2, cos, sin):
    B, S, _, d = q.shape
    G, hpg, T = NSA_GROUPS, NSA_HPG, Q_BLOCK
    scale = d ** -0.5
    dt = v_sel.dtype
    qg = apply_rope(rms_norm(q, q_norm), cos, sin).reshape(B, S, G, hpg, d)
    k_sel = apply_rope(rms_norm(k_sel, k_norm[1]), cos, sin)
    k_win = apply_rope(rms_norm(k_win, k_norm[2]), cos, sin)
    gates = jax.nn.sigmoid(gate_logits).reshape(B, S, G, hpg, 3)

    kc = rms_norm(nsa_compress(k_cmp, pe[0], w1[0], w2[0]), k_norm[0])
    nc = kc.shape[1]
    cmp_start = jnp.arange(nc) * NSA_CMP_STRIDE
    cmp_end = cmp_start + NSA_CMP_LEN - 1
    cos_c, sin_c = rope_tables(cmp_end, ROT_DIM)
    kc = apply_rope(kc, cos_c, sin_c).transpose(0, 2, 1, 3)
    vc = nsa_compress(v_cmp, pe[1], w1[1], w2[1]).transpose(0, 2, 1, 3)

    ns = S // NSA_SEL_LEN
    n_sel = min(NSA_SEL_N, ns)
    ks_blocks = k_sel.reshape(B, ns, NSA_SEL_LEN, G, d).transpose(0, 3, 1, 2, 4)
    vs_blocks = v_sel.reshape(B, ns, NSA_SEL_LEN, G, d).transpose(0, 3, 1, 2, 4)
    sel_start = jnp.arange(ns) * NSA_SEL_LEN
    overlap = ((cmp_start[:, None] < sel_start[None, :] + NSA_SEL_LEN)
               & (cmp_end[:, None] >= sel_start[None, :])).astype(jnp.float32)
    gather_blocks = jax.vmap(jax.vmap(lambda src, ix: src[ix]))

    pad = ((0, 0), (NSA_WINDOW, 0), (0, 0), (0, 0))
    kw_pad = jnp.pad(k_win, pad)
    vw_pad = jnp.pad(v_win, pad)

    def block(i):
        q0 = i * T
        t_pos = q0 + jnp.arange(T)
        qb = lax.dynamic_slice_in_dim(qg, q0, T, axis=1)

        sc = jnp.einsum('btghd,bgcd->bghtc', qb, kc) * scale
        pc = masked_softmax(sc, cmp_end[None, :] <= t_pos[:, None])
        o_cmp = jnp.einsum('bghtc,bgcd->btghd', pc.astype(dt), vc)

        imp = jnp.einsum('bghtc,cn->bgtn', pc, overlap)
        j = jnp.arange(ns)[None, :]
        cur = (t_pos // NSA_SEL_LEN)[:, None]
        visible = j * NSA_SEL_LEN <= t_pos[:, None]
        forced = (j == 0) | (j == cur) | (j == cur - 1)
        score = jnp.where(visible, jnp.where(forced, NSA_FORCED_SCORE, imp), -jnp.inf)
        _, sel = lax.top_k(score, n_sel)
        k_g = gather_blocks(ks_blocks, sel)
        v_g = gather_blocks(vs_blocks, sel).reshape(B, G, T, n_sel * NSA_SEL_LEN, d)
        tok = sel[..., None] * NSA_SEL_LEN + jnp.arange(NSA_SEL_LEN)
        smask = (tok <= t_pos[:, None, None]).reshape(B, G, 1, T, n_sel * NSA_SEL_LEN)
        ss = jnp.einsum('btghd,bgtnld->bghtnl', qb, k_g) * scale
        ps = masked_softmax(ss.reshape(B, G, hpg, T, n_sel * NSA_SEL_LEN), smask)
        o_sel = jnp.einsum('bghtm,bgtmd->btghd', ps.astype(dt), v_g)

        kwb = lax.dynamic_slice_in_dim(kw_pad, q0, T + NSA_WINDOW, axis=1)
        vwb = lax.dynamic_slice_in_dim(vw_pad, q0, T + NSA_WINDOW, axis=1)
        s_abs = q0 - NSA_WINDOW + jnp.arange(T + NSA_WINDOW)
        wmask = ((s_abs[None, :] <= t_pos[:, None]) & (s_abs[None, :] > t_pos[:, None] - NSA_WINDOW)
                 & (s_abs[None, :] >= 0))
        sw = jnp.einsum('btghd,bsgd->bghts', qb, kwb) * scale
        pw = masked_softmax(sw, wmask)
        o_win = jnp.einsum('bghts,bsgd->btghd', pw.astype(dt), vwb)

        g = lax.dynamic_slice_in_dim(gates, q0, T, axis=1)
        return g[..., 0:1] * o_cmp + g[..., 1:2] * o_sel + g[..., 2:3] * o_win

    out = unblock(lax.map(block, jnp.arange(S // T)))
    return out.reshape(B, S, NSA_HEADS * d)


def dsa_attention(q, k, v, iq, ik, iw, q_norm, k_norm, cos, sin, cos_i, sin_i):
    B, S, _, d = q.shape
    T = Q_BLOCK
    scale = d ** -0.5
    dt = v.dtype
    top_k = min(DSA_TOPK, S // 4)
    q = apply_rope(rms_norm(q, q_norm), cos, sin)
    k = apply_rope(rms_norm(k[:, :, None, :], k_norm), cos, sin)[:, :, 0, :]
    iq = apply_rope(iq, cos_i, sin_i)
    ik = apply_rope(ik[:, :, None, :], cos_i, sin_i)[:, :, 0, :]
    iw = iw.astype(jnp.float32) * (DSA_IDX_HEADS ** -0.5) * (DSA_IDX_DIM ** -0.5)
    s_pos = jnp.arange(S)
    gather_rows = jax.vmap(lambda src, ix: src[ix])

    def block(i):
        q0 = i * T
        t_pos = q0 + jnp.arange(T)
        qi = lax.dynamic_slice_in_dim(iq, q0, T, axis=1)
        wi = lax.dynamic_slice_in_dim(iw, q0, T, axis=1)
        logits = jax.nn.relu(jnp.einsum('bthd,bsd->bths', qi, ik).astype(jnp.float32))
        score = jnp.einsum('bths,bth->bts', logits, wi)
        score = jnp.where(s_pos[None, None, :] <= t_pos[None, :, None], score, -jnp.inf)
        _, idx = lax.top_k(score, top_k)
        kg = gather_rows(k, idx)
        vg = gather_rows(v, idx)
        qb = lax.dynamic_slice_in_dim(q, q0, T, axis=1)
        sc = jnp.einsum('bthd,btkd->bhtk', qb, kg) * scale
        p = masked_softmax(sc, (idx <= t_pos[None, :, None])[:, None])
        return jnp.einsum('bhtk,btkd->bthd', p.astype(dt), vg)

    out = unblock(lax.map(block, jnp.arange(S // T)))
    return out.reshape(B, S, DSA_HEADS * d)


def hybrid_mixer(h, w_in, w_gate, nsa_q_norm, nsa_k_norm, nsa_cmp_pe, nsa_cmp_w1, nsa_cmp_w2,
                 dsa_q_norm, dsa_k_norm, w_br_a, w_br_b, w_br_c, w_out, cos, sin, cos_i, sin_i):
    B, S, _ = h.shape
    offs = np.cumsum(IN_WIDTHS)[:-1].tolist()
    (sb_q, sb_k, sb_v, nq, nkc, nvc, nks, nvs, nkw, nvw, ngate,
     dq, dk, dv, iq, ik, iw) = jnp.split(h @ w_in, offs, axis=-1)

    def heads(t, n, d):
        return t.reshape(B, S, n, d)

    o_a = stick_breaking_attention(heads(sb_q, SB_HEADS, HEAD_DIM), heads(sb_k, SB_HEADS, HEAD_DIM),
                                   heads(sb_v, SB_HEADS, HEAD_DIM)).reshape(B, S, SB_W)
    kv = lambda t: heads(t, NSA_GROUPS, HEAD_DIM)
    o_b = nsa_attention(heads(nq, NSA_HEADS, HEAD_DIM), kv(nkc), kv(nvc), kv(nks), kv(nvs), kv(nkw), kv(nvw),
                        heads(ngate, NSA_HEADS, 3), nsa_q_norm, nsa_k_norm, nsa_cmp_pe, nsa_cmp_w1, nsa_cmp_w2,
                        cos, sin)
    o_c = dsa_attention(heads(dq, DSA_HEADS, HEAD_DIM), dk, dv, heads(iq, DSA_IDX_HEADS, DSA_IDX_DIM), ik, iw,
                        dsa_q_norm, dsa_k_norm, cos, sin, cos_i, sin_i)

    g_a, g_b, g_c = jnp.split(jax.nn.sigmoid(h @ w_gate), 3, axis=-1)
    merged = g_a * (o_a @ w_br_a) + g_b * (o_b @ w_br_b) + g_c * (o_c @ w_br_c)
    return merged @ w_out


def setup_inputs(seed: int = 0) -> dict:
    key = jax.random.key(seed)
    ks = jax.random.split(key, 21)
    f32 = jnp.float32

    def normal(k, shape, scale):
        return jax.random.normal(k, shape, f32) * scale

    def gain(k, shape):
        return 1.0 + 0.02 * jax.random.normal(k, shape, f32)

    D, L = D_MODEL, DEPTH
    cmp_in = NSA_CMP_LEN * HEAD_DIM
    return {
        "x": jax.random.normal(ks[0], (BATCH, SEQ, D), f32),
        "ffn1_norm": gain(ks[1], (L, D)),
        "ffn1_w_gu": normal(ks[2], (L, D, 2 * D_FF), D ** -0.5),
        "ffn1_w_down": normal(ks[3], (L, D_FF, D), D_FF ** -0.5),
        "mix_norm": gain(ks[4], (L, D)),
        "w_in": normal(ks[5], (L, D, D_IN), D ** -0.5),
        "w_gate": normal(ks[6], (L, D, 3 * D), D ** -0.5),
        "nsa_q_norm": gain(ks[7], (L, HEAD_DIM)),
        "nsa_k_norm": gain(ks[8], (L, 3, HEAD_DIM)),
        "nsa_cmp_pe": normal(ks[9], (L, 2, NSA_CMP_LEN, HEAD_DIM), 0.1),
        "nsa_cmp_w1": normal(ks[10], (L, 2, cmp_in, NSA_CMP_HIDDEN), cmp_in ** -0.5),
        "nsa_cmp_w2": normal(ks[11], (L, 2, NSA_CMP_HIDDEN, HEAD_DIM), NSA_CMP_HIDDEN ** -0.5),
        "dsa_q_norm": gain(ks[12], (L, HEAD_DIM)),
        "dsa_k_norm": gain(ks[13], (L, HEAD_DIM)),
        "w_br_a": normal(ks[14], (L, SB_W, D), SB_W ** -0.5),
        "w_br_b": normal(ks[15], (L, NSA_QW, D), NSA_QW ** -0.5),
        "w_br_c": normal(ks[16], (L, DSA_QW, D), DSA_QW ** -0.5),
        "w_out": normal(ks[17], (L, D, D), D ** -0.5),
        "ffn2_norm": gain(ks[18], (L, D)),
        "ffn2_w_gu": normal(ks[19], (L, D, 2 * D_FF), D ** -0.5),
        "ffn2_w_down": normal(ks[20], (L, D_FF, D), D_FF ** -0.5),
    }


def reference(x, ffn1_norm, ffn1_w_gu, ffn1_w_down, mix_norm, w_in, w_gate, nsa_q_norm, nsa_k_norm,
              nsa_cmp_pe, nsa_cmp_w1, nsa_cmp_w2, dsa_q_norm, dsa_k_norm, w_br_a, w_br_b, w_br_c, w_out,
              ffn2_norm, ffn2_w_gu, ffn2_w_down):
    S = x.shape[1]
    pos = jnp.arange(S)
    cos, sin = rope_tables(pos, ROT_DIM)
    cos_i, sin_i = rope_tables(pos, DSA_IDX_ROT)
    for l in range(DEPTH):
        x = x + 0.5 * swiglu_ffn(x, ffn1_norm[l], ffn1_w_gu[l], ffn1_w_down[l])
        h = rms_norm(x, mix_norm[l])
        x = x + hybrid_mixer(h, w_in[l], w_gate[l], nsa_q_norm[l], nsa_k_norm[l], nsa_cmp_pe[l],
                             nsa_cmp_w1[l], nsa_cmp_w2[l], dsa_q_norm[l], dsa_k_norm[l],
                             w_br_a[l], w_br_b[l], w_br_c[l], w_out[l], cos, sin, cos_i, sin_i)
        x = x + 0.5 * swiglu_ffn(x, ffn2_norm[l], ffn2_w_gu[l], ffn2_w_down[l])
    return x
```

```python
import functools

import numpy as np
import jax
import jax.numpy as jnp
from jax import lax
from jax.experimental import pallas as pl
from jax.experimental.pallas import tpu as pltpu

HEAD_DIM = 64
ROT_DIM = HEAD_DIM // 4
ROPE_THETA = 500000.0
NORM_EPS = 1e-6

SB_HEADS = 4
NSA_HEADS = 8
NSA_GROUPS = 2
NSA_HPG = NSA_HEADS // NSA_GROUPS
NSA_CMP_LEN = 32
NSA_CMP_STRIDE = 16
NSA_CMP_HIDDEN = 2 * HEAD_DIM
NSA_SEL_LEN = 64
NSA_SEL_N = 16
NSA_WINDOW = 512
NSA_FORCED_SCORE = 1.0e4
DSA_HEADS = 4
DSA_IDX_HEADS = 8
DSA_IDX_DIM = 32
DSA_IDX_ROT = DSA_IDX_DIM // 4
DSA_TOPK = 256

SB_W = SB_HEADS * HEAD_DIM
NSA_QW = NSA_HEADS * HEAD_DIM
NSA_KVW = NSA_GROUPS * HEAD_DIM
DSA_QW = DSA_HEADS * HEAD_DIM
IDX_QW = DSA_IDX_HEADS * DSA_IDX_DIM

LANES = 128
VMEM_LIMIT = 56 * 1024 * 1024
MXU_DTYPE = jnp.bfloat16
NEG = -1.0e30
SB_SKIP = -120.0
INT_MIN = -2 ** 31

F32 = jnp.float32


def _dot(a, b):
    return jnp.dot(a, b, preferred_element_type=F32)


def _dot_nt(a, b):
    return lax.dot_general(a, b, (((1,), (1,)), ((), ())), preferred_element_type=F32)


def _params(n_axes):
    return pltpu.CompilerParams(dimension_semantics=("arbitrary",) * n_axes,
                                vmem_limit_bytes=VMEM_LIMIT)


def _lane_iota(shape):
    return lax.broadcasted_iota(jnp.int32, shape, len(shape) - 1)


def _row_iota(shape):
    return lax.broadcasted_iota(jnp.int32, shape, len(shape) - 2)


def _swap_halves(x):
    return pltpu.roll(x, LANES // 2, 1)


def _head_rms(ys):
    lo = _lane_iota(ys.shape) < HEAD_DIM
    sq = ys * ys
    s_lo = jnp.sum(jnp.where(lo, sq, 0.0), axis=-1, keepdims=True)
    s_hi = jnp.sum(jnp.where(lo, 0.0, sq), axis=-1, keepdims=True)
    ms = jnp.where(lo, s_lo, s_hi) * (1.0 / HEAD_DIM)
    return ys * lax.rsqrt(ms + NORM_EPS)


def _rope(ys, c, sm, sp, half):
    return ys * c + pltpu.roll(ys, LANES - half, 1) * sm + pltpu.roll(ys, half, 1) * sp


def _ffn_kernel(x_ref, g_ref, wg_ref, wu_ref, wd_ref, o_ref, h_ref, acc_ref):
    f = pl.program_id(1)

    @pl.when(f == 0)
    def _():
        x = x_ref[...]
        ms = jnp.mean(x * x, axis=-1, keepdims=True)
        h_ref[...] = (x * lax.rsqrt(ms + NORM_EPS) * g_ref[...]).astype(h_ref.dtype)
        acc_ref[...] = jnp.zeros_like(acc_ref)

    h = h_ref[...]
    gate = _dot(h, wg_ref[...])
    up = _dot(h, wu_ref[...])
    act = (gate * jax.nn.sigmoid(gate) * up).astype(h_ref.dtype)
    acc_ref[...] += _dot(act, wd_ref[...])

    @pl.when(f == pl.num_programs(1) - 1)
    def _():
        o_ref[...] = x_ref[...] + 0.5 * acc_ref[...]


def _ffn(x2, g, w_gu, w_down, *, tm, tf):
    n, d = x2.shape
    d_ff = w_down.shape[0]
    nf = d_ff // tf
    return pl.pallas_call(
        _ffn_kernel,
        grid=(n // tm, nf),
        in_specs=[
            pl.BlockSpec((tm, d), lambda i, f: (i, 0)),
            pl.BlockSpec((1, d), lambda i, f: (0, 0)),
            pl.BlockSpec((d, tf), lambda i, f: (0, f)),
            pl.BlockSpec((d, tf), lambda i, f: (0, f + nf)),
            pl.BlockSpec((tf, d), lambda i, f: (f, 0)),
        ],
        out_specs=pl.BlockSpec((tm, d), lambda i, f: (i, 0)),
        out_shape=jax.ShapeDtypeStruct((n, d), F32),
        scratch_shapes=[pltpu.VMEM((tm, d), MXU_DTYPE), pltpu.VMEM((tm, d), F32)],
        compiler_params=_params(2),
        name="ffn",
    )(x2, g, w_gu, w_gu, w_down)


_P_SB = 0
_P_NQ = _P_SB + 3 * SB_W
_P_NCV = _P_NQ + NSA_QW
_P_NKV = _P_NCV + 2 * NSA_KVW
_P_NG = _P_NKV + 4 * NSA_KVW
_P_DQ = _P_NG + 3 * NSA_QW
_P_DKV = _P_DQ + DSA_QW
_P_IQ = _P_DKV + LANES
_P_IK = _P_IQ + IDX_QW
_P_IW = _P_IK + LANES
_P_END = _P_IW + LANES


def _proj_kernel(x_ref, g_ref, w_ref, nq_g_ref, nk_g_ref, dq_g_ref, dk_g_ref,
                 rc_ref, rm_ref, rp_ref, ic_ref, im_ref, ip_ref,
                 sb_ref, nq_ref, nkc_ref, nvc_ref, nkv_ref, ng_ref, dq_ref, dkv_ref,
                 iq_ref, ik_ref, iw_ref):
    x = x_ref[...]
    ms = jnp.mean(x * x, axis=-1, keepdims=True)
    h = (x * lax.rsqrt(ms + NORM_EPS) * g_ref[...]).astype(MXU_DTYPE)
    rc, rm, rp = rc_ref[...], rm_ref[...], rp_ref[...]
    ic, im, ip = ic_ref[...], im_ref[...], ip_ref[...]
    half, ihalf = ROT_DIM // 2, DSA_IDX_ROT // 2
    scale = HEAD_DIM ** -0.5

    def cols(a, width):
        return _dot(h, w_ref[:, a:a + width])

    def slab(y, s):
        return y[:, s * LANES:(s + 1) * LANES]

    y = cols(_P_SB, 3 * SB_W)
    sb_ref[:, 0:SB_W] = (y[:, 0:SB_W] * scale).astype(sb_ref.dtype)
    sb_ref[:, SB_W:3 * SB_W] = y[:, SB_W:3 * SB_W].astype(sb_ref.dtype)

    y = cols(_P_NQ, NSA_QW)
    for s in range(NSA_QW // LANES):
        ys = _head_rms(slab(y, s)) * slab(nq_g_ref[...], s)
        nq_ref[:, s * LANES:(s + 1) * LANES] = (_rope(ys, rc, rm, rp, half) * scale).astype(nq_ref.dtype)

    y = cols(_P_NCV, 2 * NSA_KVW)
    nkc_ref[...] = slab(y, 0)
    nvc_ref[...] = slab(y, 1)

    y = cols(_P_NKV, 4 * NSA_KVW)
    for s, gi in ((0, 0), (2, 1)):
        ys = _head_rms(slab(y, s)) * nk_g_ref[gi:gi + 1, :]
        nkv_ref[:, s * LANES:(s + 1) * LANES] = _rope(ys, rc, rm, rp, half).astype(nkv_ref.dtype)
    for s in (1, 3):
        nkv_ref[:, s * LANES:(s + 1) * LANES] = slab(y, s).astype(nkv_ref.dtype)

    ng_ref[...] = jax.nn.sigmoid(cols(_P_NG, 3 * NSA_QW))

    y = cols(_P_DQ, DSA_QW)
    for s in range(DSA_QW // LANES):
        ys = _head_rms(slab(y, s)) * slab(dq_g_ref[...], s)
        dq_ref[:, s * LANES:(s + 1) * LANES] = (_rope(ys, rc, rm, rp, half) * scale).astype(dq_ref.dtype)

    y = cols(_P_DKV, LANES)
    yk = _rope(_head_rms(y) * dk_g_ref[...], rc, rm, rp, half)
    dkv_ref[...] = jnp.where(_lane_iota(y.shape) < HEAD_DIM, yk, y).astype(dkv_ref.dtype)

    y = cols(_P_IQ, IDX_QW)
    for s in range(IDX_QW // LANES):
        iq_ref[:, s * LANES:(s + 1) * LANES] = _rope(slab(y, s), ic, im, ip, ihalf).astype(iq_ref.dtype)
    ik_ref[...] = _rope(cols(_P_IK, LANES), ic, im, ip, ihalf).astype(ik_ref.dtype)
    iw_ref[...] = cols(_P_IW, LANES) * (DSA_IDX_HEADS ** -0.5) * (DSA_IDX_DIM ** -0.5)


def _proj(x2, g, w, nq_g, nk_g, dq_g, dk_g, rope_main, rope_idx, *, tm, seq):
    n, d = x2.shape
    nt = seq // tm
    row = lambda i: (i, 0)
    const = lambda i: (0, 0)
    pos = lambda i: (i % nt, 0)
    widths = [(3 * SB_W, MXU_DTYPE), (NSA_QW, MXU_DTYPE), (NSA_KVW, F32), (NSA_KVW, F32),
              (4 * NSA_KVW, MXU_DTYPE), (3 * NSA_QW, F32), (DSA_QW, MXU_DTYPE), (LANES, MXU_DTYPE),
              (IDX_QW, MXU_DTYPE), (LANES, MXU_DTYPE), (LANES, F32)]
    return pl.pallas_call(
        _proj_kernel,
        grid=(n // tm,),
        in_specs=[pl.BlockSpec((tm, d), row), pl.BlockSpec((1, d), const),
                  pl.BlockSpec((d, _P_END), const),
                  pl.BlockSpec((1, NSA_QW), const), pl.BlockSpec((2, LANES), const),
                  pl.BlockSpec((1, DSA_QW), const), pl.BlockSpec((1, LANES), const)]
                 + [pl.BlockSpec((tm, LANES), pos)] * 6,
        out_specs=[pl.BlockSpec((tm, wd), row) for wd, _ in widths],
        out_shape=[jax.ShapeDtypeStruct((n, wd), dt) for wd, dt in widths],
        compiler_params=_params(1),
        name="proj",
    )(x2, g, w, nq_g, nk_g, dq_g, dk_g, *rope_main, *rope_idx)


def _compress_kernel(xk_ref, xv_ref, pe_ref, w1_ref, w2_ref, g_ref, rc_ref, rm_ref, rp_ref,
                     kc_ref, vc_ref):
    ncp = xk_ref.shape[1]

    def compress(x, kv):
        a0 = _dot((x + pe_ref[kv, 0:1, :]).astype(MXU_DTYPE), w1_ref[kv, 0])
        a1 = _dot((x + pe_ref[kv, 1:2, :]).astype(MXU_DTYPE), w1_ref[kv, 1])
        pre = a0 + pltpu.roll(a1, ncp - 1, 0)
        hid = pre * jax.nn.sigmoid(pre)
        return _dot(hid.astype(MXU_DTYPE), w2_ref[kv])

    yk = _head_rms(compress(xk_ref[0], 0)) * g_ref[...]
    kc_ref[0] = _rope(yk, rc_ref[...], rm_ref[...], rp_ref[...], ROT_DIM // 2).astype(kc_ref.dtype)
    vc_ref[0] = compress(xv_ref[0], 1).astype(vc_ref.dtype)


def _compress(xk, xv, pe, w1, w2, g, rope_c):
    b, ncp, wide = xk.shape
    c3 = lambda i: (0, 0, 0)
    c4 = lambda i: (0, 0, 0, 0)
    c2 = lambda i: (0, 0)
    per_b = lambda i: (i, 0, 0)
    return pl.pallas_call(
        _compress_kernel,
        grid=(b,),
        in_specs=[pl.BlockSpec((1, ncp, wide), per_b), pl.BlockSpec((1, ncp, wide), per_b),
                  pl.BlockSpec(pe.shape, c3), pl.BlockSpec(w1.shape, c4), pl.BlockSpec(w2.shape, c3),
                  pl.BlockSpec((1, LANES), c2)] + [pl.BlockSpec((ncp, LANES), c2)] * 3,
        out_specs=[pl.BlockSpec((1, ncp, LANES), per_b)] * 2,
        out_shape=[jax.ShapeDtypeStruct((b, ncp, LANES), MXU_DTYPE)] * 2,
        compiler_params=_params(1),
        name="compress",
    )(xk, xv, pe, w1, w2, g, *rope_c)


def _sb_kernel(q_ref, k_ref, v_ref, o_ref, *, tq):
    i = pl.program_id(2)
    q0 = i * tq
    q = q_ref[0].astype(F32)
    lane = _lane_iota(q.shape)
    rows = _row_iota((tq, tq))
    colsq = _lane_iota((tq, tq))
    upper = jnp.where(rows > colsq, 1.0, 0.0).astype(MXU_DTYPE)
    out = jnp.zeros((tq, LANES), F32)

    for p in range(2):
        in_head = (lane >= p * HEAD_DIM) & (lane < (p + 1) * HEAD_DIM)
        qh = jnp.where(in_head, q, 0.0).astype(MXU_DTYPE)

        def tile(kt, carry, acc, diagonal):
            ks = k_ref[0, pl.ds(kt * tq, tq), :]
            vs = v_ref[0, pl.ds(kt * tq, tq), :]
            z = _dot_nt(qh, ks)
            l = -(jnp.maximum(z, 0.0) + jnp.log1p(jnp.exp(-jnp.abs(z))))
            if diagonal:
                past = colsq < rows
                l = jnp.where(past, l, 0.0)
            hi = l.astype(MXU_DTYPE)
            lo = (l - hi.astype(F32)).astype(MXU_DTYPE)
            tail = _dot(hi, upper) + _dot(lo, upper) + carry
            w = jnp.exp(z + l + tail)
            if diagonal:
                w = jnp.where(past, w, 0.0)
            acc = acc + _dot(w.astype(MXU_DTYPE), vs)
            carry = carry + jnp.sum(l, axis=-1, keepdims=True)
            return carry, acc

        carry, acc = tile(i, jnp.zeros((tq, 1), F32), jnp.zeros((tq, LANES), F32), True)

        def cond(st):
            kt, carry, _ = st
            return (kt >= 0) & (jnp.max(carry) > SB_SKIP)

        def body(st):
            kt, carry, acc = st
            carry, acc = tile(kt, carry, acc, False)
            return kt - 1, carry, acc

        _, _, acc = lax.while_loop(cond, body, (i - 1, carry, acc))
        out = jnp.where(in_head, acc, out)

    del q0
    o_ref[0] = out.astype(o_ref.dtype)


def _sb_attention(sb, *, tq):
    b, s, _ = sb.shape
    nslab = SB_W // LANES
    return pl.pallas_call(
        functools.partial(_sb_kernel, tq=tq),
        grid=(b, nslab, s // tq),
        in_specs=[pl.BlockSpec((1, tq, LANES), lambda bi, sl, i: (bi, i, sl)),
                  pl.BlockSpec((1, s, LANES), lambda bi, sl, i: (bi, 0, nslab + sl)),
                  pl.BlockSpec((1, s, LANES), lambda bi, sl, i: (bi, 0, 2 * nslab + sl))],
        out_specs=pl.BlockSpec((1, tq, LANES), lambda bi, sl, i: (bi, i, sl)),
        out_shape=jax.ShapeDtypeStruct((b, s, SB_W), MXU_DTYPE),
        compiler_params=_params(3),
        name="sb_attn",
    )(sb, sb, sb)


def _softmax_rows(s, ok):
    s = jnp.where(ok, s, NEG)
    m = jnp.max(s, axis=-1, keepdims=True)
    p = jnp.where(ok, jnp.exp(s - m), 0.0)
    den = jnp.sum(p, axis=-1, keepdims=True)
    return p / jnp.where(den > 0.0, den, 1.0)


def _nsa_kernel(q_ref, gc_ref, gs_ref, gw_ref, kc_ref, vc_ref, ks_ref, vs_ref, kw_ref, vw_ref,
                ov_ref, o_ref, *, tq, tks, seq):
    g = pl.program_id(1)
    i = pl.program_id(2)
    q0 = i * tq
    ncp = kc_ref.shape[1]
    lane = _lane_iota((tq, LANES))
    in_g = (lane // HEAD_DIM) == g
    lane_lo = lane < HEAD_DIM

    qa = []
    for s in range(2):
        qs = q_ref[0, :, s * LANES:(s + 1) * LANES].astype(F32)
        for p in range(2):
            mine = jnp.where((lane // HEAD_DIM) == p, qs, 0.0)
            both = mine + _swap_halves(mine)
            qa.append(jnp.where(in_g, both, 0.0).astype(MXU_DTYPE))

    t_c = q0 + _row_iota((tq, ncp))
    cmask = (_lane_iota((tq, ncp)) * NSA_CMP_STRIDE + (NSA_CMP_LEN - 1)) <= t_c
    kc = kc_ref[0]
    vc = vc_ref[0]
    pc_sum = jnp.zeros((tq, ncp), F32)
    o_cmp = []
    for j in range(NSA_HPG):
        pc = _softmax_rows(_dot_nt(qa[j], kc), cmask)
        pc_sum = pc_sum + pc
        o_cmp.append(_dot(pc.astype(MXU_DTYPE), vc))

    hi = pc_sum.astype(MXU_DTYPE)
    lo = (pc_sum - hi.astype(F32)).astype(MXU_DTYPE)
    imp = _dot(hi, ov_ref[...]) + _dot(lo, ov_ref[...])
    t_b = q0 + _row_iota((tq, LANES))
    cur = t_b // NSA_SEL_LEN
    visible = lane * NSA_SEL_LEN <= t_b
    forced = (lane == 0) | (lane == cur) | (lane == cur - 1)
    work = jnp.where(visible, jnp.where(forced, NSA_FORCED_SCORE, imp), NEG)
    lane_f = lane.astype(F32)
    sel = jnp.zeros((tq, LANES), F32)
    for _ in range(NSA_SEL_N):
        m = jnp.max(work, axis=-1, keepdims=True)
        first = jnp.min(jnp.where(work == m, lane_f, float(LANES)), axis=-1, keepdims=True)
        pick = lane_f == first
        sel = jnp.where(pick, 1.0, sel)
        work = jnp.where(pick, -3.0e38, work)
    sel = jnp.where(visible, sel, 0.0).astype(MXU_DTYPE)

    n_tiles = (q0 + tq + tks - 1) // tks
    t_s = q0 + _row_iota((tq, tks))
    blk_of_col = _lane_iota((LANES, tks)) // NSA_SEL_LEN
    blk_row = _row_iota((LANES, tks))

    def sel_body(kt, st):
        ks = ks_ref[0, pl.ds(kt * tks, tks), :]
        vs = vs_ref[0, pl.ds(kt * tks, tks), :]
        expand = jnp.where(blk_row - kt * (tks // NSA_SEL_LEN) == blk_of_col, 1.0, 0.0).astype(MXU_DTYPE)
        chosen = _dot(sel, expand)
        s_abs = kt * tks + _lane_iota((tq, tks))
        ok = jnp.where(s_abs <= t_s, chosen, 0.0) > 0.5
        new = []
        for j in range(NSA_HPG):
            m_j, l_j, acc_j = st[j]
            s = jnp.where(ok, _dot_nt(qa[j], ks), NEG)
            m_new = jnp.maximum(m_j, jnp.max(s, axis=-1, keepdims=True))
            alpha = jnp.exp(m_j - m_new)
            p = jnp.where(ok, jnp.exp(s - m_new), 0.0)
            l_j = alpha * l_j + jnp.sum(p, axis=-1, keepdims=True)
            acc_j = alpha * acc_j + _dot(p.astype(MXU_DTYPE), vs)
            new.append((m_new, l_j, acc_j))
        return tuple(new)

    init = tuple((jnp.full((tq, 1), NEG, F32), jnp.zeros((tq, 1), F32), jnp.zeros((tq, LANES), F32))
                 for _ in range(NSA_HPG))
    st = lax.fori_loop(0, n_tiles, sel_body, init)
    o_sel = [acc_j / jnp.where(l_j > 0.0, l_j, 1.0) for _, l_j, acc_j in st]

    band = tq + NSA_WINDOW
    start = pl.multiple_of(jnp.maximum(q0 - NSA_WINDOW, 0), tq)
    kw = kw_ref[0, pl.ds(start, band), :]
    vw = vw_ref[0, pl.ds(start, band), :]
    t_w = q0 + _row_iota((tq, band))
    s_w = start + _lane_iota((tq, band))
    wmask = (s_w <= t_w) & (s_w > t_w - NSA_WINDOW)
    o_win = [_dot(_softmax_rows(_dot_nt(qa[j], kw), wmask).astype(MXU_DTYPE), vw) for j in range(NSA_HPG)]

    del seq
    for s in range(2):
        tot = jnp.zeros((tq, LANES), F32)
        for branch, gate_ref in ((o_cmp, gc_ref), (o_sel, gs_ref), (o_win, gw_ref)):
            a = jnp.where(in_g, branch[2 * s], 0.0)
            bb = jnp.where(in_g, branch[2 * s + 1], 0.0)
            x = a + _swap_halves(bb)
            x = jnp.where(g == 0, x, _swap_halves(x))
            tot = tot + gate_ref[0, :, s * LANES:(s + 1) * LANES] * x
        o_ref[0, :, s * LANES:(s + 1) * LANES] = tot.astype(o_ref.dtype)
    del lane_lo


def _nsa_attention(nq, ng, kc, vc, nkv, overlap, *, tq, tks):
    b, s, _ = nq.shape
    ncp = kc.shape[1]
    gw = NSA_HPG * HEAD_DIM
    full = lambda col: (lambda bi, g, i: (bi, 0, col))
    return pl.pallas_call(
        functools.partial(_nsa_kernel, tq=tq, tks=tks, seq=s),
        grid=(b, NSA_GROUPS, s // tq),
        in_specs=[pl.BlockSpec((1, tq, gw), lambda bi, g, i: (bi, i, g)),
                  pl.BlockSpec((1, tq, gw), lambda bi, g, i: (bi, i, g)),
                  pl.BlockSpec((1, tq, gw), lambda bi, g, i: (bi, i, NSA_GROUPS + g)),
                  pl.BlockSpec((1, tq, gw), lambda bi, g, i: (bi, i, 2 * NSA_GROUPS + g)),
                  pl.BlockSpec((1, ncp, LANES), full(0)),
                  pl.BlockSpec((1, ncp, LANES), full(0)),
                  pl.BlockSpec((1, s, LANES), full(0)),
                  pl.BlockSpec((1, s, LANES), full(1)),
                  pl.BlockSpec((1, s, LANES), full(2)),
                  pl.BlockSpec((1, s, LANES), full(3)),
                  pl.BlockSpec(overlap.shape, lambda bi, g, i: (0, 0))],
        out_specs=pl.BlockSpec((1, tq, gw), lambda bi, g, i: (bi, i, g)),
        out_shape=jax.ShapeDtypeStruct((b, s, NSA_QW), MXU_DTYPE),
        compiler_params=_params(3),
        name="nsa_attn",
    )(nq, ng, ng, ng, kc, vc, nkv, nkv, nkv, nkv, overlap)


def _dsa_kernel(iq_ref, iw_ref, ik_ref, q_ref, kv_ref, o_ref, keys_ref, *, tq, tk, seq, top_k):
    i = pl.program_id(1)
    q0 = i * tq
    n_tiles = (q0 + tq + tk - 1) // tk
    lane = _lane_iota((tq, LANES))
    t_k = q0 + _row_iota((tq, tk))
    fold = tk // LANES

    iqa = []
    for s in range(IDX_QW // LANES):
        qs = iq_ref[0, :, s * LANES:(s + 1) * LANES].astype(F32)
        for p in range(LANES // DSA_IDX_DIM):
            mine = jnp.where((lane // DSA_IDX_DIM) == p, qs, 0.0)
            if p:
                mine = pltpu.roll(mine, LANES - p * DSA_IDX_DIM, 1)
            iqa.append(mine.astype(MXU_DTYPE))
    iw = iw_ref[0]
    w_h = [iw[:, h:h + 1] for h in range(DSA_IDX_HEADS)]

    def score_body(kt, _):
        ik = ik_ref[0, pl.ds(kt * tk, tk), :]
        score = jnp.zeros((tq, tk), F32)
        for h in range(DSA_IDX_HEADS):
            score = score + w_h[h] * jnp.maximum(_dot_nt(iqa[h], ik), 0.0)
        score = jnp.where(score == 0.0, 0.0, score)
        bits = pltpu.bitcast(score, jnp.int32)
        key = jnp.where(bits < 0, bits ^ 0x7FFFFFFF, bits)
        s_abs = kt * tk + _lane_iota((tq, tk))
        keys_ref[kt] = jnp.where(s_abs <= t_k, key, INT_MIN + 1)
        return 0

    lax.fori_loop(0, n_tiles, score_body, 0)

    def count(pred):
        def body(kt, acc):
            c = jnp.where(pred(keys_ref[kt], kt), 1.0, 0.0)
            for f in range(fold):
                acc = acc + c[:, f * LANES:(f + 1) * LANES]
            return acc
        acc = lax.fori_loop(0, n_tiles, body, jnp.zeros((tq, LANES), F32))
        return jnp.sum(acc, axis=-1, keepdims=True)

    def thr_body(b, ans_u):
        cand_u = ans_u | lax.shift_left(jnp.int32(1), 31 - b)
        cand_s = cand_u ^ INT_MIN
        cnt = count(lambda k, kt: k >= cand_s)
        return jnp.where(cnt >= top_k, cand_u, ans_u)

    ans_u = lax.fori_loop(0, 32, thr_body, jnp.zeros((tq, 1), jnp.int32))
    thr = ans_u ^ INT_MIN
    need = top_k - count(lambda k, kt: k > thr)

    nbits = int(seq).bit_length()

    def tie_body(b, j_cur):
        cand = j_cur | lax.shift_left(jnp.int32(1), nbits - 1 - b)
        cnt = count(lambda k, kt: (k == thr) & ((kt * tk + _lane_iota((tq, tk))) < cand))
        return jnp.where(cnt <= need, cand, j_cur)

    j_lim = lax.fori_loop(0, nbits, tie_body, jnp.zeros((tq, 1), jnp.int32))

    qa = []
    for s in range(DSA_QW // LANES):
        qs = q_ref[0, :, s * LANES:(s + 1) * LANES].astype(F32)
        lo = jnp.where(lane < HEAD_DIM, qs, 0.0)
        hi = _swap_halves(jnp.where(lane < HEAD_DIM, 0.0, qs))
        qa += [lo.astype(MXU_DTYPE), hi.astype(MXU_DTYPE)]

    def att_body(kt, st):
        kv = kv_ref[0, pl.ds(kt * tk, tk), :]
        key = keys_ref[kt]
        s_abs = kt * tk + _lane_iota((tq, tk))
        taken = (key > thr) | ((key == thr) & (s_abs < j_lim))
        ok = (s_abs <= t_k) & taken
        new = []
        for h in range(DSA_HEADS):
            m_h, l_h, acc_h = st[h]
            s = jnp.where(ok, _dot_nt(qa[h], kv), NEG)
            m_new = jnp.maximum(m_h, jnp.max(s, axis=-1, keepdims=True))
            alpha = jnp.exp(m_h - m_new)
            p = jnp.where(ok, jnp.exp(s - m_new), 0.0)
            l_h = alpha * l_h + jnp.sum(p, axis=-1, keepdims=True)
            acc_h = alpha * acc_h + _dot(p.astype(MXU_DTYPE), kv)
            new.append((m_new, l_h, acc_h))
        return tuple(new)

    init = tuple((jnp.full((tq, 1), NEG, F32), jnp.zeros((tq, 1), F32), jnp.zeros((tq, LANES), F32))
                 for _ in range(DSA_HEADS))
    st = lax.fori_loop(0, n_tiles, att_body, init)

    for s in range(DSA_QW // LANES):
        o_even = st[2 * s][2] / jnp.where(st[2 * s][1] > 0.0, st[2 * s][1], 1.0)
        o_odd = st[2 * s + 1][2] / jnp.where(st[2 * s + 1][1] > 0.0, st[2 * s + 1][1], 1.0)
        out = jnp.where(lane < HEAD_DIM, _swap_halves(o_even), o_odd)
        o_ref[0, :, s * LANES:(s + 1) * LANES] = out.astype(o_ref.dtype)


def _dsa_attention(iq, iw, ik, dq, dkv, *, tq, tk, top_k):
    b, s, _ = dq.shape
    tile = lambda width: pl.BlockSpec((1, tq, width), lambda bi, i: (bi, i, 0))
    full = pl.BlockSpec((1, s, LANES), lambda bi, i: (bi, 0, 0))
    return pl.pallas_call(
        functools.partial(_dsa_kernel, tq=tq, tk=tk, seq=s, top_k=top_k),
        grid=(b, s // tq),
        in_specs=[tile(IDX_QW), tile(LANES), full, tile(DSA_QW), full],
        out_specs=tile(DSA_QW),
        out_shape=jax.ShapeDtypeStruct((b, s, DSA_QW), MXU_DTYPE),
        scratch_shapes=[pltpu.VMEM((s // tk, tq, tk), jnp.int32)],
        compiler_params=_params(2),
        name="dsa_attn",
    )(iq, iw, ik, dq, dkv)


def _merge_kernel(x_ref, g_ref, oa_ref, ob_ref, oc_ref, wg_ref, wa_ref, wb_ref, wc_ref, wo_ref, o_ref):
    x = x_ref[...]
    d = x.shape[1]
    ms = jnp.mean(x * x, axis=-1, keepdims=True)
    h = (x * lax.rsqrt(ms + NORM_EPS) * g_ref[...]).astype(MXU_DTYPE)
    merged = jnp.zeros(x.shape, F32)
    for r, (o_r, w_r) in enumerate(((oa_ref, wa_ref), (ob_ref, wb_ref), (oc_ref, wc_ref))):
        gate = jax.nn.sigmoid(_dot(h, wg_ref[:, r * d:(r + 1) * d]))
        merged = merged + gate * _dot(o_r[...], w_r[...])
    o_ref[...] = x + _dot(merged.astype(MXU_DTYPE), wo_ref[...])


def _merge(x2, g, oa, ob, oc, w_gate, wa, wb, wc, wo, *, tm):
    n, d = x2.shape
    row = lambda i: (i, 0)
    const = lambda i: (0, 0)
    return pl.pallas_call(
        _merge_kernel,
        grid=(n // tm,),
        in_specs=[pl.BlockSpec((tm, d), row), pl.BlockSpec((1, d), const),
                  pl.BlockSpec((tm, SB_W), row), pl.BlockSpec((tm, NSA_QW), row),
                  pl.BlockSpec((tm, DSA_QW), row),
                  pl.BlockSpec(w_gate.shape, const), pl.BlockSpec(wa.shape, const),
                  pl.BlockSpec(wb.shape, const), pl.BlockSpec(wc.shape, const),
                  pl.BlockSpec(wo.shape, const)],
        out_specs=pl.BlockSpec((tm, d), row),
        out_shape=jax.ShapeDtypeStruct((n, d), F32),
        compiler_params=_params(1),
        name="merge",
    )(x2, g, oa, ob, oc, w_gate, wa, wb, wc, wo)


def _rope_tables(pos, rot_dim, head_dim):
    half = rot_dim // 2
    inv = ROPE_THETA ** (-jnp.arange(0, rot_dim, 2, dtype=F32) / rot_dim)
    ang = pos.astype(F32)[:, None] * inv[None, :]
    cos, sin = jnp.cos(ang), jnp.sin(ang)
    n = pos.shape[0]
    rest = head_dim - rot_dim
    zero_h = jnp.zeros((n, half), F32)
    c = jnp.concatenate([cos, cos, jnp.ones((n, rest), F32)], axis=-1)
    sm = jnp.concatenate([-sin, zero_h, jnp.zeros((n, rest), F32)], axis=-1)
    sp = jnp.concatenate([zero_h, sin, jnp.zeros((n, rest), F32)], axis=-1)
    reps = LANES // head_dim
    return tuple(jnp.tile(t, (1, reps)) for t in (c, sm, sp))


def _pad_cols(w, width):
    return jnp.pad(w, ((0, 0), (0, width - w.shape[1])))


def _relayout_w_in(w_in):
    offs = np.cumsum((SB_W, SB_W, SB_W, NSA_QW) + (NSA_KVW,) * 6
                     + (3 * NSA_HEADS, DSA_QW, HEAD_DIM, HEAD_DIM, IDX_QW, DSA_IDX_DIM, DSA_IDX_HEADS))
    o_gate, o_dq, o_dk, o_dv, o_iq, o_ik, o_iw, o_end = offs[9:17].tolist()
    gate = w_in[:, o_gate:o_dq].reshape(-1, NSA_HEADS, 3)
    gate = jnp.repeat(jnp.transpose(gate, (0, 2, 1)), HEAD_DIM, axis=2)
    parts = [w_in[:, :o_gate], gate.reshape(w_in.shape[0], 3 * NSA_QW), w_in[:, o_dq:o_dk],
             w_in[:, o_dk:o_iq], w_in[:, o_iq:o_ik], _pad_cols(w_in[:, o_ik:o_iw], LANES),
             _pad_cols(w_in[:, o_iw:o_end], LANES)]
    w = jnp.concatenate(parts, axis=1)
    assert w.shape[1] == _P_END
    return w.astype(MXU_DTYPE)


def _compress_weights(pe, w1, w2):
    r = NSA_CMP_LEN // NSA_CMP_STRIDE
    hid = NSA_CMP_HIDDEN
    w1r = w1.reshape(2, r, NSA_CMP_STRIDE, HEAD_DIM, hid)
    eye = jnp.eye(NSA_GROUPS, dtype=w1.dtype)
    w1e = jnp.einsum('kmldj,gh->kmlgdhj', w1r, eye)
    w1e = w1e.reshape(2, r, NSA_CMP_STRIDE * NSA_KVW, NSA_GROUPS * hid)
    w2e = jnp.einsum('kjd,gh->kgjhd', w2, eye).reshape(2, NSA_GROUPS * hid, NSA_KVW)
    pe_e = jnp.broadcast_to(pe.reshape(2, r, NSA_CMP_STRIDE, 1, HEAD_DIM),
                            (2, r, NSA_CMP_STRIDE, NSA_GROUPS, HEAD_DIM))
    return pe_e.reshape(2, r, NSA_CMP_STRIDE * NSA_KVW), w1e.astype(MXU_DTYPE), w2e.astype(MXU_DTYPE)


def _overlap_matrix(ncp, seq):
    c = np.arange(ncp)[:, None] * NSA_CMP_STRIDE
    j = np.arange(LANES)[None, :] * NSA_SEL_LEN
    ov = (c < j + NSA_SEL_LEN) & (c + NSA_CMP_LEN - 1 >= j) & (j < seq)
    ov &= (np.arange(ncp)[:, None] < ncp - (NSA_CMP_LEN // NSA_CMP_STRIDE - 1))
    return jnp.asarray(ov, dtype=MXU_DTYPE)


def kernel(x, ffn1_norm, ffn1_w_gu, ffn1_w_down, mix_norm, w_in, w_gate, nsa_q_norm, nsa_k_norm,
           nsa_cmp_pe, nsa_cmp_w1, nsa_cmp_w2, dsa_q_norm, dsa_k_norm, w_br_a, w_br_b, w_br_c, w_out,
           ffn2_norm, ffn2_w_gu, ffn2_w_down):
    b, s, d = x.shape
    depth = w_in.shape[0]
    n = b * s
    ncp = s // NSA_CMP_STRIDE
    assert s // NSA_SEL_LEN <= LANES and s % 1024 == 0
    top_k = min(DSA_TOPK, s // 4)
    cast = lambda w: w.astype(MXU_DTYPE)
    tile2 = lambda v, reps: jnp.tile(v, reps)[None, :]

    pos = jnp.arange(s)
    rope_main = _rope_tables(pos, ROT_DIM, HEAD_DIM)
    rope_idx = _rope_tables(pos, DSA_IDX_ROT, DSA_IDX_DIM)
    rope_cmp = _rope_tables(jnp.arange(ncp) * NSA_CMP_STRIDE + NSA_CMP_LEN - 1, ROT_DIM, HEAD_DIM)
    overlap = _overlap_matrix(ncp, s)

    x2 = x.reshape(n, d)
    for l in range(depth):
        x2 = _ffn(x2, ffn1_norm[l][None, :], cast(ffn1_w_gu[l]), cast(ffn1_w_down[l]), tm=1024, tf=256)

        (sb, nq, nkc, nvc, nkv, ng, dq, dkv, iq, ik, iw) = _proj(
            x2, mix_norm[l][None, :], _relayout_w_in(w_in[l]),
            tile2(nsa_q_norm[l], NSA_HEADS),
            jnp.stack([jnp.tile(nsa_k_norm[l, 1], NSA_GROUPS), jnp.tile(nsa_k_norm[l, 2], NSA_GROUPS)]),
            tile2(dsa_q_norm[l], DSA_HEADS), tile2(dsa_k_norm[l], 2),
            rope_main, rope_idx, tm=256, seq=s)
        r3 = lambda t: t.reshape(b, s, t.shape[-1])

        pe_e, w1e, w2e = _compress_weights(nsa_cmp_pe[l], nsa_cmp_w1[l], nsa_cmp_w2[l])
        kc, vc = _compress(nkc.reshape(b, ncp, NSA_CMP_STRIDE * NSA_KVW),
                           nvc.reshape(b, ncp, NSA_CMP_STRIDE * NSA_KVW),
                           pe_e, w1e, w2e, tile2(nsa_k_norm[l, 0], NSA_GROUPS), rope_cmp)

        o_a = _sb_attention(r3(sb), tq=256)
        o_b = _nsa_attention(r3(nq), r3(ng), kc, vc, r3(nkv), overlap, tq=128, tks=512)
        o_c = _dsa_attention(r3(iq), r3(iw), r3(ik), r3(dq), r3(dkv), tq=128, tk=512, top_k=top_k)

        x2 = _merge(x2, mix_norm[l][None, :], o_a.reshape(n, SB_W), o_b.reshape(n, NSA_QW),
                    o_c.reshape(n, DSA_QW), cast(w_gate[l]), cast(w_br_a[l]), cast(w_br_b[l]),
                    cast(w_br_c[l]), cast(w_out[l]), tm=256)

        x2 = _ffn(x2, ffn2_norm[l][None, :], cast(ffn2_w_gu[l]), cast(ffn2_w_down[l]), tm=1024, tf=256)
    return x2.reshape(b, s, d)
```

```python
import functools

import numpy as np
import jax
import jax.numpy as jnp
from jax import lax
from jax.experimental import pallas as pl
from jax.experimental.pallas import tpu as pltpu

HEAD_DIM = 64
ROT_DIM = HEAD_DIM // 4
ROPE_THETA = 500000.0
NORM_EPS = 1e-6

SB_HEADS = 4
NSA_HEADS = 8
NSA_GROUPS = 2
NSA_HPG = NSA_HEADS // NSA_GROUPS
NSA_CMP_LEN = 32
NSA_CMP_STRIDE = 16
NSA_CMP_HIDDEN = 2 * HEAD_DIM
NSA_SEL_LEN = 64
NSA_SEL_N = 16
NSA_WINDOW = 512
NSA_FORCED_SCORE = 1.0e4
DSA_HEADS = 4
DSA_IDX_HEADS = 8
DSA_IDX_DIM = 32
DSA_IDX_ROT = DSA_IDX_DIM // 4
DSA_TOPK = 256

SB_W = SB_HEADS * HEAD_DIM
NSA_QW = NSA_HEADS * HEAD_DIM
NSA_KVW = NSA_GROUPS * HEAD_DIM
DSA_QW = DSA_HEADS * HEAD_DIM
IDX_QW = DSA_IDX_HEADS * DSA_IDX_DIM

LANES = 128
VMEM_LIMIT = 56 * 1024 * 1024
MXU_DTYPE = jnp.bfloat16
NEG = -1.0e30
SB_SKIP = -120.0
INT_MIN = -2 ** 31

F32 = jnp.float32

TQ = 256
TKS = 512
assert TKS % TQ == 0 and NSA_WINDOW % TQ == 0


def _dot(a, b):
    return jnp.dot(a, b, preferred_element_type=F32)


def _dot_nt(a, b):
    return lax.dot_general(a, b, (((1,), (1,)), ((), ())), preferred_element_type=F32)


def _params(n_axes):
    return pltpu.CompilerParams(dimension_semantics=("arbitrary",) * n_axes,
                                vmem_limit_bytes=VMEM_LIMIT)


def _lane_iota(shape):
    return lax.broadcasted_iota(jnp.int32, shape, len(shape) - 1)


def _row_iota(shape):
    return lax.broadcasted_iota(jnp.int32, shape, len(shape) - 2)


def _swap_halves(x):
    return pltpu.roll(x, LANES // 2, 1)


def _head_rms(ys):
    lo = _lane_iota(ys.shape) < HEAD_DIM
    sq = ys * ys
    s_lo = jnp.sum(jnp.where(lo, sq, 0.0), axis=-1, keepdims=True)
    s_hi = jnp.sum(jnp.where(lo, 0.0, sq), axis=-1, keepdims=True)
    ms = jnp.where(lo, s_lo, s_hi) * (1.0 / HEAD_DIM)
    return ys * lax.rsqrt(ms + NORM_EPS)


def _rope(ys, c, sm, sp, half):
    return ys * c + pltpu.roll(ys, LANES - half, 1) * sm + pltpu.roll(ys, half, 1) * sp


def _ffn_kernel(x_ref, g_ref, wg_ref, wu_ref, wd_ref, o_ref, h_ref, acc_ref):
    f = pl.program_id(1)

    @pl.when(f == 0)
    def _():
        x = x_ref[...]
        ms = jnp.mean(x * x, axis=-1, keepdims=True)
        h_ref[...] = (x * lax.rsqrt(ms + NORM_EPS) * g_ref[...]).astype(h_ref.dtype)
        acc_ref[...] = jnp.zeros_like(acc_ref)

    h = h_ref[...]
    gate = _dot(h, wg_ref[...])
    up = _dot(h, wu_ref[...])
    act = (gate * jax.nn.sigmoid(gate) * up).astype(h_ref.dtype)
    acc_ref[...] += _dot(act, wd_ref[...])

    @pl.when(f == pl.num_programs(1) - 1)
    def _():
        o_ref[...] = x_ref[...] + 0.5 * acc_ref[...]


def _ffn(x2, g, w_gu, w_down, *, tm, tf):
    n, d = x2.shape
    d_ff = w_down.shape[0]
    nf = d_ff // tf
    return pl.pallas_call(
        _ffn_kernel,
        grid=(n // tm, nf),
        in_specs=[
            pl.BlockSpec((tm, d), lambda i, f: (i, 0)),
            pl.BlockSpec((1, d), lambda i, f: (0, 0)),
            pl.BlockSpec((d, tf), lambda i, f: (0, f)),
            pl.BlockSpec((d, tf), lambda i, f: (0, f + nf)),
            pl.BlockSpec((tf, d), lambda i, f: (f, 0)),
        ],
        out_specs=pl.BlockSpec((tm, d), lambda i, f: (i, 0)),
        out_shape=jax.ShapeDtypeStruct((n, d), F32),
        scratch_shapes=[pltpu.VMEM((tm, d), MXU_DTYPE), pltpu.VMEM((tm, d), F32)],
        compiler_params=_params(2),
        name="ffn",
    )(x2, g, w_gu, w_gu, w_down)


_P_SB = 0
_P_NQ = _P_SB + 3 * SB_W
_P_NCV = _P_NQ + NSA_QW
_P_NKV = _P_NCV + 2 * NSA_KVW
_P_NG = _P_NKV + 4 * NSA_KVW
_P_DQ = _P_NG + 3 * NSA_QW
_P_DKV = _P_DQ + DSA_QW
_P_IQ = _P_DKV + LANES
_P_IK = _P_IQ + IDX_QW
_P_IW = _P_IK + LANES
_P_END = _P_IW + LANES


def _proj_kernel(x_ref, g_ref, w_ref, nq_g_ref, nk_g_ref, dq_g_ref, dk_g_ref,
                 rc_ref, rm_ref, rp_ref, ic_ref, im_ref, ip_ref,
                 sb_ref, nq_ref, nkc_ref, nvc_ref, nkv_ref, ng_ref, dq_ref, dkv_ref,
                 iq_ref, ik_ref, iw_ref):
    x = x_ref[...]
    ms = jnp.mean(x * x, axis=-1, keepdims=True)
    h = (x * lax.rsqrt(ms + NORM_EPS) * g_ref[...]).astype(MXU_DTYPE)
    rc, rm, rp = rc_ref[...], rm_ref[...], rp_ref[...]
    ic, im, ip = ic_ref[...], im_ref[...], ip_ref[...]
    half, ihalf = ROT_DIM // 2, DSA_IDX_ROT // 2
    scale = HEAD_DIM ** -0.5

    def cols(a, width):
        return _dot(h, w_ref[:, a:a + width])

    def slab(y, s):
        return y[:, s * LANES:(s + 1) * LANES]

    y = cols(_P_SB, 3 * SB_W)
    sb_ref[:, 0:SB_W] = (y[:, 0:SB_W] * scale).astype(sb_ref.dtype)
    sb_ref[:, SB_W:3 * SB_W] = y[:, SB_W:3 * SB_W].astype(sb_ref.dtype)

    y = cols(_P_NQ, NSA_QW)
    for s in range(NSA_QW // LANES):
        ys = _head_rms(slab(y, s)) * slab(nq_g_ref[...], s)
        nq_ref[:, s * LANES:(s + 1) * LANES] = (_rope(ys, rc, rm, rp, half) * scale).astype(nq_ref.dtype)

    y = cols(_P_NCV, 2 * NSA_KVW)
    nkc_ref[...] = slab(y, 0)
    nvc_ref[...] = slab(y, 1)

    y = cols(_P_NKV, 4 * NSA_KVW)
    for s, gi in ((0, 0), (2, 1)):
        ys = _head_rms(slab(y, s)) * nk_g_ref[gi:gi + 1, :]
        nkv_ref[:, s * LANES:(s + 1) * LANES] = _rope(ys, rc, rm, rp, half).astype(nkv_ref.dtype)
    for s in (1, 3):
        nkv_ref[:, s * LANES:(s + 1) * LANES] = slab(y, s).astype(nkv_ref.dtype)

    ng_ref[...] = jax.nn.sigmoid(cols(_P_NG, 3 * NSA_QW))

    y = cols(_P_DQ, DSA_QW)
    for s in range(DSA_QW // LANES):
        ys = _head_rms(slab(y, s)) * slab(dq_g_ref[...], s)
        dq_ref[:, s * LANES:(s + 1) * LANES] = (_rope(ys, rc, rm, rp, half) * scale).astype(dq_ref.dtype)

    y = cols(_P_DKV, LANES)
    yk = _rope(_head_rms(y) * dk_g_ref[...], rc, rm, rp, half)
    dkv_ref[...] = jnp.where(_lane_iota(y.shape) < HEAD_DIM, yk, y).astype(dkv_ref.dtype)

    y = cols(_P_IQ, IDX_QW)
    for s in range(IDX_QW // LANES):
        iq_ref[:, s * LANES:(s + 1) * LANES] = _rope(slab(y, s), ic, im, ip, ihalf).astype(iq_ref.dtype)
    ik_ref[...] = _rope(cols(_P_IK, LANES), ic, im, ip, ihalf).astype(ik_ref.dtype)
    iw_ref[...] = cols(_P_IW, LANES) * (DSA_IDX_HEADS ** -0.5) * (DSA_IDX_DIM ** -0.5)


def _proj(x2, g, w, nq_g, nk_g, dq_g, dk_g, rope_main, rope_idx, *, tm, seq):
    n, d = x2.shape
    nt = seq // tm
    row = lambda i: (i, 0)
    const = lambda i: (0, 0)
    pos = lambda i: (i % nt, 0)
    widths = [(3 * SB_W, MXU_DTYPE), (NSA_QW, MXU_DTYPE), (NSA_KVW, F32), (NSA_KVW, F32),
              (4 * NSA_KVW, MXU_DTYPE), (3 * NSA_QW, F32), (DSA_QW, MXU_DTYPE), (LANES, MXU_DTYPE),
              (IDX_QW, MXU_DTYPE), (LANES, MXU_DTYPE), (LANES, F32)]
    return pl.pallas_call(
        _proj_kernel,
        grid=(n // tm,),
        in_specs=[pl.BlockSpec((tm, d), row), pl.BlockSpec((1, d), const),
                  pl.BlockSpec((d, _P_END), const),
                  pl.BlockSpec((1, NSA_QW), const), pl.BlockSpec((2, LANES), const),
                  pl.BlockSpec((1, DSA_QW), const), pl.BlockSpec((1, LANES), const)]
                 + [pl.BlockSpec((tm, LANES), pos)] * 6,
        out_specs=[pl.BlockSpec((tm, wd), row) for wd, _ in widths],
        out_shape=[jax.ShapeDtypeStruct((n, wd), dt) for wd, dt in widths],
        compiler_params=_params(1),
        name="proj",
    )(x2, g, w, nq_g, nk_g, dq_g, dk_g, *rope_main, *rope_idx)


def _compress_kernel(xk_ref, xv_ref, pe_ref, w1_ref, w2_ref, g_ref, rc_ref, rm_ref, rp_ref,
                     kc_ref, vc_ref):
    ncp = xk_ref.shape[1]

    def compress(x, kv):
        a0 = _dot((x + pe_ref[kv, 0:1, :]).astype(MXU_DTYPE), w1_ref[kv, 0])
        a1 = _dot((x + pe_ref[kv, 1:2, :]).astype(MXU_DTYPE), w1_ref[kv, 1])
        pre = a0 + pltpu.roll(a1, ncp - 1, 0)
        hid = pre * jax.nn.sigmoid(pre)
        return _dot(hid.astype(MXU_DTYPE), w2_ref[kv])

    yk = _head_rms(compress(xk_ref[0], 0)) * g_ref[...]
    kc_ref[0] = _rope(yk, rc_ref[...], rm_ref[...], rp_ref[...], ROT_DIM // 2).astype(kc_ref.dtype)
    vc_ref[0] = compress(xv_ref[0], 1).astype(vc_ref.dtype)


def _compress(xk, xv, pe, w1, w2, g, rope_c):
    b, ncp, wide = xk.shape
    c3 = lambda i: (0, 0, 0)
    c4 = lambda i: (0, 0, 0, 0)
    c2 = lambda i: (0, 0)
    per_b = lambda i: (i, 0, 0)
    return pl.pallas_call(
        _compress_kernel,
        grid=(b,),
        in_specs=[pl.BlockSpec((1, ncp, wide), per_b), pl.BlockSpec((1, ncp, wide), per_b),
                  pl.BlockSpec(pe.shape, c3), pl.BlockSpec(w1.shape, c4), pl.BlockSpec(w2.shape, c3),
                  pl.BlockSpec((1, LANES), c2)] + [pl.BlockSpec((ncp, LANES), c2)] * 3,
        out_specs=[pl.BlockSpec((1, ncp, LANES), per_b)] * 2,
        out_shape=[jax.ShapeDtypeStruct((b, ncp, LANES), MXU_DTYPE)] * 2,
        compiler_params=_params(1),
        name="compress",
    )(xk, xv, pe, w1, w2, g, *rope_c)


def _sb_kernel(q_ref, k_ref, v_ref, o_ref, *, tq):
    i = pl.program_id(2)
    q0 = i * tq
    q = q_ref[0].astype(F32)
    lane = _lane_iota(q.shape)
    rows = _row_iota((tq, tq))
    colsq = _lane_iota((tq, tq))
    upper = jnp.where(rows > colsq, 1.0, 0.0).astype(MXU_DTYPE)
    out = jnp.zeros((tq, LANES), F32)

    for p in range(2):
        in_head = (lane >= p * HEAD_DIM) & (lane < (p + 1) * HEAD_DIM)
        qh = jnp.where(in_head, q, 0.0).astype(MXU_DTYPE)

        def tile(kt, carry, acc, diagonal):
            ks = k_ref[0, pl.ds(kt * tq, tq), :]
            vs = v_ref[0, pl.ds(kt * tq, tq), :]
            z = _dot_nt(qh, ks)
            l = -(jnp.maximum(z, 0.0) + jnp.log1p(jnp.exp(-jnp.abs(z))))
            if diagonal:
                past = colsq < rows
                l = jnp.where(past, l, 0.0)
            hi = l.astype(MXU_DTYPE)
            lo = (l - hi.astype(F32)).astype(MXU_DTYPE)
            tail = _dot(hi, upper) + _dot(lo, upper) + carry
            w = jnp.exp(z + l + tail)
            if diagonal:
                w = jnp.where(past, w, 0.0)
            acc = acc + _dot(w.astype(MXU_DTYPE), vs)
            carry = carry + jnp.sum(l, axis=-1, keepdims=True)
            return carry, acc

        carry, acc = tile(i, jnp.zeros((tq, 1), F32), jnp.zeros((tq, LANES), F32), True)

        def cond(st):
            kt, carry, _ = st
            return (kt >= 0) & (jnp.max(carry) > SB_SKIP)

        def body(st):
            kt, carry, acc = st
            carry, acc = tile(kt, carry, acc, False)
            return kt - 1, carry, acc

        _, _, acc = lax.while_loop(cond, body, (i - 1, carry, acc))
        out = jnp.where(in_head, acc, out)

    del q0
    o_ref[0] = out.astype(o_ref.dtype)


def _sb_attention(sb, *, tq):
    b, s, _ = sb.shape
    nslab = SB_W // LANES
    return pl.pallas_call(
        functools.partial(_sb_kernel, tq=tq),
        grid=(b, nslab, s // tq),
        in_specs=[pl.BlockSpec((1, tq, LANES), lambda bi, sl, i: (bi, i, sl)),
                  pl.BlockSpec((1, s, LANES), lambda bi, sl, i: (bi, 0, nslab + sl)),
                  pl.BlockSpec((1, s, LANES), lambda bi, sl, i: (bi, 0, 2 * nslab + sl))],
        out_specs=pl.BlockSpec((1, tq, LANES), lambda bi, sl, i: (bi, i, sl)),
        out_shape=jax.ShapeDtypeStruct((b, s, SB_W), MXU_DTYPE),
        compiler_params=_params(3),
        name="sb_attn",
    )(sb, sb, sb)


def _group_queries(q_ref, g, tq):
    lane = _lane_iota((tq, LANES))
    in_g = (lane >> 6) == g
    qa = []
    for s in range(2):
        qs = q_ref[0, :, s * LANES:(s + 1) * LANES].astype(F32)
        for p in range(2):
            mine = jnp.where((lane >> 6) == p, qs, 0.0)
            both = mine + _swap_halves(mine)
            qa.append(jnp.where(in_g, both, 0.0).astype(MXU_DTYPE))
    return qa


def _fold_rows(x, op):
    while x.shape[0] > 64 and x.shape[0] % 16 == 0:
        half = x.shape[0] // 2
        x = op(x[:half], x[half:])
    return x


def _key_max(x):
    return jnp.max(_fold_rows(x, jnp.maximum), axis=0, keepdims=True)


def _key_sum(x):
    return jnp.sum(_fold_rows(x, jnp.add), axis=0, keepdims=True)


def _online_update(ml, acc_ref, s, v_t):
    m, l = ml
    m_new = jnp.maximum(m, _key_max(s))
    alpha = jnp.exp(m - m_new)
    p = jnp.exp(s - m_new)
    acc_ref[...] = alpha * acc_ref[...] + _dot(v_t, p.astype(MXU_DTYPE))
    return m_new, alpha * l + _key_sum(p)


def _online_init(tq, acc_ref):
    acc_ref[...] = jnp.zeros(acc_ref.shape, F32)
    return tuple((jnp.full((1, tq), NEG, F32), jnp.zeros((1, tq), F32)) for _ in range(acc_ref.shape[0]))


def _nsa_kernel(q_ref, gc_ref, gs_ref, gw_ref, kc_ref, vct_ref, ks_ref, vst_ref, kw_ref, vwt_ref,
                ovt_ref, oh_ref, o_ref, qaug_ref, acc_ref, *, tq, tks):
    g = pl.program_id(1)
    i = pl.program_id(2)
    q0 = i * tq
    ncp = kc_ref.shape[1]
    qa = _group_queries(q_ref, g, tq)

    c_vis = (_row_iota((ncp, tq)) * NSA_CMP_STRIDE + (NSA_CMP_LEN - 1)) <= q0 + _lane_iota((ncp, tq))
    kc = kc_ref[0]
    vct = vct_ref[0]
    pc_sum = jnp.zeros((ncp, tq), F32)
    o_cmp = []
    scores = [_dot_nt(kc, qa[j]) for j in range(NSA_HPG)]
    for j in range(NSA_HPG):
        s = jnp.where(c_vis, scores[j], NEG)
        p = jnp.where(c_vis, jnp.exp(s - _key_max(s)), 0.0)
        den = _key_sum(p)
        p = p * (1.0 / jnp.where(den > 0.0, den, 1.0))
        pc_sum = pc_sum + p
        o_cmp.append(_dot(vct, p.astype(MXU_DTYPE)))

    hi = pc_sum.astype(MXU_DTYPE)
    lo = (pc_sum - hi.astype(F32)).astype(MXU_DTYPE)
    imp = _dot(ovt_ref[...], hi) + _dot(ovt_ref[...], lo)
    blk = _row_iota((LANES, tq))
    t_b = q0 + _lane_iota((LANES, tq))
    cur = t_b >> 6
    visible = blk * NSA_SEL_LEN <= t_b
    forced = (blk == 0) | (blk == cur) | (blk == cur - 1)
    work = jnp.where(visible, jnp.where(forced, NSA_FORCED_SCORE, imp), NEG)
    blk_f = blk.astype(F32)
    sel = jnp.zeros((LANES, tq), F32)
    for _ in range(NSA_SEL_N):
        m = jnp.max(work, axis=0, keepdims=True)
        first = jnp.min(jnp.where(work == m, blk_f, float(LANES)), axis=0, keepdims=True)
        pick = blk_f == first
        sel = jnp.where(pick, 1.0, sel)
        work = jnp.where(pick, -3.0e38, work)
    sel_bias = jnp.where(visible, (sel - 1.0) * (-NEG), NEG).T.astype(MXU_DTYPE)

    n_tiles = (q0 + tq + tks - 1) // tks
    t_s = q0 + _lane_iota((tks, tq))
    row_s = _row_iota((tks, tq))
    for j in range(NSA_HPG):
        qaug_ref[j] = jnp.concatenate([qa[j], sel_bias], axis=1)

    def sel_tile(kt, ml, diagonal):
        k_aug = jnp.concatenate([ks_ref[0, pl.ds(kt * tks, tks), :], oh_ref[pl.ds(kt * tks, tks), :]], axis=1)
        v_t = vst_ref[0, kt]
        scores = [_dot_nt(k_aug, qaug_ref[j]) for j in range(NSA_HPG)]
        if diagonal:
            causal = kt * tks + row_s <= t_s
            scores = [jnp.where(causal, s, NEG) for s in scores]
        return tuple(_online_update(ml[j], acc_ref.at[j], scores[j], v_t) for j in range(NSA_HPG))

    ml = lax.fori_loop(0, n_tiles - 1, functools.partial(sel_tile, diagonal=False), _online_init(tq, acc_ref))
    ml = sel_tile(n_tiles - 1, ml, True)
    o_sel = [acc_ref[j] * (1.0 / ml[j][1]) for j in range(NSA_HPG)]

    band = tq + NSA_WINDOW
    start = pl.multiple_of(jnp.maximum(q0 - NSA_WINDOW, 0), tq)
    kw = kw_ref[0, pl.ds(start, band), :]
    s_w = start + _row_iota((band, tq))
    t_w = q0 + _lane_iota((band, tq))
    w_ok = (s_w <= t_w) & (s_w > t_w - NSA_WINDOW)
    o_win = []
    scores = [_dot_nt(kw, qa[j]) for j in range(NSA_HPG)]
    for j in range(NSA_HPG):
        s = jnp.where(w_ok, scores[j], NEG)
        p = jnp.exp(s - _key_max(s))
        den = _key_sum(p)
        p = p.astype(MXU_DTYPE)
        acc = jnp.zeros((LANES, tq), F32)
        for c in range(band // tq):
            acc = acc + _dot(vwt_ref[0, start // tq + c], p[c * tq:(c + 1) * tq, :])
        o_win.append(acc * (1.0 / den))

    for s in range(2):
        tot = jnp.zeros((tq, LANES), F32)
        for branch, gate_ref in ((o_cmp, gc_ref), (o_sel, gs_ref), (o_win, gw_ref)):
            pair = [jnp.where(g == 0, branch[h][:HEAD_DIM], branch[h][HEAD_DIM:]) for h in (2 * s, 2 * s + 1)]
            tot = tot + gate_ref[0, :, s * LANES:(s + 1) * LANES] * jnp.concatenate(pair, axis=0).T
        o_ref[0, :, s * LANES:(s + 1) * LANES] = tot.astype(o_ref.dtype)


def _nsa_attention(nq, ng, kc, vct, ks, vst, kw, vwt, overlap_t, onehot, *, tq, tks):
    b, s, _ = nq.shape
    ncp = kc.shape[1]
    gw = NSA_HPG * HEAD_DIM
    per_b3 = lambda bi, g, i: (bi, 0, 0)
    per_b4 = lambda bi, g, i: (bi, 0, 0, 0)
    const = lambda bi, g, i: (0, 0)
    return pl.pallas_call(
        functools.partial(_nsa_kernel, tq=tq, tks=tks),
        grid=(b, NSA_GROUPS, s // tq),
        in_specs=[pl.BlockSpec((1, tq, gw), lambda bi, g, i: (bi, i, g)),
                  pl.BlockSpec((1, tq, gw), lambda bi, g, i: (bi, i, g)),
                  pl.BlockSpec((1, tq, gw), lambda bi, g, i: (bi, i, NSA_GROUPS + g)),
                  pl.BlockSpec((1, tq, gw), lambda bi, g, i: (bi, i, 2 * NSA_GROUPS + g)),
                  pl.BlockSpec((1, ncp, LANES), per_b3),
                  pl.BlockSpec((1, LANES, ncp), per_b3),
                  pl.BlockSpec((1, s, LANES), per_b3),
                  pl.BlockSpec((1,) + vst.shape[1:], per_b4),
                  pl.BlockSpec((1, s, LANES), per_b3),
                  pl.BlockSpec((1,) + vwt.shape[1:], per_b4),
                  pl.BlockSpec(overlap_t.shape, const),
                  pl.BlockSpec(onehot.shape, const)],
        out_specs=pl.BlockSpec((1, tq, gw), lambda bi, g, i: (bi, i, g)),
        out_shape=jax.ShapeDtypeStruct((b, s, NSA_QW), MXU_DTYPE),
        scratch_shapes=[pltpu.VMEM((NSA_HPG, tq, 2 * LANES), MXU_DTYPE),
                        pltpu.VMEM((NSA_HPG, LANES, tq), F32)],
        compiler_params=_params(3),
        name="nsa_attn",
    )(nq, ng, ng, ng, kc, vct, ks, vst, kw, vwt, overlap_t, onehot)


def _dsa_kernel(iq_ref, iwt_ref, ik_ref, q_ref, kv_ref, kvt_ref, o_ref, keys_ref, bias_ref, qa_ref, acc_ref,
                *, tq, tk, top_k):
    i = pl.program_id(1)
    q0 = i * tq
    n_tiles = (q0 + tq + tk - 1) // tk
    lane = _lane_iota((tq, LANES))
    t_q = q0 + _lane_iota((tk, tq))
    row_k = _row_iota((tk, tq))

    iqa = []
    for s in range(IDX_QW // LANES):
        qs = iq_ref[0, :, s * LANES:(s + 1) * LANES].astype(F32)
        for p in range(LANES // DSA_IDX_DIM):
            mine = jnp.where((lane >> 5) == p, qs, 0.0)
            if p:
                mine = pltpu.roll(mine, LANES - p * DSA_IDX_DIM, 1)
            iqa.append(mine.astype(MXU_DTYPE))
    w_h = [iwt_ref[0, h:h + 1, :] for h in range(DSA_IDX_HEADS)]

    def score_body(kt, _):
        ik = ik_ref[0, pl.ds(kt * tk, tk), :]
        score = jnp.zeros((tk, tq), F32)
        for h in range(DSA_IDX_HEADS):
            score = score + w_h[h] * jnp.maximum(_dot_nt(ik, iqa[h]), 0.0)
        score = jnp.where(score == 0.0, 0.0, score)
        bits = pltpu.bitcast(score, jnp.int32)
        key = jnp.where(bits < 0, bits ^ 0x7FFFFFFF, bits)
        keys_ref[kt] = jnp.where(kt * tk + row_k <= t_q, key, INT_MIN + 1)
        return 0

    lax.fori_loop(0, n_tiles, score_body, 0)

    def count(pred):
        def body(kt, acc):
            return acc + _fold_rows(jnp.where(pred(keys_ref[kt]), 1.0, 0.0), jnp.add)
        acc = lax.fori_loop(0, n_tiles, body, jnp.zeros((64, tq), F32))
        return jnp.sum(acc, axis=0, keepdims=True)

    def thr_body(b, ans_u):
        cand_u = ans_u | lax.shift_left(jnp.int32(1), 31 - b)
        cand_s = cand_u ^ INT_MIN
        cnt = count(lambda k: k >= cand_s)
        return jnp.where(cnt >= top_k, cand_u, ans_u)

    ans_u = lax.fori_loop(0, 32, thr_body, jnp.zeros((1, tq), jnp.int32))
    thr = ans_u ^ INT_MIN
    need = top_k - count(lambda k: k > thr)

    lower = jnp.where(_lane_iota((tk, tk)) < _row_iota((tk, tk)), 1.0, 0.0).astype(MXU_DTYPE)

    def bias_body(kt, seen):
        key = keys_ref[kt]
        tie = jnp.where(key == thr, 1.0, 0.0)
        rank = _dot(lower, tie.astype(MXU_DTYPE)) + seen
        take = jnp.where(key > thr, 1.0, jnp.where(rank < need, tie, 0.0))
        take = jnp.where(kt * tk + row_k <= t_q, take, 0.0)
        bias_ref[kt] = ((take - 1.0) * (-NEG)).astype(bias_ref.dtype)
        return seen + _key_sum(tie)

    lax.fori_loop(0, n_tiles, bias_body, jnp.zeros((1, tq), F32))

    for s in range(DSA_QW // LANES):
        qs = q_ref[0, :, s * LANES:(s + 1) * LANES].astype(F32)
        lo = jnp.where(lane < HEAD_DIM, qs, 0.0)
        hi = _swap_halves(jnp.where(lane < HEAD_DIM, 0.0, qs))
        qa_ref[2 * s] = lo.astype(MXU_DTYPE)
        qa_ref[2 * s + 1] = hi.astype(MXU_DTYPE)

    def att_body(kt, ml):
        kv = kv_ref[0, pl.ds(kt * tk, tk), :]
        kv_t = kvt_ref[0, kt]
        bias = bias_ref[kt].astype(F32)
        scores = [_dot_nt(kv, qa_ref[h]) + bias for h in range(DSA_HEADS)]
        return tuple(_online_update(ml[h], acc_ref.at[h], scores[h], kv_t) for h in range(DSA_HEADS))

    ml = lax.fori_loop(0, n_tiles, att_body, _online_init(tq, acc_ref))
    outs = [acc_ref[h, HEAD_DIM:, :] * (1.0 / ml[h][1]) for h in range(DSA_HEADS)]

    for s in range(DSA_QW // LANES):
        pair = jnp.concatenate([outs[2 * s], outs[2 * s + 1]], axis=0)
        o_ref[0, :, s * LANES:(s + 1) * LANES] = pair.T.astype(o_ref.dtype)


def _dsa_attention(iq, iwt, ik, dq, dkv, dkvt, *, tq, tk, top_k):
    b, s, _ = dq.shape
    tile = lambda width: pl.BlockSpec((1, tq, width), lambda bi, i: (bi, i, 0))
    full = pl.BlockSpec((1, s, LANES), lambda bi, i: (bi, 0, 0))
    return pl.pallas_call(
        functools.partial(_dsa_kernel, tq=tq, tk=tk, top_k=top_k),
        grid=(b, s // tq),
        in_specs=[tile(IDX_QW), pl.BlockSpec((1, DSA_IDX_HEADS, tq), lambda bi, i: (bi, 0, i)), full,
                  tile(DSA_QW), full, pl.BlockSpec((1,) + dkvt.shape[1:], lambda bi, i: (bi, 0, 0, 0))],
        out_specs=tile(DSA_QW),
        out_shape=jax.ShapeDtypeStruct((b, s, DSA_QW), MXU_DTYPE),
        scratch_shapes=[pltpu.VMEM((s // tk, tk, tq), jnp.int32), pltpu.VMEM((s // tk, tk, tq), MXU_DTYPE),
                        pltpu.VMEM((DSA_HEADS, tq, LANES), MXU_DTYPE), pltpu.VMEM((DSA_HEADS, LANES, tq), F32)],
        compiler_params=_params(2),
        name="dsa_attn",
    )(iq, iwt, ik, dq, dkv, dkvt)


def _merge_kernel(x_ref, g_ref, oa_ref, ob_ref, oc_ref, wg_ref, wa_ref, wb_ref, wc_ref, wo_ref, o_ref):
    x = x_ref[...]
    d = x.shape[1]
    ms = jnp.mean(x * x, axis=-1, keepdims=True)
    h = (x * lax.rsqrt(ms + NORM_EPS) * g_ref[...]).astype(MXU_DTYPE)
    merged = jnp.zeros(x.shape, F32)
    for r, (o_r, w_r) in enumerate(((oa_ref, wa_ref), (ob_ref, wb_ref), (oc_ref, wc_ref))):
        gate = jax.nn.sigmoid(_dot(h, wg_ref[:, r * d:(r + 1) * d]))
        merged = merged + gate * _dot(o_r[...], w_r[...])
    o_ref[...] = x + _dot(merged.astype(MXU_DTYPE), wo_ref[...])


def _merge(x2, g, oa, ob, oc, w_gate, wa, wb, wc, wo, *, tm):
    n, d = x2.shape
    row = lambda i: (i, 0)
    const = lambda i: (0, 0)
    return pl.pallas_call(
        _merge_kernel,
        grid=(n // tm,),
        in_specs=[pl.BlockSpec((tm, d), row), pl.BlockSpec((1, d), const),
                  pl.BlockSpec((tm, SB_W), row), pl.BlockSpec((tm, NSA_QW), row),
                  pl.BlockSpec((tm, DSA_QW), row),
                  pl.BlockSpec(w_gate.shape, const), pl.BlockSpec(wa.shape, const),
                  pl.BlockSpec(wb.shape, const), pl.BlockSpec(wc.shape, const),
                  pl.BlockSpec(wo.shape, const)],
        out_specs=pl.BlockSpec((tm, d), row),
        out_shape=jax.ShapeDtypeStruct((n, d), F32),
        compiler_params=_params(1),
        name="merge",
    )(x2, g, oa, ob, oc, w_gate, wa, wb, wc, wo)


def _rope_tables(pos, rot_dim, head_dim):
    half = rot_dim // 2
    inv = ROPE_THETA ** (-jnp.arange(0, rot_dim, 2, dtype=F32) / rot_dim)
    ang = pos.astype(F32)[:, None] * inv[None, :]
    cos, sin = jnp.cos(ang), jnp.sin(ang)
    n = pos.shape[0]
    rest = head_dim - rot_dim
    zero_h = jnp.zeros((n, half), F32)
    c = jnp.concatenate([cos, cos, jnp.ones((n, rest), F32)], axis=-1)
    sm = jnp.concatenate([-sin, zero_h, jnp.zeros((n, rest), F32)], axis=-1)
    sp = jnp.concatenate([zero_h, sin, jnp.zeros((n, rest), F32)], axis=-1)
    reps = LANES // head_dim
    return tuple(jnp.tile(t, (1, reps)) for t in (c, sm, sp))


def _pad_cols(w, width):
    return jnp.pad(w, ((0, 0), (0, width - w.shape[1])))


def _relayout_w_in(w_in):
    offs = np.cumsum((SB_W, SB_W, SB_W, NSA_QW) + (NSA_KVW,) * 6
                     + (3 * NSA_HEADS, DSA_QW, HEAD_DIM, HEAD_DIM, IDX_QW, DSA_IDX_DIM, DSA_IDX_HEADS))
    o_gate, o_dq, o_dk, o_dv, o_iq, o_ik, o_iw, o_end = offs[9:17].tolist()
    gate = w_in[:, o_gate:o_dq].reshape(-1, NSA_HEADS, 3)
    gate = jnp.repeat(jnp.transpose(gate, (0, 2, 1)), HEAD_DIM, axis=2)
    parts = [w_in[:, :o_gate], gate.reshape(w_in.shape[0], 3 * NSA_QW), w_in[:, o_dq:o_dk],
             w_in[:, o_dk:o_iq], w_in[:, o_iq:o_ik], _pad_cols(w_in[:, o_ik:o_iw], LANES),
             _pad_cols(w_in[:, o_iw:o_end], LANES)]
    w = jnp.concatenate(parts, axis=1)
    assert w.shape[1] == _P_END
    return w.astype(MXU_DTYPE)


def _compress_weights(pe, w1, w2):
    r = NSA_CMP_LEN // NSA_CMP_STRIDE
    hid = NSA_CMP_HIDDEN
    w1r = w1.reshape(2, r, NSA_CMP_STRIDE, HEAD_DIM, hid)
    eye = jnp.eye(NSA_GROUPS, dtype=w1.dtype)
    w1e = jnp.einsum('kmldj,gh->kmlgdhj', w1r, eye)
    w1e = w1e.reshape(2, r, NSA_CMP_STRIDE * NSA_KVW, NSA_GROUPS * hid)
    w2e = jnp.einsum('kjd,gh->kgjhd', w2, eye).reshape(2, NSA_GROUPS * hid, NSA_KVW)
    pe_e = jnp.broadcast_to(pe.reshape(2, r, NSA_CMP_STRIDE, 1, HEAD_DIM),
                            (2, r, NSA_CMP_STRIDE, NSA_GROUPS, HEAD_DIM))
    return pe_e.reshape(2, r, NSA_CMP_STRIDE * NSA_KVW), w1e.astype(MXU_DTYPE), w2e.astype(MXU_DTYPE)


def _overlap_matrix_t(ncp, seq):
    c = np.arange(ncp)[None, :] * NSA_CMP_STRIDE
    j = np.arange(LANES)[:, None] * NSA_SEL_LEN
    ov = (c < j + NSA_SEL_LEN) & (c + NSA_CMP_LEN - 1 >= j) & (j < seq)
    ov &= (np.arange(ncp)[None, :] < ncp - (NSA_CMP_LEN // NSA_CMP_STRIDE - 1))
    return jnp.asarray(ov, dtype=MXU_DTYPE)


def _block_onehot(seq):
    return jnp.asarray(np.arange(seq)[:, None] // NSA_SEL_LEN == np.arange(LANES)[None, :], dtype=MXU_DTYPE)


def _key_tiles_t(v, tile):
    b, s, w = v.shape
    return jnp.swapaxes(v.reshape(b, s // tile, tile, w), 2, 3)


def kernel(x, ffn1_norm, ffn1_w_gu, ffn1_w_down, mix_norm, w_in, w_gate, nsa_q_norm, nsa_k_norm,
           nsa_cmp_pe, nsa_cmp_w1, nsa_cmp_w2, dsa_q_norm, dsa_k_norm, w_br_a, w_br_b, w_br_c, w_out,
           ffn2_norm, ffn2_w_gu, ffn2_w_down):
    b, s, d = x.shape
    depth = w_in.shape[0]
    n = b * s
    ncp = s // NSA_CMP_STRIDE
    assert s // NSA_SEL_LEN <= LANES and s % 1024 == 0
    top_k = min(DSA_TOPK, s // 4)
    cast = lambda w: w.astype(MXU_DTYPE)
    tile2 = lambda v, reps: jnp.tile(v, reps)[None, :]

    pos = jnp.arange(s)
    rope_main = _rope_tables(pos, ROT_DIM, HEAD_DIM)
    rope_idx = _rope_tables(pos, DSA_IDX_ROT, DSA_IDX_DIM)
    rope_cmp = _rope_tables(jnp.arange(ncp) * NSA_CMP_STRIDE + NSA_CMP_LEN - 1, ROT_DIM, HEAD_DIM)
    overlap_t = _overlap_matrix_t(ncp, s)
    onehot = _block_onehot(s)

    x2 = x.reshape(n, d)
    for l in range(depth):
        x2 = _ffn(x2, ffn1_norm[l][None, :], cast(ffn1_w_gu[l]), cast(ffn1_w_down[l]), tm=1024, tf=256)

        (sb, nq, nkc, nvc, nkv, ng, dq, dkv, iq, ik, iw) = _proj(
            x2, mix_norm[l][None, :], _relayout_w_in(w_in[l]),
            tile2(nsa_q_norm[l], NSA_HEADS),
            jnp.stack([jnp.tile(nsa_k_norm[l, 1], NSA_GROUPS), jnp.tile(nsa_k_norm[l, 2], NSA_GROUPS)]),
            tile2(dsa_q_norm[l], DSA_HEADS), tile2(dsa_k_norm[l], 2),
            rope_main, rope_idx, tm=256, seq=s)
        r3 = lambda t: t.reshape(b, s, t.shape[-1])

        pe_e, w1e, w2e = _compress_weights(nsa_cmp_pe[l], nsa_cmp_w1[l], nsa_cmp_w2[l])
        kc, vc = _compress(nkc.reshape(b, ncp, NSA_CMP_STRIDE * NSA_KVW),
                           nvc.reshape(b, ncp, NSA_CMP_STRIDE * NSA_KVW),
                           pe_e, w1e, w2e, tile2(nsa_k_norm[l, 0], NSA_GROUPS), rope_cmp)

        o_a = _sb_attention(r3(sb), tq=256)

        nkv = r3(nkv)
        ks, vs, kw, vw = (nkv[:, :, c * LANES:(c + 1) * LANES] for c in range(4))
        o_b = _nsa_attention(r3(nq), r3(ng), kc, jnp.swapaxes(vc, 1, 2), ks, _key_tiles_t(vs, TKS),
                             kw, _key_tiles_t(vw, TQ), overlap_t, onehot, tq=TQ, tks=TKS)

        dkv = r3(dkv)
        iwt = jnp.swapaxes(r3(iw)[:, :, :DSA_IDX_HEADS], 1, 2)
        o_c = _dsa_attention(r3(iq), iwt, r3(ik), r3(dq), dkv, _key_tiles_t(dkv, TKS),
                             tq=TQ, tk=TKS, top_k=top_k)

        x2 = _merge(x2, mix_norm[l][None, :], o_a.reshape(n, SB_W), o_b.reshape(n, NSA_QW),
                    o_c.reshape(n, DSA_QW), cast(w_gate[l]), cast(w_br_a[l]), cast(w_br_b[l]),
                    cast(w_br_c[l]), cast(w_out[l]), tm=256)

        x2 = _ffn(x2, ffn2_norm[l][None, :], cast(ffn2_w_gu[l]), cast(ffn2_w_down[l]), tm=1024, tf=256)
    return x2.reshape(b, s, d)
```

```python
import functools

import numpy as np
import jax
import jax.numpy as jnp
from jax import lax
from jax.experimental import pallas as pl
from jax.experimental.pallas import tpu as pltpu

HEAD_DIM = 64
ROT_DIM = HEAD_DIM // 4
ROPE_THETA = 500000.0
NORM_EPS = 1e-6

SB_HEADS = 4
NSA_HEADS = 8
NSA_GROUPS = 2
NSA_HPG = NSA_HEADS // NSA_GROUPS
NSA_CMP_LEN = 32
NSA_CMP_STRIDE = 16
NSA_CMP_HIDDEN = 2 * HEAD_DIM
NSA_SEL_LEN = 64
NSA_SEL_N = 16
NSA_WINDOW = 512
NSA_FORCED_SCORE = 1.0e4
DSA_HEADS = 4
DSA_IDX_HEADS = 8
DSA_IDX_DIM = 32
DSA_IDX_ROT = DSA_IDX_DIM // 4
DSA_TOPK = 256

SB_W = SB_HEADS * HEAD_DIM
NSA_QW = NSA_HEADS * HEAD_DIM
NSA_KVW = NSA_GROUPS * HEAD_DIM
DSA_QW = DSA_HEADS * HEAD_DIM
IDX_QW = DSA_IDX_HEADS * DSA_IDX_DIM

LANES = 128
VMEM_LIMIT = 56 * 1024 * 1024
MXU_DTYPE = jnp.bfloat16
NEG = -1.0e30
SB_SKIP = -120.0
INT_MIN = -2 ** 31
LOG2E = 1.4426950408889634

F32 = jnp.float32

TQ = 256
TKS = 512
assert TKS % TQ == 0 and NSA_WINDOW % TQ == 0


def _dot(a, b):
    return jnp.dot(a, b, preferred_element_type=F32)


def _dot_nt(a, b):
    return lax.dot_general(a, b, (((1,), (1,)), ((), ())), preferred_element_type=F32)


def _params(n_axes):
    return pltpu.CompilerParams(dimension_semantics=("arbitrary",) * n_axes,
                                vmem_limit_bytes=VMEM_LIMIT)


def _lane_iota(shape):
    return lax.broadcasted_iota(jnp.int32, shape, len(shape) - 1)


def _row_iota(shape):
    return lax.broadcasted_iota(jnp.int32, shape, len(shape) - 2)


def _swap_halves(x):
    return pltpu.roll(x, LANES // 2, 1)


def _head_rms(ys):
    lo = _lane_iota(ys.shape) < HEAD_DIM
    sq = ys * ys
    s_lo = jnp.sum(jnp.where(lo, sq, 0.0), axis=-1, keepdims=True)
    s_hi = jnp.sum(jnp.where(lo, 0.0, sq), axis=-1, keepdims=True)
    ms = jnp.where(lo, s_lo, s_hi) * (1.0 / HEAD_DIM)
    return ys * lax.rsqrt(ms + NORM_EPS)


def _rope(ys, c, sm, sp, half):
    return ys * c + pltpu.roll(ys, LANES - half, 1) * sm + pltpu.roll(ys, half, 1) * sp


def _ffn_kernel(x_ref, g_ref, wg_ref, wu_ref, wd_ref, o_ref, h_ref, acc_ref):
    f = pl.program_id(1)

    @pl.when(f == 0)
    def _():
        x = x_ref[...]
        ms = jnp.mean(x * x, axis=-1, keepdims=True)
        h_ref[...] = (x * lax.rsqrt(ms + NORM_EPS) * g_ref[...]).astype(h_ref.dtype)
        acc_ref[...] = jnp.zeros_like(acc_ref)

    h = h_ref[...]
    gate = _dot(h, wg_ref[...])
    up = _dot(h, wu_ref[...])
    act = (gate * jax.nn.sigmoid(gate) * up).astype(h_ref.dtype)
    acc_ref[...] += _dot(act, wd_ref[...])

    @pl.when(f == pl.num_programs(1) - 1)
    def _():
        o_ref[...] = x_ref[...] + 0.5 * acc_ref[...]


def _ffn(x2, g, w_gu, w_down, *, tm, tf):
    n, d = x2.shape
    d_ff = w_down.shape[0]
    nf = d_ff // tf
    return pl.pallas_call(
        _ffn_kernel,
        grid=(n // tm, nf),
        in_specs=[
            pl.BlockSpec((tm, d), lambda i, f: (i, 0)),
            pl.BlockSpec((1, d), lambda i, f: (0, 0)),
            pl.BlockSpec((d, tf), lambda i, f: (0, f)),
            pl.BlockSpec((d, tf), lambda i, f: (0, f + nf)),
            pl.BlockSpec((tf, d), lambda i, f: (f, 0)),
        ],
        out_specs=pl.BlockSpec((tm, d), lambda i, f: (i, 0)),
        out_shape=jax.ShapeDtypeStruct((n, d), F32),
        scratch_shapes=[pltpu.VMEM((tm, d), MXU_DTYPE), pltpu.VMEM((tm, d), F32)],
        compiler_params=_params(2),
        name="ffn",
    )(x2, g, w_gu, w_gu, w_down)


_P_SB = 0
_P_NQ = _P_SB + 3 * SB_W
_P_NCV = _P_NQ + NSA_QW
_P_NKV = _P_NCV + 2 * NSA_KVW
_P_NG = _P_NKV + 4 * NSA_KVW
_P_DQ = _P_NG + 3 * NSA_QW
_P_DKV = _P_DQ + DSA_QW
_P_IQ = _P_DKV + LANES
_P_IK = _P_IQ + IDX_QW
_P_IW = _P_IK + LANES
_P_END = _P_IW + LANES


def _proj_kernel(x_ref, g_ref, w_ref, nq_g_ref, nk_g_ref, dq_g_ref, dk_g_ref,
                 rc_ref, rm_ref, rp_ref, ic_ref, im_ref, ip_ref,
                 sb_ref, nq_ref, nkc_ref, nvc_ref, nkv_ref, ng_ref, dq_ref, dkv_ref,
                 iq_ref, ik_ref, iw_ref):
    x = x_ref[...]
    ms = jnp.mean(x * x, axis=-1, keepdims=True)
    h = (x * lax.rsqrt(ms + NORM_EPS) * g_ref[...]).astype(MXU_DTYPE)
    rc, rm, rp = rc_ref[...], rm_ref[...], rp_ref[...]
    ic, im, ip = ic_ref[...], im_ref[...], ip_ref[...]
    half, ihalf = ROT_DIM // 2, DSA_IDX_ROT // 2
    scale = HEAD_DIM ** -0.5
    scale2 = scale * LOG2E

    def cols(a, width):
        return _dot(h, w_ref[:, a:a + width])

    def slab(y, s):
        return y[:, s * LANES:(s + 1) * LANES]

    y = cols(_P_SB, 3 * SB_W)
    sb_ref[:, 0:SB_W] = (y[:, 0:SB_W] * scale).astype(sb_ref.dtype)
    sb_ref[:, SB_W:3 * SB_W] = y[:, SB_W:3 * SB_W].astype(sb_ref.dtype)

    y = cols(_P_NQ, NSA_QW)
    for s in range(NSA_QW // LANES):
        ys = _head_rms(slab(y, s)) * slab(nq_g_ref[...], s)
        nq_ref[:, s * LANES:(s + 1) * LANES] = (_rope(ys, rc, rm, rp, half) * scale2).astype(nq_ref.dtype)

    y = cols(_P_NCV, 2 * NSA_KVW)
    nkc_ref[...] = slab(y, 0)
    nvc_ref[...] = slab(y, 1)

    y = cols(_P_NKV, 4 * NSA_KVW)
    for s, gi in ((0, 0), (2, 1)):
        ys = _head_rms(slab(y, s)) * nk_g_ref[gi:gi + 1, :]
        nkv_ref[:, s * LANES:(s + 1) * LANES] = _rope(ys, rc, rm, rp, half).astype(nkv_ref.dtype)
    for s in (1, 3):
        nkv_ref[:, s * LANES:(s + 1) * LANES] = slab(y, s).astype(nkv_ref.dtype)

    ng_ref[...] = jax.nn.sigmoid(cols(_P_NG, 3 * NSA_QW))

    y = cols(_P_DQ, DSA_QW)
    for s in range(DSA_QW // LANES):
        ys = _head_rms(slab(y, s)) * slab(dq_g_ref[...], s)
        dq_ref[:, s * LANES:(s + 1) * LANES] = (_rope(ys, rc, rm, rp, half) * scale2).astype(dq_ref.dtype)

    y = cols(_P_DKV, LANES)
    yk = _rope(_head_rms(y) * dk_g_ref[...], rc, rm, rp, half)
    dkv_ref[...] = jnp.where(_lane_iota(y.shape) < HEAD_DIM, yk, y).astype(dkv_ref.dtype)

    y = cols(_P_IQ, IDX_QW)
    for s in range(IDX_QW // LANES):
        iq_ref[:, s * LANES:(s + 1) * LANES] = _rope(slab(y, s), ic, im, ip, ihalf).astype(iq_ref.dtype)
    ik_ref[...] = _rope(cols(_P_IK, LANES), ic, im, ip, ihalf).astype(ik_ref.dtype)
    iw_ref[...] = cols(_P_IW, LANES) * (DSA_IDX_HEADS ** -0.5) * (DSA_IDX_DIM ** -0.5)


def _proj(x2, g, w, nq_g, nk_g, dq_g, dk_g, rope_main, rope_idx, *, tm, seq):
    n, d = x2.shape
    nt = seq // tm
    row = lambda i: (i, 0)
    const = lambda i: (0, 0)
    pos = lambda i: (i % nt, 0)
    widths = [(3 * SB_W, MXU_DTYPE), (NSA_QW, MXU_DTYPE), (NSA_KVW, F32), (NSA_KVW, F32),
              (4 * NSA_KVW, MXU_DTYPE), (3 * NSA_QW, F32), (DSA_QW, MXU_DTYPE), (LANES, MXU_DTYPE),
              (IDX_QW, MXU_DTYPE), (LANES, MXU_DTYPE), (LANES, F32)]
    return pl.pallas_call(
        _proj_kernel,
        grid=(n // tm,),
        in_specs=[pl.BlockSpec((tm, d), row), pl.BlockSpec((1, d), const),
                  pl.BlockSpec((d, _P_END), const),
                  pl.BlockSpec((1, NSA_QW), const), pl.BlockSpec((2, LANES), const),
                  pl.BlockSpec((1, DSA_QW), const), pl.BlockSpec((1, LANES), const)]
                 + [pl.BlockSpec((tm, LANES), pos)] * 6,
        out_specs=[pl.BlockSpec((tm, wd), row) for wd, _ in widths],
        out_shape=[jax.ShapeDtypeStruct((n, wd), dt) for wd, dt in widths],
        compiler_params=_params(1),
        name="proj",
    )(x2, g, w, nq_g, nk_g, dq_g, dk_g, *rope_main, *rope_idx)


def _compress_kernel(xk_ref, xv_ref, pe_ref, w1_ref, w2_ref, g_ref, rc_ref, rm_ref, rp_ref,
                     kc_ref, vc_ref):
    ncp = xk_ref.shape[1]

    def compress(x, kv):
        a0 = _dot((x + pe_ref[kv, 0:1, :]).astype(MXU_DTYPE), w1_ref[kv, 0])
        a1 = _dot((x + pe_ref[kv, 1:2, :]).astype(MXU_DTYPE), w1_ref[kv, 1])
        pre = a0 + pltpu.roll(a1, ncp - 1, 0)
        hid = pre * jax.nn.sigmoid(pre)
        return _dot(hid.astype(MXU_DTYPE), w2_ref[kv])

    yk = _head_rms(compress(xk_ref[0], 0)) * g_ref[...]
    kc_ref[0] = _rope(yk, rc_ref[...], rm_ref[...], rp_ref[...], ROT_DIM // 2).astype(kc_ref.dtype)
    vc_ref[0] = compress(xv_ref[0], 1).astype(vc_ref.dtype)


def _compress(xk, xv, pe, w1, w2, g, rope_c):
    b, ncp, wide = xk.shape
    c3 = lambda i: (0, 0, 0)
    c4 = lambda i: (0, 0, 0, 0)
    c2 = lambda i: (0, 0)
    per_b = lambda i: (i, 0, 0)
    return pl.pallas_call(
        _compress_kernel,
        grid=(b,),
        in_specs=[pl.BlockSpec((1, ncp, wide), per_b), pl.BlockSpec((1, ncp, wide), per_b),
                  pl.BlockSpec(pe.shape, c3), pl.BlockSpec(w1.shape, c4), pl.BlockSpec(w2.shape, c3),
                  pl.BlockSpec((1, LANES), c2)] + [pl.BlockSpec((ncp, LANES), c2)] * 3,
        out_specs=[pl.BlockSpec((1, ncp, LANES), per_b)] * 2,
        out_shape=[jax.ShapeDtypeStruct((b, ncp, LANES), MXU_DTYPE)] * 2,
        compiler_params=_params(1),
        name="compress",
    )(xk, xv, pe, w1, w2, g, *rope_c)


def _sb_kernel(q_ref, k_ref, v_ref, o_ref, *, tq):
    i = pl.program_id(2)
    q0 = i * tq
    q = q_ref[0].astype(F32)
    lane = _lane_iota(q.shape)
    rows = _row_iota((tq, tq))
    colsq = _lane_iota((tq, tq))
    upper = jnp.where(rows > colsq, 1.0, 0.0).astype(MXU_DTYPE)
    out = jnp.zeros((tq, LANES), F32)

    for p in range(2):
        in_head = (lane >= p * HEAD_DIM) & (lane < (p + 1) * HEAD_DIM)
        qh = jnp.where(in_head, q, 0.0).astype(MXU_DTYPE)

        def tile(kt, carry, acc, diagonal):
            ks = k_ref[0, pl.ds(kt * tq, tq), :]
            vs = v_ref[0, pl.ds(kt * tq, tq), :]
            z = _dot_nt(qh, ks)
            l = -(jnp.maximum(z, 0.0) + jnp.log1p(jnp.exp(-jnp.abs(z))))
            if diagonal:
                past = colsq < rows
                l = jnp.where(past, l, 0.0)
            hi = l.astype(MXU_DTYPE)
            lo = (l - hi.astype(F32)).astype(MXU_DTYPE)
            tail = _dot(hi, upper) + _dot(lo, upper) + carry
            w = jnp.exp(z + l + tail)
            if diagonal:
                w = jnp.where(past, w, 0.0)
            acc = acc + _dot(w.astype(MXU_DTYPE), vs)
            carry = carry + jnp.sum(l, axis=-1, keepdims=True)
            return carry, acc

        carry, acc = tile(i, jnp.zeros((tq, 1), F32), jnp.zeros((tq, LANES), F32), True)

        def cond(st):
            kt, carry, _ = st
            return (kt >= 0) & (jnp.max(carry) > SB_SKIP)

        def body(st):
            kt, carry, acc = st
            carry, acc = tile(kt, carry, acc, False)
            return kt - 1, carry, acc

        _, _, acc = lax.while_loop(cond, body, (i - 1, carry, acc))
        out = jnp.where(in_head, acc, out)

    del q0
    o_ref[0] = out.astype(o_ref.dtype)


def _sb_attention(sb, *, tq):
    b, s, _ = sb.shape
    nslab = SB_W // LANES
    return pl.pallas_call(
        functools.partial(_sb_kernel, tq=tq),
        grid=(b, nslab, s // tq),
        in_specs=[pl.BlockSpec((1, tq, LANES), lambda bi, sl, i: (bi, i, sl)),
                  pl.BlockSpec((1, s, LANES), lambda bi, sl, i: (bi, 0, nslab + sl)),
                  pl.BlockSpec((1, s, LANES), lambda bi, sl, i: (bi, 0, 2 * nslab + sl))],
        out_specs=pl.BlockSpec((1, tq, LANES), lambda bi, sl, i: (bi, i, sl)),
        out_shape=jax.ShapeDtypeStruct((b, s, SB_W), MXU_DTYPE),
        compiler_params=_params(3),
        name="sb_attn",
    )(sb, sb, sb)


def _group_queries(q_ref, g, tq):
    lane = _lane_iota((tq, LANES))
    in_g = (lane >> 6) == g
    qa = []
    for s in range(2):
        qs = q_ref[0, :, s * LANES:(s + 1) * LANES].astype(F32)
        for p in range(2):
            mine = jnp.where((lane >> 6) == p, qs, 0.0)
            both = mine + _swap_halves(mine)
            qa.append(jnp.where(in_g, both, 0.0).astype(MXU_DTYPE))
    return qa


def _fold_rows(x, op, rows=64):
    while x.shape[0] > rows and x.shape[0] % 16 == 0:
        half = x.shape[0] // 2
        x = op(x[:half], x[half:])
    return x


def _key_max(x):
    return jnp.max(_fold_rows(x, jnp.maximum), axis=0, keepdims=True)


def _key_sum(x):
    return jnp.sum(_fold_rows(x, jnp.add), axis=0, keepdims=True)


def _online_update(m, acc_ref, s, v_t):
    m_new = jnp.maximum(m, _key_max(s))
    p = jnp.exp2(s - m_new)
    acc_ref[...] = jnp.exp2(m - m_new) * acc_ref[...] + _dot(v_t, p.astype(MXU_DTYPE))
    return m_new


def _attend_tiles(n_tiles, scores_fn, softmax_fn, ms, fix_last):
    def step(tiles, ms, last):
        scores = [scores_fn(kt) for kt in tiles]
        if last:
            scores[-1] = fix_last(scores[-1])
        for kt, sc in zip(tiles, scores):
            ms = softmax_fn(kt, ms, sc)
        return ms

    plain = n_tiles - 1
    ms = lax.fori_loop(0, plain // 2, lambda p, ms: step([2 * p, 2 * p + 1], ms, False), ms)
    return lax.cond(plain % 2 == 1,
                    lambda ms: step([n_tiles - 2, n_tiles - 1], ms, True),
                    lambda ms: step([n_tiles - 1], ms, True), ms)


def _online_init(tq, acc_ref):
    acc_ref[...] = jnp.zeros(acc_ref.shape, F32)
    return tuple(jnp.full((1, tq), NEG, F32) for _ in range(acc_ref.shape[0]))


def _nsa_kernel(q_ref, gc_ref, gs_ref, gw_ref, kc_ref, vct_ref, ks_ref, vst_ref, kw_ref, vwt_ref,
                ovt_ref, oh_ref, o_ref, qaug_ref, acc_ref, *, tq, tks):
    g = pl.program_id(1)
    i = pl.program_id(2)
    q0 = i * tq
    ncp = kc_ref.shape[1]
    qa = _group_queries(q_ref, g, tq)

    c_vis = (_row_iota((ncp, tq)) * NSA_CMP_STRIDE + (NSA_CMP_LEN - 1)) <= q0 + _lane_iota((ncp, tq))
    kc = kc_ref[0]
    vct = vct_ref[0, 0]
    pc_sum = jnp.zeros((ncp, tq), F32)
    o_cmp = []
    scores = [_dot_nt(kc, qa[j]) for j in range(NSA_HPG)]
    for j in range(NSA_HPG):
        s = jnp.where(c_vis, scores[j], NEG)
        p = jnp.where(c_vis, jnp.exp2(s - _key_max(s)), 0.0)
        den = _key_sum(p)
        p = p * (1.0 / jnp.where(den > 0.0, den, 1.0))
        pc_sum = pc_sum + p
        o_cmp.append(_dot(vct, p.astype(MXU_DTYPE))[:HEAD_DIM])

    hi = pc_sum.astype(MXU_DTYPE)
    lo = (pc_sum - hi.astype(F32)).astype(MXU_DTYPE)
    imp = _dot(ovt_ref[...], hi) + _dot(ovt_ref[...], lo)
    blk = _row_iota((LANES, tq))
    t_b = q0 + _lane_iota((LANES, tq))
    cur = t_b >> 6
    visible = blk * NSA_SEL_LEN <= t_b
    forced = (blk == 0) | (blk == cur) | (blk == cur - 1)
    work = jnp.where(visible, jnp.where(forced, -3.0e38, imp), NEG)
    blk_f = blk.astype(F32)
    sel = jnp.where(forced, 1.0, 0.0)
    for _ in range(NSA_SEL_N - 3):
        m = jnp.max(work, axis=0, keepdims=True)
        first = jnp.min(jnp.where(work == m, blk_f, float(LANES)), axis=0, keepdims=True)
        pick = blk_f == first
        sel = jnp.where(pick, 1.0, sel)
        work = jnp.where(pick, -3.0e38, work)
    sel_bias = jnp.where(visible, (sel - 1.0) * (-NEG), NEG).T.astype(MXU_DTYPE)

    n_tiles = (q0 + tq + tks - 1) // tks
    t_s = q0 + _lane_iota((tks, tq))
    row_s = _row_iota((tks, tq))
    for j in range(NSA_HPG):
        qaug_ref[j] = jnp.concatenate([qa[j], sel_bias], axis=1)

    def sel_scores(kt):
        k_aug = jnp.concatenate([ks_ref[0, pl.ds(kt * tks, tks), :], oh_ref[pl.ds(kt * tks, tks), :]], axis=1)
        return tuple(_dot_nt(k_aug, qaug_ref[j]) for j in range(NSA_HPG))

    def sel_softmax(kt, ms, scores):
        v_t = vst_ref[0, 0, kt]
        return tuple(_online_update(ms[j], acc_ref.at[j], scores[j], v_t) for j in range(NSA_HPG))

    def sel_causal(scores):
        causal = (n_tiles - 1) * tks + row_s <= t_s
        return tuple(jnp.where(causal, s, NEG) for s in scores)

    _attend_tiles(n_tiles, sel_scores, sel_softmax, _online_init(tq, acc_ref), sel_causal)
    o_sel = [acc_ref[j, :HEAD_DIM, :] * (1.0 / acc_ref[j, HEAD_DIM:HEAD_DIM + 1, :]) for j in range(NSA_HPG)]

    band = tq + NSA_WINDOW
    start = pl.multiple_of(jnp.maximum(q0 - NSA_WINDOW, 0), tq)
    kw = kw_ref[0, pl.ds(start, band), :]
    s_w = start + _row_iota((band, tq))
    t_w = q0 + _lane_iota((band, tq))
    w_ok = (s_w <= t_w) & (s_w > t_w - NSA_WINDOW)
    o_win = []
    scores = [_dot_nt(kw, qa[j]) for j in range(NSA_HPG)]
    for j in range(NSA_HPG):
        s = jnp.where(w_ok, scores[j], NEG)
        p = jnp.exp2(s - _key_max(s)).astype(MXU_DTYPE)
        acc = jnp.zeros((LANES, tq), F32)
        for c in range(band // tq):
            acc = acc + _dot(vwt_ref[0, 0, start // tq + c], p[c * tq:(c + 1) * tq, :])
        o_win.append(acc[:HEAD_DIM] * (1.0 / acc[HEAD_DIM:HEAD_DIM + 1]))

    for s in range(2):
        tot = jnp.zeros((tq, LANES), F32)
        for branch, gate_ref in ((o_cmp, gc_ref), (o_sel, gs_ref), (o_win, gw_ref)):
            pair = jnp.concatenate([branch[2 * s], branch[2 * s + 1]], axis=0)
            tot = tot + gate_ref[0, :, s * LANES:(s + 1) * LANES] * pair.T
        o_ref[0, :, s * LANES:(s + 1) * LANES] = tot.astype(o_ref.dtype)


def _nsa_attention(nq, ng, kc, vct, ks, vst, kw, vwt, overlap_t, onehot, *, tq, tks):
    b, s, _ = nq.shape
    ncp = kc.shape[1]
    gw = NSA_HPG * HEAD_DIM
    per_b3 = lambda bi, g, i: (bi, 0, 0)
    per_bg4 = lambda bi, g, i: (bi, g, 0, 0)
    per_bg5 = lambda bi, g, i: (bi, g, 0, 0, 0)
    const = lambda bi, g, i: (0, 0)
    return pl.pallas_call(
        functools.partial(_nsa_kernel, tq=tq, tks=tks),
        grid=(b, NSA_GROUPS, s // tq),
        in_specs=[pl.BlockSpec((1, tq, gw), lambda bi, g, i: (bi, i, g)),
                  pl.BlockSpec((1, tq, gw), lambda bi, g, i: (bi, i, g)),
                  pl.BlockSpec((1, tq, gw), lambda bi, g, i: (bi, i, NSA_GROUPS + g)),
                  pl.BlockSpec((1, tq, gw), lambda bi, g, i: (bi, i, 2 * NSA_GROUPS + g)),
                  pl.BlockSpec((1, ncp, LANES), per_b3),
                  pl.BlockSpec((1, 1) + vct.shape[2:], per_bg4),
                  pl.BlockSpec((1, s, LANES), per_b3),
                  pl.BlockSpec((1, 1) + vst.shape[2:], per_bg5),
                  pl.BlockSpec((1, s, LANES), per_b3),
                  pl.BlockSpec((1, 1) + vwt.shape[2:], per_bg5),
                  pl.BlockSpec(overlap_t.shape, const),
                  pl.BlockSpec(onehot.shape, const)],
        out_specs=pl.BlockSpec((1, tq, gw), lambda bi, g, i: (bi, i, g)),
        out_shape=jax.ShapeDtypeStruct((b, s, NSA_QW), MXU_DTYPE),
        scratch_shapes=[pltpu.VMEM((NSA_HPG, tq, 2 * LANES), MXU_DTYPE),
                        pltpu.VMEM((NSA_HPG, LANES, tq), F32)],
        compiler_params=_params(3),
        name="nsa_attn",
    )(nq, ng, ng, ng, kc, vct, ks, vst, kw, vwt, overlap_t, onehot)


def _dsa_kernel(iq_ref, iwt_ref, ik_ref, q_ref, kv_ref, kvt_ref, o_ref, khi_ref, klo_ref, bias_ref, qa_ref,
                acc_ref, *, tq, tk, top_k):
    i = pl.program_id(1)
    q0 = i * tq
    n_tiles = (q0 + tq + tk - 1) // tk
    lane = _lane_iota((tq, LANES))
    t_q = q0 + _lane_iota((tk, tq))
    row_k = _row_iota((tk, tq))

    iqa = []
    for s in range(IDX_QW // LANES):
        qs = iq_ref[0, :, s * LANES:(s + 1) * LANES].astype(F32)
        for p in range(LANES // DSA_IDX_DIM):
            mine = jnp.where((lane >> 5) == p, qs, 0.0)
            if p:
                mine = pltpu.roll(mine, LANES - p * DSA_IDX_DIM, 1)
            iqa.append(mine.astype(MXU_DTYPE))
    w_h = [iwt_ref[0, h:h + 1, :] for h in range(DSA_IDX_HEADS)]

    def score_body(kt, _):
        ik = ik_ref[0, pl.ds(kt * tk, tk), :]
        score = jnp.zeros((tk, tq), F32)
        for h in range(DSA_IDX_HEADS):
            score = score + w_h[h] * jnp.maximum(_dot_nt(ik, iqa[h]), 0.0)
        score = jnp.where(score == 0.0, 0.0, score)
        bits = pltpu.bitcast(score, jnp.int32)
        key = jnp.where(bits < 0, bits ^ 0x7FFFFFFF, bits)
        key = jnp.where(kt * tk + row_k <= t_q, key, INT_MIN)
        khi_ref[kt] = (key >> 16).astype(jnp.int16)
        klo_ref[kt] = ((key & 0xFFFF) - 0x8000).astype(jnp.int16)
        return 0

    lax.fori_loop(0, n_tiles, score_body, 0)
    one, zero = jnp.ones((), jnp.bfloat16), jnp.zeros((), jnp.bfloat16)

    def count(flags):
        def body(kt, acc):
            return acc + _fold_rows(flags(kt), jnp.add, rows=32).astype(F32)
        acc = lax.fori_loop(0, n_tiles, body, jnp.zeros((32, tq), F32))
        return jnp.sum(acc, axis=0, keepdims=True)

    def kth_largest(ref):
        def body(b, ans_u):
            cand_u = ans_u | lax.shift_left(jnp.int32(1), 15 - b)
            cand = (cand_u - 0x8000).astype(jnp.int16)
            cnt = count(lambda kt: jnp.where(ref[kt] >= cand, one, zero))
            return jnp.where(cnt >= top_k, cand_u, ans_u)
        ans_u = lax.fori_loop(0, 16, body, jnp.zeros((1, tq), jnp.int32))
        return ans_u - 0x8000

    t_hi32 = kth_largest(khi_ref)
    t_hi = t_hi32.astype(jnp.int16)

    def pin_body(kt, _):
        hi = khi_ref[kt]
        klo_ref[kt] = jnp.where(hi == t_hi, klo_ref[kt],
                                jnp.where(hi > t_hi, jnp.int16(32767), jnp.int16(-32768)))
        return 0

    lax.fori_loop(0, n_tiles, pin_body, 0)
    t_lo32 = kth_largest(klo_ref)
    t_lo = jnp.where(t_hi32 == -0x8000, jnp.maximum(t_lo32, 1 - 0x8000), t_lo32).astype(jnp.int16)

    def at_least(kt, yes, no):
        hi = khi_ref[kt]
        return jnp.where(hi > t_hi, yes, jnp.where(hi == t_hi, jnp.where(klo_ref[kt] >= t_lo, yes, no), no))

    def above(kt):
        return jnp.where(khi_ref[kt] > t_hi, one, jnp.where(klo_ref[kt] > t_lo, one, zero))

    bias_yes, bias_no = jnp.zeros((), bias_ref.dtype), jnp.full((), NEG, bias_ref.dtype)

    def plain_bias():
        def body(kt, _):
            bias_ref[kt] = at_least(kt, bias_yes, bias_no)
            return 0
        lax.fori_loop(0, n_tiles, body, 0)

    def ranked_bias():
        need = top_k - count(above)
        lower = jnp.where(_lane_iota((tk, tk)) < _row_iota((tk, tk)), 1.0, 0.0).astype(MXU_DTYPE)

        def body(kt, seen):
            tie = jnp.where(khi_ref[kt] == t_hi, jnp.where(klo_ref[kt] == t_lo, one, zero), zero)
            rank = _dot(lower, tie.astype(MXU_DTYPE)) + seen
            tie = tie.astype(F32)
            take = jnp.where(above(kt).astype(F32) > 0.0, 1.0, jnp.where(rank < need, tie, 0.0))
            bias_ref[kt] = ((take - 1.0) * (-NEG)).astype(bias_ref.dtype)
            return seen + _key_sum(tie)

        lax.fori_loop(0, n_tiles, body, jnp.zeros((1, tq), F32))

    reached = count(lambda kt: at_least(kt, one, zero))
    lax.cond(jnp.max(reached) > top_k, ranked_bias, plain_bias)

    for s in range(DSA_QW // LANES):
        qs = q_ref[0, :, s * LANES:(s + 1) * LANES].astype(F32)
        lo = jnp.where(lane < HEAD_DIM, qs, 0.0)
        hi = _swap_halves(jnp.where(lane < HEAD_DIM, 0.0, qs))
        qa_ref[2 * s] = lo.astype(MXU_DTYPE)
        qa_ref[2 * s + 1] = hi.astype(MXU_DTYPE)

    def att_scores(kt):
        kv = kv_ref[0, pl.ds(kt * tk, tk), :]
        bias = bias_ref[kt].astype(F32)
        return tuple(_dot_nt(kv, qa_ref[h]) + bias for h in range(DSA_HEADS))

    def att_softmax(kt, ms, scores):
        kv_t = kvt_ref[0, kt]
        return tuple(_online_update(ms[h], acc_ref.at[h], scores[h], kv_t) for h in range(DSA_HEADS))

    _attend_tiles(n_tiles, att_scores, att_softmax, _online_init(tq, acc_ref), lambda scores: scores)
    outs = [acc_ref[h, HEAD_DIM:, :] * (1.0 / acc_ref[h, 0:1, :]) for h in range(DSA_HEADS)]

    for s in range(DSA_QW // LANES):
        pair = jnp.concatenate([outs[2 * s], outs[2 * s + 1]], axis=0)
        o_ref[0, :, s * LANES:(s + 1) * LANES] = pair.T.astype(o_ref.dtype)


def _dsa_attention(iq, iwt, ik, dq, dkv, dkvt, *, tq, tk, top_k):
    b, s, _ = dq.shape
    tile = lambda width: pl.BlockSpec((1, tq, width), lambda bi, i: (bi, i, 0))
    full = pl.BlockSpec((1, s, LANES), lambda bi, i: (bi, 0, 0))
    return pl.pallas_call(
        functools.partial(_dsa_kernel, tq=tq, tk=tk, top_k=top_k),
        grid=(b, s // tq),
        in_specs=[tile(IDX_QW), pl.BlockSpec((1, DSA_IDX_HEADS, tq), lambda bi, i: (bi, 0, i)), full,
                  tile(DSA_QW), full, pl.BlockSpec((1,) + dkvt.shape[1:], lambda bi, i: (bi, 0, 0, 0))],
        out_specs=tile(DSA_QW),
        out_shape=jax.ShapeDtypeStruct((b, s, DSA_QW), MXU_DTYPE),
        scratch_shapes=[pltpu.VMEM((s // tk, tk, tq), jnp.int16), pltpu.VMEM((s // tk, tk, tq), jnp.int16),
                        pltpu.VMEM((s // tk, tk, tq), MXU_DTYPE),
                        pltpu.VMEM((DSA_HEADS, tq, LANES), MXU_DTYPE), pltpu.VMEM((DSA_HEADS, LANES, tq), F32)],
        compiler_params=_params(2),
        name="dsa_attn",
    )(iq, iwt, ik, dq, dkv, dkvt)


def _merge_kernel(x_ref, g_ref, oa_ref, ob_ref, oc_ref, wg_ref, wa_ref, wb_ref, wc_ref, wo_ref, o_ref):
    x = x_ref[...]
    d = x.shape[1]
    ms = jnp.mean(x * x, axis=-1, keepdims=True)
    h = (x * lax.rsqrt(ms + NORM_EPS) * g_ref[...]).astype(MXU_DTYPE)
    merged = jnp.zeros(x.shape, F32)
    for r, (o_r, w_r) in enumerate(((oa_ref, wa_ref), (ob_ref, wb_ref), (oc_ref, wc_ref))):
        gate = jax.nn.sigmoid(_dot(h, wg_ref[:, r * d:(r + 1) * d]))
        merged = merged + gate * _dot(o_r[...], w_r[...])
    o_ref[...] = x + _dot(merged.astype(MXU_DTYPE), wo_ref[...])


def _merge(x2, g, oa, ob, oc, w_gate, wa, wb, wc, wo, *, tm):
    n, d = x2.shape
    row = lambda i: (i, 0)
    const = lambda i: (0, 0)
    return pl.pallas_call(
        _merge_kernel,
        grid=(n // tm,),
        in_specs=[pl.BlockSpec((tm, d), row), pl.BlockSpec((1, d), const),
                  pl.BlockSpec((tm, SB_W), row), pl.BlockSpec((tm, NSA_QW), row),
                  pl.BlockSpec((tm, DSA_QW), row),
                  pl.BlockSpec(w_gate.shape, const), pl.BlockSpec(wa.shape, const),
                  pl.BlockSpec(wb.shape, const), pl.BlockSpec(wc.shape, const),
                  pl.BlockSpec(wo.shape, const)],
        out_specs=pl.BlockSpec((tm, d), row),
        out_shape=jax.ShapeDtypeStruct((n, d), F32),
        compiler_params=_params(1),
        name="merge",
    )(x2, g, oa, ob, oc, w_gate, wa, wb, wc, wo)


def _rope_tables(pos, rot_dim, head_dim):
    half = rot_dim // 2
    inv = ROPE_THETA ** (-jnp.arange(0, rot_dim, 2, dtype=F32) / rot_dim)
    ang = pos.astype(F32)[:, None] * inv[None, :]
    cos, sin = jnp.cos(ang), jnp.sin(ang)
    n = pos.shape[0]
    rest = head_dim - rot_dim
    zero_h = jnp.zeros((n, half), F32)
    c = jnp.concatenate([cos, cos, jnp.ones((n, rest), F32)], axis=-1)
    sm = jnp.concatenate([-sin, zero_h, jnp.zeros((n, rest), F32)], axis=-1)
    sp = jnp.concatenate([zero_h, sin, jnp.zeros((n, rest), F32)], axis=-1)
    reps = LANES // head_dim
    return tuple(jnp.tile(t, (1, reps)) for t in (c, sm, sp))


def _pad_cols(w, width):
    return jnp.pad(w, ((0, 0), (0, width - w.shape[1])))


def _relayout_w_in(w_in):
    offs = np.cumsum((SB_W, SB_W, SB_W, NSA_QW) + (NSA_KVW,) * 6
                     + (3 * NSA_HEADS, DSA_QW, HEAD_DIM, HEAD_DIM, IDX_QW, DSA_IDX_DIM, DSA_IDX_HEADS))
    o_gate, o_dq, o_dk, o_dv, o_iq, o_ik, o_iw, o_end = offs[9:17].tolist()
    gate = w_in[:, o_gate:o_dq].reshape(-1, NSA_HEADS, 3)
    gate = jnp.repeat(jnp.transpose(gate, (0, 2, 1)), HEAD_DIM, axis=2)
    parts = [w_in[:, :o_gate], gate.reshape(w_in.shape[0], 3 * NSA_QW), w_in[:, o_dq:o_dk],
             w_in[:, o_dk:o_iq], w_in[:, o_iq:o_ik], _pad_cols(w_in[:, o_ik:o_iw], LANES),
             _pad_cols(w_in[:, o_iw:o_end], LANES)]
    w = jnp.concatenate(parts, axis=1)
    assert w.shape[1] == _P_END
    return w.astype(MXU_DTYPE)


def _compress_weights(pe, w1, w2):
    r = NSA_CMP_LEN // NSA_CMP_STRIDE
    hid = NSA_CMP_HIDDEN
    w1r = w1.reshape(2, r, NSA_CMP_STRIDE, HEAD_DIM, hid)
    eye = jnp.eye(NSA_GROUPS, dtype=w1.dtype)
    w1e = jnp.einsum('kmldj,gh->kmlgdhj', w1r, eye)
    w1e = w1e.reshape(2, r, NSA_CMP_STRIDE * NSA_KVW, NSA_GROUPS * hid)
    w2e = jnp.einsum('kjd,gh->kgjhd', w2, eye).reshape(2, NSA_GROUPS * hid, NSA_KVW)
    pe_e = jnp.broadcast_to(pe.reshape(2, r, NSA_CMP_STRIDE, 1, HEAD_DIM),
                            (2, r, NSA_CMP_STRIDE, NSA_GROUPS, HEAD_DIM))
    return pe_e.reshape(2, r, NSA_CMP_STRIDE * NSA_KVW), w1e.astype(MXU_DTYPE), w2e.astype(MXU_DTYPE)


def _overlap_matrix_t(ncp, seq):
    c = np.arange(ncp)[None, :] * NSA_CMP_STRIDE
    j = np.arange(LANES)[:, None] * NSA_SEL_LEN
    ov = (c < j + NSA_SEL_LEN) & (c + NSA_CMP_LEN - 1 >= j) & (j < seq)
    ov &= (np.arange(ncp)[None, :] < ncp - (NSA_CMP_LEN // NSA_CMP_STRIDE - 1))
    return jnp.asarray(ov, dtype=MXU_DTYPE)


def _block_onehot(seq):
    return jnp.asarray(np.arange(seq)[:, None] // NSA_SEL_LEN == np.arange(LANES)[None, :], dtype=MXU_DTYPE)


def _key_tiles_t(v, tile):
    b, s, w = v.shape
    return jnp.swapaxes(v.reshape(b, s // tile, tile, w), 2, 3)


def _group_values_t(v, tile):
    b, s, _ = v.shape
    vt = _key_tiles_t(v, tile).reshape(b, s // tile, NSA_GROUPS, HEAD_DIM, tile)
    vt = jnp.swapaxes(vt, 1, 2)
    pad = jnp.zeros((b, NSA_GROUPS, s // tile, LANES - HEAD_DIM, tile), v.dtype).at[:, :, :, 0, :].set(1)
    return jnp.concatenate([vt, pad], axis=3)


def kernel(x, ffn1_norm, ffn1_w_gu, ffn1_w_down, mix_norm, w_in, w_gate, nsa_q_norm, nsa_k_norm,
           nsa_cmp_pe, nsa_cmp_w1, nsa_cmp_w2, dsa_q_norm, dsa_k_norm, w_br_a, w_br_b, w_br_c, w_out,
           ffn2_norm, ffn2_w_gu, ffn2_w_down):
    b, s, d = x.shape
    depth = w_in.shape[0]
    n = b * s
    ncp = s // NSA_CMP_STRIDE
    assert s // NSA_SEL_LEN <= LANES and s % 1024 == 0
    top_k = min(DSA_TOPK, s // 4)
    cast = lambda w: w.astype(MXU_DTYPE)
    tile2 = lambda v, reps: jnp.tile(v, reps)[None, :]

    pos = jnp.arange(s)
    rope_main = _rope_tables(pos, ROT_DIM, HEAD_DIM)
    rope_idx = _rope_tables(pos, DSA_IDX_ROT, DSA_IDX_DIM)
    rope_cmp = _rope_tables(jnp.arange(ncp) * NSA_CMP_STRIDE + NSA_CMP_LEN - 1, ROT_DIM, HEAD_DIM)
    overlap_t = _overlap_matrix_t(ncp, s)
    onehot = _block_onehot(s)

    x2 = x.reshape(n, d)
    for l in range(depth):
        x2 = _ffn(x2, ffn1_norm[l][None, :], cast(ffn1_w_gu[l]), cast(ffn1_w_down[l]), tm=1024, tf=256)

        (sb, nq, nkc, nvc, nkv, ng, dq, dkv, iq, ik, iw) = _proj(
            x2, mix_norm[l][None, :], _relayout_w_in(w_in[l]),
            tile2(nsa_q_norm[l], NSA_HEADS),
            jnp.stack([jnp.tile(nsa_k_norm[l, 1], NSA_GROUPS), jnp.tile(nsa_k_norm[l, 2], NSA_GROUPS)]),
            tile2(dsa_q_norm[l], DSA_HEADS), tile2(dsa_k_norm[l], 2),
            rope_main, rope_idx, tm=256, seq=s)
        r3 = lambda t: t.reshape(b, s, t.shape[-1])

        pe_e, w1e, w2e = _compress_weights(nsa_cmp_pe[l], nsa_cmp_w1[l], nsa_cmp_w2[l])
        kc, vc = _compress(nkc.reshape(b, ncp, NSA_CMP_STRIDE * NSA_KVW),
                           nvc.reshape(b, ncp, NSA_CMP_STRIDE * NSA_KVW),
                           pe_e, w1e, w2e, tile2(nsa_k_norm[l, 0], NSA_GROUPS), rope_cmp)

        o_a = _sb_attention(r3(sb), tq=256)

        nkv = r3(nkv)
        ks, vs, kw, vw = (nkv[:, :, c * LANES:(c + 1) * LANES] for c in range(4))
        o_b = _nsa_attention(r3(nq), r3(ng), kc, _group_values_t(vc, ncp)[:, :, 0], ks,
                             _group_values_t(vs, TKS), kw, _group_values_t(vw, TQ), overlap_t, onehot,
                             tq=TQ, tks=TKS)

        dkv = r3(dkv)
        iwt = jnp.swapaxes(r3(iw)[:, :, :DSA_IDX_HEADS], 1, 2)
        dkvt = _key_tiles_t(dkv, TKS).at[:, :, 0, :].set(1)
        o_c = _dsa_attention(r3(iq), iwt, r3(ik), r3(dq), dkv, dkvt, tq=TQ, tk=TKS, top_k=top_k)

        x2 = _merge(x2, mix_norm[l][None, :], o_a.reshape(n, SB_W), o_b.reshape(n, NSA_QW),
                    o_c.reshape(n, DSA_QW), cast(w_gate[l]), cast(w_br_a[l]), cast(w_br_b[l]),
                    cast(w_br_c[l]), cast(w_out[l]), tm=256)

        x2 = _ffn(x2, ffn2_norm[l][None, :], cast(ffn2_w_gu[l]), cast(ffn2_w_down[l]), tm=1024, tf=256)
    return x2.reshape(b, s, d)
```

```python
import functools

import numpy as np
import jax
import jax.numpy as jnp
from jax import lax
from jax.experimental import pallas as pl
from jax.experimental.pallas import tpu as pltpu

HEAD_DIM = 64
ROT_DIM = HEAD_DIM // 4
ROPE_THETA = 500000.0
NORM_EPS = 1e-6

SB_HEADS = 4
NSA_HEADS = 8
NSA_GROUPS = 2
NSA_HPG = NSA_HEADS // NSA_GROUPS
NSA_CMP_LEN = 32
NSA_CMP_STRIDE = 16
NSA_CMP_HIDDEN = 2 * HEAD_DIM
NSA_SEL_LEN = 64
NSA_SEL_N = 16
NSA_WINDOW = 512
NSA_FORCED_SCORE = 1.0e4
DSA_HEADS = 4
DSA_IDX_HEADS = 8
DSA_IDX_DIM = 32
DSA_IDX_ROT = DSA_IDX_DIM // 4
DSA_TOPK = 256

SB_W = SB_HEADS * HEAD_DIM
NSA_QW = NSA_HEADS * HEAD_DIM
NSA_KVW = NSA_GROUPS * HEAD_DIM
DSA_QW = DSA_HEADS * HEAD_DIM
IDX_QW = DSA_IDX_HEADS * DSA_IDX_DIM

LANES = 128
VMEM_LIMIT = 56 * 1024 * 1024
MXU_DTYPE = jnp.bfloat16
NEG = -1.0e30
SB_SKIP = -120.0
INT_MIN = -2 ** 31
LOG2E = 1.4426950408889634

F32 = jnp.float32

TQ = 256
TKS = 512
assert TKS % TQ == 0 and NSA_WINDOW % TQ == 0


def _dot(a, b):
    return jnp.dot(a, b, preferred_element_type=F32)


def _dot_nt(a, b):
    return lax.dot_general(a, b, (((1,), (1,)), ((), ())), preferred_element_type=F32)


def _params(n_axes):
    return pltpu.CompilerParams(dimension_semantics=("arbitrary",) * n_axes,
                                vmem_limit_bytes=VMEM_LIMIT)


def _lane_iota(shape):
    return lax.broadcasted_iota(jnp.int32, shape, len(shape) - 1)


def _row_iota(shape):
    return lax.broadcasted_iota(jnp.int32, shape, len(shape) - 2)


def _swap_halves(x):
    return pltpu.roll(x, LANES // 2, 1)


def _head_rms(ys):
    lo = _lane_iota(ys.shape) < HEAD_DIM
    sq = ys * ys
    s_lo = jnp.sum(jnp.where(lo, sq, 0.0), axis=-1, keepdims=True)
    s_hi = jnp.sum(jnp.where(lo, 0.0, sq), axis=-1, keepdims=True)
    ms = jnp.where(lo, s_lo, s_hi) * (1.0 / HEAD_DIM)
    return ys * lax.rsqrt(ms + NORM_EPS)


def _rope(ys, c, sm, sp, half):
    return ys * c + pltpu.roll(ys, LANES - half, 1) * sm + pltpu.roll(ys, half, 1) * sp


def _ffn_kernel(x_ref, g_ref, wgu_ref, wd_ref, o_ref, h_ref, acc_ref):
    f = pl.program_id(1)
    tf = wd_ref.shape[0]

    @pl.when(f == 0)
    def _():
        x = x_ref[...]
        ms = jnp.mean(x * x, axis=-1, keepdims=True)
        h_ref[...] = (x * lax.rsqrt(ms + NORM_EPS) * g_ref[...]).astype(h_ref.dtype)
        acc_ref[...] = jnp.zeros_like(acc_ref)

    gu = _dot(h_ref[...], wgu_ref[...])
    gate, up = gu[:, :tf], gu[:, tf:]
    act = (gate * jax.nn.sigmoid(gate) * up).astype(h_ref.dtype)
    acc_ref[...] += _dot(act, wd_ref[...])

    @pl.when(f == pl.num_programs(1) - 1)
    def _():
        o_ref[...] = x_ref[...] + 0.5 * acc_ref[...]


def _ffn(x2, g, w_gu, w_down, *, tm, tf):
    n, d = x2.shape
    d_ff = w_down.shape[0]
    nf = d_ff // tf
    w_gu = jnp.swapaxes(w_gu.reshape(d, 2, nf, tf), 1, 2).reshape(d, 2 * d_ff)
    return pl.pallas_call(
        _ffn_kernel,
        grid=(n // tm, nf),
        in_specs=[
            pl.BlockSpec((tm, d), lambda i, f: (i, 0)),
            pl.BlockSpec((1, d), lambda i, f: (0, 0)),
            pl.BlockSpec((d, 2 * tf), lambda i, f: (0, f)),
            pl.BlockSpec((tf, d), lambda i, f: (f, 0)),
        ],
        out_specs=pl.BlockSpec((tm, d), lambda i, f: (i, 0)),
        out_shape=jax.ShapeDtypeStruct((n, d), F32),
        scratch_shapes=[pltpu.VMEM((tm, d), MXU_DTYPE), pltpu.VMEM((tm, d), F32)],
        compiler_params=_params(2),
        name="ffn",
    )(x2, g, w_gu, w_down)


_P_SB = 0
_P_NQ = _P_SB + 3 * SB_W
_P_NCV = _P_NQ + NSA_QW
_P_NKV = _P_NCV + 2 * NSA_KVW
_P_NG = _P_NKV + 4 * NSA_KVW
_P_DQ = _P_NG + 3 * NSA_QW
_P_DKV = _P_DQ + DSA_QW
_P_IQ = _P_DKV + LANES
_P_IK = _P_IQ + IDX_QW
_P_IW = _P_IK + LANES
_P_END = _P_IW + LANES


def _proj_kernel(x_ref, g_ref, w_ref, nq_g_ref, nk_g_ref, dq_g_ref, dk_g_ref,
                 rc_ref, rm_ref, rp_ref, ic_ref, im_ref, ip_ref,
                 sb_ref, nq_ref, nkc_ref, nvc_ref, nkv_ref, ng_ref, dq_ref, dkv_ref,
                 iq_ref, ik_ref, iw_ref):
    x = x_ref[...]
    ms = jnp.mean(x * x, axis=-1, keepdims=True)
    h = (x * lax.rsqrt(ms + NORM_EPS) * g_ref[...]).astype(MXU_DTYPE)
    rc, rm, rp = rc_ref[...], rm_ref[...], rp_ref[...]
    ic, im, ip = ic_ref[...], im_ref[...], ip_ref[...]
    half, ihalf = ROT_DIM // 2, DSA_IDX_ROT // 2
    scale = HEAD_DIM ** -0.5
    scale2 = scale * LOG2E

    def cols(a, width):
        return _dot(h, w_ref[:, a:a + width])

    def slab(y, s):
        return y[:, s * LANES:(s + 1) * LANES]

    y = cols(_P_SB, 3 * SB_W)
    sb_ref[:, 0:SB_W] = (y[:, 0:SB_W] * scale).astype(sb_ref.dtype)
    sb_ref[:, SB_W:3 * SB_W] = y[:, SB_W:3 * SB_W].astype(sb_ref.dtype)

    y = cols(_P_NQ, NSA_QW)
    for s in range(NSA_QW // LANES):
        ys = _head_rms(slab(y, s)) * slab(nq_g_ref[...], s)
        nq_ref[:, s * LANES:(s + 1) * LANES] = (_rope(ys, rc, rm, rp, half) * scale2).astype(nq_ref.dtype)

    y = cols(_P_NCV, 2 * NSA_KVW)
    nkc_ref[...] = slab(y, 0)
    nvc_ref[...] = slab(y, 1)

    y = cols(_P_NKV, 4 * NSA_KVW)
    for s, gi in ((0, 0), (2, 1)):
        ys = _head_rms(slab(y, s)) * nk_g_ref[gi:gi + 1, :]
        nkv_ref[:, s * LANES:(s + 1) * LANES] = _rope(ys, rc, rm, rp, half).astype(nkv_ref.dtype)
    for s in (1, 3):
        nkv_ref[:, s * LANES:(s + 1) * LANES] = slab(y, s).astype(nkv_ref.dtype)

    ng_ref[...] = jax.nn.sigmoid(cols(_P_NG, 3 * NSA_QW))

    y = cols(_P_DQ, DSA_QW)
    for s in range(DSA_QW // LANES):
        ys = _head_rms(slab(y, s)) * slab(dq_g_ref[...], s)
        dq_ref[:, s * LANES:(s + 1) * LANES] = (_rope(ys, rc, rm, rp, half) * scale2).astype(dq_ref.dtype)

    y = cols(_P_DKV, LANES)
    yk = _rope(_head_rms(y) * dk_g_ref[...], rc, rm, rp, half)
    dkv_ref[...] = jnp.where(_lane_iota(y.shape) < HEAD_DIM, yk, y).astype(dkv_ref.dtype)

    y = cols(_P_IQ, IDX_QW)
    for s in range(IDX_QW // LANES):
        iq_ref[:, s * LANES:(s + 1) * LANES] = _rope(slab(y, s), ic, im, ip, ihalf).astype(iq_ref.dtype)
    ik_ref[...] = _rope(cols(_P_IK, LANES), ic, im, ip, ihalf).astype(ik_ref.dtype)
    iw_ref[...] = cols(_P_IW, LANES) * (DSA_IDX_HEADS ** -0.5) * (DSA_IDX_DIM ** -0.5)


def _proj(x2, g, w, nq_g, nk_g, dq_g, dk_g, rope_main, rope_idx, *, tm, seq):
    n, d = x2.shape
    nt = seq // tm
    row = lambda i: (i, 0)
    const = lambda i: (0, 0)
    pos = lambda i: (i % nt, 0)
    widths = [(3 * SB_W, MXU_DTYPE), (NSA_QW, MXU_DTYPE), (NSA_KVW, F32), (NSA_KVW, F32),
              (4 * NSA_KVW, MXU_DTYPE), (3 * NSA_QW, F32), (DSA_QW, MXU_DTYPE), (LANES, MXU_DTYPE),
              (IDX_QW, MXU_DTYPE), (LANES, MXU_DTYPE), (LANES, F32)]
    return pl.pallas_call(
        _proj_kernel,
        grid=(n // tm,),
        in_specs=[pl.BlockSpec((tm, d), row), pl.BlockSpec((1, d), const),
                  pl.BlockSpec((d, _P_END), const),
                  pl.BlockSpec((1, NSA_QW), const), pl.BlockSpec((2, LANES), const),
                  pl.BlockSpec((1, DSA_QW), const), pl.BlockSpec((1, LANES), const)]
                 + [pl.BlockSpec((tm, LANES), pos)] * 6,
        out_specs=[pl.BlockSpec((tm, wd), row) for wd, _ in widths],
        out_shape=[jax.ShapeDtypeStruct((n, wd), dt) for wd, dt in widths],
        compiler_params=_params(1),
        name="proj",
    )(x2, g, w, nq_g, nk_g, dq_g, dk_g, *rope_main, *rope_idx)


def _compress_kernel(xk_ref, xv_ref, pe_ref, w1_ref, w2_ref, g_ref, rc_ref, rm_ref, rp_ref,
                     kc_ref, vc_ref):
    ncp = xk_ref.shape[1]

    def compress(x, kv):
        a0 = _dot((x + pe_ref[kv, 0:1, :]).astype(MXU_DTYPE), w1_ref[kv, 0])
        a1 = _dot((x + pe_ref[kv, 1:2, :]).astype(MXU_DTYPE), w1_ref[kv, 1])
        pre = a0 + pltpu.roll(a1, ncp - 1, 0)
        hid = pre * jax.nn.sigmoid(pre)
        return _dot(hid.astype(MXU_DTYPE), w2_ref[kv])

    yk = _head_rms(compress(xk_ref[0], 0)) * g_ref[...]
    kc_ref[0] = _rope(yk, rc_ref[...], rm_ref[...], rp_ref[...], ROT_DIM // 2).astype(kc_ref.dtype)
    vc_ref[0] = compress(xv_ref[0], 1).astype(vc_ref.dtype)


def _compress(xk, xv, pe, w1, w2, g, rope_c):
    b, ncp, wide = xk.shape
    c3 = lambda i: (0, 0, 0)
    c4 = lambda i: (0, 0, 0, 0)
    c2 = lambda i: (0, 0)
    per_b = lambda i: (i, 0, 0)
    return pl.pallas_call(
        _compress_kernel,
        grid=(b,),
        in_specs=[pl.BlockSpec((1, ncp, wide), per_b), pl.BlockSpec((1, ncp, wide), per_b),
                  pl.BlockSpec(pe.shape, c3), pl.BlockSpec(w1.shape, c4), pl.BlockSpec(w2.shape, c3),
                  pl.BlockSpec((1, LANES), c2)] + [pl.BlockSpec((ncp, LANES), c2)] * 3,
        out_specs=[pl.BlockSpec((1, ncp, LANES), per_b)] * 2,
        out_shape=[jax.ShapeDtypeStruct((b, ncp, LANES), MXU_DTYPE)] * 2,
        compiler_params=_params(1),
        name="compress",
    )(xk, xv, pe, w1, w2, g, *rope_c)


def _sb_kernel(q_ref, k_ref, v_ref, o_ref, *, tq):
    i = pl.program_id(2)
    q0 = i * tq
    q = q_ref[0].astype(F32)
    lane = _lane_iota(q.shape)
    rows = _row_iota((tq, tq))
    colsq = _lane_iota((tq, tq))
    upper = jnp.where(_row_iota((tq, tq + LANES)) > _lane_iota((tq, tq + LANES)), 1.0,
                      jnp.where(_lane_iota((tq, tq + LANES)) >= tq, 1.0, 0.0)).astype(MXU_DTYPE)
    in_head = [(lane >> 6) == p for p in range(2)]
    qh = [jnp.where(m, q, 0.0).astype(MXU_DTYPE) for m in in_head]
    reps = tq // LANES

    def tile(kt, carries, accs, diagonal):
        ks = k_ref[0, pl.ds(kt * tq, tq), :]
        vs = v_ref[0, pl.ds(kt * tq, tq), :]
        zs = [_dot_nt(qh[p], ks) for p in range(2)]
        new_c, new_a = [], []
        for p in range(2):
            z = zs[p]
            l = -(jnp.maximum(z, 0.0) + jnp.log1p(jnp.exp(-jnp.abs(z))))
            if diagonal:
                past = colsq < rows
                l = jnp.where(past, l, 0.0)
            hi = l.astype(MXU_DTYPE)
            lo = (l - hi.astype(F32)).astype(MXU_DTYPE)
            sums = _dot(hi, upper) + _dot(lo, upper)
            tail = sums[:, :tq] + jnp.concatenate([carries[p]] * reps, axis=1)
            w = jnp.exp(z + l + tail)
            if diagonal:
                w = jnp.where(past, w, 0.0)
            new_a.append(accs[p] + _dot(w.astype(MXU_DTYPE), vs))
            new_c.append(carries[p] + sums[:, tq:])
        return tuple(new_c), tuple(new_a)

    zeros = (jnp.zeros((tq, LANES), F32),) * 2
    carries, accs = tile(i, zeros, zeros, True)

    def cond(st):
        kt, carries, _ = st
        return (kt >= 0) & (jnp.max(jnp.maximum(carries[0], carries[1])) > SB_SKIP)

    def body(st):
        kt, carries, accs = st
        carries, accs = tile(kt, carries, accs, False)
        return kt - 1, carries, accs

    _, _, accs = lax.while_loop(cond, body, (i - 1, carries, accs))
    del q0
    o_ref[0] = jnp.where(in_head[0], accs[0], accs[1]).astype(o_ref.dtype)


def _sb_attention(sb, *, tq):
    b, s, _ = sb.shape
    nslab = SB_W // LANES
    return pl.pallas_call(
        functools.partial(_sb_kernel, tq=tq),
        grid=(b, nslab, s // tq),
        in_specs=[pl.BlockSpec((1, tq, LANES), lambda bi, sl, i: (bi, i, sl)),
                  pl.BlockSpec((1, s, LANES), lambda bi, sl, i: (bi, 0, nslab + sl)),
                  pl.BlockSpec((1, s, LANES), lambda bi, sl, i: (bi, 0, 2 * nslab + sl))],
        out_specs=pl.BlockSpec((1, tq, LANES), lambda bi, sl, i: (bi, i, sl)),
        out_shape=jax.ShapeDtypeStruct((b, s, SB_W), MXU_DTYPE),
        compiler_params=_params(3),
        name="sb_attn",
    )(sb, sb, sb)


def _group_queries(q_ref, g, tq):
    lane = _lane_iota((tq, LANES))
    in_g = (lane >> 6) == g
    qa = []
    for s in range(2):
        qs = q_ref[0, :, s * LANES:(s + 1) * LANES].astype(F32)
        for p in range(2):
            mine = jnp.where((lane >> 6) == p, qs, 0.0)
            both = mine + _swap_halves(mine)
            qa.append(jnp.where(in_g, both, 0.0).astype(MXU_DTYPE))
    return qa


def _fold_rows(x, op, rows=64):
    while x.shape[0] > rows and x.shape[0] % 16 == 0:
        half = x.shape[0] // 2
        x = op(x[:half], x[half:])
    return x


def _key_max(x):
    return jnp.max(_fold_rows(x, jnp.maximum), axis=0, keepdims=True)


def _key_sum(x):
    return jnp.sum(_fold_rows(x, jnp.add), axis=0, keepdims=True)


def _online_update(m, acc_ref, s, v_t):
    m_new = jnp.maximum(m, _key_max(s))
    p = jnp.exp2(s - m_new)
    acc_ref[...] = jnp.exp2(m - m_new) * acc_ref[...] + _dot(v_t, p.astype(MXU_DTYPE))
    return m_new


def _attend_tiles(n_tiles, scores_fn, softmax_fn, ms, fix_last):
    def step(tiles, ms, last):
        scores = [scores_fn(kt) for kt in tiles]
        if last:
            scores[-1] = fix_last(scores[-1])
        for kt, sc in zip(tiles, scores):
            ms = softmax_fn(kt, ms, sc)
        return ms

    plain = n_tiles - 1
    ms = lax.fori_loop(0, plain // 2, lambda p, ms: step([2 * p, 2 * p + 1], ms, False), ms)
    return lax.cond(plain % 2 == 1,
                    lambda ms: step([n_tiles - 2, n_tiles - 1], ms, True),
                    lambda ms: step([n_tiles - 1], ms, True), ms)


def _attend_pipelined(n_tiles, scores_fn, softmax_fn, ms, fix_last, s_ref):
    heads = s_ref.shape[1]

    def issue(slot, kt):
        for h, sc in enumerate(scores_fn(kt)):
            s_ref[slot, h] = sc

    def fold(slot, kt, ms):
        return softmax_fn(kt, ms, tuple(s_ref[slot, h] for h in range(heads)))

    plain = n_tiles - 1

    @pl.when(plain > 0)
    def _():
        issue(0, 0)

    def body(p, ms):
        issue(1, 2 * p + 1)
        ms = fold(0, 2 * p, ms)
        issue(0, jnp.minimum(2 * p + 2, plain - 1))
        return fold(1, 2 * p + 1, ms)

    ms = lax.fori_loop(0, plain // 2, body, ms)
    ms = lax.cond(plain % 2 == 1, lambda ms: fold(0, plain - 1, ms), lambda ms: ms, ms)
    return softmax_fn(n_tiles - 1, ms, fix_last(scores_fn(n_tiles - 1)))


def _online_init(tq, acc_ref):
    acc_ref[...] = jnp.zeros(acc_ref.shape, F32)
    return tuple(jnp.full((1, tq), NEG, F32) for _ in range(acc_ref.shape[0]))


def _nsa_kernel(q_ref, gc_ref, gs_ref, gw_ref, kc_ref, vct_ref, ks_ref, vst_ref, kw_ref, vwt_ref,
                ovt_ref, oh_ref, o_ref, qaug_ref, acc_ref, s_ref, *, tq, tks):
    g = pl.program_id(1)
    i = pl.program_id(2)
    q0 = i * tq
    ncp = kc_ref.shape[1]
    qa = _group_queries(q_ref, g, tq)

    c_vis = (_row_iota((ncp, tq)) * NSA_CMP_STRIDE + (NSA_CMP_LEN - 1)) <= q0 + _lane_iota((ncp, tq))
    kc = kc_ref[0]
    vct = vct_ref[0, 0]
    pc_sum = jnp.zeros((ncp, tq), F32)
    o_cmp = []
    scores = [_dot_nt(kc, qa[j]) for j in range(NSA_HPG)]
    for j in range(NSA_HPG):
        s = jnp.where(c_vis, scores[j], NEG)
        p = jnp.where(c_vis, jnp.exp2(s - _key_max(s)), 0.0)
        den = _key_sum(p)
        p = p * (1.0 / jnp.where(den > 0.0, den, 1.0))
        pc_sum = pc_sum + p
        o_cmp.append(_dot(vct, p.astype(MXU_DTYPE))[:HEAD_DIM])

    band = tq + NSA_WINDOW
    start = pl.multiple_of(jnp.maximum(q0 - NSA_WINDOW, 0), tq)
    kw = kw_ref[0, pl.ds(start, band), :]
    s_w = start + _row_iota((band, tq))
    t_w = q0 + _lane_iota((band, tq))
    w_ok = (s_w <= t_w) & (s_w > t_w - NSA_WINDOW)
    o_win = []
    scores = [_dot_nt(kw, qa[j]) for j in range(NSA_HPG)]
    for j in range(NSA_HPG):
        s = jnp.where(w_ok, scores[j], NEG)
        p = jnp.exp2(s - _key_max(s)).astype(MXU_DTYPE)
        acc = jnp.zeros((LANES, tq), F32)
        for c in range(band // tq):
            acc = acc + _dot(vwt_ref[0, 0, start // tq + c], p[c * tq:(c + 1) * tq, :])
        o_win.append(acc[:HEAD_DIM] * (1.0 / acc[HEAD_DIM:HEAD_DIM + 1]))

    hi = pc_sum.astype(MXU_DTYPE)
    lo = (pc_sum - hi.astype(F32)).astype(MXU_DTYPE)
    imp = _dot(ovt_ref[...], hi) + _dot(ovt_ref[...], lo)
    blk = _row_iota((LANES, tq))
    t_b = q0 + _lane_iota((LANES, tq))
    cur = t_b >> 6
    visible = blk * NSA_SEL_LEN <= t_b
    forced = (blk == 0) | (blk == cur) | (blk == cur - 1)
    work = jnp.where(visible, jnp.where(forced, -3.0e38, imp), NEG)
    blk_f = blk.astype(F32)
    sel = jnp.where(forced, 1.0, 0.0)
    for _ in range(NSA_SEL_N - 3):
        m = jnp.max(work, axis=0, keepdims=True)
        first = jnp.min(jnp.where(work == m, blk_f, float(LANES)), axis=0, keepdims=True)
        pick = blk_f == first
        sel = jnp.where(pick, 1.0, sel)
        work = jnp.where(pick, -3.0e38, work)
    sel_bias = jnp.where(visible, (sel - 1.0) * (-NEG), NEG).T.astype(MXU_DTYPE)

    n_tiles = (q0 + tq + tks - 1) // tks
    t_s = q0 + _lane_iota((tks, tq))
    row_s = _row_iota((tks, tq))
    for j in range(NSA_HPG):
        qaug_ref[j] = jnp.concatenate([qa[j], sel_bias], axis=1)

    def sel_scores(kt):
        k_aug = jnp.concatenate([ks_ref[0, pl.ds(kt * tks, tks), :], oh_ref[pl.ds(kt * tks, tks), :]], axis=1)
        return tuple(_dot_nt(k_aug, qaug_ref[j]) for j in range(NSA_HPG))

    def sel_softmax(kt, ms, scores):
        v_t = vst_ref[0, 0, kt]
        return tuple(_online_update(ms[j], acc_ref.at[j], scores[j], v_t) for j in range(NSA_HPG))

    def sel_causal(scores):
        causal = (n_tiles - 1) * tks + row_s <= t_s
        return tuple(jnp.where(causal, s, NEG) for s in scores)

    _attend_pipelined(n_tiles, sel_scores, sel_softmax, _online_init(tq, acc_ref), sel_causal, s_ref)
    o_sel = [acc_ref[j, :HEAD_DIM, :] * (1.0 / acc_ref[j, HEAD_DIM:HEAD_DIM + 1, :]) for j in range(NSA_HPG)]

    for s in range(2):
        tot = jnp.zeros((tq, LANES), F32)
        for branch, gate_ref in ((o_cmp, gc_ref), (o_sel, gs_ref), (o_win, gw_ref)):
            pair = jnp.concatenate([branch[2 * s], branch[2 * s + 1]], axis=0)
            tot = tot + gate_ref[0, :, s * LANES:(s + 1) * LANES] * pair.T
        o_ref[0, :, s * LANES:(s + 1) * LANES] = tot.astype(o_ref.dtype)


def _nsa_attention(nq, ng, kc, vct, ks, vst, kw, vwt, overlap_t, onehot, *, tq, tks):
    b, s, _ = nq.shape
    ncp = kc.shape[1]
    gw = NSA_HPG * HEAD_DIM
    per_b3 = lambda bi, g, i: (bi, 0, 0)
    per_bg4 = lambda bi, g, i: (bi, g, 0, 0)
    per_bg5 = lambda bi, g, i: (bi, g, 0, 0, 0)
    const = lambda bi, g, i: (0, 0)
    return pl.pallas_call(
        functools.partial(_nsa_kernel, tq=tq, tks=tks),
        grid=(b, NSA_GROUPS, s // tq),
        in_specs=[pl.BlockSpec((1, tq, gw), lambda bi, g, i: (bi, i, g)),
                  pl.BlockSpec((1, tq, gw), lambda bi, g, i: (bi, i, g)),
                  pl.BlockSpec((1, tq, gw), lambda bi, g, i: (bi, i, NSA_GROUPS + g)),
                  pl.BlockSpec((1, tq, gw), lambda bi, g, i: (bi, i, 2 * NSA_GROUPS + g)),
                  pl.BlockSpec((1, ncp, LANES), per_b3),
                  pl.BlockSpec((1, 1) + vct.shape[2:], per_bg4),
                  pl.BlockSpec((1, s, LANES), per_b3),
                  pl.BlockSpec((1, 1) + vst.shape[2:], per_bg5),
                  pl.BlockSpec((1, s, LANES), per_b3),
                  pl.BlockSpec((1, 1) + vwt.shape[2:], per_bg5),
                  pl.BlockSpec(overlap_t.shape, const),
                  pl.BlockSpec(onehot.shape, const)],
        out_specs=pl.BlockSpec((1, tq, gw), lambda bi, g, i: (bi, i, g)),
        out_shape=jax.ShapeDtypeStruct((b, s, NSA_QW), MXU_DTYPE),
        scratch_shapes=[pltpu.VMEM((NSA_HPG, tq, 2 * LANES), MXU_DTYPE),
                        pltpu.VMEM((NSA_HPG, LANES, tq), F32),
                        pltpu.VMEM((2, NSA_HPG, tks, tq), F32)],
        compiler_params=_params(3),
        name="nsa_attn",
    )(nq, ng, ng, ng, kc, vct, ks, vst, kw, vwt, overlap_t, onehot)


def _dsa_kernel(iq_ref, iwt_ref, ik_ref, q_ref, kv_ref, kvt_ref, o_ref, khi_ref, klo_ref, bias_ref, qa_ref,
                acc_ref, s_ref, *, tq, tk, top_k):
    i = pl.program_id(1)
    q0 = i * tq
    n_tiles = (q0 + tq + tk - 1) // tk
    lane = _lane_iota((tq, LANES))
    t_q = q0 + _lane_iota((tk, tq))
    row_k = _row_iota((tk, tq))

    iqa = []
    for s in range(IDX_QW // LANES):
        qs = iq_ref[0, :, s * LANES:(s + 1) * LANES].astype(F32)
        for p in range(LANES // DSA_IDX_DIM):
            mine = jnp.where((lane >> 5) == p, qs, 0.0)
            if p:
                mine = pltpu.roll(mine, LANES - p * DSA_IDX_DIM, 1)
            iqa.append(mine.astype(MXU_DTYPE))
    w_h = [iwt_ref[0, h:h + 1, :] for h in range(DSA_IDX_HEADS)]

    def score_body(kt, _, diagonal):
        ik = ik_ref[0, pl.ds(kt * tk, tk), :]
        score = jnp.zeros((tk, tq), F32)
        for h in range(DSA_IDX_HEADS):
            score = score + w_h[h] * jnp.maximum(_dot_nt(ik, iqa[h]), 0.0)
        score = jnp.where(score == 0.0, 0.0, score)
        bits = pltpu.bitcast(score, jnp.int32)
        key = jnp.where(bits < 0, bits ^ 0x7FFFFFFF, bits)
        if diagonal:
            key = jnp.where(kt * tk + row_k <= t_q, key, INT_MIN)
        khi_ref[kt] = (key >> 16).astype(jnp.int16)
        klo_ref[kt] = ((key & 0xFFFF) - 0x8000).astype(jnp.int16)
        return 0

    lax.fori_loop(0, n_tiles - 1, functools.partial(score_body, diagonal=False), 0)
    score_body(n_tiles - 1, 0, True)
    one, zero = jnp.ones((), jnp.bfloat16), jnp.zeros((), jnp.bfloat16)

    def count(flags):
        def body(kt, acc):
            return acc + _fold_rows(flags(kt), jnp.add, rows=32).astype(F32)
        acc = lax.fori_loop(0, n_tiles, body, jnp.zeros((32, tq), F32))
        return jnp.sum(acc, axis=0, keepdims=True)

    def kth_largest(ref):
        def body(b, ans_u):
            cand_u = ans_u | lax.shift_left(jnp.int32(1), 15 - b)
            cand = (cand_u - 0x8000).astype(jnp.int16)
            cnt = count(lambda kt: jnp.where(ref[kt] >= cand, one, zero))
            return jnp.where(cnt >= top_k, cand_u, ans_u)
        ans_u = lax.fori_loop(0, 16, body, jnp.zeros((1, tq), jnp.int32))
        return ans_u - 0x8000

    t_hi32 = kth_largest(khi_ref)
    t_hi = t_hi32.astype(jnp.int16)

    def pin_body(kt, _):
        hi = khi_ref[kt]
        klo_ref[kt] = jnp.where(hi == t_hi, klo_ref[kt],
                                jnp.where(hi > t_hi, jnp.int16(32767), jnp.int16(-32768)))
        return 0

    lax.fori_loop(0, n_tiles, pin_body, 0)
    t_lo32 = kth_largest(klo_ref)
    t_lo = jnp.where(t_hi32 == -0x8000, jnp.maximum(t_lo32, 1 - 0x8000), t_lo32).astype(jnp.int16)

    def at_least(kt, yes, no):
        hi = khi_ref[kt]
        return jnp.where(hi > t_hi, yes, jnp.where(hi == t_hi, jnp.where(klo_ref[kt] >= t_lo, yes, no), no))

    def above(kt):
        return jnp.where(khi_ref[kt] > t_hi, one, jnp.where(klo_ref[kt] > t_lo, one, zero))

    bias_yes, bias_no = jnp.zeros((), bias_ref.dtype), jnp.full((), NEG, bias_ref.dtype)

    def plain_bias():
        def body(kt, _):
            bias_ref[kt] = at_least(kt, bias_yes, bias_no)
            return 0
        lax.fori_loop(0, n_tiles, body, 0)

    def ranked_bias():
        need = top_k - count(above)
        lower = jnp.where(_lane_iota((tk, tk)) < _row_iota((tk, tk)), 1.0, 0.0).astype(MXU_DTYPE)

        def body(kt, seen):
            tie = jnp.where(khi_ref[kt] == t_hi, jnp.where(klo_ref[kt] == t_lo, one, zero), zero)
            rank = _dot(lower, tie.astype(MXU_DTYPE)) + seen
            tie = tie.astype(F32)
            take = jnp.where(above(kt).astype(F32) > 0.0, 1.0, jnp.where(rank < need, tie, 0.0))
            bias_ref[kt] = ((take - 1.0) * (-NEG)).astype(bias_ref.dtype)
            return seen + _key_sum(tie)

        lax.fori_loop(0, n_tiles, body, jnp.zeros((1, tq), F32))

    reached = count(lambda kt: at_least(kt, one, zero))
    lax.cond(jnp.max(reached) > top_k, ranked_bias, plain_bias)

    for s in range(DSA_QW // LANES):
        qs = q_ref[0, :, s * LANES:(s + 1) * LANES].astype(F32)
        lo = jnp.where(lane < HEAD_DIM, qs, 0.0)
        hi = _swap_halves(jnp.where(lane < HEAD_DIM, 0.0, qs))
        qa_ref[2 * s] = lo.astype(MXU_DTYPE)
        qa_ref[2 * s + 1] = hi.astype(MXU_DTYPE)

    def att_scores(kt):
        kv = kv_ref[0, pl.ds(kt * tk, tk), :]
        bias = bias_ref[kt].astype(F32)
        return tuple(_dot_nt(kv, qa_ref[h]) + bias for h in range(DSA_HEADS))

    def att_softmax(kt, ms, scores):
        kv_t = kvt_ref[0, kt]
        return tuple(_online_update(ms[h], acc_ref.at[h], scores[h], kv_t) for h in range(DSA_HEADS))

    _attend_pipelined(n_tiles, att_scores, att_softmax, _online_init(tq, acc_ref), lambda scores: scores, s_ref)
    outs = [acc_ref[h, HEAD_DIM:, :] * (1.0 / acc_ref[h, 0:1, :]) for h in range(DSA_HEADS)]

    for s in range(DSA_QW // LANES):
        pair = jnp.concatenate([outs[2 * s], outs[2 * s + 1]], axis=0)
        o_ref[0, :, s * LANES:(s + 1) * LANES] = pair.T.astype(o_ref.dtype)


def _dsa_attention(iq, iwt, ik, dq, dkv, dkvt, *, tq, tk, top_k):
    b, s, _ = dq.shape
    tile = lambda width: pl.BlockSpec((1, tq, width), lambda bi, i: (bi, i, 0))
    full = pl.BlockSpec((1, s, LANES), lambda bi, i: (bi, 0, 0))
    return pl.pallas_call(
        functools.partial(_dsa_kernel, tq=tq, tk=tk, top_k=top_k),
        grid=(b, s // tq),
        in_specs=[tile(IDX_QW), pl.BlockSpec((1, DSA_IDX_HEADS, tq), lambda bi, i: (bi, 0, i)), full,
                  tile(DSA_QW), full, pl.BlockSpec((1,) + dkvt.shape[1:], lambda bi, i: (bi, 0, 0, 0))],
        out_specs=tile(DSA_QW),
        out_shape=jax.ShapeDtypeStruct((b, s, DSA_QW), MXU_DTYPE),
        scratch_shapes=[pltpu.VMEM((s // tk, tk, tq), jnp.int16), pltpu.VMEM((s // tk, tk, tq), jnp.int16),
                        pltpu.VMEM((s // tk, tk, tq), MXU_DTYPE),
                        pltpu.VMEM((DSA_HEADS, tq, LANES), MXU_DTYPE), pltpu.VMEM((DSA_HEADS, LANES, tq), F32),
                        pltpu.VMEM((2, DSA_HEADS, tk, tq), F32)],
        compiler_params=_params(2),
        name="dsa_attn",
    )(iq, iwt, ik, dq, dkv, dkvt)


def _merge_kernel(x_ref, g_ref, oa_ref, ob_ref, oc_ref, wg_ref, wa_ref, wb_ref, wc_ref, wo_ref, o_ref):
    x = x_ref[...]
    d = x.shape[1]
    ms = jnp.mean(x * x, axis=-1, keepdims=True)
    h = (x * lax.rsqrt(ms + NORM_EPS) * g_ref[...]).astype(MXU_DTYPE)
    merged = jnp.zeros(x.shape, F32)
    for r, (o_r, w_r) in enumerate(((oa_ref, wa_ref), (ob_ref, wb_ref), (oc_ref, wc_ref))):
        gate = jax.nn.sigmoid(_dot(h, wg_ref[:, r * d:(r + 1) * d]))
        merged = merged + gate * _dot(o_r[...], w_r[...])
    o_ref[...] = x + _dot(merged.astype(MXU_DTYPE), wo_ref[...])


def _merge(x2, g, oa, ob, oc, w_gate, wa, wb, wc, wo, *, tm):
    n, d = x2.shape
    row = lambda i: (i, 0)
    const = lambda i: (0, 0)
    return pl.pallas_call(
        _merge_kernel,
        grid=(n // tm,),
        in_specs=[pl.BlockSpec((tm, d), row), pl.BlockSpec((1, d), const),
                  pl.BlockSpec((tm, SB_W), row), pl.BlockSpec((tm, NSA_QW), row),
                  pl.BlockSpec((tm, DSA_QW), row),
                  pl.BlockSpec(w_gate.shape, const), pl.BlockSpec(wa.shape, const),
                  pl.BlockSpec(wb.shape, const), pl.BlockSpec(wc.shape, const),
                  pl.BlockSpec(wo.shape, const)],
        out_specs=pl.BlockSpec((tm, d), row),
        out_shape=jax.ShapeDtypeStruct((n, d), F32),
        compiler_params=_params(1),
        name="merge",
    )(x2, g, oa, ob, oc, w_gate, wa, wb, wc, wo)


def _rope_tables(pos, rot_dim, head_dim):
    half = rot_dim // 2
    inv = ROPE_THETA ** (-jnp.arange(0, rot_dim, 2, dtype=F32) / rot_dim)
    ang = pos.astype(F32)[:, None] * inv[None, :]
    cos, sin = jnp.cos(ang), jnp.sin(ang)
    n = pos.shape[0]
    rest = head_dim - rot_dim
    zero_h = jnp.zeros((n, half), F32)
    c = jnp.concatenate([cos, cos, jnp.ones((n, rest), F32)], axis=-1)
    sm = jnp.concatenate([-sin, zero_h, jnp.zeros((n, rest), F32)], axis=-1)
    sp = jnp.concatenate([zero_h, sin, jnp.zeros((n, rest), F32)], axis=-1)
    reps = LANES // head_dim
    return tuple(jnp.tile(t, (1, reps)) for t in (c, sm, sp))


def _pad_cols(w, width):
    return jnp.pad(w, ((0, 0), (0, width - w.shape[1])))


def _relayout_w_in(w_in):
    offs = np.cumsum((SB_W, SB_W, SB_W, NSA_QW) + (NSA_KVW,) * 6
                     + (3 * NSA_HEADS, DSA_QW, HEAD_DIM, HEAD_DIM, IDX_QW, DSA_IDX_DIM, DSA_IDX_HEADS))
    o_gate, o_dq, o_dk, o_dv, o_iq, o_ik, o_iw, o_end = offs[9:17].tolist()
    gate = w_in[:, o_gate:o_dq].reshape(-1, NSA_HEADS, 3)
    gate = jnp.repeat(jnp.transpose(gate, (0, 2, 1)), HEAD_DIM, axis=2)
    parts = [w_in[:, :o_gate], gate.reshape(w_in.shape[0], 3 * NSA_QW), w_in[:, o_dq:o_dk],
             w_in[:, o_dk:o_iq], w_in[:, o_iq:o_ik], _pad_cols(w_in[:, o_ik:o_iw], LANES),
             _pad_cols(w_in[:, o_iw:o_end], LANES)]
    w = jnp.concatenate(parts, axis=1)
    assert w.shape[1] == _P_END
    return w.astype(MXU_DTYPE)


def _compress_weights(pe, w1, w2):
    r = NSA_CMP_LEN // NSA_CMP_STRIDE
    hid = NSA_CMP_HIDDEN
    w1r = w1.reshape(2, r, NSA_CMP_STRIDE, HEAD_DIM, hid)
    eye = jnp.eye(NSA_GROUPS, dtype=w1.dtype)
    w1e = jnp.einsum('kmldj,gh->kmlgdhj', w1r, eye)
    w1e = w1e.reshape(2, r, NSA_CMP_STRIDE * NSA_KVW, NSA_GROUPS * hid)
    w2e = jnp.einsum('kjd,gh->kgjhd', w2, eye).reshape(2, NSA_GROUPS * hid, NSA_KVW)
    pe_e = jnp.broadcast_to(pe.reshape(2, r, NSA_CMP_STRIDE, 1, HEAD_DIM),
                            (2, r, NSA_CMP_STRIDE, NSA_GROUPS, HEAD_DIM))
    return pe_e.reshape(2, r, NSA_CMP_STRIDE * NSA_KVW), w1e.astype(MXU_DTYPE), w2e.astype(MXU_DTYPE)


def _overlap_matrix_t(ncp, seq):
    c = np.arange(ncp)[None, :] * NSA_CMP_STRIDE
    j = np.arange(LANES)[:, None] * NSA_SEL_LEN
    ov = (c < j + NSA_SEL_LEN) & (c + NSA_CMP_LEN - 1 >= j) & (j < seq)
    ov &= (np.arange(ncp)[None, :] < ncp - (NSA_CMP_LEN // NSA_CMP_STRIDE - 1))
    return jnp.asarray(ov, dtype=MXU_DTYPE)


def _block_onehot(seq):
    return jnp.asarray(np.arange(seq)[:, None] // NSA_SEL_LEN == np.arange(LANES)[None, :], dtype=MXU_DTYPE)


def _key_tiles_t(v, tile):
    b, s, w = v.shape
    return jnp.swapaxes(v.reshape(b, s // tile, tile, w), 2, 3)


def _group_values_t(v, tile):
    b, s, _ = v.shape
    vt = _key_tiles_t(v, tile).reshape(b, s // tile, NSA_GROUPS, HEAD_DIM, tile)
    vt = jnp.swapaxes(vt, 1, 2)
    pad = jnp.zeros((b, NSA_GROUPS, s // tile, LANES - HEAD_DIM, tile), v.dtype).at[:, :, :, 0, :].set(1)
    return jnp.concatenate([vt, pad], axis=3)


def kernel(x, ffn1_norm, ffn1_w_gu, ffn1_w_down, mix_norm, w_in, w_gate, nsa_q_norm, nsa_k_norm,
           nsa_cmp_pe, nsa_cmp_w1, nsa_cmp_w2, dsa_q_norm, dsa_k_norm, w_br_a, w_br_b, w_br_c, w_out,
           ffn2_norm, ffn2_w_gu, ffn2_w_down):
    b, s, d = x.shape
    depth = w_in.shape[0]
    n = b * s
    ncp = s // NSA_CMP_STRIDE
    assert s // NSA_SEL_LEN <= LANES and s % 1024 == 0
    top_k = min(DSA_TOPK, s // 4)
    cast = lambda w: w.astype(MXU_DTYPE)
    tile2 = lambda v, reps: jnp.tile(v, reps)[None, :]

    pos = jnp.arange(s)
    rope_main = _rope_tables(pos, ROT_DIM, HEAD_DIM)
    rope_idx = _rope_tables(pos, DSA_IDX_ROT, DSA_IDX_DIM)
    rope_cmp = _rope_tables(jnp.arange(ncp) * NSA_CMP_STRIDE + NSA_CMP_LEN - 1, ROT_DIM, HEAD_DIM)
    overlap_t = _overlap_matrix_t(ncp, s)
    onehot = _block_onehot(s)

    x2 = x.reshape(n, d)
    for l in range(depth):
        x2 = _ffn(x2, ffn1_norm[l][None, :], cast(ffn1_w_gu[l]), cast(ffn1_w_down[l]), tm=1024, tf=256)

        (sb, nq, nkc, nvc, nkv, ng, dq, dkv, iq, ik, iw) = _proj(
            x2, mix_norm[l][None, :], _relayout_w_in(w_in[l]),
            tile2(nsa_q_norm[l], NSA_HEADS),
            jnp.stack([jnp.tile(nsa_k_norm[l, 1], NSA_GROUPS), jnp.tile(nsa_k_norm[l, 2], NSA_GROUPS)]),
            tile2(dsa_q_norm[l], DSA_HEADS), tile2(dsa_k_norm[l], 2),
            rope_main, rope_idx, tm=256, seq=s)
        r3 = lambda t: t.reshape(b, s, t.shape[-1])

        pe_e, w1e, w2e = _compress_weights(nsa_cmp_pe[l], nsa_cmp_w1[l], nsa_cmp_w2[l])
        kc, vc = _compress(nkc.reshape(b, ncp, NSA_CMP_STRIDE * NSA_KVW),
                           nvc.reshape(b, ncp, NSA_CMP_STRIDE * NSA_KVW),
                           pe_e, w1e, w2e, tile2(nsa_k_norm[l, 0], NSA_GROUPS), rope_cmp)

        o_a = _sb_attention(r3(sb), tq=256)

        nkv = r3(nkv)
        ks, vs, kw, vw = (nkv[:, :, c * LANES:(c + 1) * LANES] for c in range(4))
        o_b = _nsa_attention(r3(nq), r3(ng), kc, _group_values_t(vc, ncp)[:, :, 0], ks,
                             _group_values_t(vs, TKS), kw, _group_values_t(vw, TQ), overlap_t, onehot,
                             tq=TQ, tks=TKS)

        dkv = r3(dkv)
        iwt = jnp.swapaxes(r3(iw)[:, :, :DSA_IDX_HEADS], 1, 2)
        dkvt = _key_tiles_t(dkv, TKS).at[:, :, 0, :].set(1)
        o_c = _dsa_attention(r3(iq), iwt, r3(ik), r3(dq), dkv, dkvt, tq=TQ, tk=TKS, top_k=top_k)

        x2 = _merge(x2, mix_norm[l][None, :], o_a.reshape(n, SB_W), o_b.reshape(n, NSA_QW),
                    o_c.reshape(n, DSA_QW), cast(w_gate[l]), cast(w_br_a[l]), cast(w_br_b[l]),
                    cast(w_br_c[l]), cast(w_out[l]), tm=256)

        x2 = _ffn(x2, ffn2_norm[l][None, :], cast(ffn2_w_gu[l]), cast(ffn2_w_down[l]), tm=1024, tf=256)
    return x2.reshape(b, s, d)
```

```python
import functools

import numpy as np
import jax
import jax.numpy as jnp
from jax import lax
from jax.experimental import pallas as pl
from jax.experimental.pallas import tpu as pltpu

HEAD_DIM = 64
ROT_DIM = HEAD_DIM // 4
ROPE_THETA = 500000.0
NORM_EPS = 1e-6

SB_HEADS = 4
NSA_HEADS = 8
NSA_GROUPS = 2
NSA_HPG = NSA_HEADS // NSA_GROUPS
NSA_CMP_LEN = 32
NSA_CMP_STRIDE = 16
NSA_CMP_HIDDEN = 2 * HEAD_DIM
NSA_SEL_LEN = 64
NSA_SEL_N = 16
NSA_WINDOW = 512
NSA_FORCED_SCORE = 1.0e4
DSA_HEADS = 4
DSA_IDX_HEADS = 8
DSA_IDX_DIM = 32
DSA_IDX_ROT = DSA_IDX_DIM // 4
DSA_TOPK = 256

SB_W = SB_HEADS * HEAD_DIM
NSA_QW = NSA_HEADS * HEAD_DIM
NSA_KVW = NSA_GROUPS * HEAD_DIM
DSA_QW = DSA_HEADS * HEAD_DIM
IDX_QW = DSA_IDX_HEADS * DSA_IDX_DIM

LANES = 128
VMEM_LIMIT = 56 * 1024 * 1024
MXU_DTYPE = jnp.bfloat16
NEG = -1.0e30
SB_SKIP = -120.0
INT_MIN = -2 ** 31
LOG2E = 1.4426950408889634

F32 = jnp.float32

TQ = 256
TQ_NSA = 512
TKS = 512
FFN_CHUNK = 512
FFN_TM = 512
assert TKS % TQ == 0 and TKS % TQ_NSA == 0 and NSA_WINDOW % TQ_NSA == 0


def _dot(a, b):
    return jnp.dot(a, b, preferred_element_type=F32)


def _dot_nt(a, b):
    return lax.dot_general(a, b, (((1,), (1,)), ((), ())), preferred_element_type=F32)


def _params(n_axes):
    return pltpu.CompilerParams(dimension_semantics=("arbitrary",) * n_axes,
                                vmem_limit_bytes=VMEM_LIMIT)


def _lane_iota(shape):
    return lax.broadcasted_iota(jnp.int32, shape, len(shape) - 1)


def _row_iota(shape):
    return lax.broadcasted_iota(jnp.int32, shape, len(shape) - 2)


def _swap_halves(x):
    return pltpu.roll(x, LANES // 2, 1)


def _head_rms(ys):
    lo = _lane_iota(ys.shape) < HEAD_DIM
    sq = ys * ys
    s_lo = jnp.sum(jnp.where(lo, sq, 0.0), axis=-1, keepdims=True)
    s_hi = jnp.sum(jnp.where(lo, 0.0, sq), axis=-1, keepdims=True)
    ms = jnp.where(lo, s_lo, s_hi) * (1.0 / HEAD_DIM)
    return ys * lax.rsqrt(ms + NORM_EPS)


def _rope(ys, c, sm, sp, half):
    return ys * c + pltpu.roll(ys, LANES - half, 1) * sm + pltpu.roll(ys, half, 1) * sp


def _ffn_kernel(x_ref, g_ref, wg_ref, wu_ref, wd_ref, o_ref, h_ref, acc_ref):
    f = pl.program_id(1)

    @pl.when(f == 0)
    def _():
        x = x_ref[...]
        ms = jnp.mean(x * x, axis=-1, keepdims=True)
        h_ref[...] = (x * lax.rsqrt(ms + NORM_EPS) * g_ref[...]).astype(h_ref.dtype)
        acc_ref[...] = jnp.zeros_like(acc_ref)

    h = h_ref[...]
    tf = wd_ref.shape[0]
    bounds = [(c, min(c + FFN_CHUNK, tf)) for c in range(0, tf, FFN_CHUNK)]

    def gate_up(c0, c1):
        return _dot(h, wg_ref[:, c0:c1]), _dot(h, wu_ref[:, c0:c1])

    ahead = gate_up(*bounds[0])
    for idx, (c0, c1) in enumerate(bounds):
        gate, up = ahead
        if idx + 1 < len(bounds):
            ahead = gate_up(*bounds[idx + 1])
        act = (gate * jax.nn.sigmoid(gate) * up).astype(h_ref.dtype)
        acc_ref[...] += _dot(act, wd_ref[c0:c1, :])

    @pl.when(f == pl.num_programs(1) - 1)
    def _():
        o_ref[...] = x_ref[...] + 0.5 * acc_ref[...]


def _ffn(x2, g, w_gu, w_down, *, tm, tf):
    n, d = x2.shape
    d_ff = w_down.shape[0]
    nf = d_ff // tf
    return pl.pallas_call(
        _ffn_kernel,
        grid=(n // tm, nf),
        in_specs=[
            pl.BlockSpec((tm, d), lambda i, f: (i, 0)),
            pl.BlockSpec((1, d), lambda i, f: (0, 0)),
            pl.BlockSpec((d, tf), lambda i, f: (0, f)),
            pl.BlockSpec((d, tf), lambda i, f: (0, f + nf)),
            pl.BlockSpec((tf, d), lambda i, f: (f, 0)),
        ],
        out_specs=pl.BlockSpec((tm, d), lambda i, f: (i, 0)),
        out_shape=jax.ShapeDtypeStruct((n, d), F32),
        scratch_shapes=[pltpu.VMEM((tm, d), MXU_DTYPE), pltpu.VMEM((tm, d), F32)],
        compiler_params=_params(2),
        name="ffn",
    )(x2, g, w_gu, w_gu, w_down)


_P_SB = 0
_P_NQ = _P_SB + 3 * SB_W
_P_NCV = _P_NQ + NSA_QW
_P_NKV = _P_NCV + 2 * NSA_KVW
_P_NG = _P_NKV + 4 * NSA_KVW
_P_DQ = _P_NG + 3 * NSA_QW
_P_DKV = _P_DQ + DSA_QW
_P_IQ = _P_DKV + LANES
_P_IK = _P_IQ + IDX_QW
_P_IW = _P_IK + LANES
_P_END = _P_IW + LANES


def _proj_kernel(x_ref, g_ref, w_ref, nq_g_ref, nk_g_ref, dq_g_ref, dk_g_ref,
                 rc_ref, rm_ref, rp_ref, ic_ref, im_ref, ip_ref,
                 sb_ref, nq_ref, nkc_ref, nvc_ref, nkv_ref, ng_ref, dq_ref, dkv_ref,
                 iq_ref, ik_ref, iw_ref):
    x = x_ref[...]
    ms = jnp.mean(x * x, axis=-1, keepdims=True)
    h = (x * lax.rsqrt(ms + NORM_EPS) * g_ref[...]).astype(MXU_DTYPE)
    rc, rm, rp = rc_ref[...], rm_ref[...], rp_ref[...]
    ic, im, ip = ic_ref[...], im_ref[...], ip_ref[...]
    half, ihalf = ROT_DIM // 2, DSA_IDX_ROT // 2
    scale = HEAD_DIM ** -0.5
    scale2 = scale * LOG2E

    def cols(a, width):
        return _dot(h, w_ref[:, a:a + width])

    def slab(y, s):
        return y[:, s * LANES:(s + 1) * LANES]

    y = cols(_P_SB, 3 * SB_W)
    sb_ref[:, 0:SB_W] = (y[:, 0:SB_W] * scale).astype(sb_ref.dtype)
    sb_ref[:, SB_W:3 * SB_W] = y[:, SB_W:3 * SB_W].astype(sb_ref.dtype)

    y = cols(_P_NQ, NSA_QW)
    for s in range(NSA_QW // LANES):
        ys = _head_rms(slab(y, s)) * slab(nq_g_ref[...], s)
        nq_ref[:, s * LANES:(s + 1) * LANES] = (_rope(ys, rc, rm, rp, half) * scale2).astype(nq_ref.dtype)

    y = cols(_P_NCV, 2 * NSA_KVW)
    nkc_ref[...] = slab(y, 0)
    nvc_ref[...] = slab(y, 1)

    y = cols(_P_NKV, 4 * NSA_KVW)
    for s, gi in ((0, 0), (2, 1)):
        ys = _head_rms(slab(y, s)) * nk_g_ref[gi:gi + 1, :]
        nkv_ref[:, s * LANES:(s + 1) * LANES] = _rope(ys, rc, rm, rp, half).astype(nkv_ref.dtype)
    for s in (1, 3):
        nkv_ref[:, s * LANES:(s + 1) * LANES] = slab(y, s).astype(nkv_ref.dtype)

    ng_ref[...] = jax.nn.sigmoid(cols(_P_NG, 3 * NSA_QW))

    y = cols(_P_DQ, DSA_QW)
    for s in range(DSA_QW // LANES):
        ys = _head_rms(slab(y, s)) * slab(dq_g_ref[...], s)
        dq_ref[:, s * LANES:(s + 1) * LANES] = (_rope(ys, rc, rm, rp, half) * scale2).astype(dq_ref.dtype)

    y = cols(_P_DKV, LANES)
    yk = _rope(_head_rms(y) * dk_g_ref[...], rc, rm, rp, half)
    dkv_ref[...] = jnp.where(_lane_iota(y.shape) < HEAD_DIM, yk, y).astype(dkv_ref.dtype)

    y = cols(_P_IQ, IDX_QW)
    for s in range(IDX_QW // LANES):
        iq_ref[:, s * LANES:(s + 1) * LANES] = _rope(slab(y, s), ic, im, ip, ihalf).astype(iq_ref.dtype)
    ik_ref[...] = _rope(cols(_P_IK, LANES), ic, im, ip, ihalf).astype(ik_ref.dtype)
    iw_ref[...] = cols(_P_IW, LANES) * (DSA_IDX_HEADS ** -0.5) * (DSA_IDX_DIM ** -0.5)


def _proj(x2, g, w, nq_g, nk_g, dq_g, dk_g, rope_main, rope_idx, *, tm, seq):
    n, d = x2.shape
    nt = seq // tm
    row = lambda i: (i, 0)
    const = lambda i: (0, 0)
    pos = lambda i: (i % nt, 0)
    widths = [(3 * SB_W, MXU_DTYPE), (NSA_QW, MXU_DTYPE), (NSA_KVW, F32), (NSA_KVW, F32),
              (4 * NSA_KVW, MXU_DTYPE), (3 * NSA_QW, F32), (DSA_QW, MXU_DTYPE), (LANES, MXU_DTYPE),
              (IDX_QW, MXU_DTYPE), (LANES, MXU_DTYPE), (LANES, F32)]
    return pl.pallas_call(
        _proj_kernel,
        grid=(n // tm,),
        in_specs=[pl.BlockSpec((tm, d), row), pl.BlockSpec((1, d), const),
                  pl.BlockSpec((d, _P_END), const),
                  pl.BlockSpec((1, NSA_QW), const), pl.BlockSpec((2, LANES), const),
                  pl.BlockSpec((1, DSA_QW), const), pl.BlockSpec((1, LANES), const)]
                 + [pl.BlockSpec((tm, LANES), pos)] * 6,
        out_specs=[pl.BlockSpec((tm, wd), row) for wd, _ in widths],
        out_shape=[jax.ShapeDtypeStruct((n, wd), dt) for wd, dt in widths],
        compiler_params=_params(1),
        name="proj",
    )(x2, g, w, nq_g, nk_g, dq_g, dk_g, *rope_main, *rope_idx)


def _compress_kernel(xk_ref, xv_ref, pe_ref, w1_ref, w2_ref, g_ref, rc_ref, rm_ref, rp_ref,
                     kc_ref, vc_ref):
    ncp = xk_ref.shape[1]

    def compress(x, kv):
        a0 = _dot((x + pe_ref[kv, 0:1, :]).astype(MXU_DTYPE), w1_ref[kv, 0])
        a1 = _dot((x + pe_ref[kv, 1:2, :]).astype(MXU_DTYPE), w1_ref[kv, 1])
        pre = a0 + pltpu.roll(a1, ncp - 1, 0)
        hid = pre * jax.nn.sigmoid(pre)
        return _dot(hid.astype(MXU_DTYPE), w2_ref[kv])

    yk = _head_rms(compress(xk_ref[0], 0)) * g_ref[...]
    kc_ref[0] = _rope(yk, rc_ref[...], rm_ref[...], rp_ref[...], ROT_DIM // 2).astype(kc_ref.dtype)
    vc_ref[0] = compress(xv_ref[0], 1).astype(vc_ref.dtype)


def _compress(xk, xv, pe, w1, w2, g, rope_c):
    b, ncp, wide = xk.shape
    c3 = lambda i: (0, 0, 0)
    c4 = lambda i: (0, 0, 0, 0)
    c2 = lambda i: (0, 0)
    per_b = lambda i: (i, 0, 0)
    return pl.pallas_call(
        _compress_kernel,
        grid=(b,),
        in_specs=[pl.BlockSpec((1, ncp, wide), per_b), pl.BlockSpec((1, ncp, wide), per_b),
                  pl.BlockSpec(pe.shape, c3), pl.BlockSpec(w1.shape, c4), pl.BlockSpec(w2.shape, c3),
                  pl.BlockSpec((1, LANES), c2)] + [pl.BlockSpec((ncp, LANES), c2)] * 3,
        out_specs=[pl.BlockSpec((1, ncp, LANES), per_b)] * 2,
        out_shape=[jax.ShapeDtypeStruct((b, ncp, LANES), MXU_DTYPE)] * 2,
        compiler_params=_params(1),
        name="compress",
    )(xk, xv, pe, w1, w2, g, *rope_c)


def _sb_kernel(q_ref, k_ref, v_ref, o_ref, *, tq):
    i = pl.program_id(2)
    q0 = i * tq
    q = q_ref[0].astype(F32)
    lane = _lane_iota(q.shape)
    rows = _row_iota((tq, tq))
    colsq = _lane_iota((tq, tq))
    upper = jnp.where(_row_iota((tq, tq + LANES)) > _lane_iota((tq, tq + LANES)), 1.0,
                      jnp.where(_lane_iota((tq, tq + LANES)) >= tq, 1.0, 0.0)).astype(MXU_DTYPE)
    in_head = [(lane >> 6) == p for p in range(2)]
    qh = [jnp.where(m, q, 0.0).astype(MXU_DTYPE) for m in in_head]
    reps = tq // LANES

    def tile(kt, carries, accs, diagonal):
        ks = k_ref[0, pl.ds(kt * tq, tq), :]
        vs = v_ref[0, pl.ds(kt * tq, tq), :]
        zs = [_dot_nt(qh[p], ks) for p in range(2)]
        new_c, new_a = [], []
        for p in range(2):
            z = zs[p]
            l = -(jnp.maximum(z, 0.0) + jnp.log1p(jnp.exp(-jnp.abs(z))))
            if diagonal:
                past = colsq < rows
                l = jnp.where(past, l, 0.0)
            hi = l.astype(MXU_DTYPE)
            lo = (l - hi.astype(F32)).astype(MXU_DTYPE)
            sums = _dot(hi, upper) + _dot(lo, upper)
            tail = sums[:, :tq] + jnp.concatenate([carries[p]] * reps, axis=1)
            w = jnp.exp(z + l + tail)
            if diagonal:
                w = jnp.where(past, w, 0.0)
            new_a.append(accs[p] + _dot(w.astype(MXU_DTYPE), vs))
            new_c.append(carries[p] + sums[:, tq:])
        return tuple(new_c), tuple(new_a)

    zeros = (jnp.zeros((tq, LANES), F32),) * 2
    carries, accs = tile(i, zeros, zeros, True)

    def cond(st):
        kt, carries, _ = st
        return (kt >= 0) & (jnp.max(jnp.maximum(carries[0], carries[1])) > SB_SKIP)

    def body(st):
        kt, carries, accs = st
        carries, accs = tile(kt, carries, accs, False)
        return kt - 1, carries, accs

    _, _, accs = lax.while_loop(cond, body, (i - 1, carries, accs))
    del q0
    o_ref[0] = jnp.where(in_head[0], accs[0], accs[1]).astype(o_ref.dtype)


def _sb_attention(sb, *, tq):
    b, s, _ = sb.shape
    nslab = SB_W // LANES
    return pl.pallas_call(
        functools.partial(_sb_kernel, tq=tq),
        grid=(b, nslab, s // tq),
        in_specs=[pl.BlockSpec((1, tq, LANES), lambda bi, sl, i: (bi, i, sl)),
                  pl.BlockSpec((1, s, LANES), lambda bi, sl, i: (bi, 0, nslab + sl)),
                  pl.BlockSpec((1, s, LANES), lambda bi, sl, i: (bi, 0, 2 * nslab + sl))],
        out_specs=pl.BlockSpec((1, tq, LANES), lambda bi, sl, i: (bi, i, sl)),
        out_shape=jax.ShapeDtypeStruct((b, s, SB_W), MXU_DTYPE),
        compiler_params=_params(3),
        name="sb_attn",
    )(sb, sb, sb)


def _group_queries(q_ref, g, tq):
    lane = _lane_iota((tq, LANES))
    in_g = (lane >> 6) == g
    qa = []
    for s in range(2):
        qs = q_ref[0, :, s * LANES:(s + 1) * LANES].astype(F32)
        for p in range(2):
            mine = jnp.where((lane >> 6) == p, qs, 0.0)
            both = mine + _swap_halves(mine)
            qa.append(jnp.where(in_g, both, 0.0).astype(MXU_DTYPE))
    return qa


def _fold_rows(x, op, rows=64):
    while x.shape[0] > rows and x.shape[0] % 16 == 0:
        half = x.shape[0] // 2
        x = op(x[:half], x[half:])
    return x


def _key_max(x):
    return jnp.max(_fold_rows(x, jnp.maximum), axis=0, keepdims=True)


def _key_sum(x):
    return jnp.sum(_fold_rows(x, jnp.add), axis=0, keepdims=True)


def _online_update(m, acc_ref, s, v_t):
    m_new = jnp.maximum(m, _key_max(s))
    p = jnp.exp2(s - m_new)
    acc_ref[...] = jnp.exp2(m - m_new) * acc_ref[...] + _dot(v_t, p.astype(MXU_DTYPE))
    return m_new


def _attend_tiles(n_tiles, scores_fn, softmax_fn, ms, fix_last):
    def step(tiles, ms, last):
        scores = [scores_fn(kt) for kt in tiles]
        if last:
            scores[-1] = fix_last(scores[-1])
        for kt, sc in zip(tiles, scores):
            ms = softmax_fn(kt, ms, sc)
        return ms

    plain = n_tiles - 1
    ms = lax.fori_loop(0, plain // 2, lambda p, ms: step([2 * p, 2 * p + 1], ms, False), ms)
    return lax.cond(plain % 2 == 1,
                    lambda ms: step([n_tiles - 2, n_tiles - 1], ms, True),
                    lambda ms: step([n_tiles - 1], ms, True), ms)


def _attend_pipelined(n_tiles, scores_fn, softmax_fn, ms, fix_last, s_ref):
    heads = s_ref.shape[1]

    def issue(slot, kt):
        for h, sc in enumerate(scores_fn(kt)):
            s_ref[slot, h] = sc

    def fold(slot, kt, ms):
        return softmax_fn(kt, ms, tuple(s_ref[slot, h] for h in range(heads)))

    plain = n_tiles - 1

    @pl.when(plain > 0)
    def _():
        issue(0, 0)

    def body(p, ms):
        issue(1, 2 * p + 1)
        ms = fold(0, 2 * p, ms)
        issue(0, jnp.minimum(2 * p + 2, plain - 1))
        return fold(1, 2 * p + 1, ms)

    ms = lax.fori_loop(0, plain // 2, body, ms)
    ms = lax.cond(plain % 2 == 1, lambda ms: fold(0, plain - 1, ms), lambda ms: ms, ms)
    return softmax_fn(n_tiles - 1, ms, fix_last(scores_fn(n_tiles - 1)))


def _online_init(tq, acc_ref):
    acc_ref[...] = jnp.zeros(acc_ref.shape, F32)
    return tuple(jnp.full((1, tq), NEG, F32) for _ in range(acc_ref.shape[0]))


def _nsa_kernel(q_ref, gc_ref, gs_ref, gw_ref, kc_ref, vct_ref, ks_ref, vst_ref, kw_ref, vwt_ref,
                ovt_ref, oh_ref, o_ref, qaug_ref, acc_ref, s_ref, *, tq, tks):
    g = pl.program_id(1)
    i = pl.program_id(2)
    q0 = i * tq
    ncp = kc_ref.shape[1]
    qa = _group_queries(q_ref, g, tq)

    c_vis = (_row_iota((ncp, tq)) * NSA_CMP_STRIDE + (NSA_CMP_LEN - 1)) <= q0 + _lane_iota((ncp, tq))
    kc = kc_ref[0]
    vct = vct_ref[0, 0]
    pc_sum = jnp.zeros((ncp, tq), F32)
    o_cmp = []
    scores = [_dot_nt(kc, qa[j]) for j in range(NSA_HPG)]
    c_bias = jnp.where(c_vis, 0.0, NEG)
    for j in range(NSA_HPG):
        s = scores[j] + c_bias
        p = jnp.exp2(s - jnp.maximum(_key_max(s), 0.5 * NEG))
        den = _key_sum(p)
        p = p * (1.0 / jnp.where(den > 0.0, den, 1.0))
        pc_sum = pc_sum + p
        o_cmp.append(_dot(vct, p.astype(MXU_DTYPE))[:HEAD_DIM])

    band = tq + NSA_WINDOW
    start = pl.multiple_of(jnp.maximum(q0 - NSA_WINDOW, 0), tq)
    kw = kw_ref[0, pl.ds(start, band), :]
    s_w = start + _row_iota((band, tq))
    t_w = q0 + _lane_iota((band, tq))
    w_bias = jnp.where(s_w <= t_w, jnp.where(s_w > t_w - NSA_WINDOW, 0.0, NEG), NEG)
    o_win = []
    scores = [_dot_nt(kw, qa[j]) for j in range(NSA_HPG)]
    for j in range(NSA_HPG):
        s = scores[j] + w_bias
        p = jnp.exp2(s - _key_max(s)).astype(MXU_DTYPE)
        acc = jnp.zeros((LANES, tq), F32)
        for c in range(band // tq):
            acc = acc + _dot(vwt_ref[0, 0, start // tq + c], p[c * tq:(c + 1) * tq, :])
        o_win.append(acc[:HEAD_DIM] * (1.0 / acc[HEAD_DIM:HEAD_DIM + 1]))

    hi = pc_sum.astype(MXU_DTYPE)
    lo = (pc_sum - hi.astype(F32)).astype(MXU_DTYPE)
    imp = _dot(ovt_ref[...], hi) + _dot(ovt_ref[...], lo)
    blk = _row_iota((LANES, tq))
    t_b = q0 + _lane_iota((LANES, tq))
    cur = t_b >> 6
    visible = blk * NSA_SEL_LEN <= t_b
    forced = (blk == 0) | (blk == cur) | (blk == cur - 1)
    taken = -3.0e38
    work = jnp.where(visible, jnp.where(forced, taken, imp), NEG)
    blk_f = blk.astype(F32)
    for _ in range(NSA_SEL_N - 3):
        m = jnp.max(work, axis=0, keepdims=True)
        first = jnp.min(jnp.where(work == m, blk_f, float(LANES)), axis=0, keepdims=True)
        work = jnp.where(blk_f == first, taken, work)
    sel_bias = jnp.where(visible, jnp.where(work == taken, 0.0, NEG), NEG).T.astype(MXU_DTYPE)

    n_tiles = (q0 + tq + tks - 1) // tks
    t_s = q0 + _lane_iota((tks, tq))
    row_s = _row_iota((tks, tq))
    for j in range(NSA_HPG):
        qaug_ref[j] = jnp.concatenate([qa[j], sel_bias], axis=1)

    def sel_scores(kt):
        k_aug = jnp.concatenate([ks_ref[0, pl.ds(kt * tks, tks), :], oh_ref[pl.ds(kt * tks, tks), :]], axis=1)
        return tuple(_dot_nt(k_aug, qaug_ref[j]) for j in range(NSA_HPG))

    def sel_softmax(kt, ms, scores):
        v_t = vst_ref[0, 0, kt]
        return tuple(_online_update(ms[j], acc_ref.at[j], scores[j], v_t) for j in range(NSA_HPG))

    def sel_causal(scores):
        causal = jnp.where((n_tiles - 1) * tks + row_s <= t_s, 0.0, NEG)
        return tuple(s + causal for s in scores)

    _attend_pipelined(n_tiles, sel_scores, sel_softmax, _online_init(tq, acc_ref), sel_causal, s_ref)
    o_sel = [acc_ref[j, :HEAD_DIM, :] * (1.0 / acc_ref[j, HEAD_DIM:HEAD_DIM + 1, :]) for j in range(NSA_HPG)]

    for s in range(2):
        tot = jnp.zeros((tq, LANES), F32)
        for branch, gate_ref in ((o_cmp, gc_ref), (o_sel, gs_ref), (o_win, gw_ref)):
            pair = jnp.concatenate([branch[2 * s], branch[2 * s + 1]], axis=0)
            tot = tot + gate_ref[0, :, s * LANES:(s + 1) * LANES] * pair.T
        o_ref[0, :, s * LANES:(s + 1) * LANES] = tot.astype(o_ref.dtype)


def _nsa_attention(nq, ng, kc, vct, ks, vst, kw, vwt, overlap_t, onehot, *, tq, tks):
    b, s, _ = nq.shape
    ncp = kc.shape[1]
    gw = NSA_HPG * HEAD_DIM
    per_b3 = lambda bi, g, i: (bi, 0, 0)
    per_bg4 = lambda bi, g, i: (bi, g, 0, 0)
    per_bg5 = lambda bi, g, i: (bi, g, 0, 0, 0)
    const = lambda bi, g, i: (0, 0)
    return pl.pallas_call(
        functools.partial(_nsa_kernel, tq=tq, tks=tks),
        grid=(b, NSA_GROUPS, s // tq),
        in_specs=[pl.BlockSpec((1, tq, gw), lambda bi, g, i: (bi, i, g)),
                  pl.BlockSpec((1, tq, gw), lambda bi, g, i: (bi, i, g)),
                  pl.BlockSpec((1, tq, gw), lambda bi, g, i: (bi, i, NSA_GROUPS + g)),
                  pl.BlockSpec((1, tq, gw), lambda bi, g, i: (bi, i, 2 * NSA_GROUPS + g)),
                  pl.BlockSpec((1, ncp, LANES), per_b3),
                  pl.BlockSpec((1, 1) + vct.shape[2:], per_bg4),
                  pl.BlockSpec((1, s, LANES), per_b3),
                  pl.BlockSpec((1, 1) + vst.shape[2:], per_bg5),
                  pl.BlockSpec((1, s, LANES), per_b3),
                  pl.BlockSpec((1, 1) + vwt.shape[2:], per_bg5),
                  pl.BlockSpec(overlap_t.shape, const),
                  pl.BlockSpec(onehot.shape, const)],
        out_specs=pl.BlockSpec((1, tq, gw), lambda bi, g, i: (bi, i, g)),
        out_shape=jax.ShapeDtypeStruct((b, s, NSA_QW), MXU_DTYPE),
        scratch_shapes=[pltpu.VMEM((NSA_HPG, tq, 2 * LANES), MXU_DTYPE),
                        pltpu.VMEM((NSA_HPG, LANES, tq), F32),
                        pltpu.VMEM((2, NSA_HPG, tks, tq), F32)],
        compiler_params=_params(3),
        name="nsa_attn",
    )(nq, ng, ng, ng, kc, vct, ks, vst, kw, vwt, overlap_t, onehot)


def _dsa_kernel(iq_ref, iwt_ref, ik_ref, q_ref, kv_ref, kvt_ref, o_ref, khi_ref, klo_ref, bias_ref, qa_ref,
                acc_ref, s_ref, *, tq, tk, top_k):
    i = pl.program_id(1)
    q0 = i * tq
    n_tiles = (q0 + tq + tk - 1) // tk
    lane = _lane_iota((tq, LANES))
    t_q = q0 + _lane_iota((tk, tq))
    row_k = _row_iota((tk, tq))

    iqa = []
    for s in range(IDX_QW // LANES):
        qs = iq_ref[0, :, s * LANES:(s + 1) * LANES].astype(F32)
        for p in range(LANES // DSA_IDX_DIM):
            mine = jnp.where((lane >> 5) == p, qs, 0.0)
            if p:
                mine = pltpu.roll(mine, LANES - p * DSA_IDX_DIM, 1)
            iqa.append(mine.astype(MXU_DTYPE))
    w_h = [iwt_ref[0, h:h + 1, :] for h in range(DSA_IDX_HEADS)]

    def score_body(kt, _, diagonal):
        ik = ik_ref[0, pl.ds(kt * tk, tk), :]
        score = jnp.zeros((tk, tq), F32)
        for h in range(DSA_IDX_HEADS):
            score = score + w_h[h] * jnp.maximum(_dot_nt(ik, iqa[h]), 0.0)
        score = jnp.where(score == 0.0, 0.0, score)
        bits = pltpu.bitcast(score, jnp.int32)
        key = jnp.where(bits < 0, bits ^ 0x7FFFFFFF, bits)
        if diagonal:
            key = jnp.where(kt * tk + row_k <= t_q, key, INT_MIN)
        khi_ref[kt] = (key >> 16).astype(jnp.int16)
        klo_ref[kt] = ((key & 0xFFFF) - 0x8000).astype(jnp.int16)
        return 0

    lax.fori_loop(0, n_tiles - 1, functools.partial(score_body, diagonal=False), 0)
    score_body(n_tiles - 1, 0, True)
    one, zero = jnp.ones((), jnp.bfloat16), jnp.zeros((), jnp.bfloat16)

    def count(flags):
        def body(kt, acc):
            return acc + _fold_rows(flags(kt), jnp.add, rows=32).astype(F32)
        acc = lax.fori_loop(0, n_tiles, body, jnp.zeros((32, tq), F32))
        return jnp.sum(acc, axis=0, keepdims=True)

    def kth_largest(ref):
        def body(b, ans_u):
            cand_u = ans_u | lax.shift_left(jnp.int32(1), 15 - b)
            cand = (cand_u - 0x8000).astype(jnp.int16)
            cnt = count(lambda kt: jnp.where(ref[kt] >= cand, one, zero))
            return jnp.where(cnt >= top_k, cand_u, ans_u)
        ans_u = lax.fori_loop(0, 16, body, jnp.zeros((1, tq), jnp.int32))
        return ans_u - 0x8000

    t_hi32 = kth_largest(khi_ref)
    t_hi = t_hi32.astype(jnp.int16)

    def pin_body(kt, _):
        hi = khi_ref[kt]
        klo_ref[kt] = jnp.where(hi == t_hi, klo_ref[kt],
                                jnp.where(hi > t_hi, jnp.int16(32767), jnp.int16(-32768)))
        return 0

    lax.fori_loop(0, n_tiles, pin_body, 0)
    t_lo32 = kth_largest(klo_ref)
    t_lo = jnp.where(t_hi32 == -0x8000, jnp.maximum(t_lo32, 1 - 0x8000), t_lo32).astype(jnp.int16)

    def at_least(kt, yes, no):
        hi = khi_ref[kt]
        return jnp.where(hi > t_hi, yes, jnp.where(hi == t_hi, jnp.where(klo_ref[kt] >= t_lo, yes, no), no))

    def above(kt):
        return jnp.where(khi_ref[kt] > t_hi, one, jnp.where(klo_ref[kt] > t_lo, one, zero))

    bias_yes, bias_no = jnp.zeros((), bias_ref.dtype), jnp.full((), NEG, bias_ref.dtype)

    def plain_bias():
        def body(kt, _):
            bias_ref[kt] = at_least(kt, bias_yes, bias_no)
            return 0
        lax.fori_loop(0, n_tiles, body, 0)

    def ranked_bias():
        need = top_k - count(above)
        lower = jnp.where(_lane_iota((tk, tk)) < _row_iota((tk, tk)), 1.0, 0.0).astype(MXU_DTYPE)

        def body(kt, seen):
            tie = jnp.where(khi_ref[kt] == t_hi, jnp.where(klo_ref[kt] == t_lo, one, zero), zero)
            rank = _dot(lower, tie.astype(MXU_DTYPE)) + seen
            tie = tie.astype(F32)
            take = jnp.where(above(kt).astype(F32) > 0.0, 1.0, jnp.where(rank < need, tie, 0.0))
            bias_ref[kt] = ((take - 1.0) * (-NEG)).astype(bias_ref.dtype)
            return seen + _key_sum(tie)

        lax.fori_loop(0, n_tiles, body, jnp.zeros((1, tq), F32))

    reached = count(lambda kt: at_least(kt, one, zero))
    lax.cond(jnp.max(reached) > top_k, ranked_bias, plain_bias)

    for s in range(DSA_QW // LANES):
        qs = q_ref[0, :, s * LANES:(s + 1) * LANES].astype(F32)
        lo = jnp.where(lane < HEAD_DIM, qs, 0.0)
        hi = _swap_halves(jnp.where(lane < HEAD_DIM, 0.0, qs))
        qa_ref[2 * s] = lo.astype(MXU_DTYPE)
        qa_ref[2 * s + 1] = hi.astype(MXU_DTYPE)

    def att_scores(kt):
        kv = kv_ref[0, pl.ds(kt * tk, tk), :]
        bias = bias_ref[kt].astype(F32)
        return tuple(_dot_nt(kv, qa_ref[h]) + bias for h in range(DSA_HEADS))

    def att_softmax(kt, ms, scores):
        kv_t = kvt_ref[0, kt]
        return tuple(_online_update(ms[h], acc_ref.at[h], scores[h], kv_t) for h in range(DSA_HEADS))

    _attend_pipelined(n_tiles, att_scores, att_softmax, _online_init(tq, acc_ref), lambda scores: scores, s_ref)
    outs = [acc_ref[h, HEAD_DIM:, :] * (1.0 / acc_ref[h, 0:1, :]) for h in range(DSA_HEADS)]

    for s in range(DSA_QW // LANES):
        pair = jnp.concatenate([outs[2 * s], outs[2 * s + 1]], axis=0)
        o_ref[0, :, s * LANES:(s + 1) * LANES] = pair.T.astype(o_ref.dtype)


def _dsa_attention(iq, iwt, ik, dq, dkv, dkvt, *, tq, tk, top_k):
    b, s, _ = dq.shape
    tile = lambda width: pl.BlockSpec((1, tq, width), lambda bi, i: (bi, i, 0))
    full = pl.BlockSpec((1, s, LANES), lambda bi, i: (bi, 0, 0))
    return pl.pallas_call(
        functools.partial(_dsa_kernel, tq=tq, tk=tk, top_k=top_k),
        grid=(b, s // tq),
        in_specs=[tile(IDX_QW), pl.BlockSpec((1, DSA_IDX_HEADS, tq), lambda bi, i: (bi, 0, i)), full,
                  tile(DSA_QW), full, pl.BlockSpec((1,) + dkvt.shape[1:], lambda bi, i: (bi, 0, 0, 0))],
        out_specs=tile(DSA_QW),
        out_shape=jax.ShapeDtypeStruct((b, s, DSA_QW), MXU_DTYPE),
        scratch_shapes=[pltpu.VMEM((s // tk, tk, tq), jnp.int16), pltpu.VMEM((s // tk, tk, tq), jnp.int16),
                        pltpu.VMEM((s // tk, tk, tq), MXU_DTYPE),
                        pltpu.VMEM((DSA_HEADS, tq, LANES), MXU_DTYPE), pltpu.VMEM((DSA_HEADS, LANES, tq), F32),
                        pltpu.VMEM((2, DSA_HEADS, tk, tq), F32)],
        compiler_params=_params(2),
        name="dsa_attn",
    )(iq, iwt, ik, dq, dkv, dkvt)


def _merge_kernel(x_ref, g_ref, oa_ref, ob_ref, oc_ref, wg_ref, wa_ref, wb_ref, wc_ref, wo_ref, o_ref):
    x = x_ref[...]
    d = x.shape[1]
    ms = jnp.mean(x * x, axis=-1, keepdims=True)
    h = (x * lax.rsqrt(ms + NORM_EPS) * g_ref[...]).astype(MXU_DTYPE)
    merged = jnp.zeros(x.shape, F32)
    for r, (o_r, w_r) in enumerate(((oa_ref, wa_ref), (ob_ref, wb_ref), (oc_ref, wc_ref))):
        gate = jax.nn.sigmoid(_dot(h, wg_ref[:, r * d:(r + 1) * d]))
        merged = merged + gate * _dot(o_r[...], w_r[...])
    o_ref[...] = x + _dot(merged.astype(MXU_DTYPE), wo_ref[...])


def _merge(x2, g, oa, ob, oc, w_gate, wa, wb, wc, wo, *, tm):
    n, d = x2.shape
    row = lambda i: (i, 0)
    const = lambda i: (0, 0)
    return pl.pallas_call(
        _merge_kernel,
        grid=(n // tm,),
        in_specs=[pl.BlockSpec((tm, d), row), pl.BlockSpec((1, d), const),
                  pl.BlockSpec((tm, SB_W), row), pl.BlockSpec((tm, NSA_QW), row),
                  pl.BlockSpec((tm, DSA_QW), row),
                  pl.BlockSpec(w_gate.shape, const), pl.BlockSpec(wa.shape, const),
                  pl.BlockSpec(wb.shape, const), pl.BlockSpec(wc.shape, const),
                  pl.BlockSpec(wo.shape, const)],
        out_specs=pl.BlockSpec((tm, d), row),
        out_shape=jax.ShapeDtypeStruct((n, d), F32),
        compiler_params=_params(1),
        name="merge",
    )(x2, g, oa, ob, oc, w_gate, wa, wb, wc, wo)


def _rope_tables(pos, rot_dim, head_dim):
    half = rot_dim // 2
    inv = ROPE_THETA ** (-jnp.arange(0, rot_dim, 2, dtype=F32) / rot_dim)
    ang = pos.astype(F32)[:, None] * inv[None, :]
    cos, sin = jnp.cos(ang), jnp.sin(ang)
    n = pos.shape[0]
    rest = head_dim - rot_dim
    zero_h = jnp.zeros((n, half), F32)
    c = jnp.concatenate([cos, cos, jnp.ones((n, rest), F32)], axis=-1)
    sm = jnp.concatenate([-sin, zero_h, jnp.zeros((n, rest), F32)], axis=-1)
    sp = jnp.concatenate([zero_h, sin, jnp.zeros((n, rest), F32)], axis=-1)
    reps = LANES // head_dim
    return tuple(jnp.tile(t, (1, reps)) for t in (c, sm, sp))


def _pad_cols(w, width):
    return jnp.pad(w, ((0, 0), (0, width - w.shape[1])))


def _relayout_w_in(w_in):
    offs = np.cumsum((SB_W, SB_W, SB_W, NSA_QW) + (NSA_KVW,) * 6
                     + (3 * NSA_HEADS, DSA_QW, HEAD_DIM, HEAD_DIM, IDX_QW, DSA_IDX_DIM, DSA_IDX_HEADS))
    o_gate, o_dq, o_dk, o_dv, o_iq, o_ik, o_iw, o_end = offs[9:17].tolist()
    gate = w_in[:, o_gate:o_dq].reshape(-1, NSA_HEADS, 3)
    gate = jnp.repeat(jnp.transpose(gate, (0, 2, 1)), HEAD_DIM, axis=2)
    parts = [w_in[:, :o_gate], gate.reshape(w_in.shape[0], 3 * NSA_QW), w_in[:, o_dq:o_dk],
             w_in[:, o_dk:o_iq], w_in[:, o_iq:o_ik], _pad_cols(w_in[:, o_ik:o_iw], LANES),
             _pad_cols(w_in[:, o_iw:o_end], LANES)]
    w = jnp.concatenate(parts, axis=1)
    assert w.shape[1] == _P_END
    return w.astype(MXU_DTYPE)


def _compress_weights(pe, w1, w2):
    r = NSA_CMP_LEN // NSA_CMP_STRIDE
    hid = NSA_CMP_HIDDEN
    w1r = w1.reshape(2, r, NSA_CMP_STRIDE, HEAD_DIM, hid)
    eye = jnp.eye(NSA_GROUPS, dtype=w1.dtype)
    w1e = jnp.einsum('kmldj,gh->kmlgdhj', w1r, eye)
    w1e = w1e.reshape(2, r, NSA_CMP_STRIDE * NSA_KVW, NSA_GROUPS * hid)
    w2e = jnp.einsum('kjd,gh->kgjhd', w2, eye).reshape(2, NSA_GROUPS * hid, NSA_KVW)
    pe_e = jnp.broadcast_to(pe.reshape(2, r, NSA_CMP_STRIDE, 1, HEAD_DIM),
                            (2, r, NSA_CMP_STRIDE, NSA_GROUPS, HEAD_DIM))
    return pe_e.reshape(2, r, NSA_CMP_STRIDE * NSA_KVW), w1e.astype(MXU_DTYPE), w2e.astype(MXU_DTYPE)


def _overlap_matrix_t(ncp, seq):
    c = np.arange(ncp)[None, :] * NSA_CMP_STRIDE
    j = np.arange(LANES)[:, None] * NSA_SEL_LEN
    ov = (c < j + NSA_SEL_LEN) & (c + NSA_CMP_LEN - 1 >= j) & (j < seq)
    ov &= (np.arange(ncp)[None, :] < ncp - (NSA_CMP_LEN // NSA_CMP_STRIDE - 1))
    return jnp.asarray(ov, dtype=MXU_DTYPE)


def _block_onehot(seq):
    return jnp.asarray(np.arange(seq)[:, None] // NSA_SEL_LEN == np.arange(LANES)[None, :], dtype=MXU_DTYPE)


def _key_tiles_t(v, tile):
    b, s, w = v.shape
    return jnp.swapaxes(v.reshape(b, s // tile, tile, w), 2, 3)


def _group_values_t(v, tile):
    b, s, _ = v.shape
    vt = _key_tiles_t(v, tile).reshape(b, s // tile, NSA_GROUPS, HEAD_DIM, tile)
    vt = jnp.swapaxes(vt, 1, 2)
    pad = jnp.zeros((b, NSA_GROUPS, s // tile, LANES - HEAD_DIM, tile), v.dtype).at[:, :, :, 0, :].set(1)
    return jnp.concatenate([vt, pad], axis=3)


def kernel(x, ffn1_norm, ffn1_w_gu, ffn1_w_down, mix_norm, w_in, w_gate, nsa_q_norm, nsa_k_norm,
           nsa_cmp_pe, nsa_cmp_w1, nsa_cmp_w2, dsa_q_norm, dsa_k_norm, w_br_a, w_br_b, w_br_c, w_out,
           ffn2_norm, ffn2_w_gu, ffn2_w_down):
    b, s, d = x.shape
    depth = w_in.shape[0]
    n = b * s
    ncp = s // NSA_CMP_STRIDE
    assert s // NSA_SEL_LEN <= LANES and s % 1024 == 0
    top_k = min(DSA_TOPK, s // 4)
    ffn_tf = ffn1_w_down.shape[1]
    cast = lambda w: w.astype(MXU_DTYPE)
    tile2 = lambda v, reps: jnp.tile(v, reps)[None, :]

    pos = jnp.arange(s)
    rope_main = _rope_tables(pos, ROT_DIM, HEAD_DIM)
    rope_idx = _rope_tables(pos, DSA_IDX_ROT, DSA_IDX_DIM)
    rope_cmp = _rope_tables(jnp.arange(ncp) * NSA_CMP_STRIDE + NSA_CMP_LEN - 1, ROT_DIM, HEAD_DIM)
    overlap_t = _overlap_matrix_t(ncp, s)
    onehot = _block_onehot(s)

    x2 = x.reshape(n, d)
    for l in range(depth):
        x2 = _ffn(x2, ffn1_norm[l][None, :], cast(ffn1_w_gu[l]), cast(ffn1_w_down[l]), tm=FFN_TM, tf=ffn_tf)

        (sb, nq, nkc, nvc, nkv, ng, dq, dkv, iq, ik, iw) = _proj(
            x2, mix_norm[l][None, :], _relayout_w_in(w_in[l]),
            tile2(nsa_q_norm[l], NSA_HEADS),
            jnp.stack([jnp.tile(nsa_k_norm[l, 1], NSA_GROUPS), jnp.tile(nsa_k_norm[l, 2], NSA_GROUPS)]),
            tile2(dsa_q_norm[l], DSA_HEADS), tile2(dsa_k_norm[l], 2),
            rope_main, rope_idx, tm=256, seq=s)
        r3 = lambda t: t.reshape(b, s, t.shape[-1])

        pe_e, w1e, w2e = _compress_weights(nsa_cmp_pe[l], nsa_cmp_w1[l], nsa_cmp_w2[l])
        kc, vc = _compress(nkc.reshape(b, ncp, NSA_CMP_STRIDE * NSA_KVW),
                           nvc.reshape(b, ncp, NSA_CMP_STRIDE * NSA_KVW),
                           pe_e, w1e, w2e, tile2(nsa_k_norm[l, 0], NSA_GROUPS), rope_cmp)

        o_a = _sb_attention(r3(sb), tq=256)

        nkv = r3(nkv)
        ks, vs, kw, vw = (nkv[:, :, c * LANES:(c + 1) * LANES] for c in range(4))
        o_b = _nsa_attention(r3(nq), r3(ng), kc, _group_values_t(vc, ncp)[:, :, 0], ks,
                             _group_values_t(vs, TKS), kw, _group_values_t(vw, TQ_NSA), overlap_t, onehot,
                             tq=TQ_NSA, tks=TKS)

        dkv = r3(dkv)
        iwt = jnp.swapaxes(r3(iw)[:, :, :DSA_IDX_HEADS], 1, 2)
        dkvt = _key_tiles_t(dkv, TKS).at[:, :, 0, :].set(1)
        o_c = _dsa_attention(r3(iq), iwt, r3(ik), r3(dq), dkv, dkvt, tq=TQ, tk=TKS, top_k=top_k)

        x2 = _merge(x2, mix_norm[l][None, :], o_a.reshape(n, SB_W), o_b.reshape(n, NSA_QW),
                    o_c.reshape(n, DSA_QW), cast(w_gate[l]), cast(w_br_a[l]), cast(w_br_b[l]),
                    cast(w_br_c[l]), cast(w_out[l]), tm=256)

        x2 = _ffn(x2, ffn2_norm[l][None, :], cast(ffn2_w_gu[l]), cast(ffn2_w_down[l]), tm=FFN_TM, tf=ffn_tf)
    return x2.reshape(b, s, d)
```

```python
import functools

import numpy as np
import jax
import jax.numpy as jnp
from jax import lax
from jax.experimental import pallas as pl
from jax.experimental.pallas import tpu as pltpu

HEAD_DIM = 64
ROT_DIM = HEAD_DIM // 4
ROPE_THETA = 500000.0
NORM_EPS = 1e-6

SB_HEADS = 4
NSA_HEADS = 8
NSA_GROUPS = 2
NSA_HPG = NSA_HEADS // NSA_GROUPS
NSA_CMP_LEN = 32
NSA_CMP_STRIDE = 16
NSA_CMP_HIDDEN = 2 * HEAD_DIM
NSA_SEL_LEN = 64
NSA_SEL_N = 16
NSA_WINDOW = 512
NSA_FORCED_SCORE = 1.0e4
DSA_HEADS = 4
DSA_IDX_HEADS = 8
DSA_IDX_DIM = 32
DSA_IDX_ROT = DSA_IDX_DIM // 4
DSA_TOPK = 256

SB_W = SB_HEADS * HEAD_DIM
NSA_QW = NSA_HEADS * HEAD_DIM
NSA_KVW = NSA_GROUPS * HEAD_DIM
DSA_QW = DSA_HEADS * HEAD_DIM
IDX_QW = DSA_IDX_HEADS * DSA_IDX_DIM

LANES = 128
VMEM_LIMIT = 56 * 1024 * 1024
MXU_DTYPE = jnp.bfloat16
NEG = -1.0e30
SB_SKIP = -120.0
INT_MIN = -2 ** 31
LOG2E = 1.4426950408889634

F32 = jnp.float32

TQ = 256
TQ_NSA = 256
TKS = 512
FFN_CHUNK = 512
FFN_TM = 512
assert TKS % TQ == 0 and TKS % TQ_NSA == 0 and NSA_WINDOW % TQ_NSA == 0


def _dot(a, b):
    return jnp.dot(a, b, preferred_element_type=F32)


def _dot_nt(a, b):
    return lax.dot_general(a, b, (((1,), (1,)), ((), ())), preferred_element_type=F32)


def _params(n_axes):
    return pltpu.CompilerParams(dimension_semantics=("arbitrary",) * n_axes,
                                vmem_limit_bytes=VMEM_LIMIT)


def _lane_iota(shape):
    return lax.broadcasted_iota(jnp.int32, shape, len(shape) - 1)


def _row_iota(shape):
    return lax.broadcasted_iota(jnp.int32, shape, len(shape) - 2)


def _swap_halves(x):
    return pltpu.roll(x, LANES // 2, 1)


def _head_rms(ys):
    lo = _lane_iota(ys.shape) < HEAD_DIM
    sq = ys * ys
    s_lo = jnp.sum(jnp.where(lo, sq, 0.0), axis=-1, keepdims=True)
    s_hi = jnp.sum(jnp.where(lo, 0.0, sq), axis=-1, keepdims=True)
    ms = jnp.where(lo, s_lo, s_hi) * (1.0 / HEAD_DIM)
    return ys * lax.rsqrt(ms + NORM_EPS)


def _rope(ys, c, sm, sp, half):
    return ys * c + pltpu.roll(ys, LANES - half, 1) * sm + pltpu.roll(ys, half, 1) * sp


def _ffn_kernel(x_ref, g_ref, wg_ref, wu_ref, wd_ref, o_ref, h_ref, acc_ref):
    f = pl.program_id(1)

    @pl.when(f == 0)
    def _():
        x = x_ref[...]
        ms = jnp.mean(x * x, axis=-1, keepdims=True)
        h_ref[...] = (x * lax.rsqrt(ms + NORM_EPS) * g_ref[...]).astype(h_ref.dtype)
        acc_ref[...] = jnp.zeros_like(acc_ref)

    h = h_ref[...]
    tf = wd_ref.shape[0]
    bounds = [(c, min(c + FFN_CHUNK, tf)) for c in range(0, tf, FFN_CHUNK)]

    def gate_up(c0, c1):
        return _dot(h, wg_ref[:, c0:c1]), _dot(h, wu_ref[:, c0:c1])

    ahead = gate_up(*bounds[0])
    for idx, (c0, c1) in enumerate(bounds):
        gate, up = ahead
        if idx + 1 < len(bounds):
            ahead = gate_up(*bounds[idx + 1])
        act = (gate * jax.nn.sigmoid(gate) * up).astype(h_ref.dtype)
        acc_ref[...] += _dot(act, wd_ref[c0:c1, :])

    @pl.when(f == pl.num_programs(1) - 1)
    def _():
        o_ref[...] = x_ref[...] + 0.5 * acc_ref[...]


def _ffn(x2, g, w_gu, w_down, *, tm, tf):
    n, d = x2.shape
    d_ff = w_down.shape[0]
    nf = d_ff // tf
    return pl.pallas_call(
        _ffn_kernel,
        grid=(n // tm, nf),
        in_specs=[
            pl.BlockSpec((tm, d), lambda i, f: (i, 0)),
            pl.BlockSpec((1, d), lambda i, f: (0, 0)),
            pl.BlockSpec((d, tf), lambda i, f: (0, f)),
            pl.BlockSpec((d, tf), lambda i, f: (0, f + nf)),
            pl.BlockSpec((tf, d), lambda i, f: (f, 0)),
        ],
        out_specs=pl.BlockSpec((tm, d), lambda i, f: (i, 0)),
        out_shape=jax.ShapeDtypeStruct((n, d), F32),
        scratch_shapes=[pltpu.VMEM((tm, d), MXU_DTYPE), pltpu.VMEM((tm, d), F32)],
        compiler_params=_params(2),
        name="ffn",
    )(x2, g, w_gu, w_gu, w_down)


_P_SB = 0
_P_NQ = _P_SB + 3 * SB_W
_P_NCV = _P_NQ + NSA_QW
_P_NKV = _P_NCV + 2 * NSA_KVW
_P_NG = _P_NKV + 4 * NSA_KVW
_P_DQ = _P_NG + LANES
_P_DKV = _P_DQ + DSA_QW
_P_IQ = _P_DKV + LANES
_P_IK = _P_IQ + IDX_QW
_P_IW = _P_IK + LANES
_P_END = _P_IW + LANES


def _proj_kernel(x_ref, g_ref, w_ref, nq_g_ref, nk_g_ref, dq_g_ref, dk_g_ref,
                 rc_ref, rm_ref, rp_ref, ic_ref, im_ref, ip_ref,
                 sb_ref, nq_ref, nkc_ref, nvc_ref, nkv_ref, ng_ref, dq_ref, dkv_ref,
                 iq_ref, ik_ref, iw_ref):
    x = x_ref[...]
    ms = jnp.mean(x * x, axis=-1, keepdims=True)
    h = (x * lax.rsqrt(ms + NORM_EPS) * g_ref[...]).astype(MXU_DTYPE)
    rc, rm, rp = rc_ref[...], rm_ref[...], rp_ref[...]
    ic, im, ip = ic_ref[...], im_ref[...], ip_ref[...]
    half, ihalf = ROT_DIM // 2, DSA_IDX_ROT // 2
    scale = HEAD_DIM ** -0.5
    scale2 = scale * LOG2E

    def cols(a, width):
        return _dot(h, w_ref[:, a:a + width])

    def slab(y, s):
        return y[:, s * LANES:(s + 1) * LANES]

    y = cols(_P_SB, 3 * SB_W)
    sb_ref[:, 0:SB_W] = (y[:, 0:SB_W] * scale).astype(sb_ref.dtype)
    sb_ref[:, SB_W:3 * SB_W] = y[:, SB_W:3 * SB_W].astype(sb_ref.dtype)

    y = cols(_P_NQ, NSA_QW)
    for s in range(NSA_QW // LANES):
        ys = _head_rms(slab(y, s)) * slab(nq_g_ref[...], s)
        nq_ref[:, s * LANES:(s + 1) * LANES] = (_rope(ys, rc, rm, rp, half) * scale2).astype(nq_ref.dtype)

    y = cols(_P_NCV, 2 * NSA_KVW)
    nkc_ref[...] = slab(y, 0)
    nvc_ref[...] = slab(y, 1)

    y = cols(_P_NKV, 4 * NSA_KVW)
    for s, gi in ((0, 0), (2, 1)):
        ys = _head_rms(slab(y, s)) * nk_g_ref[gi:gi + 1, :]
        nkv_ref[:, s * LANES:(s + 1) * LANES] = _rope(ys, rc, rm, rp, half).astype(nkv_ref.dtype)
    for s in (1, 3):
        nkv_ref[:, s * LANES:(s + 1) * LANES] = slab(y, s).astype(nkv_ref.dtype)

    ng_ref[...] = jax.nn.sigmoid(cols(_P_NG, LANES))

    y = cols(_P_DQ, DSA_QW)
    for s in range(DSA_QW // LANES):
        ys = _head_rms(slab(y, s)) * slab(dq_g_ref[...], s)
        dq_ref[:, s * LANES:(s + 1) * LANES] = (_rope(ys, rc, rm, rp, half) * scale2).astype(dq_ref.dtype)

    y = cols(_P_DKV, LANES)
    yk = _rope(_head_rms(y) * dk_g_ref[...], rc, rm, rp, half)
    dkv_ref[...] = jnp.where(_lane_iota(y.shape) < HEAD_DIM, yk, y).astype(dkv_ref.dtype)

    y = cols(_P_IQ, IDX_QW)
    for s in range(IDX_QW // LANES):
        iq_ref[:, s * LANES:(s + 1) * LANES] = _rope(slab(y, s), ic, im, ip, ihalf).astype(iq_ref.dtype)
    ik_ref[...] = _rope(cols(_P_IK, LANES), ic, im, ip, ihalf).astype(ik_ref.dtype)
    iw_ref[...] = cols(_P_IW, LANES) * (DSA_IDX_HEADS ** -0.5) * (DSA_IDX_DIM ** -0.5)


def _proj(x2, g, w, nq_g, nk_g, dq_g, dk_g, rope_main, rope_idx, *, tm, seq):
    n, d = x2.shape
    nt = seq // tm
    row = lambda i: (i, 0)
    const = lambda i: (0, 0)
    pos = lambda i: (i % nt, 0)
    widths = [(3 * SB_W, MXU_DTYPE), (NSA_QW, MXU_DTYPE), (NSA_KVW, F32), (NSA_KVW, F32),
              (4 * NSA_KVW, MXU_DTYPE), (LANES, F32), (DSA_QW, MXU_DTYPE), (LANES, MXU_DTYPE),
              (IDX_QW, MXU_DTYPE), (LANES, MXU_DTYPE), (LANES, F32)]
    return pl.pallas_call(
        _proj_kernel,
        grid=(n // tm,),
        in_specs=[pl.BlockSpec((tm, d), row), pl.BlockSpec((1, d), const),
                  pl.BlockSpec((d, _P_END), const),
                  pl.BlockSpec((1, NSA_QW), const), pl.BlockSpec((2, LANES), const),
                  pl.BlockSpec((1, DSA_QW), const), pl.BlockSpec((1, LANES), const)]
                 + [pl.BlockSpec((tm, LANES), pos)] * 6,
        out_specs=[pl.BlockSpec((tm, wd), row) for wd, _ in widths],
        out_shape=[jax.ShapeDtypeStruct((n, wd), dt) for wd, dt in widths],
        compiler_params=_params(1),
        name="proj",
    )(x2, g, w, nq_g, nk_g, dq_g, dk_g, *rope_main, *rope_idx)


def _compress_kernel(xk_ref, xv_ref, pe_ref, w1_ref, w2_ref, g_ref, rc_ref, rm_ref, rp_ref,
                     kc_ref, vc_ref):
    ncp = xk_ref.shape[1]

    def compress(x, kv):
        a0 = _dot((x + pe_ref[kv, 0:1, :]).astype(MXU_DTYPE), w1_ref[kv, 0])
        a1 = _dot((x + pe_ref[kv, 1:2, :]).astype(MXU_DTYPE), w1_ref[kv, 1])
        pre = a0 + pltpu.roll(a1, ncp - 1, 0)
        hid = pre * jax.nn.sigmoid(pre)
        return _dot(hid.astype(MXU_DTYPE), w2_ref[kv])

    yk = _head_rms(compress(xk_ref[0], 0)) * g_ref[...]
    kc_ref[0] = _rope(yk, rc_ref[...], rm_ref[...], rp_ref[...], ROT_DIM // 2).astype(kc_ref.dtype)
    vc_ref[0] = compress(xv_ref[0], 1).astype(vc_ref.dtype)


def _compress(xk, xv, pe, w1, w2, g, rope_c):
    b, ncp, wide = xk.shape
    c3 = lambda i: (0, 0, 0)
    c4 = lambda i: (0, 0, 0, 0)
    c2 = lambda i: (0, 0)
    per_b = lambda i: (i, 0, 0)
    return pl.pallas_call(
        _compress_kernel,
        grid=(b,),
        in_specs=[pl.BlockSpec((1, ncp, wide), per_b), pl.BlockSpec((1, ncp, wide), per_b),
                  pl.BlockSpec(pe.shape, c3), pl.BlockSpec(w1.shape, c4), pl.BlockSpec(w2.shape, c3),
                  pl.BlockSpec((1, LANES), c2)] + [pl.BlockSpec((ncp, LANES), c2)] * 3,
        out_specs=[pl.BlockSpec((1, ncp, LANES), per_b)] * 2,
        out_shape=[jax.ShapeDtypeStruct((b, ncp, LANES), MXU_DTYPE)] * 2,
        compiler_params=_params(1),
        name="compress",
    )(xk, xv, pe, w1, w2, g, *rope_c)


def _sb_kernel(q_ref, k_ref, v_ref, o_ref, *, tq):
    i = pl.program_id(2)
    q0 = i * tq
    q = q_ref[0].astype(F32)
    lane = _lane_iota(q.shape)
    rows = _row_iota((tq, tq))
    colsq = _lane_iota((tq, tq))
    upper = jnp.where(_row_iota((tq, tq + LANES)) > _lane_iota((tq, tq + LANES)), 1.0,
                      jnp.where(_lane_iota((tq, tq + LANES)) >= tq, 1.0, 0.0)).astype(MXU_DTYPE)
    in_head = [(lane >> 6) == p for p in range(2)]
    qh = [jnp.where(m, q, 0.0).astype(MXU_DTYPE) for m in in_head]
    reps = tq // LANES

    def tile(kt, carries, accs, diagonal):
        ks = k_ref[0, pl.ds(kt * tq, tq), :]
        vs = v_ref[0, pl.ds(kt * tq, tq), :]
        zs = [_dot_nt(qh[p], ks) for p in range(2)]
        new_c, new_a = [], []
        for p in range(2):
            z = zs[p]
            l = -(jnp.maximum(z, 0.0) + jnp.log1p(jnp.exp(-jnp.abs(z))))
            if diagonal:
                past = colsq < rows
                l = jnp.where(past, l, 0.0)
            hi = l.astype(MXU_DTYPE)
            lo = (l - hi.astype(F32)).astype(MXU_DTYPE)
            sums = _dot(hi, upper) + _dot(lo, upper)
            tail = sums[:, :tq] + jnp.concatenate([carries[p]] * reps, axis=1)
            w = jnp.exp(z + l + tail)
            if diagonal:
                w = jnp.where(past, w, 0.0)
            new_a.append(accs[p] + _dot(w.astype(MXU_DTYPE), vs))
            new_c.append(carries[p] + sums[:, tq:])
        return tuple(new_c), tuple(new_a)

    zeros = (jnp.zeros((tq, LANES), F32),) * 2
    carries, accs = tile(i, zeros, zeros, True)

    def cond(st):
        kt, carries, _ = st
        return (kt >= 0) & (jnp.max(jnp.maximum(carries[0], carries[1])) > SB_SKIP)

    def body(st):
        kt, carries, accs = st
        carries, accs = tile(kt, carries, accs, False)
        return kt - 1, carries, accs

    _, _, accs = lax.while_loop(cond, body, (i - 1, carries, accs))
    del q0
    o_ref[0] = jnp.where(in_head[0], accs[0], accs[1]).astype(o_ref.dtype)


def _sb_attention(sb, *, tq):
    b, s, _ = sb.shape
    nslab = SB_W // LANES
    return pl.pallas_call(
        functools.partial(_sb_kernel, tq=tq),
        grid=(b, nslab, s // tq),
        in_specs=[pl.BlockSpec((1, tq, LANES), lambda bi, sl, i: (bi, i, sl)),
                  pl.BlockSpec((1, s, LANES), lambda bi, sl, i: (bi, 0, nslab + sl)),
                  pl.BlockSpec((1, s, LANES), lambda bi, sl, i: (bi, 0, 2 * nslab + sl))],
        out_specs=pl.BlockSpec((1, tq, LANES), lambda bi, sl, i: (bi, i, sl)),
        out_shape=jax.ShapeDtypeStruct((b, s, SB_W), MXU_DTYPE),
        compiler_params=_params(3),
        name="sb_attn",
    )(sb, sb, sb)


def _group_queries(q_ref, g, tq):
    lane = _lane_iota((tq, LANES))
    in_g = (lane >> 6) == g
    qa = []
    for s in range(2):
        qs = q_ref[0, :, s * LANES:(s + 1) * LANES].astype(F32)
        for p in range(2):
            mine = jnp.where((lane >> 6) == p, qs, 0.0)
            both = mine + _swap_halves(mine)
            qa.append(jnp.where(in_g, both, 0.0).astype(MXU_DTYPE))
    return qa


def _fold_rows(x, op, rows=64):
    while x.shape[0] > rows and x.shape[0] % 16 == 0:
        half = x.shape[0] // 2
        x = op(x[:half], x[half:])
    return x


def _key_max(x):
    return jnp.max(_fold_rows(x, jnp.maximum), axis=0, keepdims=True)


def _key_sum(x):
    return jnp.sum(_fold_rows(x, jnp.add), axis=0, keepdims=True)


def _online_update(m, acc_ref, s, v_t):
    m_new = jnp.maximum(m, _key_max(s))
    p = jnp.exp2(s - m_new)
    acc_ref[...] = jnp.exp2(m - m_new) * acc_ref[...] + _dot(v_t, p.astype(MXU_DTYPE))
    return m_new


def _attend_tiles(n_tiles, scores_fn, softmax_fn, ms, fix_last):
    def step(tiles, ms, last):
        scores = [scores_fn(kt) for kt in tiles]
        if last:
            scores[-1] = fix_last(scores[-1])
        for kt, sc in zip(tiles, scores):
            ms = softmax_fn(kt, ms, sc)
        return ms

    plain = n_tiles - 1
    ms = lax.fori_loop(0, plain // 2, lambda p, ms: step([2 * p, 2 * p + 1], ms, False), ms)
    return lax.cond(plain % 2 == 1,
                    lambda ms: step([n_tiles - 2, n_tiles - 1], ms, True),
                    lambda ms: step([n_tiles - 1], ms, True), ms)


def _attend_pipelined(n_tiles, scores_fn, softmax_fn, ms, fix_last, s_ref):
    heads = s_ref.shape[1]

    def issue(slot, kt):
        for h, sc in enumerate(scores_fn(kt)):
            s_ref[slot, h] = sc

    def fold(slot, kt, ms):
        return softmax_fn(kt, ms, tuple(s_ref[slot, h] for h in range(heads)))

    plain = n_tiles - 1

    @pl.when(plain > 0)
    def _():
        issue(0, 0)

    def body(p, ms):
        issue(1, 2 * p + 1)
        ms = fold(0, 2 * p, ms)
        issue(0, jnp.minimum(2 * p + 2, plain - 1))
        return fold(1, 2 * p + 1, ms)

    ms = lax.fori_loop(0, plain // 2, body, ms)
    ms = lax.cond(plain % 2 == 1, lambda ms: fold(0, plain - 1, ms), lambda ms: ms, ms)
    return softmax_fn(n_tiles - 1, ms, fix_last(scores_fn(n_tiles - 1)))


def _online_init(tq, acc_ref):
    acc_ref[...] = jnp.zeros(acc_ref.shape, F32)
    return tuple(jnp.full((1, tq), NEG, F32) for _ in range(acc_ref.shape[0]))


def _nsa_kernel(q_ref, gate_ref, kc_ref, vct_ref, ks_ref, vst_ref, kw_ref, vwt_ref,
                ovt_ref, oh_ref, o_ref, qaug_ref, acc_ref, s_ref, *, tq, tks):
    g = pl.program_id(1)
    i = pl.program_id(2)
    q0 = i * tq
    ncp = kc_ref.shape[1]
    qa = _group_queries(q_ref, g, tq)

    c_vis = (_row_iota((ncp, tq)) * NSA_CMP_STRIDE + (NSA_CMP_LEN - 1)) <= q0 + _lane_iota((ncp, tq))
    kc = kc_ref[0]
    vct = vct_ref[0, 0]
    pc_sum = jnp.zeros((ncp, tq), F32)
    o_cmp = []
    scores = [_dot_nt(kc, qa[j]) for j in range(NSA_HPG)]
    for j in range(NSA_HPG):
        s = jnp.where(c_vis, scores[j], NEG)
        p = jnp.where(c_vis, jnp.exp2(s - _key_max(s)), 0.0)
        den = _key_sum(p)
        p = p * (1.0 / jnp.where(den > 0.0, den, 1.0))
        pc_sum = pc_sum + p
        o_cmp.append(_dot(vct, p.astype(MXU_DTYPE))[:HEAD_DIM])

    band = tq + NSA_WINDOW
    start = pl.multiple_of(jnp.maximum(q0 - NSA_WINDOW, 0), tq)
    kw = kw_ref[0, pl.ds(start, band), :]
    s_w = start + _row_iota((band, tq))
    t_w = q0 + _lane_iota((band, tq))
    w_ok = (s_w <= t_w) & (s_w > t_w - NSA_WINDOW)
    o_win = []
    scores = [_dot_nt(kw, qa[j]) for j in range(NSA_HPG)]
    for j in range(NSA_HPG):
        s = jnp.where(w_ok, scores[j], NEG)
        p = jnp.exp2(s - _key_max(s)).astype(MXU_DTYPE)
        acc = jnp.zeros((LANES, tq), F32)
        for c in range(band // tq):
            acc = acc + _dot(vwt_ref[0, 0, start // tq + c], p[c * tq:(c + 1) * tq, :])
        o_win.append(acc[:HEAD_DIM] * (1.0 / acc[HEAD_DIM:HEAD_DIM + 1]))

    hi = pc_sum.astype(MXU_DTYPE)
    lo = (pc_sum - hi.astype(F32)).astype(MXU_DTYPE)
    imp = _dot(ovt_ref[...], hi) + _dot(ovt_ref[...], lo)
    blk = _row_iota((LANES, tq))
    t_b = q0 + _lane_iota((LANES, tq))
    cur = t_b >> 6
    visible = blk * NSA_SEL_LEN <= t_b
    forced = (blk == 0) | (blk == cur) | (blk == cur - 1)
    taken = -3.0e38
    work = jnp.where(visible, jnp.where(forced, taken, imp), NEG)
    blk_f = blk.astype(F32)
    for _ in range(NSA_SEL_N - 3):
        m = jnp.max(work, axis=0, keepdims=True)
        first = jnp.min(jnp.where(work == m, blk_f, float(LANES)), axis=0, keepdims=True)
        work = jnp.where(blk_f == first, taken, work)
    sel_bias = jnp.where(visible, jnp.where(work == taken, 0.0, NEG), NEG).T.astype(MXU_DTYPE)

    n_tiles = (q0 + tq + tks - 1) // tks
    t_s = q0 + _lane_iota((tks, tq))
    row_s = _row_iota((tks, tq))
    for j in range(NSA_HPG):
        qaug_ref[j] = jnp.concatenate([qa[j], sel_bias], axis=1)

    def sel_scores(kt):
        k_aug = jnp.concatenate([ks_ref[0, pl.ds(kt * tks, tks), :], oh_ref[pl.ds(kt * tks, tks), :]], axis=1)
        return tuple(_dot_nt(k_aug, qaug_ref[j]) for j in range(NSA_HPG))

    def sel_softmax(kt, ms, scores):
        v_t = vst_ref[0, 0, kt]
        return tuple(_online_update(ms[j], acc_ref.at[j], scores[j], v_t) for j in range(NSA_HPG))

    def sel_causal(scores):
        causal = (n_tiles - 1) * tks + row_s <= t_s
        return tuple(jnp.where(causal, s, NEG) for s in scores)

    _attend_pipelined(n_tiles, sel_scores, sel_softmax, _online_init(tq, acc_ref), sel_causal, s_ref)
    o_sel = [acc_ref[j, :HEAD_DIM, :] * (1.0 / acc_ref[j, HEAD_DIM:HEAD_DIM + 1, :]) for j in range(NSA_HPG)]

    gates_t = gate_ref[0].T
    merged = []
    for j in range(NSA_HPG):
        tot = jnp.zeros((HEAD_DIM, tq), F32)
        for r, branch in enumerate((o_cmp, o_sel, o_win)):
            row = 3 * j + r
            stride = 3 * NSA_HPG
            gate = jnp.where(g == 0, gates_t[row:row + 1, :], gates_t[stride + row:stride + row + 1, :])
            tot = tot + gate * branch[j]
        merged.append(tot)
    for s in range(2):
        pair = jnp.concatenate([merged[2 * s], merged[2 * s + 1]], axis=0)
        o_ref[0, :, s * LANES:(s + 1) * LANES] = pair.T.astype(o_ref.dtype)


def _nsa_attention(nq, ng, kc, vct, ks, vst, kw, vwt, overlap_t, onehot, *, tq, tks):
    b, s, _ = nq.shape
    ncp = kc.shape[1]
    gw = NSA_HPG * HEAD_DIM
    per_b3 = lambda bi, g, i: (bi, 0, 0)
    per_bg4 = lambda bi, g, i: (bi, g, 0, 0)
    per_bg5 = lambda bi, g, i: (bi, g, 0, 0, 0)
    const = lambda bi, g, i: (0, 0)
    return pl.pallas_call(
        functools.partial(_nsa_kernel, tq=tq, tks=tks),
        grid=(b, NSA_GROUPS, s // tq),
        in_specs=[pl.BlockSpec((1, tq, gw), lambda bi, g, i: (bi, i, g)),
                  pl.BlockSpec((1, tq, LANES), lambda bi, g, i: (bi, i, 0)),
                  pl.BlockSpec((1, ncp, LANES), per_b3),
                  pl.BlockSpec((1, 1) + vct.shape[2:], per_bg4),
                  pl.BlockSpec((1, s, LANES), per_b3),
                  pl.BlockSpec((1, 1) + vst.shape[2:], per_bg5),
                  pl.BlockSpec((1, s, LANES), per_b3),
                  pl.BlockSpec((1, 1) + vwt.shape[2:], per_bg5),
                  pl.BlockSpec(overlap_t.shape, const),
                  pl.BlockSpec(onehot.shape, const)],
        out_specs=pl.BlockSpec((1, tq, gw), lambda bi, g, i: (bi, i, g)),
        out_shape=jax.ShapeDtypeStruct((b, s, NSA_QW), MXU_DTYPE),
        scratch_shapes=[pltpu.VMEM((NSA_HPG, tq, 2 * LANES), MXU_DTYPE),
                        pltpu.VMEM((NSA_HPG, LANES, tq), F32),
                        pltpu.VMEM((2, NSA_HPG, tks, tq), F32)],
        compiler_params=_params(3),
        name="nsa_attn",
    )(nq, ng, kc, vct, ks, vst, kw, vwt, overlap_t, onehot)


def _dsa_kernel(iq_ref, iwt_ref, ik_ref, q_ref, kv_ref, kvt_ref, o_ref, khi_ref, klo_ref, bias_ref, qa_ref,
                acc_ref, s_ref, *, tq, tk, top_k):
    i = pl.program_id(1)
    q0 = i * tq
    n_tiles = (q0 + tq + tk - 1) // tk
    lane = _lane_iota((tq, LANES))
    t_q = q0 + _lane_iota((tk, tq))
    row_k = _row_iota((tk, tq))

    iqa = []
    for s in range(IDX_QW // LANES):
        qs = iq_ref[0, :, s * LANES:(s + 1) * LANES].astype(F32)
        for p in range(LANES // DSA_IDX_DIM):
            mine = jnp.where((lane >> 5) == p, qs, 0.0)
            if p:
                mine = pltpu.roll(mine, LANES - p * DSA_IDX_DIM, 1)
            iqa.append(mine.astype(MXU_DTYPE))
    w_h = [iwt_ref[0, h:h + 1, :] for h in range(DSA_IDX_HEADS)]

    def score_tiles(tiles, last_is_diagonal):
        logits = []
        for kt in tiles:
            ik = ik_ref[0, pl.ds(kt * tk, tk), :]
            logits.append([_dot_nt(ik, iqa[h]) for h in range(DSA_IDX_HEADS)])
        for n, kt in enumerate(tiles):
            score = jnp.zeros((tk, tq), F32)
            for h in range(DSA_IDX_HEADS):
                score = score + w_h[h] * jnp.maximum(logits[n][h], 0.0)
            score = jnp.where(score == 0.0, 0.0, score)
            bits = pltpu.bitcast(score, jnp.int32)
            key = jnp.where(bits < 0, bits ^ 0x7FFFFFFF, bits)
            if last_is_diagonal and n == len(tiles) - 1:
                key = jnp.where(kt * tk + row_k <= t_q, key, INT_MIN)
            khi_ref[kt] = (key >> 16).astype(jnp.int16)
            klo_ref[kt] = ((key & 0xFFFF) - 0x8000).astype(jnp.int16)

    def score_pair(p, _):
        score_tiles([2 * p, 2 * p + 1], False)
        return 0

    plain = n_tiles - 1
    lax.fori_loop(0, plain // 2, score_pair, 0)
    lax.cond(plain % 2 == 1,
             lambda: score_tiles([n_tiles - 2, n_tiles - 1], True),
             lambda: score_tiles([n_tiles - 1], True))
    one, zero = jnp.ones((), jnp.bfloat16), jnp.zeros((), jnp.bfloat16)

    def count(flags):
        def body(kt, acc):
            return acc + _fold_rows(flags(kt), jnp.add, rows=32).astype(F32)
        acc = lax.fori_loop(0, n_tiles, body, jnp.zeros((32, tq), F32))
        return jnp.sum(acc, axis=0, keepdims=True)

    def kth_largest(ref):
        def body(b, ans_u):
            cand_u = ans_u | lax.shift_left(jnp.int32(1), 15 - b)
            cand = (cand_u - 0x8000).astype(jnp.int16)
            cnt = count(lambda kt: jnp.where(ref[kt] >= cand, one, zero))
            return jnp.where(cnt >= top_k, cand_u, ans_u)
        ans_u = lax.fori_loop(0, 16, body, jnp.zeros((1, tq), jnp.int32))
        return ans_u - 0x8000

    t_hi32 = kth_largest(khi_ref)
    t_hi = t_hi32.astype(jnp.int16)

    def pin_body(kt, _):
        hi = khi_ref[kt]
        klo_ref[kt] = jnp.where(hi == t_hi, klo_ref[kt],
                                jnp.where(hi > t_hi, jnp.int16(32767), jnp.int16(-32768)))
        return 0

    lax.fori_loop(0, n_tiles, pin_body, 0)
    t_lo32 = kth_largest(klo_ref)
    t_lo = jnp.where(t_hi32 == -0x8000, jnp.maximum(t_lo32, 1 - 0x8000), t_lo32).astype(jnp.int16)

    def at_least(kt, yes, no):
        hi = khi_ref[kt]
        return jnp.where(hi > t_hi, yes, jnp.where(hi == t_hi, jnp.where(klo_ref[kt] >= t_lo, yes, no), no))

    def above(kt):
        return jnp.where(khi_ref[kt] > t_hi, one, jnp.where(klo_ref[kt] > t_lo, one, zero))

    bias_yes, bias_no = jnp.zeros((), bias_ref.dtype), jnp.full((), NEG, bias_ref.dtype)

    def plain_bias():
        def body(kt, _):
            bias_ref[kt] = at_least(kt, bias_yes, bias_no)
            return 0
        lax.fori_loop(0, n_tiles, body, 0)

    def ranked_bias():
        need = top_k - count(above)
        lower = jnp.where(_lane_iota((tk, tk)) < _row_iota((tk, tk)), 1.0, 0.0).astype(MXU_DTYPE)

        def body(kt, seen):
            tie = jnp.where(khi_ref[kt] == t_hi, jnp.where(klo_ref[kt] == t_lo, one, zero), zero)
            rank = _dot(lower, tie.astype(MXU_DTYPE)) + seen
            tie = tie.astype(F32)
            take = jnp.where(above(kt).astype(F32) > 0.0, 1.0, jnp.where(rank < need, tie, 0.0))
            bias_ref[kt] = ((take - 1.0) * (-NEG)).astype(bias_ref.dtype)
            return seen + _key_sum(tie)

        lax.fori_loop(0, n_tiles, body, jnp.zeros((1, tq), F32))

    reached = count(lambda kt: at_least(kt, one, zero))
    lax.cond(jnp.max(reached) > top_k, ranked_bias, plain_bias)

    for s in range(DSA_QW // LANES):
        qs = q_ref[0, :, s * LANES:(s + 1) * LANES].astype(F32)
        lo = jnp.where(lane < HEAD_DIM, qs, 0.0)
        hi = _swap_halves(jnp.where(lane < HEAD_DIM, 0.0, qs))
        qa_ref[2 * s] = lo.astype(MXU_DTYPE)
        qa_ref[2 * s + 1] = hi.astype(MXU_DTYPE)

    def att_scores(kt):
        kv = kv_ref[0, pl.ds(kt * tk, tk), :]
        bias = bias_ref[kt].astype(F32)
        return tuple(_dot_nt(kv, qa_ref[h]) + bias for h in range(DSA_HEADS))

    def att_softmax(kt, ms, scores):
        kv_t = kvt_ref[0, kt]
        return tuple(_online_update(ms[h], acc_ref.at[h], scores[h], kv_t) for h in range(DSA_HEADS))

    _attend_pipelined(n_tiles, att_scores, att_softmax, _online_init(tq, acc_ref), lambda scores: scores, s_ref)
    outs = [acc_ref[h, HEAD_DIM:, :] * (1.0 / acc_ref[h, 0:1, :]) for h in range(DSA_HEADS)]

    for s in range(DSA_QW // LANES):
        pair = jnp.concatenate([outs[2 * s], outs[2 * s + 1]], axis=0)
        o_ref[0, :, s * LANES:(s + 1) * LANES] = pair.T.astype(o_ref.dtype)


def _dsa_attention(iq, iwt, ik, dq, dkv, dkvt, *, tq, tk, top_k):
    b, s, _ = dq.shape
    tile = lambda width: pl.BlockSpec((1, tq, width), lambda bi, i: (bi, i, 0))
    full = pl.BlockSpec((1, s, LANES), lambda bi, i: (bi, 0, 0))
    return pl.pallas_call(
        functools.partial(_dsa_kernel, tq=tq, tk=tk, top_k=top_k),
        grid=(b, s // tq),
        in_specs=[tile(IDX_QW), pl.BlockSpec((1, DSA_IDX_HEADS, tq), lambda bi, i: (bi, 0, i)), full,
                  tile(DSA_QW), full, pl.BlockSpec((1,) + dkvt.shape[1:], lambda bi, i: (bi, 0, 0, 0))],
        out_specs=tile(DSA_QW),
        out_shape=jax.ShapeDtypeStruct((b, s, DSA_QW), MXU_DTYPE),
        scratch_shapes=[pltpu.VMEM((s // tk, tk, tq), jnp.int16), pltpu.VMEM((s // tk, tk, tq), jnp.int16),
                        pltpu.VMEM((s // tk, tk, tq), MXU_DTYPE),
                        pltpu.VMEM((DSA_HEADS, tq, LANES), MXU_DTYPE), pltpu.VMEM((DSA_HEADS, LANES, tq), F32),
                        pltpu.VMEM((2, DSA_HEADS, tk, tq), F32)],
        compiler_params=_params(2),
        name="dsa_attn",
    )(iq, iwt, ik, dq, dkv, dkvt)


def _merge_kernel(x_ref, g_ref, oa_ref, ob_ref, oc_ref, wg_ref, wa_ref, wb_ref, wc_ref, wo_ref, o_ref):
    x = x_ref[...]
    d = x.shape[1]
    ms = jnp.mean(x * x, axis=-1, keepdims=True)
    h = (x * lax.rsqrt(ms + NORM_EPS) * g_ref[...]).astype(MXU_DTYPE)
    merged = jnp.zeros(x.shape, F32)
    for r, (o_r, w_r) in enumerate(((oa_ref, wa_ref), (ob_ref, wb_ref), (oc_ref, wc_ref))):
        gate = jax.nn.sigmoid(_dot(h, wg_ref[:, r * d:(r + 1) * d]))
        merged = merged + gate * _dot(o_r[...], w_r[...])
    o_ref[...] = x + _dot(merged.astype(MXU_DTYPE), wo_ref[...])


def _merge(x2, g, oa, ob, oc, w_gate, wa, wb, wc, wo, *, tm):
    n, d = x2.shape
    row = lambda i: (i, 0)
    const = lambda i: (0, 0)
    return pl.pallas_call(
        _merge_kernel,
        grid=(n // tm,),
        in_specs=[pl.BlockSpec((tm, d), row), pl.BlockSpec((1, d), const),
                  pl.BlockSpec((tm, SB_W), row), pl.BlockSpec((tm, NSA_QW), row),
                  pl.BlockSpec((tm, DSA_QW), row),
                  pl.BlockSpec(w_gate.shape, const), pl.BlockSpec(wa.shape, const),
                  pl.BlockSpec(wb.shape, const), pl.BlockSpec(wc.shape, const),
                  pl.BlockSpec(wo.shape, const)],
        out_specs=pl.BlockSpec((tm, d), row),
        out_shape=jax.ShapeDtypeStruct((n, d), F32),
        compiler_params=_params(1),
        name="merge",
    )(x2, g, oa, ob, oc, w_gate, wa, wb, wc, wo)


def _rope_tables(pos, rot_dim, head_dim):
    half = rot_dim // 2
    inv = ROPE_THETA ** (-jnp.arange(0, rot_dim, 2, dtype=F32) / rot_dim)
    ang = pos.astype(F32)[:, None] * inv[None, :]
    cos, sin = jnp.cos(ang), jnp.sin(ang)
    n = pos.shape[0]
    rest = head_dim - rot_dim
    zero_h = jnp.zeros((n, half), F32)
    c = jnp.concatenate([cos, cos, jnp.ones((n, rest), F32)], axis=-1)
    sm = jnp.concatenate([-sin, zero_h, jnp.zeros((n, rest), F32)], axis=-1)
    sp = jnp.concatenate([zero_h, sin, jnp.zeros((n, rest), F32)], axis=-1)
    reps = LANES // head_dim
    return tuple(jnp.tile(t, (1, reps)) for t in (c, sm, sp))


def _pad_cols(w, width):
    return jnp.pad(w, ((0, 0), (0, width - w.shape[1])))


def _relayout_w_in(w_in):
    offs = np.cumsum((SB_W, SB_W, SB_W, NSA_QW) + (NSA_KVW,) * 6
                     + (3 * NSA_HEADS, DSA_QW, HEAD_DIM, HEAD_DIM, IDX_QW, DSA_IDX_DIM, DSA_IDX_HEADS))
    o_gate, o_dq, o_dk, o_dv, o_iq, o_ik, o_iw, o_end = offs[9:17].tolist()
    parts = [w_in[:, :o_gate], _pad_cols(w_in[:, o_gate:o_dq], LANES), w_in[:, o_dq:o_dk],
             w_in[:, o_dk:o_iq], w_in[:, o_iq:o_ik], _pad_cols(w_in[:, o_ik:o_iw], LANES),
             _pad_cols(w_in[:, o_iw:o_end], LANES)]
    w = jnp.concatenate(parts, axis=1)
    assert w.shape[1] == _P_END
    return w.astype(MXU_DTYPE)


def _compress_weights(pe, w1, w2):
    r = NSA_CMP_LEN // NSA_CMP_STRIDE
    hid = NSA_CMP_HIDDEN
    w1r = w1.reshape(2, r, NSA_CMP_STRIDE, HEAD_DIM, hid)
    eye = jnp.eye(NSA_GROUPS, dtype=w1.dtype)
    w1e = jnp.einsum('kmldj,gh->kmlgdhj', w1r, eye)
    w1e = w1e.reshape(2, r, NSA_CMP_STRIDE * NSA_KVW, NSA_GROUPS * hid)
    w2e = jnp.einsum('kjd,gh->kgjhd', w2, eye).reshape(2, NSA_GROUPS * hid, NSA_KVW)
    pe_e = jnp.broadcast_to(pe.reshape(2, r, NSA_CMP_STRIDE, 1, HEAD_DIM),
                            (2, r, NSA_CMP_STRIDE, NSA_GROUPS, HEAD_DIM))
    return pe_e.reshape(2, r, NSA_CMP_STRIDE * NSA_KVW), w1e.astype(MXU_DTYPE), w2e.astype(MXU_DTYPE)


def _overlap_matrix_t(ncp, seq):
    c = np.arange(ncp)[None, :] * NSA_CMP_STRIDE
    j = np.arange(LANES)[:, None] * NSA_SEL_LEN
    ov = (c < j + NSA_SEL_LEN) & (c + NSA_CMP_LEN - 1 >= j) & (j < seq)
    ov &= (np.arange(ncp)[None, :] < ncp - (NSA_CMP_LEN // NSA_CMP_STRIDE - 1))
    return jnp.asarray(ov, dtype=MXU_DTYPE)


def _block_onehot(seq):
    return jnp.asarray(np.arange(seq)[:, None] // NSA_SEL_LEN == np.arange(LANES)[None, :], dtype=MXU_DTYPE)


def _key_tiles_t(v, tile):
    b, s, w = v.shape
    return jnp.swapaxes(v.reshape(b, s // tile, tile, w), 2, 3)


def _group_values_t(v, tile):
    b, s, _ = v.shape
    vt = _key_tiles_t(v, tile).reshape(b, s // tile, NSA_GROUPS, HEAD_DIM, tile)
    vt = jnp.swapaxes(vt, 1, 2)
    pad = jnp.zeros((b, NSA_GROUPS, s // tile, LANES - HEAD_DIM, tile), v.dtype).at[:, :, :, 0, :].set(1)
    return jnp.concatenate([vt, pad], axis=3)


def kernel(x, ffn1_norm, ffn1_w_gu, ffn1_w_down, mix_norm, w_in, w_gate, nsa_q_norm, nsa_k_norm,
           nsa_cmp_pe, nsa_cmp_w1, nsa_cmp_w2, dsa_q_norm, dsa_k_norm, w_br_a, w_br_b, w_br_c, w_out,
           ffn2_norm, ffn2_w_gu, ffn2_w_down):
    b, s, d = x.shape
    depth = w_in.shape[0]
    n = b * s
    ncp = s // NSA_CMP_STRIDE
    assert s // NSA_SEL_LEN <= LANES and s % 1024 == 0
    top_k = min(DSA_TOPK, s // 4)
    ffn_tf = ffn1_w_down.shape[1]
    cast = lambda w: w.astype(MXU_DTYPE)
    tile2 = lambda v, reps: jnp.tile(v, reps)[None, :]

    pos = jnp.arange(s)
    rope_main = _rope_tables(pos, ROT_DIM, HEAD_DIM)
    rope_idx = _rope_tables(pos, DSA_IDX_ROT, DSA_IDX_DIM)
    rope_cmp = _rope_tables(jnp.arange(ncp) * NSA_CMP_STRIDE + NSA_CMP_LEN - 1, ROT_DIM, HEAD_DIM)
    overlap_t = _overlap_matrix_t(ncp, s)
    onehot = _block_onehot(s)

    x2 = x.reshape(n, d)
    for l in range(depth):
        x2 = _ffn(x2, ffn1_norm[l][None, :], cast(ffn1_w_gu[l]), cast(ffn1_w_down[l]), tm=FFN_TM, tf=ffn_tf)

        (sb, nq, nkc, nvc, nkv, ng, dq, dkv, iq, ik, iw) = _proj(
            x2, mix_norm[l][None, :], _relayout_w_in(w_in[l]),
            tile2(nsa_q_norm[l], NSA_HEADS),
            jnp.stack([jnp.tile(nsa_k_norm[l, 1], NSA_GROUPS), jnp.tile(nsa_k_norm[l, 2], NSA_GROUPS)]),
            tile2(dsa_q_norm[l], DSA_HEADS), tile2(dsa_k_norm[l], 2),
            rope_main, rope_idx, tm=256, seq=s)
        r3 = lambda t: t.reshape(b, s, t.shape[-1])

        pe_e, w1e, w2e = _compress_weights(nsa_cmp_pe[l], nsa_cmp_w1[l], nsa_cmp_w2[l])
        kc, vc = _compress(nkc.reshape(b, ncp, NSA_CMP_STRIDE * NSA_KVW),
                           nvc.reshape(b, ncp, NSA_CMP_STRIDE * NSA_KVW),
                           pe_e, w1e, w2e, tile2(nsa_k_norm[l, 0], NSA_GROUPS), rope_cmp)

        o_a = _sb_attention(r3(sb), tq=256)

        nkv = r3(nkv)
        ks, vs, kw, vw = (nkv[:, :, c * LANES:(c + 1) * LANES] for c in range(4))
        o_b = _nsa_attention(r3(nq), r3(ng), kc, _group_values_t(vc, ncp)[:, :, 0], ks,
                             _group_values_t(vs, TKS), kw, _group_values_t(vw, TQ_NSA), overlap_t, onehot,
                             tq=TQ_NSA, tks=TKS)

        dkv = r3(dkv)
        iwt = jnp.swapaxes(r3(iw)[:, :, :DSA_IDX_HEADS], 1, 2)
        dkvt = _key_tiles_t(dkv, TKS).at[:, :, 0, :].set(1)
        o_c = _dsa_attention(r3(iq), iwt, r3(ik), r3(dq), dkv, dkvt, tq=TQ, tk=TKS, top_k=top_k)

        x2 = _merge(x2, mix_norm[l][None, :], o_a.reshape(n, SB_W), o_b.reshape(n, NSA_QW),
                    o_c.reshape(n, DSA_QW), cast(w_gate[l]), cast(w_br_a[l]), cast(w_br_b[l]),
                    cast(w_br_c[l]), cast(w_out[l]), tm=256)

        x2 = _ffn(x2, ffn2_norm[l][None, :], cast(ffn2_w_gu[l]), cast(ffn2_w_down[l]), tm=FFN_TM, tf=ffn_tf)
    return x2.reshape(b, s, d)
```

```python
import functools

import numpy as np
import jax
import jax.numpy as jnp
from jax import lax
from jax.experimental import pallas as pl
from jax.experimental.pallas import tpu as pltpu

HEAD_DIM = 64
ROT_DIM = HEAD_DIM // 4
ROPE_THETA = 500000.0
NORM_EPS = 1e-6

SB_HEADS = 4
NSA_HEADS = 8
NSA_GROUPS = 2
NSA_HPG = NSA_HEADS // NSA_GROUPS
NSA_CMP_LEN = 32
NSA_CMP_STRIDE = 16
NSA_CMP_HIDDEN = 2 * HEAD_DIM
NSA_SEL_LEN = 64
NSA_SEL_N = 16
NSA_WINDOW = 512
NSA_FORCED_SCORE = 1.0e4
DSA_HEADS = 4
DSA_IDX_HEADS = 8
DSA_IDX_DIM = 32
DSA_IDX_ROT = DSA_IDX_DIM // 4
DSA_TOPK = 256

SB_W = SB_HEADS * HEAD_DIM
NSA_QW = NSA_HEADS * HEAD_DIM
NSA_KVW = NSA_GROUPS * HEAD_DIM
DSA_QW = DSA_HEADS * HEAD_DIM
IDX_QW = DSA_IDX_HEADS * DSA_IDX_DIM

LANES = 128
VMEM_LIMIT = 56 * 1024 * 1024
MXU_DTYPE = jnp.bfloat16
NEG = -1.0e30
SB_SKIP = -120.0
INT_MIN = -2 ** 31
LOG2E = 1.4426950408889634

F32 = jnp.float32

TQ = 256
TQ_NSA = 256
TKS = 512
V_ROWS = HEAD_DIM + 16
FFN_CHUNK = 512
FFN_TM = 512
assert TKS % TQ == 0 and TKS % TQ_NSA == 0 and NSA_WINDOW % TQ_NSA == 0


def _dot(a, b):
    return jnp.dot(a, b, preferred_element_type=F32)


def _dot_nt(a, b):
    return lax.dot_general(a, b, (((1,), (1,)), ((), ())), preferred_element_type=F32)


def _params(n_axes):
    return pltpu.CompilerParams(dimension_semantics=("arbitrary",) * n_axes,
                                vmem_limit_bytes=VMEM_LIMIT)


def _lane_iota(shape):
    return lax.broadcasted_iota(jnp.int32, shape, len(shape) - 1)


def _row_iota(shape):
    return lax.broadcasted_iota(jnp.int32, shape, len(shape) - 2)


def _swap_halves(x):
    return pltpu.roll(x, LANES // 2, 1)


def _head_rms(ys):
    lo = _lane_iota(ys.shape) < HEAD_DIM
    sq = ys * ys
    s_lo = jnp.sum(jnp.where(lo, sq, 0.0), axis=-1, keepdims=True)
    s_hi = jnp.sum(jnp.where(lo, 0.0, sq), axis=-1, keepdims=True)
    ms = jnp.where(lo, s_lo, s_hi) * (1.0 / HEAD_DIM)
    return ys * lax.rsqrt(ms + NORM_EPS)


def _rope(ys, c, sm, sp, half):
    return ys * c + pltpu.roll(ys, LANES - half, 1) * sm + pltpu.roll(ys, half, 1) * sp


def _ffn_kernel(x_ref, g_ref, wg_ref, wu_ref, wd_ref, o_ref, h_ref, acc_ref):
    f = pl.program_id(1)

    @pl.when(f == 0)
    def _():
        x = x_ref[...]
        ms = jnp.mean(x * x, axis=-1, keepdims=True)
        h_ref[...] = (x * lax.rsqrt(ms + NORM_EPS) * g_ref[...]).astype(h_ref.dtype)
        acc_ref[...] = jnp.zeros_like(acc_ref)

    h = h_ref[...]
    tf = wd_ref.shape[0]
    bounds = [(c, min(c + FFN_CHUNK, tf)) for c in range(0, tf, FFN_CHUNK)]

    def gate_up(c0, c1):
        return _dot(h, wg_ref[:, c0:c1]), _dot(h, wu_ref[:, c0:c1])

    ahead = gate_up(*bounds[0])
    for idx, (c0, c1) in enumerate(bounds):
        gate, up = ahead
        if idx + 1 < len(bounds):
            ahead = gate_up(*bounds[idx + 1])
        act = (gate * jax.nn.sigmoid(gate) * up).astype(h_ref.dtype)
        acc_ref[...] += _dot(act, wd_ref[c0:c1, :])

    @pl.when(f == pl.num_programs(1) - 1)
    def _():
        o_ref[...] = x_ref[...] + 0.5 * acc_ref[...]


def _ffn(x2, g, w_gu, w_down, *, tm, tf):
    n, d = x2.shape
    d_ff = w_down.shape[0]
    nf = d_ff // tf
    return pl.pallas_call(
        _ffn_kernel,
        grid=(n // tm, nf),
        in_specs=[
            pl.BlockSpec((tm, d), lambda i, f: (i, 0)),
            pl.BlockSpec((1, d), lambda i, f: (0, 0)),
            pl.BlockSpec((d, tf), lambda i, f: (0, f)),
            pl.BlockSpec((d, tf), lambda i, f: (0, f + nf)),
            pl.BlockSpec((tf, d), lambda i, f: (f, 0)),
        ],
        out_specs=pl.BlockSpec((tm, d), lambda i, f: (i, 0)),
        out_shape=jax.ShapeDtypeStruct((n, d), F32),
        scratch_shapes=[pltpu.VMEM((tm, d), MXU_DTYPE), pltpu.VMEM((tm, d), F32)],
        compiler_params=_params(2),
        name="ffn",
    )(x2, g, w_gu, w_gu, w_down)


_P_SB = 0
_P_NQ = _P_SB + 3 * SB_W
_P_NCV = _P_NQ + NSA_QW
_P_NKV = _P_NCV + 2 * NSA_KVW
_P_NG = _P_NKV + 4 * NSA_KVW
_P_DQ = _P_NG + LANES
_P_DKV = _P_DQ + DSA_QW
_P_IQ = _P_DKV + LANES
_P_IK = _P_IQ + IDX_QW
_P_IW = _P_IK + LANES
_P_END = _P_IW + LANES


def _proj_kernel(x_ref, g_ref, w_ref, nq_g_ref, nk_g_ref, dq_g_ref, dk_g_ref,
                 rc_ref, rm_ref, rp_ref, ic_ref, im_ref, ip_ref,
                 sb_ref, nq_ref, nkc_ref, nvc_ref, nkv_ref, ng_ref, dq_ref, dkv_ref,
                 iq_ref, ik_ref, iw_ref):
    x = x_ref[...]
    ms = jnp.mean(x * x, axis=-1, keepdims=True)
    h = (x * lax.rsqrt(ms + NORM_EPS) * g_ref[...]).astype(MXU_DTYPE)
    rc, rm, rp = rc_ref[...], rm_ref[...], rp_ref[...]
    ic, im, ip = ic_ref[...], im_ref[...], ip_ref[...]
    half, ihalf = ROT_DIM // 2, DSA_IDX_ROT // 2
    scale = HEAD_DIM ** -0.5
    scale2 = scale * LOG2E

    def cols(a, width):
        return _dot(h, w_ref[:, a:a + width])

    def slab(y, s):
        return y[:, s * LANES:(s + 1) * LANES]

    y = cols(_P_SB, 3 * SB_W)
    sb_ref[:, 0:SB_W] = (y[:, 0:SB_W] * scale).astype(sb_ref.dtype)
    sb_ref[:, SB_W:3 * SB_W] = y[:, SB_W:3 * SB_W].astype(sb_ref.dtype)

    y = cols(_P_NQ, NSA_QW)
    for s in range(NSA_QW // LANES):
        ys = _head_rms(slab(y, s)) * slab(nq_g_ref[...], s)
        nq_ref[:, s * LANES:(s + 1) * LANES] = (_rope(ys, rc, rm, rp, half) * scale2).astype(nq_ref.dtype)

    y = cols(_P_NCV, 2 * NSA_KVW)
    nkc_ref[...] = slab(y, 0)
    nvc_ref[...] = slab(y, 1)

    y = cols(_P_NKV, 4 * NSA_KVW)
    for s, gi in ((0, 0), (2, 1)):
        ys = _head_rms(slab(y, s)) * nk_g_ref[gi:gi + 1, :]
        nkv_ref[:, s * LANES:(s + 1) * LANES] = _rope(ys, rc, rm, rp, half).astype(nkv_ref.dtype)
    for s in (1, 3):
        nkv_ref[:, s * LANES:(s + 1) * LANES] = slab(y, s).astype(nkv_ref.dtype)

    ng_ref[...] = jax.nn.sigmoid(cols(_P_NG, LANES))

    y = cols(_P_DQ, DSA_QW)
    for s in range(DSA_QW // LANES):
        ys = _head_rms(slab(y, s)) * slab(dq_g_ref[...], s)
        dq_ref[:, s * LANES:(s + 1) * LANES] = (_rope(ys, rc, rm, rp, half) * scale2).astype(dq_ref.dtype)

    y = cols(_P_DKV, LANES)
    yk = _rope(_head_rms(y) * dk_g_ref[...], rc, rm, rp, half)
    dkv_ref[...] = jnp.where(_lane_iota(y.shape) < HEAD_DIM, yk, y).astype(dkv_ref.dtype)

    y = cols(_P_IQ, IDX_QW)
    for s in range(IDX_QW // LANES):
        iq_ref[:, s * LANES:(s + 1) * LANES] = _rope(slab(y, s), ic, im, ip, ihalf).astype(iq_ref.dtype)
    ik_ref[...] = _rope(cols(_P_IK, LANES), ic, im, ip, ihalf).astype(ik_ref.dtype)
    iw_ref[...] = cols(_P_IW, LANES) * (DSA_IDX_HEADS ** -0.5) * (DSA_IDX_DIM ** -0.5)


def _proj(x2, g, w, nq_g, nk_g, dq_g, dk_g, rope_main, rope_idx, *, tm, seq):
    n, d = x2.shape
    nt = seq // tm
    row = lambda i: (i, 0)
    const = lambda i: (0, 0)
    pos = lambda i: (i % nt, 0)
    widths = [(3 * SB_W, MXU_DTYPE), (NSA_QW, MXU_DTYPE), (NSA_KVW, F32), (NSA_KVW, F32),
              (4 * NSA_KVW, MXU_DTYPE), (LANES, F32), (DSA_QW, MXU_DTYPE), (LANES, MXU_DTYPE),
              (IDX_QW, MXU_DTYPE), (LANES, MXU_DTYPE), (LANES, F32)]
    return pl.pallas_call(
        _proj_kernel,
        grid=(n // tm,),
        in_specs=[pl.BlockSpec((tm, d), row), pl.BlockSpec((1, d), const),
                  pl.BlockSpec((d, _P_END), const),
                  pl.BlockSpec((1, NSA_QW), const), pl.BlockSpec((2, LANES), const),
                  pl.BlockSpec((1, DSA_QW), const), pl.BlockSpec((1, LANES), const)]
                 + [pl.BlockSpec((tm, LANES), pos)] * 6,
        out_specs=[pl.BlockSpec((tm, wd), row) for wd, _ in widths],
        out_shape=[jax.ShapeDtypeStruct((n, wd), dt) for wd, dt in widths],
        compiler_params=_params(1),
        name="proj",
    )(x2, g, w, nq_g, nk_g, dq_g, dk_g, *rope_main, *rope_idx)


def _compress_kernel(xk_ref, xv_ref, pe_ref, w1_ref, w2_ref, g_ref, rc_ref, rm_ref, rp_ref,
                     kc_ref, vc_ref):
    ncp = xk_ref.shape[1]

    def compress(x, kv):
        a0 = _dot((x + pe_ref[kv, 0:1, :]).astype(MXU_DTYPE), w1_ref[kv, 0])
        a1 = _dot((x + pe_ref[kv, 1:2, :]).astype(MXU_DTYPE), w1_ref[kv, 1])
        pre = a0 + pltpu.roll(a1, ncp - 1, 0)
        hid = pre * jax.nn.sigmoid(pre)
        return _dot(hid.astype(MXU_DTYPE), w2_ref[kv])

    yk = _head_rms(compress(xk_ref[0], 0)) * g_ref[...]
    kc_ref[0] = _rope(yk, rc_ref[...], rm_ref[...], rp_ref[...], ROT_DIM // 2).astype(kc_ref.dtype)
    vc_ref[0] = compress(xv_ref[0], 1).astype(vc_ref.dtype)


def _compress(xk, xv, pe, w1, w2, g, rope_c):
    b, ncp, wide = xk.shape
    c3 = lambda i: (0, 0, 0)
    c4 = lambda i: (0, 0, 0, 0)
    c2 = lambda i: (0, 0)
    per_b = lambda i: (i, 0, 0)
    return pl.pallas_call(
        _compress_kernel,
        grid=(b,),
        in_specs=[pl.BlockSpec((1, ncp, wide), per_b), pl.BlockSpec((1, ncp, wide), per_b),
                  pl.BlockSpec(pe.shape, c3), pl.BlockSpec(w1.shape, c4), pl.BlockSpec(w2.shape, c3),
                  pl.BlockSpec((1, LANES), c2)] + [pl.BlockSpec((ncp, LANES), c2)] * 3,
        out_specs=[pl.BlockSpec((1, ncp, LANES), per_b)] * 2,
        out_shape=[jax.ShapeDtypeStruct((b, ncp, LANES), MXU_DTYPE)] * 2,
        compiler_params=_params(1),
        name="compress",
    )(xk, xv, pe, w1, w2, g, *rope_c)


def _sb_kernel(q_ref, k_ref, v_ref, o_ref, *, tq):
    i = pl.program_id(1)
    nslab = SB_W // LANES
    heads = [(sl, p) for sl in range(nslab) for p in range(2)]
    lane = _lane_iota((tq, LANES))
    rows = _row_iota((tq, tq))
    colsq = _lane_iota((tq, tq))
    upper = jnp.where(_row_iota((tq, tq + LANES)) > _lane_iota((tq, tq + LANES)), 1.0,
                      jnp.where(_lane_iota((tq, tq + LANES)) >= tq, 1.0, 0.0)).astype(MXU_DTYPE)
    in_half = [(lane >> 6) == p for p in range(2)]
    qh = [jnp.where(in_half[p], q_ref[0, :, sl * LANES:(sl + 1) * LANES].astype(F32), 0.0).astype(MXU_DTYPE)
          for sl, p in heads]
    reps = tq // LANES

    def tile(kt, carries, accs, diagonal):
        ks = [k_ref[0, pl.ds(kt * tq, tq), sl * LANES:(sl + 1) * LANES] for sl in range(nslab)]
        vs = [v_ref[0, pl.ds(kt * tq, tq), sl * LANES:(sl + 1) * LANES] for sl in range(nslab)]
        zs = [_dot_nt(qh[n], ks[sl]) for n, (sl, _) in enumerate(heads)]
        new_c, new_a = [], []
        for p, (sl, _) in enumerate(heads):
            z = zs[p]
            l = -(jnp.maximum(z, 0.0) + jnp.log1p(jnp.exp(-jnp.abs(z))))
            if diagonal:
                past = colsq < rows
                l = jnp.where(past, l, 0.0)
            hi = l.astype(MXU_DTYPE)
            lo = (l - hi.astype(F32)).astype(MXU_DTYPE)
            sums = _dot(hi, upper) + _dot(lo, upper)
            tail = sums[:, :tq] + jnp.concatenate([carries[p]] * reps, axis=1)
            w = jnp.exp(z + l + tail)
            if diagonal:
                w = jnp.where(past, w, 0.0)
            new_a.append(accs[p] + _dot(w.astype(MXU_DTYPE), vs[sl]))
            new_c.append(carries[p] + sums[:, tq:])
        return tuple(new_c), tuple(new_a)

    zeros = (jnp.zeros((tq, LANES), F32),) * len(heads)
    carries, accs = tile(i, zeros, zeros, True)

    def cond(st):
        kt, carries, _ = st
        worst = functools.reduce(jnp.maximum, carries)
        return (kt >= 0) & (jnp.max(worst) > SB_SKIP)

    def body(st):
        kt, carries, accs = st
        carries, accs = tile(kt, carries, accs, False)
        return kt - 1, carries, accs

    _, _, accs = lax.while_loop(cond, body, (i - 1, carries, accs))
    for sl in range(nslab):
        o_ref[0, :, sl * LANES:(sl + 1) * LANES] = jnp.where(in_half[0], accs[2 * sl],
                                                             accs[2 * sl + 1]).astype(o_ref.dtype)


def _sb_attention(sb, *, tq):
    b, s, _ = sb.shape
    return pl.pallas_call(
        functools.partial(_sb_kernel, tq=tq),
        grid=(b, s // tq),
        in_specs=[pl.BlockSpec((1, tq, SB_W), lambda bi, i: (bi, i, 0)),
                  pl.BlockSpec((1, s, SB_W), lambda bi, i: (bi, 0, 1)),
                  pl.BlockSpec((1, s, SB_W), lambda bi, i: (bi, 0, 2))],
        out_specs=pl.BlockSpec((1, tq, SB_W), lambda bi, i: (bi, i, 0)),
        out_shape=jax.ShapeDtypeStruct((b, s, SB_W), MXU_DTYPE),
        compiler_params=_params(2),
        name="sb_attn",
    )(sb, sb, sb)


def _group_queries(q_ref, g, tq):
    lane = _lane_iota((tq, LANES))
    in_g = (lane >> 6) == g
    qa = []
    for s in range(2):
        qs = q_ref[0, :, s * LANES:(s + 1) * LANES].astype(F32)
        for p in range(2):
            mine = jnp.where((lane >> 6) == p, qs, 0.0)
            both = mine + _swap_halves(mine)
            qa.append(jnp.where(in_g, both, 0.0).astype(MXU_DTYPE))
    return qa


def _fold_rows(x, op, rows=64):
    while x.shape[0] > rows and x.shape[0] % 16 == 0:
        half = x.shape[0] // 2
        x = op(x[:half], x[half:])
    return x


def _key_max(x):
    return jnp.max(_fold_rows(x, jnp.maximum), axis=0, keepdims=True)


def _key_sum(x):
    return jnp.sum(_fold_rows(x, jnp.add), axis=0, keepdims=True)


def _online_update(m, acc_ref, s, v_t):
    m_new = jnp.maximum(m, _key_max(s))
    p = jnp.exp2(s - m_new)
    acc_ref[...] = jnp.exp2(m - m_new) * acc_ref[...] + _dot(v_t, p.astype(MXU_DTYPE))
    return m_new


def _attend_tiles(n_tiles, scores_fn, softmax_fn, ms, fix_last):
    def step(tiles, ms, last):
        scores = [scores_fn(kt) for kt in tiles]
        if last:
            scores[-1] = fix_last(scores[-1])
        for kt, sc in zip(tiles, scores):
            ms = softmax_fn(kt, ms, sc)
        return ms

    plain = n_tiles - 1
    ms = lax.fori_loop(0, plain // 2, lambda p, ms: step([2 * p, 2 * p + 1], ms, False), ms)
    return lax.cond(plain % 2 == 1,
                    lambda ms: step([n_tiles - 2, n_tiles - 1], ms, True),
                    lambda ms: step([n_tiles - 1], ms, True), ms)


def _attend_pipelined(n_tiles, scores_fn, softmax_fn, ms, fix_last, s_ref):
    heads = s_ref.shape[1]

    def issue(slot, kt):
        for h, sc in enumerate(scores_fn(kt)):
            s_ref[slot, h] = sc

    def fold(slot, kt, ms):
        return softmax_fn(kt, ms, tuple(s_ref[slot, h] for h in range(heads)))

    plain = n_tiles - 1

    @pl.when(plain > 0)
    def _():
        issue(0, 0)

    def body(p, ms):
        issue(1, 2 * p + 1)
        ms = fold(0, 2 * p, ms)
        issue(0, jnp.minimum(2 * p + 2, plain - 1))
        return fold(1, 2 * p + 1, ms)

    ms = lax.fori_loop(0, plain // 2, body, ms)
    ms = lax.cond(plain % 2 == 1, lambda ms: fold(0, plain - 1, ms), lambda ms: ms, ms)
    return softmax_fn(n_tiles - 1, ms, fix_last(scores_fn(n_tiles - 1)))


def _online_init(tq, acc_ref):
    acc_ref[...] = jnp.zeros(acc_ref.shape, F32)
    return tuple(jnp.full((1, tq), NEG, F32) for _ in range(acc_ref.shape[0]))


def _nsa_kernel(q_ref, gate_ref, kc_ref, vct_ref, ks_ref, vst_ref, kw_ref, vwt_ref,
                ovt_ref, oh_ref, o_ref, qaug_ref, acc_ref, s_ref, *, tq, tks):
    g = pl.program_id(1)
    i = pl.program_id(2)
    q0 = i * tq
    ncp = kc_ref.shape[1]
    qa = _group_queries(q_ref, g, tq)

    c_vis = (_row_iota((ncp, tq)) * NSA_CMP_STRIDE + (NSA_CMP_LEN - 1)) <= q0 + _lane_iota((ncp, tq))
    kc = kc_ref[0]
    vct = vct_ref[0, 0]
    pc_sum = jnp.zeros((ncp, tq), F32)
    o_cmp = []
    scores = [_dot_nt(kc, qa[j]) for j in range(NSA_HPG)]
    for j in range(NSA_HPG):
        s = jnp.where(c_vis, scores[j], NEG)
        p = jnp.where(c_vis, jnp.exp2(s - _key_max(s)), 0.0)
        den = _key_sum(p)
        p = p * (1.0 / jnp.where(den > 0.0, den, 1.0))
        pc_sum = pc_sum + p
        o_cmp.append(_dot(vct, p.astype(MXU_DTYPE))[:HEAD_DIM])

    band = tq + NSA_WINDOW
    start = pl.multiple_of(jnp.maximum(q0 - NSA_WINDOW, 0), tq)
    kw = kw_ref[0, pl.ds(start, band), :]
    s_w = start + _row_iota((band, tq))
    t_w = q0 + _lane_iota((band, tq))
    w_ok = (s_w <= t_w) & (s_w > t_w - NSA_WINDOW)
    o_win = []
    scores = [_dot_nt(kw, qa[j]) for j in range(NSA_HPG)]
    for j in range(NSA_HPG):
        s = jnp.where(w_ok, scores[j], NEG)
        p = jnp.exp2(s - _key_max(s)).astype(MXU_DTYPE)
        acc = jnp.zeros((V_ROWS, tq), F32)
        for c in range(band // tq):
            acc = acc + _dot(vwt_ref[0, 0, start // tq + c], p[c * tq:(c + 1) * tq, :])
        o_win.append(acc[:HEAD_DIM] * (1.0 / acc[HEAD_DIM:HEAD_DIM + 1]))

    hi = pc_sum.astype(MXU_DTYPE)
    lo = (pc_sum - hi.astype(F32)).astype(MXU_DTYPE)
    imp = _dot(ovt_ref[...], hi) + _dot(ovt_ref[...], lo)
    blk = _row_iota((LANES, tq))
    t_b = q0 + _lane_iota((LANES, tq))
    cur = t_b >> 6
    visible = blk * NSA_SEL_LEN <= t_b
    forced = (blk == 0) | (blk == cur) | (blk == cur - 1)
    taken = -3.0e38
    work = jnp.where(visible, jnp.where(forced, taken, imp), NEG)
    blk_f = blk.astype(F32)
    for _ in range(NSA_SEL_N - 3):
        m = jnp.max(work, axis=0, keepdims=True)
        first = jnp.min(jnp.where(work == m, blk_f, float(LANES)), axis=0, keepdims=True)
        work = jnp.where(blk_f == first, taken, work)
    sel_bias = jnp.where(visible, jnp.where(work == taken, 0.0, NEG), NEG).T.astype(MXU_DTYPE)

    n_tiles = (q0 + tq + tks - 1) // tks
    t_s = q0 + _lane_iota((tks, tq))
    row_s = _row_iota((tks, tq))
    for j in range(NSA_HPG):
        qaug_ref[j] = jnp.concatenate([qa[j], sel_bias], axis=1)

    def sel_scores(kt):
        k_aug = jnp.concatenate([ks_ref[0, pl.ds(kt * tks, tks), :], oh_ref[pl.ds(kt * tks, tks), :]], axis=1)
        return tuple(_dot_nt(k_aug, qaug_ref[j]) for j in range(NSA_HPG))

    def sel_softmax(kt, ms, scores):
        v_t = vst_ref[0, 0, kt]
        return tuple(_online_update(ms[j], acc_ref.at[j], scores[j], v_t) for j in range(NSA_HPG))

    def sel_causal(scores):
        causal = (n_tiles - 1) * tks + row_s <= t_s
        return tuple(jnp.where(causal, s, NEG) for s in scores)

    _attend_pipelined(n_tiles, sel_scores, sel_softmax, _online_init(tq, acc_ref), sel_causal, s_ref)
    o_sel = [acc_ref[j, :HEAD_DIM, :] * (1.0 / acc_ref[j, HEAD_DIM:HEAD_DIM + 1, :]) for j in range(NSA_HPG)]

    gates_t = gate_ref[0].T
    merged = []
    for j in range(NSA_HPG):
        tot = jnp.zeros((HEAD_DIM, tq), F32)
        for r, branch in enumerate((o_cmp, o_sel, o_win)):
            row = 3 * j + r
            stride = 3 * NSA_HPG
            gate = jnp.where(g == 0, gates_t[row:row + 1, :], gates_t[stride + row:stride + row + 1, :])
            tot = tot + gate * branch[j]
        merged.append(tot)
    for s in range(2):
        pair = jnp.concatenate([merged[2 * s], merged[2 * s + 1]], axis=0)
        o_ref[0, :, s * LANES:(s + 1) * LANES] = pair.T.astype(o_ref.dtype)


def _nsa_attention(nq, ng, kc, vct, ks, vst, kw, vwt, overlap_t, onehot, *, tq, tks):
    b, s, _ = nq.shape
    ncp = kc.shape[1]
    gw = NSA_HPG * HEAD_DIM
    per_b3 = lambda bi, g, i: (bi, 0, 0)
    per_bg4 = lambda bi, g, i: (bi, g, 0, 0)
    per_bg5 = lambda bi, g, i: (bi, g, 0, 0, 0)
    const = lambda bi, g, i: (0, 0)
    return pl.pallas_call(
        functools.partial(_nsa_kernel, tq=tq, tks=tks),
        grid=(b, NSA_GROUPS, s // tq),
        in_specs=[pl.BlockSpec((1, tq, gw), lambda bi, g, i: (bi, i, g)),
                  pl.BlockSpec((1, tq, LANES), lambda bi, g, i: (bi, i, 0)),
                  pl.BlockSpec((1, ncp, LANES), per_b3),
                  pl.BlockSpec((1, 1) + vct.shape[2:], per_bg4),
                  pl.BlockSpec((1, s, LANES), per_b3),
                  pl.BlockSpec((1, 1) + vst.shape[2:], per_bg5),
                  pl.BlockSpec((1, s, LANES), per_b3),
                  pl.BlockSpec((1, 1) + vwt.shape[2:], per_bg5),
                  pl.BlockSpec(overlap_t.shape, const),
                  pl.BlockSpec(onehot.shape, const)],
        out_specs=pl.BlockSpec((1, tq, gw), lambda bi, g, i: (bi, i, g)),
        out_shape=jax.ShapeDtypeStruct((b, s, NSA_QW), MXU_DTYPE),
        scratch_shapes=[pltpu.VMEM((NSA_HPG, tq, 2 * LANES), MXU_DTYPE),
                        pltpu.VMEM((NSA_HPG, V_ROWS, tq), F32),
                        pltpu.VMEM((2, NSA_HPG, tks, tq), F32)],
        compiler_params=_params(3),
        name="nsa_attn",
    )(nq, ng, kc, vct, ks, vst, kw, vwt, overlap_t, onehot)


def _dsa_kernel(iq_ref, iwt_ref, ik_ref, q_ref, kv_ref, kvt_ref, o_ref, khi_ref, klo_ref, bias_ref, qa_ref,
                acc_ref, s_ref, *, tq, tk, top_k):
    i = pl.program_id(1)
    q0 = i * tq
    n_tiles = (q0 + tq + tk - 1) // tk
    lane = _lane_iota((tq, LANES))
    t_q = q0 + _lane_iota((tk, tq))
    row_k = _row_iota((tk, tq))

    iqa = []
    for s in range(IDX_QW // LANES):
        qs = iq_ref[0, :, s * LANES:(s + 1) * LANES].astype(F32)
        for p in range(LANES // DSA_IDX_DIM):
            mine = jnp.where((lane >> 5) == p, qs, 0.0)
            if p:
                mine = pltpu.roll(mine, LANES - p * DSA_IDX_DIM, 1)
            iqa.append(mine.astype(MXU_DTYPE))
    w_h = [iwt_ref[0, h:h + 1, :] for h in range(DSA_IDX_HEADS)]

    def score_tiles(tiles, last_is_diagonal):
        logits = []
        for kt in tiles:
            ik = ik_ref[0, pl.ds(kt * tk, tk), :]
            logits.append([_dot_nt(ik, iqa[h]) for h in range(DSA_IDX_HEADS)])
        for n, kt in enumerate(tiles):
            score = jnp.zeros((tk, tq), F32)
            for h in range(DSA_IDX_HEADS):
                score = score + w_h[h] * jnp.maximum(logits[n][h], 0.0)
            score = jnp.where(score == 0.0, 0.0, score)
            bits = pltpu.bitcast(score, jnp.int32)
            key = jnp.where(bits < 0, bits ^ 0x7FFFFFFF, bits)
            if last_is_diagonal and n == len(tiles) - 1:
                key = jnp.where(kt * tk + row_k <= t_q, key, INT_MIN)
            khi_ref[kt] = (key >> 16).astype(jnp.int16)
            klo_ref[kt] = ((key & 0xFFFF) - 0x8000).astype(jnp.int16)

    def score_pair(p, _):
        score_tiles([2 * p, 2 * p + 1], False)
        return 0

    plain = n_tiles - 1
    lax.fori_loop(0, plain // 2, score_pair, 0)
    lax.cond(plain % 2 == 1,
             lambda: score_tiles([n_tiles - 2, n_tiles - 1], True),
             lambda: score_tiles([n_tiles - 1], True))
    one, zero = jnp.ones((), jnp.bfloat16), jnp.zeros((), jnp.bfloat16)

    def count(flags):
        def body(kt, acc):
            return acc + _fold_rows(flags(kt), jnp.add, rows=32).astype(F32)
        acc = lax.fori_loop(0, n_tiles, body, jnp.zeros((32, tq), F32))
        return jnp.sum(acc, axis=0, keepdims=True)

    def kth_largest(ref):
        def body(b, ans_u):
            cand_u = ans_u | lax.shift_left(jnp.int32(1), 15 - b)
            cand = (cand_u - 0x8000).astype(jnp.int16)
            cnt = count(lambda kt: jnp.where(ref[kt] >= cand, one, zero))
            return jnp.where(cnt >= top_k, cand_u, ans_u)
        ans_u = lax.fori_loop(0, 16, body, jnp.zeros((1, tq), jnp.int32))
        return ans_u - 0x8000

    t_hi32 = kth_largest(khi_ref)
    t_hi = t_hi32.astype(jnp.int16)

    def pin_body(kt, _):
        hi = khi_ref[kt]
        klo_ref[kt] = jnp.where(hi == t_hi, klo_ref[kt],
                                jnp.where(hi > t_hi, jnp.int16(32767), jnp.int16(-32768)))
        return 0

    lax.fori_loop(0, n_tiles, pin_body, 0)
    t_lo32 = kth_largest(klo_ref)
    t_lo = jnp.where(t_hi32 == -0x8000, jnp.maximum(t_lo32, 1 - 0x8000), t_lo32).astype(jnp.int16)

    def at_least(kt, yes, no):
        hi = khi_ref[kt]
        return jnp.where(hi > t_hi, yes, jnp.where(hi == t_hi, jnp.where(klo_ref[kt] >= t_lo, yes, no), no))

    def above(kt):
        return jnp.where(khi_ref[kt] > t_hi, one, jnp.where(klo_ref[kt] > t_lo, one, zero))

    bias_yes, bias_no = jnp.zeros((), bias_ref.dtype), jnp.full((), NEG, bias_ref.dtype)

    def plain_bias():
        def body(kt, _):
            bias_ref[kt] = at_least(kt, bias_yes, bias_no)
            return 0
        lax.fori_loop(0, n_tiles, body, 0)

    def ranked_bias():
        need = top_k - count(above)
        lower = jnp.where(_lane_iota((tk, tk)) < _row_iota((tk, tk)), 1.0, 0.0).astype(MXU_DTYPE)

        def body(kt, seen):
            tie = jnp.where(khi_ref[kt] == t_hi, jnp.where(klo_ref[kt] == t_lo, one, zero), zero)
            rank = _dot(lower, tie.astype(MXU_DTYPE)) + seen
            tie = tie.astype(F32)
            take = jnp.where(above(kt).astype(F32) > 0.0, 1.0, jnp.where(rank < need, tie, 0.0))
            bias_ref[kt] = ((take - 1.0) * (-NEG)).astype(bias_ref.dtype)
            return seen + _key_sum(tie)

        lax.fori_loop(0, n_tiles, body, jnp.zeros((1, tq), F32))

    reached = count(lambda kt: at_least(kt, one, zero))
    lax.cond(jnp.max(reached) > top_k, ranked_bias, plain_bias)

    for s in range(DSA_QW // LANES):
        qs = q_ref[0, :, s * LANES:(s + 1) * LANES].astype(F32)
        lo = jnp.where(lane < HEAD_DIM, qs, 0.0)
        hi = _swap_halves(jnp.where(lane < HEAD_DIM, 0.0, qs))
        qa_ref[2 * s] = lo.astype(MXU_DTYPE)
        qa_ref[2 * s + 1] = hi.astype(MXU_DTYPE)

    def att_scores(kt):
        kv = kv_ref[0, pl.ds(kt * tk, tk), :]
        bias = bias_ref[kt].astype(F32)
        return tuple(_dot_nt(kv, qa_ref[h]) + bias for h in range(DSA_HEADS))

    def att_softmax(kt, ms, scores):
        kv_t = kvt_ref[0, kt]
        return tuple(_online_update(ms[h], acc_ref.at[h], scores[h], kv_t) for h in range(DSA_HEADS))

    _attend_pipelined(n_tiles, att_scores, att_softmax, _online_init(tq, acc_ref), lambda scores: scores, s_ref)
    outs = [acc_ref[h, :HEAD_DIM, :] * (1.0 / acc_ref[h, HEAD_DIM:HEAD_DIM + 1, :]) for h in range(DSA_HEADS)]

    for s in range(DSA_QW // LANES):
        pair = jnp.concatenate([outs[2 * s], outs[2 * s + 1]], axis=0)
        o_ref[0, :, s * LANES:(s + 1) * LANES] = pair.T.astype(o_ref.dtype)


def _dsa_attention(iq, iwt, ik, dq, dkv, dkvt, *, tq, tk, top_k):
    b, s, _ = dq.shape
    tile = lambda width: pl.BlockSpec((1, tq, width), lambda bi, i: (bi, i, 0))
    full = pl.BlockSpec((1, s, LANES), lambda bi, i: (bi, 0, 0))
    return pl.pallas_call(
        functools.partial(_dsa_kernel, tq=tq, tk=tk, top_k=top_k),
        grid=(b, s // tq),
        in_specs=[tile(IDX_QW), pl.BlockSpec((1, DSA_IDX_HEADS, tq), lambda bi, i: (bi, 0, i)), full,
                  tile(DSA_QW), full, pl.BlockSpec((1,) + dkvt.shape[1:], lambda bi, i: (bi, 0, 0, 0))],
        out_specs=tile(DSA_QW),
        out_shape=jax.ShapeDtypeStruct((b, s, DSA_QW), MXU_DTYPE),
        scratch_shapes=[pltpu.VMEM((s // tk, tk, tq), jnp.int16), pltpu.VMEM((s // tk, tk, tq), jnp.int16),
                        pltpu.VMEM((s // tk, tk, tq), MXU_DTYPE),
                        pltpu.VMEM((DSA_HEADS, tq, LANES), MXU_DTYPE), pltpu.VMEM((DSA_HEADS, V_ROWS, tq), F32),
                        pltpu.VMEM((2, DSA_HEADS, tk, tq), F32)],
        compiler_params=_params(2),
        name="dsa_attn",
    )(iq, iwt, ik, dq, dkv, dkvt)


def _merge_kernel(x_ref, g_ref, oa_ref, ob_ref, oc_ref, wg_ref, wa_ref, wb_ref, wc_ref, wo_ref, o_ref):
    x = x_ref[...]
    d = x.shape[1]
    ms = jnp.mean(x * x, axis=-1, keepdims=True)
    h = (x * lax.rsqrt(ms + NORM_EPS) * g_ref[...]).astype(MXU_DTYPE)
    merged = jnp.zeros(x.shape, F32)
    for r, (o_r, w_r) in enumerate(((oa_ref, wa_ref), (ob_ref, wb_ref), (oc_ref, wc_ref))):
        gate = jax.nn.sigmoid(_dot(h, wg_ref[:, r * d:(r + 1) * d]))
        merged = merged + gate * _dot(o_r[...], w_r[...])
    o_ref[...] = x + _dot(merged.astype(MXU_DTYPE), wo_ref[...])


def _merge(x2, g, oa, ob, oc, w_gate, wa, wb, wc, wo, *, tm):
    n, d = x2.shape
    row = lambda i: (i, 0)
    const = lambda i: (0, 0)
    return pl.pallas_call(
        _merge_kernel,
        grid=(n // tm,),
        in_specs=[pl.BlockSpec((tm, d), row), pl.BlockSpec((1, d), const),
                  pl.BlockSpec((tm, SB_W), row), pl.BlockSpec((tm, NSA_QW), row),
                  pl.BlockSpec((tm, DSA_QW), row),
                  pl.BlockSpec(w_gate.shape, const), pl.BlockSpec(wa.shape, const),
                  pl.BlockSpec(wb.shape, const), pl.BlockSpec(wc.shape, const),
                  pl.BlockSpec(wo.shape, const)],
        out_specs=pl.BlockSpec((tm, d), row),
        out_shape=jax.ShapeDtypeStruct((n, d), F32),
        compiler_params=_params(1),
        name="merge",
    )(x2, g, oa, ob, oc, w_gate, wa, wb, wc, wo)


def _rope_tables(pos, rot_dim, head_dim):
    half = rot_dim // 2
    inv = ROPE_THETA ** (-jnp.arange(0, rot_dim, 2, dtype=F32) / rot_dim)
    ang = pos.astype(F32)[:, None] * inv[None, :]
    cos, sin = jnp.cos(ang), jnp.sin(ang)
    n = pos.shape[0]
    rest = head_dim - rot_dim
    zero_h = jnp.zeros((n, half), F32)
    c = jnp.concatenate([cos, cos, jnp.ones((n, rest), F32)], axis=-1)
    sm = jnp.concatenate([-sin, zero_h, jnp.zeros((n, rest), F32)], axis=-1)
    sp = jnp.concatenate([zero_h, sin, jnp.zeros((n, rest), F32)], axis=-1)
    reps = LANES // head_dim
    return tuple(jnp.tile(t, (1, reps)) for t in (c, sm, sp))


def _pad_cols(w, width):
    return jnp.pad(w, ((0, 0), (0, width - w.shape[1])))


def _relayout_w_in(w_in):
    offs = np.cumsum((SB_W, SB_W, SB_W, NSA_QW) + (NSA_KVW,) * 6
                     + (3 * NSA_HEADS, DSA_QW, HEAD_DIM, HEAD_DIM, IDX_QW, DSA_IDX_DIM, DSA_IDX_HEADS))
    o_gate, o_dq, o_dk, o_dv, o_iq, o_ik, o_iw, o_end = offs[9:17].tolist()
    parts = [w_in[:, :o_gate], _pad_cols(w_in[:, o_gate:o_dq], LANES), w_in[:, o_dq:o_dk],
             w_in[:, o_dk:o_iq], w_in[:, o_iq:o_ik], _pad_cols(w_in[:, o_ik:o_iw], LANES),
             _pad_cols(w_in[:, o_iw:o_end], LANES)]
    w = jnp.concatenate(parts, axis=1)
    assert w.shape[1] == _P_END
    return w.astype(MXU_DTYPE)


def _compress_weights(pe, w1, w2):
    r = NSA_CMP_LEN // NSA_CMP_STRIDE
    hid = NSA_CMP_HIDDEN
    w1r = w1.reshape(2, r, NSA_CMP_STRIDE, HEAD_DIM, hid)
    eye = jnp.eye(NSA_GROUPS, dtype=w1.dtype)
    w1e = jnp.einsum('kmldj,gh->kmlgdhj', w1r, eye)
    w1e = w1e.reshape(2, r, NSA_CMP_STRIDE * NSA_KVW, NSA_GROUPS * hid)
    w2e = jnp.einsum('kjd,gh->kgjhd', w2, eye).reshape(2, NSA_GROUPS * hid, NSA_KVW)
    pe_e = jnp.broadcast_to(pe.reshape(2, r, NSA_CMP_STRIDE, 1, HEAD_DIM),
                            (2, r, NSA_CMP_STRIDE, NSA_GROUPS, HEAD_DIM))
    return pe_e.reshape(2, r, NSA_CMP_STRIDE * NSA_KVW), w1e.astype(MXU_DTYPE), w2e.astype(MXU_DTYPE)


def _overlap_matrix_t(ncp, seq):
    c = np.arange(ncp)[None, :] * NSA_CMP_STRIDE
    j = np.arange(LANES)[:, None] * NSA_SEL_LEN
    ov = (c < j + NSA_SEL_LEN) & (c + NSA_CMP_LEN - 1 >= j) & (j < seq)
    ov &= (np.arange(ncp)[None, :] < ncp - (NSA_CMP_LEN // NSA_CMP_STRIDE - 1))
    return jnp.asarray(ov, dtype=MXU_DTYPE)


def _block_onehot(seq):
    return jnp.asarray(np.arange(seq)[:, None] // NSA_SEL_LEN == np.arange(LANES)[None, :], dtype=MXU_DTYPE)


def _key_tiles_t(v, tile):
    b, s, w = v.shape
    return jnp.swapaxes(v.reshape(b, s // tile, tile, w), 2, 3)


def _group_values_t(v, tile):
    b, s, w = v.shape
    groups = w // HEAD_DIM
    vt = _key_tiles_t(v, tile).reshape(b, s // tile, groups, HEAD_DIM, tile)
    vt = jnp.swapaxes(vt, 1, 2)
    pad = jnp.zeros((b, groups, s // tile, V_ROWS - HEAD_DIM, tile), v.dtype).at[:, :, :, 0, :].set(1)
    return jnp.concatenate([vt, pad], axis=3)


def kernel(x, ffn1_norm, ffn1_w_gu, ffn1_w_down, mix_norm, w_in, w_gate, nsa_q_norm, nsa_k_norm,
           nsa_cmp_pe, nsa_cmp_w1, nsa_cmp_w2, dsa_q_norm, dsa_k_norm, w_br_a, w_br_b, w_br_c, w_out,
           ffn2_norm, ffn2_w_gu, ffn2_w_down):
    b, s, d = x.shape
    depth = w_in.shape[0]
    n = b * s
    ncp = s // NSA_CMP_STRIDE
    assert s // NSA_SEL_LEN <= LANES and s % 1024 == 0
    top_k = min(DSA_TOPK, s // 4)
    ffn_tf = ffn1_w_down.shape[1]
    cast = lambda w: w.astype(MXU_DTYPE)
    tile2 = lambda v, reps: jnp.tile(v, reps)[None, :]

    pos = jnp.arange(s)
    rope_main = _rope_tables(pos, ROT_DIM, HEAD_DIM)
    rope_idx = _rope_tables(pos, DSA_IDX_ROT, DSA_IDX_DIM)
    rope_cmp = _rope_tables(jnp.arange(ncp) * NSA_CMP_STRIDE + NSA_CMP_LEN - 1, ROT_DIM, HEAD_DIM)
    overlap_t = _overlap_matrix_t(ncp, s)
    onehot = _block_onehot(s)

    x2 = x.reshape(n, d)
    for l in range(depth):
        x2 = _ffn(x2, ffn1_norm[l][None, :], cast(ffn1_w_gu[l]), cast(ffn1_w_down[l]), tm=FFN_TM, tf=ffn_tf)

        (sb, nq, nkc, nvc, nkv, ng, dq, dkv, iq, ik, iw) = _proj(
            x2, mix_norm[l][None, :], _relayout_w_in(w_in[l]),
            tile2(nsa_q_norm[l], NSA_HEADS),
            jnp.stack([jnp.tile(nsa_k_norm[l, 1], NSA_GROUPS), jnp.tile(nsa_k_norm[l, 2], NSA_GROUPS)]),
            tile2(dsa_q_norm[l], DSA_HEADS), tile2(dsa_k_norm[l], 2),
            rope_main, rope_idx, tm=256, seq=s)
        r3 = lambda t: t.reshape(b, s, t.shape[-1])

        pe_e, w1e, w2e = _compress_weights(nsa_cmp_pe[l], nsa_cmp_w1[l], nsa_cmp_w2[l])
        kc, vc = _compress(nkc.reshape(b, ncp, NSA_CMP_STRIDE * NSA_KVW),
                           nvc.reshape(b, ncp, NSA_CMP_STRIDE * NSA_KVW),
                           pe_e, w1e, w2e, tile2(nsa_k_norm[l, 0], NSA_GROUPS), rope_cmp)

        o_a = _sb_attention(r3(sb), tq=256)

        nkv = r3(nkv)
        ks, vs, kw, vw = (nkv[:, :, c * LANES:(c + 1) * LANES] for c in range(4))
        o_b = _nsa_attention(r3(nq), r3(ng), kc, _group_values_t(vc, ncp)[:, :, 0], ks,
                             _group_values_t(vs, TKS), kw, _group_values_t(vw, TQ_NSA), overlap_t, onehot,
                             tq=TQ_NSA, tks=TKS)

        dkv = r3(dkv)
        iwt = jnp.swapaxes(r3(iw)[:, :, :DSA_IDX_HEADS], 1, 2)
        dkvt = _group_values_t(dkv[:, :, HEAD_DIM:], TKS)[:, 0]
        o_c = _dsa_attention(r3(iq), iwt, r3(ik), r3(dq), dkv, dkvt, tq=TQ, tk=TKS, top_k=top_k)

        x2 = _merge(x2, mix_norm[l][None, :], o_a.reshape(n, SB_W), o_b.reshape(n, NSA_QW),
                    o_c.reshape(n, DSA_QW), cast(w_gate[l]), cast(w_br_a[l]), cast(w_br_b[l]),
                    cast(w_br_c[l]), cast(w_out[l]), tm=256)

        x2 = _ffn(x2, ffn2_norm[l][None, :], cast(ffn2_w_gu[l]), cast(ffn2_w_down[l]), tm=FFN_TM, tf=ffn_tf)
    return x2.reshape(b, s, d)
```

```python
import functools

import numpy as np
import jax
import jax.numpy as jnp
from jax import lax
from jax.experimental import pallas as pl
from jax.experimental.pallas import tpu as pltpu

HEAD_DIM = 64
ROT_DIM = HEAD_DIM // 4
ROPE_THETA = 500000.0
NORM_EPS = 1e-6

SB_HEADS = 4
NSA_HEADS = 8
NSA_GROUPS = 2
NSA_HPG = NSA_HEADS // NSA_GROUPS
NSA_CMP_LEN = 32
NSA_CMP_STRIDE = 16
NSA_CMP_HIDDEN = 2 * HEAD_DIM
NSA_SEL_LEN = 64
NSA_SEL_N = 16
NSA_WINDOW = 512
NSA_FORCED_SCORE = 1.0e4
DSA_HEADS = 4
DSA_IDX_HEADS = 8
DSA_IDX_DIM = 32
DSA_IDX_ROT = DSA_IDX_DIM // 4
DSA_TOPK = 256

SB_W = SB_HEADS * HEAD_DIM
NSA_QW = NSA_HEADS * HEAD_DIM
NSA_KVW = NSA_GROUPS * HEAD_DIM
DSA_QW = DSA_HEADS * HEAD_DIM
IDX_QW = DSA_IDX_HEADS * DSA_IDX_DIM

LANES = 128
VMEM_LIMIT = 56 * 1024 * 1024
MXU_DTYPE = jnp.bfloat16
NEG = -1.0e30
SB_SKIP = -120.0
INT_MIN = -2 ** 31
LOG2E = 1.4426950408889634

F32 = jnp.float32

TQ = 256
TQ_NSA = 256
TKS = 512
V_ROWS = LANES
FFN_CHUNK = 512
FFN_TM = 512
assert TKS % TQ == 0 and TKS % TQ_NSA == 0 and NSA_WINDOW % TQ_NSA == 0


def _dot(a, b):
    return jnp.dot(a, b, preferred_element_type=F32)


def _dot_nt(a, b):
    return lax.dot_general(a, b, (((1,), (1,)), ((), ())), preferred_element_type=F32)


def _params(n_axes):
    return pltpu.CompilerParams(dimension_semantics=("arbitrary",) * n_axes,
                                vmem_limit_bytes=VMEM_LIMIT)


def _lane_iota(shape):
    return lax.broadcasted_iota(jnp.int32, shape, len(shape) - 1)


def _row_iota(shape):
    return lax.broadcasted_iota(jnp.int32, shape, len(shape) - 2)


def _swap_halves(x):
    return pltpu.roll(x, LANES // 2, 1)


def _head_rms(ys):
    lo = _lane_iota(ys.shape) < HEAD_DIM
    sq = ys * ys
    s_lo = jnp.sum(jnp.where(lo, sq, 0.0), axis=-1, keepdims=True)
    s_hi = jnp.sum(jnp.where(lo, 0.0, sq), axis=-1, keepdims=True)
    ms = jnp.where(lo, s_lo, s_hi) * (1.0 / HEAD_DIM)
    return ys * lax.rsqrt(ms + NORM_EPS)


def _rope(ys, c, sm, sp, half):
    return ys * c + pltpu.roll(ys, LANES - half, 1) * sm + pltpu.roll(ys, half, 1) * sp


def _ffn_kernel(x_ref, g_ref, wg_ref, wu_ref, wd_ref, o_ref, h_ref, acc_ref):
    f = pl.program_id(1)

    @pl.when(f == 0)
    def _():
        x = x_ref[...]
        ms = jnp.mean(x * x, axis=-1, keepdims=True)
        h_ref[...] = (x * lax.rsqrt(ms + NORM_EPS) * g_ref[...]).astype(h_ref.dtype)
        acc_ref[...] = jnp.zeros_like(acc_ref)

    h = h_ref[...]
    tf = wd_ref.shape[0]
    bounds = [(c, min(c + FFN_CHUNK, tf)) for c in range(0, tf, FFN_CHUNK)]

    def gate_up(c0, c1):
        return _dot(h, wg_ref[:, c0:c1]), _dot(h, wu_ref[:, c0:c1])

    ahead = gate_up(*bounds[0])
    for idx, (c0, c1) in enumerate(bounds):
        gate, up = ahead
        if idx + 1 < len(bounds):
            ahead = gate_up(*bounds[idx + 1])
        act = (gate * jax.nn.sigmoid(gate) * up).astype(h_ref.dtype)
        acc_ref[...] += _dot(act, wd_ref[c0:c1, :])

    @pl.when(f == pl.num_programs(1) - 1)
    def _():
        o_ref[...] = x_ref[...] + 0.5 * acc_ref[...]


def _ffn(x2, g, w_gu, w_down, *, tm, tf):
    n, d = x2.shape
    d_ff = w_down.shape[0]
    nf = d_ff // tf
    return pl.pallas_call(
        _ffn_kernel,
        grid=(n // tm, nf),
        in_specs=[
            pl.BlockSpec((tm, d), lambda i, f: (i, 0)),
            pl.BlockSpec((1, d), lambda i, f: (0, 0)),
            pl.BlockSpec((d, tf), lambda i, f: (0, f)),
            pl.BlockSpec((d, tf), lambda i, f: (0, f + nf)),
            pl.BlockSpec((tf, d), lambda i, f: (f, 0)),
        ],
        out_specs=pl.BlockSpec((tm, d), lambda i, f: (i, 0)),
        out_shape=jax.ShapeDtypeStruct((n, d), F32),
        scratch_shapes=[pltpu.VMEM((tm, d), MXU_DTYPE), pltpu.VMEM((tm, d), F32)],
        compiler_params=_params(2),
        name="ffn",
    )(x2, g, w_gu, w_gu, w_down)


_P_SB = 0
_P_NQ = _P_SB + 3 * SB_W
_P_NCV = _P_NQ + NSA_QW
_P_NKV = _P_NCV + 2 * NSA_KVW
_P_NG = _P_NKV + 4 * NSA_KVW
_P_DQ = _P_NG + LANES
_P_DKV = _P_DQ + DSA_QW
_P_IQ = _P_DKV + LANES
_P_IK = _P_IQ + IDX_QW
_P_IW = _P_IK + LANES
_P_END = _P_IW + LANES


def _proj_kernel(x_ref, g_ref, w_ref, nq_g_ref, nk_g_ref, dq_g_ref, dk_g_ref,
                 rc_ref, rm_ref, rp_ref, ic_ref, im_ref, ip_ref,
                 sb_ref, nq_ref, nkc_ref, nvc_ref, nkv_ref, ng_ref, dq_ref, dkv_ref,
                 iq_ref, ik_ref, iw_ref):
    x = x_ref[...]
    ms = jnp.mean(x * x, axis=-1, keepdims=True)
    h = (x * lax.rsqrt(ms + NORM_EPS) * g_ref[...]).astype(MXU_DTYPE)
    rc, rm, rp = rc_ref[...], rm_ref[...], rp_ref[...]
    ic, im, ip = ic_ref[...], im_ref[...], ip_ref[...]
    half, ihalf = ROT_DIM // 2, DSA_IDX_ROT // 2
    scale = HEAD_DIM ** -0.5
    scale2 = scale * LOG2E

    def cols(a, width):
        return _dot(h, w_ref[:, a:a + width])

    def slab(y, s):
        return y[:, s * LANES:(s + 1) * LANES]

    y = cols(_P_SB, 3 * SB_W)
    sb_ref[:, 0:SB_W] = (y[:, 0:SB_W] * scale).astype(sb_ref.dtype)
    sb_ref[:, SB_W:3 * SB_W] = y[:, SB_W:3 * SB_W].astype(sb_ref.dtype)

    y = cols(_P_NQ, NSA_QW)
    for s in range(NSA_QW // LANES):
        ys = _head_rms(slab(y, s)) * slab(nq_g_ref[...], s)
        nq_ref[:, s * LANES:(s + 1) * LANES] = (_rope(ys, rc, rm, rp, half) * scale2).astype(nq_ref.dtype)

    y = cols(_P_NCV, 2 * NSA_KVW)
    nkc_ref[...] = slab(y, 0)
    nvc_ref[...] = slab(y, 1)

    y = cols(_P_NKV, 4 * NSA_KVW)
    for s, gi in ((0, 0), (2, 1)):
        ys = _head_rms(slab(y, s)) * nk_g_ref[gi:gi + 1, :]
        nkv_ref[:, s * LANES:(s + 1) * LANES] = _rope(ys, rc, rm, rp, half).astype(nkv_ref.dtype)
    for s in (1, 3):
        nkv_ref[:, s * LANES:(s + 1) * LANES] = slab(y, s).astype(nkv_ref.dtype)

    ng_ref[...] = jax.nn.sigmoid(cols(_P_NG, LANES))

    y = cols(_P_DQ, DSA_QW)
    for s in range(DSA_QW // LANES):
        ys = _head_rms(slab(y, s)) * slab(dq_g_ref[...], s)
        dq_ref[:, s * LANES:(s + 1) * LANES] = (_rope(ys, rc, rm, rp, half) * scale2).astype(dq_ref.dtype)

    y = cols(_P_DKV, LANES)
    yk = _rope(_head_rms(y) * dk_g_ref[...], rc, rm, rp, half)
    dkv_ref[...] = jnp.where(_lane_iota(y.shape) < HEAD_DIM, yk, y).astype(dkv_ref.dtype)

    y = cols(_P_IQ, IDX_QW)
    for s in range(IDX_QW // LANES):
        iq_ref[:, s * LANES:(s + 1) * LANES] = _rope(slab(y, s), ic, im, ip, ihalf).astype(iq_ref.dtype)
    ik_ref[...] = _rope(cols(_P_IK, LANES), ic, im, ip, ihalf).astype(ik_ref.dtype)
    iw_ref[...] = cols(_P_IW, LANES) * (DSA_IDX_HEADS ** -0.5) * (DSA_IDX_DIM ** -0.5)


def _proj(x2, g, w, nq_g, nk_g, dq_g, dk_g, rope_main, rope_idx, *, tm, seq):
    n, d = x2.shape
    nt = seq // tm
    row = lambda i: (i, 0)
    const = lambda i: (0, 0)
    pos = lambda i: (i % nt, 0)
    widths = [(3 * SB_W, MXU_DTYPE), (NSA_QW, MXU_DTYPE), (NSA_KVW, F32), (NSA_KVW, F32),
              (4 * NSA_KVW, MXU_DTYPE), (LANES, F32), (DSA_QW, MXU_DTYPE), (LANES, MXU_DTYPE),
              (IDX_QW, MXU_DTYPE), (LANES, MXU_DTYPE), (LANES, F32)]
    return pl.pallas_call(
        _proj_kernel,
        grid=(n // tm,),
        in_specs=[pl.BlockSpec((tm, d), row), pl.BlockSpec((1, d), const),
                  pl.BlockSpec((d, _P_END), const),
                  pl.BlockSpec((1, NSA_QW), const), pl.BlockSpec((2, LANES), const),
                  pl.BlockSpec((1, DSA_QW), const), pl.BlockSpec((1, LANES), const)]
                 + [pl.BlockSpec((tm, LANES), pos)] * 6,
        out_specs=[pl.BlockSpec((tm, wd), row) for wd, _ in widths],
        out_shape=[jax.ShapeDtypeStruct((n, wd), dt) for wd, dt in widths],
        compiler_params=_params(1),
        name="proj",
    )(x2, g, w, nq_g, nk_g, dq_g, dk_g, *rope_main, *rope_idx)


def _compress_kernel(xk_ref, xv_ref, pe_ref, w1_ref, w2_ref, g_ref, rc_ref, rm_ref, rp_ref,
                     kc_ref, vc_ref):
    ncp = xk_ref.shape[1]

    def compress(x, kv):
        a0 = _dot((x + pe_ref[kv, 0:1, :]).astype(MXU_DTYPE), w1_ref[kv, 0])
        a1 = _dot((x + pe_ref[kv, 1:2, :]).astype(MXU_DTYPE), w1_ref[kv, 1])
        pre = a0 + pltpu.roll(a1, ncp - 1, 0)
        hid = pre * jax.nn.sigmoid(pre)
        return _dot(hid.astype(MXU_DTYPE), w2_ref[kv])

    yk = _head_rms(compress(xk_ref[0], 0)) * g_ref[...]
    kc_ref[0] = _rope(yk, rc_ref[...], rm_ref[...], rp_ref[...], ROT_DIM // 2).astype(kc_ref.dtype)
    vc_ref[0] = compress(xv_ref[0], 1).astype(vc_ref.dtype)


def _compress(xk, xv, pe, w1, w2, g, rope_c):
    b, ncp, wide = xk.shape
    c3 = lambda i: (0, 0, 0)
    c4 = lambda i: (0, 0, 0, 0)
    c2 = lambda i: (0, 0)
    per_b = lambda i: (i, 0, 0)
    return pl.pallas_call(
        _compress_kernel,
        grid=(b,),
        in_specs=[pl.BlockSpec((1, ncp, wide), per_b), pl.BlockSpec((1, ncp, wide), per_b),
                  pl.BlockSpec(pe.shape, c3), pl.BlockSpec(w1.shape, c4), pl.BlockSpec(w2.shape, c3),
                  pl.BlockSpec((1, LANES), c2)] + [pl.BlockSpec((ncp, LANES), c2)] * 3,
        out_specs=[pl.BlockSpec((1, ncp, LANES), per_b)] * 2,
        out_shape=[jax.ShapeDtypeStruct((b, ncp, LANES), MXU_DTYPE)] * 2,
        compiler_params=_params(1),
        name="compress",
    )(xk, xv, pe, w1, w2, g, *rope_c)


def _sb_kernel(q_ref, k_ref, v_ref, o_ref, *, tq):
    i = pl.program_id(1)
    nslab = SB_W // LANES
    heads = [(sl, p) for sl in range(nslab) for p in range(2)]
    lane = _lane_iota((tq, LANES))
    rows = _row_iota((tq, tq))
    colsq = _lane_iota((tq, tq))
    upper = jnp.where(_row_iota((tq, tq + LANES)) > _lane_iota((tq, tq + LANES)), 1.0,
                      jnp.where(_lane_iota((tq, tq + LANES)) >= tq, 1.0, 0.0)).astype(MXU_DTYPE)
    in_half = [(lane >> 6) == p for p in range(2)]
    qh = [jnp.where(in_half[p], q_ref[0, :, sl * LANES:(sl + 1) * LANES].astype(F32), 0.0).astype(MXU_DTYPE)
          for sl, p in heads]
    reps = tq // LANES

    def tile(kt, carries, accs, diagonal):
        ks = [k_ref[0, pl.ds(kt * tq, tq), sl * LANES:(sl + 1) * LANES] for sl in range(nslab)]
        vs = [v_ref[0, pl.ds(kt * tq, tq), sl * LANES:(sl + 1) * LANES] for sl in range(nslab)]
        zs = [_dot_nt(qh[n], ks[sl]) for n, (sl, _) in enumerate(heads)]
        new_c, new_a = [], []
        for p, (sl, _) in enumerate(heads):
            z = zs[p]
            l = -(jnp.maximum(z, 0.0) + jnp.log1p(jnp.exp(-jnp.abs(z))))
            if diagonal:
                past = colsq < rows
                l = jnp.where(past, l, 0.0)
            hi = l.astype(MXU_DTYPE)
            lo = (l - hi.astype(F32)).astype(MXU_DTYPE)
            sums = _dot(hi, upper) + _dot(lo, upper)
            tail = sums[:, :tq] + jnp.concatenate([carries[p]] * reps, axis=1)
            w = jnp.exp(z + l + tail)
            if diagonal:
                w = jnp.where(past, w, 0.0)
            new_a.append(accs[p] + _dot(w.astype(MXU_DTYPE), vs[sl]))
            new_c.append(carries[p] + sums[:, tq:])
        return tuple(new_c), tuple(new_a)

    zeros = (jnp.zeros((tq, LANES), F32),) * len(heads)
    carries, accs = tile(i, zeros, zeros, True)

    def cond(st):
        kt, carries, _ = st
        worst = functools.reduce(jnp.maximum, carries)
        return (kt >= 0) & (jnp.max(worst) > SB_SKIP)

    def body(st):
        kt, carries, accs = st
        carries, accs = tile(kt, carries, accs, False)
        return kt - 1, carries, accs

    _, _, accs = lax.while_loop(cond, body, (i - 1, carries, accs))
    for sl in range(nslab):
        o_ref[0, :, sl * LANES:(sl + 1) * LANES] = jnp.where(in_half[0], accs[2 * sl],
                                                             accs[2 * sl + 1]).astype(o_ref.dtype)


def _sb_attention(sb, *, tq):
    b, s, _ = sb.shape
    return pl.pallas_call(
        functools.partial(_sb_kernel, tq=tq),
        grid=(b, s // tq),
        in_specs=[pl.BlockSpec((1, tq, SB_W), lambda bi, i: (bi, i, 0)),
                  pl.BlockSpec((1, s, SB_W), lambda bi, i: (bi, 0, 1)),
                  pl.BlockSpec((1, s, SB_W), lambda bi, i: (bi, 0, 2))],
        out_specs=pl.BlockSpec((1, tq, SB_W), lambda bi, i: (bi, i, 0)),
        out_shape=jax.ShapeDtypeStruct((b, s, SB_W), MXU_DTYPE),
        compiler_params=_params(2),
        name="sb_attn",
    )(sb, sb, sb)


def _group_queries(q_ref, g, tq):
    lane = _lane_iota((tq, LANES))
    in_g = (lane >> 6) == g
    qa = []
    for s in range(2):
        qs = q_ref[0, :, s * LANES:(s + 1) * LANES].astype(F32)
        for p in range(2):
            mine = jnp.where((lane >> 6) == p, qs, 0.0)
            both = mine + _swap_halves(mine)
            qa.append(jnp.where(in_g, both, 0.0).astype(MXU_DTYPE))
    return qa


def _fold_rows(x, op, rows=64):
    while x.shape[0] > rows and x.shape[0] % 16 == 0:
        half = x.shape[0] // 2
        x = op(x[:half], x[half:])
    return x


def _key_max(x):
    return jnp.max(_fold_rows(x, jnp.maximum), axis=0, keepdims=True)


def _key_sum(x):
    return jnp.sum(_fold_rows(x, jnp.add), axis=0, keepdims=True)


def _online_update(m, acc_ref, s, v_t):
    m_new = jnp.maximum(m, _key_max(s))
    p = jnp.exp2(s - m_new)
    acc_ref[...] = jnp.exp2(m - m_new) * acc_ref[...] + _dot(v_t, p.astype(MXU_DTYPE))
    return m_new


def _attend_tiles(n_tiles, scores_fn, softmax_fn, ms, fix_last):
    def step(tiles, ms, last):
        scores = [scores_fn(kt) for kt in tiles]
        if last:
            scores[-1] = fix_last(scores[-1])
        for kt, sc in zip(tiles, scores):
            ms = softmax_fn(kt, ms, sc)
        return ms

    plain = n_tiles - 1
    ms = lax.fori_loop(0, plain // 2, lambda p, ms: step([2 * p, 2 * p + 1], ms, False), ms)
    return lax.cond(plain % 2 == 1,
                    lambda ms: step([n_tiles - 2, n_tiles - 1], ms, True),
                    lambda ms: step([n_tiles - 1], ms, True), ms)


def _attend_pipelined(n_tiles, scores_fn, softmax_fn, ms, fix_last, s_ref):
    heads = s_ref.shape[1]

    def issue(slot, kt):
        for h, sc in enumerate(scores_fn(kt)):
            s_ref[slot, h] = sc

    def fold(slot, kt, ms):
        return softmax_fn(kt, ms, tuple(s_ref[slot, h] for h in range(heads)))

    plain = n_tiles - 1

    @pl.when(plain > 0)
    def _():
        issue(0, 0)

    def body(p, ms):
        issue(1, 2 * p + 1)
        ms = fold(0, 2 * p, ms)
        issue(0, jnp.minimum(2 * p + 2, plain - 1))
        return fold(1, 2 * p + 1, ms)

    ms = lax.fori_loop(0, plain // 2, body, ms)
    ms = lax.cond(plain % 2 == 1, lambda ms: fold(0, plain - 1, ms), lambda ms: ms, ms)
    return softmax_fn(n_tiles - 1, ms, fix_last(scores_fn(n_tiles - 1)))


def _online_init(tq, acc_ref):
    acc_ref[...] = jnp.zeros(acc_ref.shape, F32)
    return tuple(jnp.full((1, tq), NEG, F32) for _ in range(acc_ref.shape[0]))


def _nsa_kernel(q_ref, gate_ref, kc_ref, vct_ref, ks_ref, vst_ref, kw_ref, vwt_ref,
                ovt_ref, oh_ref, o_ref, qaug_ref, acc_ref, s_ref, *, tq, tks):
    g = pl.program_id(1)
    i = pl.program_id(2)
    q0 = i * tq
    ncp = kc_ref.shape[1]
    qa = _group_queries(q_ref, g, tq)

    c_vis = (_row_iota((ncp, tq)) * NSA_CMP_STRIDE + (NSA_CMP_LEN - 1)) <= q0 + _lane_iota((ncp, tq))
    kc = kc_ref[0]
    vct = vct_ref[0, 0]
    pc_sum = jnp.zeros((ncp, tq), F32)
    o_cmp = []
    scores = [_dot_nt(kc, qa[j]) for j in range(NSA_HPG)]
    for j in range(NSA_HPG):
        s = jnp.where(c_vis, scores[j], NEG)
        p = jnp.where(c_vis, jnp.exp2(s - _key_max(s)), 0.0)
        den = _key_sum(p)
        p = p * (1.0 / jnp.where(den > 0.0, den, 1.0))
        pc_sum = pc_sum + p
        o_cmp.append(_dot(vct, p.astype(MXU_DTYPE))[:HEAD_DIM])

    band = tq + NSA_WINDOW
    start = pl.multiple_of(jnp.maximum(q0 - NSA_WINDOW, 0), tq)
    kw = kw_ref[0, pl.ds(start, band), :]
    s_w = start + _row_iota((band, tq))
    t_w = q0 + _lane_iota((band, tq))
    w_ok = (s_w <= t_w) & (s_w > t_w - NSA_WINDOW)
    o_win = []
    scores = [_dot_nt(kw, qa[j]) for j in range(NSA_HPG)]
    for j in range(NSA_HPG):
        s = jnp.where(w_ok, scores[j], NEG)
        p = jnp.exp2(s - _key_max(s)).astype(MXU_DTYPE)
        acc = jnp.zeros((V_ROWS, tq), F32)
        for c in range(band // tq):
            acc = acc + _dot(vwt_ref[0, 0, start // tq + c], p[c * tq:(c + 1) * tq, :])
        o_win.append(acc[:HEAD_DIM] * (1.0 / acc[HEAD_DIM:HEAD_DIM + 1]))

    hi = pc_sum.astype(MXU_DTYPE)
    lo = (pc_sum - hi.astype(F32)).astype(MXU_DTYPE)
    imp = _dot(ovt_ref[...], hi) + _dot(ovt_ref[...], lo)
    blk = _row_iota((LANES, tq))
    t_b = q0 + _lane_iota((LANES, tq))
    cur = t_b >> 6
    visible = blk * NSA_SEL_LEN <= t_b
    forced = (blk == 0) | (blk == cur) | (blk == cur - 1)
    taken = -3.0e38
    work = jnp.where(visible, jnp.where(forced, taken, imp), NEG)
    blk_f = blk.astype(F32)
    for _ in range(NSA_SEL_N - 3):
        m = jnp.max(work, axis=0, keepdims=True)
        first = jnp.min(jnp.where(work == m, blk_f, float(LANES)), axis=0, keepdims=True)
        work = jnp.where(blk_f == first, taken, work)
    sel_bias = jnp.where(visible, jnp.where(work == taken, 0.0, NEG), NEG).T.astype(MXU_DTYPE)

    n_tiles = (q0 + tq + tks - 1) // tks
    t_s = q0 + _lane_iota((tks, tq))
    row_s = _row_iota((tks, tq))
    for j in range(NSA_HPG):
        qaug_ref[j] = jnp.concatenate([qa[j], sel_bias], axis=1)

    def sel_scores(kt):
        k_aug = jnp.concatenate([ks_ref[0, pl.ds(kt * tks, tks), :], oh_ref[pl.ds(kt * tks, tks), :]], axis=1)
        return tuple(_dot_nt(k_aug, qaug_ref[j]) for j in range(NSA_HPG))

    def sel_softmax(kt, ms, scores):
        v_t = vst_ref[0, 0, kt]
        return tuple(_online_update(ms[j], acc_ref.at[j], scores[j], v_t) for j in range(NSA_HPG))

    def sel_causal(scores):
        causal = (n_tiles - 1) * tks + row_s <= t_s
        return tuple(jnp.where(causal, s, NEG) for s in scores)

    _attend_pipelined(n_tiles, sel_scores, sel_softmax, _online_init(tq, acc_ref), sel_causal, s_ref)
    o_sel = [acc_ref[j, :HEAD_DIM, :] * (1.0 / acc_ref[j, HEAD_DIM:HEAD_DIM + 1, :]) for j in range(NSA_HPG)]

    gates_t = gate_ref[0].T
    merged = []
    for j in range(NSA_HPG):
        tot = jnp.zeros((HEAD_DIM, tq), F32)
        for r, branch in enumerate((o_cmp, o_sel, o_win)):
            row = 3 * j + r
            stride = 3 * NSA_HPG
            gate = jnp.where(g == 0, gates_t[row:row + 1, :], gates_t[stride + row:stride + row + 1, :])
            tot = tot + gate * branch[j]
        merged.append(tot)
    for s in range(2):
        pair = jnp.concatenate([merged[2 * s], merged[2 * s + 1]], axis=0)
        o_ref[0, :, s * LANES:(s + 1) * LANES] = pair.T.astype(o_ref.dtype)


def _nsa_attention(nq, ng, kc, vct, ks, vst, kw, vwt, overlap_t, onehot, *, tq, tks):
    b, s, _ = nq.shape
    ncp = kc.shape[1]
    gw = NSA_HPG * HEAD_DIM
    per_b3 = lambda bi, g, i: (bi, 0, 0)
    per_bg4 = lambda bi, g, i: (bi, g, 0, 0)
    per_bg5 = lambda bi, g, i: (bi, g, 0, 0, 0)
    const = lambda bi, g, i: (0, 0)
    return pl.pallas_call(
        functools.partial(_nsa_kernel, tq=tq, tks=tks),
        grid=(b, NSA_GROUPS, s // tq),
        in_specs=[pl.BlockSpec((1, tq, gw), lambda bi, g, i: (bi, i, g)),
                  pl.BlockSpec((1, tq, LANES), lambda bi, g, i: (bi, i, 0)),
                  pl.BlockSpec((1, ncp, LANES), per_b3),
                  pl.BlockSpec((1, 1) + vct.shape[2:], per_bg4),
                  pl.BlockSpec((1, s, LANES), per_b3),
                  pl.BlockSpec((1, 1) + vst.shape[2:], per_bg5),
                  pl.BlockSpec((1, s, LANES), per_b3),
                  pl.BlockSpec((1, 1) + vwt.shape[2:], per_bg5),
                  pl.BlockSpec(overlap_t.shape, const),
                  pl.BlockSpec(onehot.shape, const)],
        out_specs=pl.BlockSpec((1, tq, gw), lambda bi, g, i: (bi, i, g)),
        out_shape=jax.ShapeDtypeStruct((b, s, NSA_QW), MXU_DTYPE),
        scratch_shapes=[pltpu.VMEM((NSA_HPG, tq, 2 * LANES), MXU_DTYPE),
                        pltpu.VMEM((NSA_HPG, V_ROWS, tq), F32),
                        pltpu.VMEM((2, NSA_HPG, tks, tq), F32)],
        compiler_params=_params(3),
        name="nsa_attn",
    )(nq, ng, kc, vct, ks, vst, kw, vwt, overlap_t, onehot)


def _dsa_kernel(iq_ref, iwt_ref, ik_ref, q_ref, kv_ref, kvt_ref, o_ref, khi_ref, klo_ref, bias_ref, qa_ref,
                acc_ref, s_ref, *, tq, tk, top_k):
    i = pl.program_id(1)
    q0 = i * tq
    n_tiles = (q0 + tq + tk - 1) // tk
    lane = _lane_iota((tq, LANES))
    t_q = q0 + _lane_iota((tk, tq))
    row_k = _row_iota((tk, tq))

    iqa = []
    for s in range(IDX_QW // LANES):
        qs = iq_ref[0, :, s * LANES:(s + 1) * LANES].astype(F32)
        for p in range(LANES // DSA_IDX_DIM):
            mine = jnp.where((lane >> 5) == p, qs, 0.0)
            if p:
                mine = pltpu.roll(mine, LANES - p * DSA_IDX_DIM, 1)
            iqa.append(mine.astype(MXU_DTYPE))
    w_h = [iwt_ref[0, h:h + 1, :] for h in range(DSA_IDX_HEADS)]

    def score_tiles(tiles, last_is_diagonal):
        logits = []
        for kt in tiles:
            ik = ik_ref[0, pl.ds(kt * tk, tk), :]
            logits.append([_dot_nt(ik, iqa[h]) for h in range(DSA_IDX_HEADS)])
        for n, kt in enumerate(tiles):
            score = jnp.zeros((tk, tq), F32)
            for h in range(DSA_IDX_HEADS):
                score = score + w_h[h] * jnp.maximum(logits[n][h], 0.0)
            score = jnp.where(score == 0.0, 0.0, score)
            bits = pltpu.bitcast(score, jnp.int32)
            key = jnp.where(bits < 0, bits ^ 0x7FFFFFFF, bits)
            if last_is_diagonal and n == len(tiles) - 1:
                key = jnp.where(kt * tk + row_k <= t_q, key, INT_MIN)
            khi_ref[kt] = (key >> 16).astype(jnp.int16)
            klo_ref[kt] = ((key & 0xFFFF) - 0x8000).astype(jnp.int16)

    def score_pair(p, _):
        score_tiles([2 * p, 2 * p + 1], False)
        return 0

    plain = n_tiles - 1
    lax.fori_loop(0, plain // 2, score_pair, 0)
    lax.cond(plain % 2 == 1,
             lambda: score_tiles([n_tiles - 2, n_tiles - 1], True),
             lambda: score_tiles([n_tiles - 1], True))
    one, zero = jnp.ones((), jnp.bfloat16), jnp.zeros((), jnp.bfloat16)

    def count(flags):
        def body(kt, acc):
            return acc + _fold_rows(flags(kt), jnp.add, rows=32).astype(F32)
        acc = lax.fori_loop(0, n_tiles, body, jnp.zeros((32, tq), F32))
        return jnp.sum(acc, axis=0, keepdims=True)

    uncounted = float(2 ** 24)

    def kth_largest(ref):
        def body(b, carry):
            ans_u, reached = carry
            cand_u = ans_u | lax.shift_left(jnp.int32(1), 15 - b)
            cand = (cand_u - 0x8000).astype(jnp.int16)
            cnt = count(lambda kt: jnp.where(ref[kt] >= cand, one, zero))
            ok = cnt >= top_k
            return jnp.where(ok, cand_u, ans_u), jnp.where(ok, cnt, reached)
        ans_u, reached = lax.fori_loop(0, 16, body, (jnp.zeros((1, tq), jnp.int32),
                                                     jnp.full((1, tq), uncounted, F32)))
        return ans_u - 0x8000, reached

    t_hi32, reached_hi = kth_largest(khi_ref)
    t_hi = t_hi32.astype(jnp.int16)

    def pin_body(kt, _):
        hi = khi_ref[kt]
        klo_ref[kt] = jnp.where(hi == t_hi, klo_ref[kt],
                                jnp.where(hi > t_hi, jnp.int16(32767), jnp.int16(-32768)))
        return 0

    lax.fori_loop(0, n_tiles, pin_body, 0)
    t_lo32, reached_lo = kth_largest(klo_ref)
    reached = jnp.where(reached_lo < uncounted, reached_lo, reached_hi)
    t_lo = jnp.where(t_hi32 == -0x8000, jnp.maximum(t_lo32, 1 - 0x8000), t_lo32).astype(jnp.int16)

    def at_least(kt, yes, no):
        hi = khi_ref[kt]
        return jnp.where(hi > t_hi, yes, jnp.where(hi == t_hi, jnp.where(klo_ref[kt] >= t_lo, yes, no), no))

    def above(kt):
        return jnp.where(khi_ref[kt] > t_hi, one, jnp.where(klo_ref[kt] > t_lo, one, zero))

    bias_yes, bias_no = jnp.zeros((), bias_ref.dtype), jnp.full((), NEG, bias_ref.dtype)

    def plain_bias():
        def body(kt, _):
            bias_ref[kt] = at_least(kt, bias_yes, bias_no)
            return 0
        lax.fori_loop(0, n_tiles, body, 0)

    def ranked_bias():
        need = top_k - count(above)
        lower = jnp.where(_lane_iota((tk, tk)) < _row_iota((tk, tk)), 1.0, 0.0).astype(MXU_DTYPE)

        def body(kt, seen):
            tie = jnp.where(khi_ref[kt] == t_hi, jnp.where(klo_ref[kt] == t_lo, one, zero), zero)
            rank = _dot(lower, tie.astype(MXU_DTYPE)) + seen
            tie = tie.astype(F32)
            take = jnp.where(above(kt).astype(F32) > 0.0, 1.0, jnp.where(rank < need, tie, 0.0))
            bias_ref[kt] = ((take - 1.0) * (-NEG)).astype(bias_ref.dtype)
            return seen + _key_sum(tie)

        lax.fori_loop(0, n_tiles, body, jnp.zeros((1, tq), F32))

    lax.cond(jnp.max(reached) > top_k, ranked_bias, plain_bias)

    for s in range(DSA_QW // LANES):
        qs = q_ref[0, :, s * LANES:(s + 1) * LANES].astype(F32)
        lo = jnp.where(lane < HEAD_DIM, qs, 0.0)
        hi = _swap_halves(jnp.where(lane < HEAD_DIM, 0.0, qs))
        qa_ref[2 * s] = lo.astype(MXU_DTYPE)
        qa_ref[2 * s + 1] = hi.astype(MXU_DTYPE)

    def att_scores(kt):
        kv = kv_ref[0, pl.ds(kt * tk, tk), :]
        bias = bias_ref[kt].astype(F32)
        return tuple(_dot_nt(kv, qa_ref[h]) + bias for h in range(DSA_HEADS))

    def att_softmax(kt, ms, scores):
        kv_t = kvt_ref[0, kt]
        return tuple(_online_update(ms[h], acc_ref.at[h], scores[h], kv_t) for h in range(DSA_HEADS))

    _attend_pipelined(n_tiles, att_scores, att_softmax, _online_init(tq, acc_ref), lambda scores: scores, s_ref)
    outs = [acc_ref[h, :HEAD_DIM, :] * (1.0 / acc_ref[h, HEAD_DIM:HEAD_DIM + 1, :]) for h in range(DSA_HEADS)]

    for s in range(DSA_QW // LANES):
        pair = jnp.concatenate([outs[2 * s], outs[2 * s + 1]], axis=0)
        o_ref[0, :, s * LANES:(s + 1) * LANES] = pair.T.astype(o_ref.dtype)


def _dsa_attention(iq, iwt, ik, dq, dkv, dkvt, *, tq, tk, top_k):
    b, s, _ = dq.shape
    tile = lambda width: pl.BlockSpec((1, tq, width), lambda bi, i: (bi, i, 0))
    full = pl.BlockSpec((1, s, LANES), lambda bi, i: (bi, 0, 0))
    return pl.pallas_call(
        functools.partial(_dsa_kernel, tq=tq, tk=tk, top_k=top_k),
        grid=(b, s // tq),
        in_specs=[tile(IDX_QW), pl.BlockSpec((1, DSA_IDX_HEADS, tq), lambda bi, i: (bi, 0, i)), full,
                  tile(DSA_QW), full, pl.BlockSpec((1,) + dkvt.shape[1:], lambda bi, i: (bi, 0, 0, 0))],
        out_specs=tile(DSA_QW),
        out_shape=jax.ShapeDtypeStruct((b, s, DSA_QW), MXU_DTYPE),
        scratch_shapes=[pltpu.VMEM((s // tk, tk, tq), jnp.int16), pltpu.VMEM((s // tk, tk, tq), jnp.int16),
                        pltpu.VMEM((s // tk, tk, tq), MXU_DTYPE),
                        pltpu.VMEM((DSA_HEADS, tq, LANES), MXU_DTYPE), pltpu.VMEM((DSA_HEADS, V_ROWS, tq), F32),
                        pltpu.VMEM((2, DSA_HEADS, tk, tq), F32)],
        compiler_params=_params(2),
        name="dsa_attn",
    )(iq, iwt, ik, dq, dkv, dkvt)


def _merge_kernel(x_ref, g_ref, oa_ref, ob_ref, oc_ref, wg_ref, wa_ref, wb_ref, wc_ref, wo_ref, o_ref):
    x = x_ref[...]
    d = x.shape[1]
    ms = jnp.mean(x * x, axis=-1, keepdims=True)
    h = (x * lax.rsqrt(ms + NORM_EPS) * g_ref[...]).astype(MXU_DTYPE)
    merged = jnp.zeros(x.shape, F32)
    for r, (o_r, w_r) in enumerate(((oa_ref, wa_ref), (ob_ref, wb_ref), (oc_ref, wc_ref))):
        gate = jax.nn.sigmoid(_dot(h, wg_ref[:, r * d:(r + 1) * d]))
        merged = merged + gate * _dot(o_r[...], w_r[...])
    o_ref[...] = x + _dot(merged.astype(MXU_DTYPE), wo_ref[...])


def _merge(x2, g, oa, ob, oc, w_gate, wa, wb, wc, wo, *, tm):
    n, d = x2.shape
    row = lambda i: (i, 0)
    const = lambda i: (0, 0)
    return pl.pallas_call(
        _merge_kernel,
        grid=(n // tm,),
        in_specs=[pl.BlockSpec((tm, d), row), pl.BlockSpec((1, d), const),
                  pl.BlockSpec((tm, SB_W), row), pl.BlockSpec((tm, NSA_QW), row),
                  pl.BlockSpec((tm, DSA_QW), row),
                  pl.BlockSpec(w_gate.shape, const), pl.BlockSpec(wa.shape, const),
                  pl.BlockSpec(wb.shape, const), pl.BlockSpec(wc.shape, const),
                  pl.BlockSpec(wo.shape, const)],
        out_specs=pl.BlockSpec((tm, d), row),
        out_shape=jax.ShapeDtypeStruct((n, d), F32),
        compiler_params=_params(1),
        name="merge",
    )(x2, g, oa, ob, oc, w_gate, wa, wb, wc, wo)


def _rope_tables(pos, rot_dim, head_dim):
    half = rot_dim // 2
    inv = ROPE_THETA ** (-jnp.arange(0, rot_dim, 2, dtype=F32) / rot_dim)
    ang = pos.astype(F32)[:, None] * inv[None, :]
    cos, sin = jnp.cos(ang), jnp.sin(ang)
    n = pos.shape[0]
    rest = head_dim - rot_dim
    zero_h = jnp.zeros((n, half), F32)
    c = jnp.concatenate([cos, cos, jnp.ones((n, rest), F32)], axis=-1)
    sm = jnp.concatenate([-sin, zero_h, jnp.zeros((n, rest), F32)], axis=-1)
    sp = jnp.concatenate([zero_h, sin, jnp.zeros((n, rest), F32)], axis=-1)
    reps = LANES // head_dim
    return tuple(jnp.tile(t, (1, reps)) for t in (c, sm, sp))


def _pad_cols(w, width):
    return jnp.pad(w, ((0, 0), (0, width - w.shape[1])))


def _relayout_w_in(w_in):
    offs = np.cumsum((SB_W, SB_W, SB_W, NSA_QW) + (NSA_KVW,) * 6
                     + (3 * NSA_HEADS, DSA_QW, HEAD_DIM, HEAD_DIM, IDX_QW, DSA_IDX_DIM, DSA_IDX_HEADS))
    o_gate, o_dq, o_dk, o_dv, o_iq, o_ik, o_iw, o_end = offs[9:17].tolist()
    parts = [w_in[:, :o_gate], _pad_cols(w_in[:, o_gate:o_dq], LANES), w_in[:, o_dq:o_dk],
             w_in[:, o_dk:o_iq], w_in[:, o_iq:o_ik], _pad_cols(w_in[:, o_ik:o_iw], LANES),
             _pad_cols(w_in[:, o_iw:o_end], LANES)]
    w = jnp.concatenate(parts, axis=1)
    assert w.shape[1] == _P_END
    return w.astype(MXU_DTYPE)


def _compress_weights(pe, w1, w2):
    r = NSA_CMP_LEN // NSA_CMP_STRIDE
    hid = NSA_CMP_HIDDEN
    w1r = w1.reshape(2, r, NSA_CMP_STRIDE, HEAD_DIM, hid)
    eye = jnp.eye(NSA_GROUPS, dtype=w1.dtype)
    w1e = jnp.einsum('kmldj,gh->kmlgdhj', w1r, eye)
    w1e = w1e.reshape(2, r, NSA_CMP_STRIDE * NSA_KVW, NSA_GROUPS * hid)
    w2e = jnp.einsum('kjd,gh->kgjhd', w2, eye).reshape(2, NSA_GROUPS * hid, NSA_KVW)
    pe_e = jnp.broadcast_to(pe.reshape(2, r, NSA_CMP_STRIDE, 1, HEAD_DIM),
                            (2, r, NSA_CMP_STRIDE, NSA_GROUPS, HEAD_DIM))
    return pe_e.reshape(2, r, NSA_CMP_STRIDE * NSA_KVW), w1e.astype(MXU_DTYPE), w2e.astype(MXU_DTYPE)


def _overlap_matrix_t(ncp, seq):
    c = np.arange(ncp)[None, :] * NSA_CMP_STRIDE
    j = np.arange(LANES)[:, None] * NSA_SEL_LEN
    ov = (c < j + NSA_SEL_LEN) & (c + NSA_CMP_LEN - 1 >= j) & (j < seq)
    ov &= (np.arange(ncp)[None, :] < ncp - (NSA_CMP_LEN // NSA_CMP_STRIDE - 1))
    return jnp.asarray(ov, dtype=MXU_DTYPE)


def _block_onehot(seq):
    return jnp.asarray(np.arange(seq)[:, None] // NSA_SEL_LEN == np.arange(LANES)[None, :], dtype=MXU_DTYPE)


def _key_tiles_t(v, tile):
    b, s, w = v.shape
    return jnp.swapaxes(v.reshape(b, s // tile, tile, w), 2, 3)


def _group_values_t(v, tile):
    b, s, w = v.shape
    groups = w // HEAD_DIM
    vt = _key_tiles_t(v, tile).reshape(b, s // tile, groups, HEAD_DIM, tile)
    vt = jnp.swapaxes(vt, 1, 2)
    pad = jnp.zeros((b, groups, s // tile, V_ROWS - HEAD_DIM, tile), v.dtype).at[:, :, :, 0, :].set(1)
    return jnp.concatenate([vt, pad], axis=3)


def kernel(x, ffn1_norm, ffn1_w_gu, ffn1_w_down, mix_norm, w_in, w_gate, nsa_q_norm, nsa_k_norm,
           nsa_cmp_pe, nsa_cmp_w1, nsa_cmp_w2, dsa_q_norm, dsa_k_norm, w_br_a, w_br_b, w_br_c, w_out,
           ffn2_norm, ffn2_w_gu, ffn2_w_down):
    b, s, d = x.shape
    depth = w_in.shape[0]
    n = b * s
    ncp = s // NSA_CMP_STRIDE
    assert s // NSA_SEL_LEN <= LANES and s % 1024 == 0
    top_k = min(DSA_TOPK, s // 4)
    ffn_tf = ffn1_w_down.shape[1]
    cast = lambda w: w.astype(MXU_DTYPE)
    tile2 = lambda v, reps: jnp.tile(v, reps)[None, :]

    pos = jnp.arange(s)
    rope_main = _rope_tables(pos, ROT_DIM, HEAD_DIM)
    rope_idx = _rope_tables(pos, DSA_IDX_ROT, DSA_IDX_DIM)
    rope_cmp = _rope_tables(jnp.arange(ncp) * NSA_CMP_STRIDE + NSA_CMP_LEN - 1, ROT_DIM, HEAD_DIM)
    overlap_t = _overlap_matrix_t(ncp, s)
    onehot = _block_onehot(s)

    x2 = x.reshape(n, d)
    for l in range(depth):
        x2 = _ffn(x2, ffn1_norm[l][None, :], cast(ffn1_w_gu[l]), cast(ffn1_w_down[l]), tm=FFN_TM, tf=ffn_tf)

        (sb, nq, nkc, nvc, nkv, ng, dq, dkv, iq, ik, iw) = _proj(
            x2, mix_norm[l][None, :], _relayout_w_in(w_in[l]),
            tile2(nsa_q_norm[l], NSA_HEADS),
            jnp.stack([jnp.tile(nsa_k_norm[l, 1], NSA_GROUPS), jnp.tile(nsa_k_norm[l, 2], NSA_GROUPS)]),
            tile2(dsa_q_norm[l], DSA_HEADS), tile2(dsa_k_norm[l], 2),
            rope_main, rope_idx, tm=256, seq=s)
        r3 = lambda t: t.reshape(b, s, t.shape[-1])

        pe_e, w1e, w2e = _compress_weights(nsa_cmp_pe[l], nsa_cmp_w1[l], nsa_cmp_w2[l])
        kc, vc = _compress(nkc.reshape(b, ncp, NSA_CMP_STRIDE * NSA_KVW),
                           nvc.reshape(b, ncp, NSA_CMP_STRIDE * NSA_KVW),
                           pe_e, w1e, w2e, tile2(nsa_k_norm[l, 0], NSA_GROUPS), rope_cmp)

        o_a = _sb_attention(r3(sb), tq=256)

        nkv = r3(nkv)
        ks, vs, kw, vw = (nkv[:, :, c * LANES:(c + 1) * LANES] for c in range(4))
        o_b = _nsa_attention(r3(nq), r3(ng), kc, _group_values_t(vc, ncp)[:, :, 0], ks,
                             _group_values_t(vs, TKS), kw, _group_values_t(vw, TQ_NSA), overlap_t, onehot,
                             tq=TQ_NSA, tks=TKS)

        dkv = r3(dkv)
        iwt = jnp.swapaxes(r3(iw)[:, :, :DSA_IDX_HEADS], 1, 2)
        dkvt = _group_values_t(dkv[:, :, HEAD_DIM:], TKS)[:, 0]
        o_c = _dsa_attention(r3(iq), iwt, r3(ik), r3(dq), dkv, dkvt, tq=TQ, tk=TKS, top_k=top_k)

        x2 = _merge(x2, mix_norm[l][None, :], o_a.reshape(n, SB_W), o_b.reshape(n, NSA_QW),
                    o_c.reshape(n, DSA_QW), cast(w_gate[l]), cast(w_br_a[l]), cast(w_br_b[l]),
                    cast(w_br_c[l]), cast(w_out[l]), tm=256)

        x2 = _ffn(x2, ffn2_norm[l][None, :], cast(ffn2_w_gu[l]), cast(ffn2_w_down[l]), tm=FFN_TM, tf=ffn_tf)
    return x2.reshape(b, s, d)
```

```python
import functools

import numpy as np
import jax
import jax.numpy as jnp
from jax import lax
from jax.experimental import pallas as pl
from jax.experimental.pallas import tpu as pltpu

HEAD_DIM = 64
ROT_DIM = HEAD_DIM // 4
ROPE_THETA = 500000.0
NORM_EPS = 1e-6

SB_HEADS = 4
NSA_HEADS = 8
NSA_GROUPS = 2
NSA_HPG = NSA_HEADS // NSA_GROUPS
NSA_CMP_LEN = 32
NSA_CMP_STRIDE = 16
NSA_CMP_HIDDEN = 2 * HEAD_DIM
NSA_SEL_LEN = 64
NSA_SEL_N = 16
NSA_WINDOW = 512
NSA_FORCED_SCORE = 1.0e4
DSA_HEADS = 4
DSA_IDX_HEADS = 8
DSA_IDX_DIM = 32
DSA_IDX_ROT = DSA_IDX_DIM // 4
DSA_TOPK = 256

SB_W = SB_HEADS * HEAD_DIM
NSA_QW = NSA_HEADS * HEAD_DIM
NSA_KVW = NSA_GROUPS * HEAD_DIM
DSA_QW = DSA_HEADS * HEAD_DIM
IDX_QW = DSA_IDX_HEADS * DSA_IDX_DIM

LANES = 128
VMEM_LIMIT = 56 * 1024 * 1024
MXU_DTYPE = jnp.bfloat16
NEG = -1.0e30
SB_SKIP = -120.0
INT_MIN = -2 ** 31
LOG2E = 1.4426950408889634

F32 = jnp.float32

TQ = 256
TQ_NSA = 256
TKS = 512
V_ROWS = LANES
FFN_CHUNK = 512
FFN_TM = 512
assert TKS % TQ == 0 and TKS % TQ_NSA == 0 and NSA_WINDOW % TQ_NSA == 0


def _dot(a, b):
    return jnp.dot(a, b, preferred_element_type=F32)


def _dot_nt(a, b):
    return lax.dot_general(a, b, (((1,), (1,)), ((), ())), preferred_element_type=F32)


def _params(n_axes):
    return pltpu.CompilerParams(dimension_semantics=("arbitrary",) * n_axes,
                                vmem_limit_bytes=VMEM_LIMIT)


def _lane_iota(shape):
    return lax.broadcasted_iota(jnp.int32, shape, len(shape) - 1)


def _row_iota(shape):
    return lax.broadcasted_iota(jnp.int32, shape, len(shape) - 2)


def _swap_halves(x):
    return pltpu.roll(x, LANES // 2, 1)


def _head_rms(ys):
    lo = _lane_iota(ys.shape) < HEAD_DIM
    sq = ys * ys
    s_lo = jnp.sum(jnp.where(lo, sq, 0.0), axis=-1, keepdims=True)
    s_hi = jnp.sum(jnp.where(lo, 0.0, sq), axis=-1, keepdims=True)
    ms = jnp.where(lo, s_lo, s_hi) * (1.0 / HEAD_DIM)
    return ys * lax.rsqrt(ms + NORM_EPS)


def _rope(ys, c, sm, sp, half):
    return ys * c + pltpu.roll(ys, LANES - half, 1) * sm + pltpu.roll(ys, half, 1) * sp


def _ffn_kernel(x_ref, g_ref, wg_ref, wu_ref, wd_ref, o_ref, h_ref, acc_ref):
    f = pl.program_id(1)

    @pl.when(f == 0)
    def _():
        x = x_ref[...]
        ms = jnp.mean(x * x, axis=-1, keepdims=True)
        h_ref[...] = (x * lax.rsqrt(ms + NORM_EPS) * g_ref[...]).astype(h_ref.dtype)
        acc_ref[...] = jnp.zeros_like(acc_ref)

    h = h_ref[...]
    tf = wd_ref.shape[0]
    bounds = [(c, min(c + FFN_CHUNK, tf)) for c in range(0, tf, FFN_CHUNK)]

    def gate_up(c0, c1):
        return _dot(h, wg_ref[:, c0:c1]), _dot(h, wu_ref[:, c0:c1])

    ahead = gate_up(*bounds[0])
    for idx, (c0, c1) in enumerate(bounds):
        gate, up = ahead
        if idx + 1 < len(bounds):
            ahead = gate_up(*bounds[idx + 1])
        act = (gate * jax.nn.sigmoid(gate) * up).astype(h_ref.dtype)
        acc_ref[...] += _dot(act, wd_ref[c0:c1, :])

    @pl.when(f == pl.num_programs(1) - 1)
    def _():
        o_ref[...] = x_ref[...] + 0.5 * acc_ref[...]


def _ffn(x2, g, w_gu, w_down, *, tm, tf):
    n, d = x2.shape
    d_ff = w_down.shape[0]
    nf = d_ff // tf
    return pl.pallas_call(
        _ffn_kernel,
        grid=(n // tm, nf),
        in_specs=[
            pl.BlockSpec((tm, d), lambda i, f: (i, 0)),
            pl.BlockSpec((1, d), lambda i, f: (0, 0)),
            pl.BlockSpec((d, tf), lambda i, f: (0, f)),
            pl.BlockSpec((d, tf), lambda i, f: (0, f + nf)),
            pl.BlockSpec((tf, d), lambda i, f: (f, 0)),
        ],
        out_specs=pl.BlockSpec((tm, d), lambda i, f: (i, 0)),
        out_shape=jax.ShapeDtypeStruct((n, d), F32),
        scratch_shapes=[pltpu.VMEM((tm, d), MXU_DTYPE), pltpu.VMEM((tm, d), F32)],
        compiler_params=_params(2),
        name="ffn",
    )(x2, g, w_gu, w_gu, w_down)


_P_SB = 0
_P_NQ = _P_SB + 3 * SB_W
_P_NCV = _P_NQ + NSA_QW
_P_NKV = _P_NCV + 2 * NSA_KVW
_P_NG = _P_NKV + 4 * NSA_KVW
_P_DQ = _P_NG + LANES
_P_DKV = _P_DQ + DSA_QW
_P_IQ = _P_DKV + LANES
_P_IK = _P_IQ + IDX_QW
_P_IW = _P_IK + LANES
_P_END = _P_IW + LANES


def _proj_kernel(x_ref, g_ref, w_ref, nq_g_ref, nk_g_ref, dq_g_ref, dk_g_ref,
                 rc_ref, rm_ref, rp_ref, ic_ref, im_ref, ip_ref,
                 sb_ref, nq_ref, nkc_ref, nvc_ref, nkv_ref, ng_ref, dq_ref, dkv_ref,
                 iq_ref, ik_ref, iw_ref):
    x = x_ref[...]
    ms = jnp.mean(x * x, axis=-1, keepdims=True)
    h = (x * lax.rsqrt(ms + NORM_EPS) * g_ref[...]).astype(MXU_DTYPE)
    rc, rm, rp = rc_ref[...], rm_ref[...], rp_ref[...]
    ic, im, ip = ic_ref[...], im_ref[...], ip_ref[...]
    half, ihalf = ROT_DIM // 2, DSA_IDX_ROT // 2
    scale = HEAD_DIM ** -0.5
    scale2 = scale * LOG2E

    def cols(a, width):
        return _dot(h, w_ref[:, a:a + width])

    def slab(y, s):
        return y[:, s * LANES:(s + 1) * LANES]

    y = cols(_P_SB, 3 * SB_W)
    sb_ref[:, 0:SB_W] = (y[:, 0:SB_W] * scale).astype(sb_ref.dtype)
    sb_ref[:, SB_W:3 * SB_W] = y[:, SB_W:3 * SB_W].astype(sb_ref.dtype)

    y = cols(_P_NQ, NSA_QW)
    for s in range(NSA_QW // LANES):
        ys = _head_rms(slab(y, s)) * slab(nq_g_ref[...], s)
        nq_ref[:, s * LANES:(s + 1) * LANES] = (_rope(ys, rc, rm, rp, half) * scale2).astype(nq_ref.dtype)

    y = cols(_P_NCV, 2 * NSA_KVW)
    nkc_ref[...] = slab(y, 0)
    nvc_ref[...] = slab(y, 1)

    y = cols(_P_NKV, 4 * NSA_KVW)
    for s, gi in ((0, 0), (2, 1)):
        ys = _head_rms(slab(y, s)) * nk_g_ref[gi:gi + 1, :]
        nkv_ref[:, s * LANES:(s + 1) * LANES] = _rope(ys, rc, rm, rp, half).astype(nkv_ref.dtype)
    for s in (1, 3):
        nkv_ref[:, s * LANES:(s + 1) * LANES] = slab(y, s).astype(nkv_ref.dtype)

    ng_ref[...] = jax.nn.sigmoid(cols(_P_NG, LANES))

    y = cols(_P_DQ, DSA_QW)
    for s in range(DSA_QW // LANES):
        ys = _head_rms(slab(y, s)) * slab(dq_g_ref[...], s)
        dq_ref[:, s * LANES:(s + 1) * LANES] = (_rope(ys, rc, rm, rp, half) * scale2).astype(dq_ref.dtype)

    y = cols(_P_DKV, LANES)
    yk = _rope(_head_rms(y) * dk_g_ref[...], rc, rm, rp, half)
    dkv_ref[...] = jnp.where(_lane_iota(y.shape) < HEAD_DIM, yk, y).astype(dkv_ref.dtype)

    y = cols(_P_IQ, IDX_QW)
    for s in range(IDX_QW // LANES):
        iq_ref[:, s * LANES:(s + 1) * LANES] = _rope(slab(y, s), ic, im, ip, ihalf).astype(iq_ref.dtype)
    ik_ref[...] = _rope(cols(_P_IK, LANES), ic, im, ip, ihalf).astype(ik_ref.dtype)
    iw_ref[...] = cols(_P_IW, LANES) * (DSA_IDX_HEADS ** -0.5) * (DSA_IDX_DIM ** -0.5)


def _proj(x2, g, w, nq_g, nk_g, dq_g, dk_g, rope_main, rope_idx, *, tm, seq):
    n, d = x2.shape
    nt = seq // tm
    row = lambda i: (i, 0)
    const = lambda i: (0, 0)
    pos = lambda i: (i % nt, 0)
    widths = [(3 * SB_W, MXU_DTYPE), (NSA_QW, MXU_DTYPE), (NSA_KVW, F32), (NSA_KVW, F32),
              (4 * NSA_KVW, MXU_DTYPE), (LANES, F32), (DSA_QW, MXU_DTYPE), (LANES, MXU_DTYPE),
              (IDX_QW, MXU_DTYPE), (LANES, MXU_DTYPE), (LANES, F32)]
    return pl.pallas_call(
        _proj_kernel,
        grid=(n // tm,),
        in_specs=[pl.BlockSpec((tm, d), row), pl.BlockSpec((1, d), const),
                  pl.BlockSpec((d, _P_END), const),
                  pl.BlockSpec((1, NSA_QW), const), pl.BlockSpec((2, LANES), const),
                  pl.BlockSpec((1, DSA_QW), const), pl.BlockSpec((1, LANES), const)]
                 + [pl.BlockSpec((tm, LANES), pos)] * 6,
        out_specs=[pl.BlockSpec((tm, wd), row) for wd, _ in widths],
        out_shape=[jax.ShapeDtypeStruct((n, wd), dt) for wd, dt in widths],
        compiler_params=_params(1),
        name="proj",
    )(x2, g, w, nq_g, nk_g, dq_g, dk_g, *rope_main, *rope_idx)


def _compress_kernel(xk_ref, xv_ref, pe_ref, w1_ref, w2_ref, g_ref, rc_ref, rm_ref, rp_ref,
                     kc_ref, vc_ref):
    ncp = xk_ref.shape[1]

    def compress(x, kv):
        a0 = _dot((x + pe_ref[kv, 0:1, :]).astype(MXU_DTYPE), w1_ref[kv, 0])
        a1 = _dot((x + pe_ref[kv, 1:2, :]).astype(MXU_DTYPE), w1_ref[kv, 1])
        pre = a0 + pltpu.roll(a1, ncp - 1, 0)
        hid = pre * jax.nn.sigmoid(pre)
        return _dot(hid.astype(MXU_DTYPE), w2_ref[kv])

    yk = _head_rms(compress(xk_ref[0], 0)) * g_ref[...]
    kc_ref[0] = _rope(yk, rc_ref[...], rm_ref[...], rp_ref[...], ROT_DIM // 2).astype(kc_ref.dtype)
    vc_ref[0] = compress(xv_ref[0], 1).astype(vc_ref.dtype)


def _compress(xk, xv, pe, w1, w2, g, rope_c):
    b, ncp, wide = xk.shape
    c3 = lambda i: (0, 0, 0)
    c4 = lambda i: (0, 0, 0, 0)
    c2 = lambda i: (0, 0)
    per_b = lambda i: (i, 0, 0)
    return pl.pallas_call(
        _compress_kernel,
        grid=(b,),
        in_specs=[pl.BlockSpec((1, ncp, wide), per_b), pl.BlockSpec((1, ncp, wide), per_b),
                  pl.BlockSpec(pe.shape, c3), pl.BlockSpec(w1.shape, c4), pl.BlockSpec(w2.shape, c3),
                  pl.BlockSpec((1, LANES), c2)] + [pl.BlockSpec((ncp, LANES), c2)] * 3,
        out_specs=[pl.BlockSpec((1, ncp, LANES), per_b)] * 2,
        out_shape=[jax.ShapeDtypeStruct((b, ncp, LANES), MXU_DTYPE)] * 2,
        compiler_params=_params(1),
        name="compress",
    )(xk, xv, pe, w1, w2, g, *rope_c)


def _sb_kernel(q_ref, k_ref, v_ref, o_ref, *, tq):
    i = pl.program_id(1)
    nslab = SB_W // LANES
    heads = [(sl, p) for sl in range(nslab) for p in range(2)]
    lane = _lane_iota((tq, LANES))
    rows = _row_iota((tq, tq))
    colsq = _lane_iota((tq, tq))
    upper = jnp.where(_row_iota((tq, tq + LANES)) > _lane_iota((tq, tq + LANES)), 1.0,
                      jnp.where(_lane_iota((tq, tq + LANES)) >= tq, 1.0, 0.0)).astype(MXU_DTYPE)
    in_half = [(lane >> 6) == p for p in range(2)]
    qh = [jnp.where(in_half[p], q_ref[0, :, sl * LANES:(sl + 1) * LANES].astype(F32), 0.0).astype(MXU_DTYPE)
          for sl, p in heads]
    reps = tq // LANES

    def tile(kt, carries, accs, diagonal):
        ks = [k_ref[0, pl.ds(kt * tq, tq), sl * LANES:(sl + 1) * LANES] for sl in range(nslab)]
        vs = [v_ref[0, pl.ds(kt * tq, tq), sl * LANES:(sl + 1) * LANES] for sl in range(nslab)]
        zs = [_dot_nt(qh[n], ks[sl]) for n, (sl, _) in enumerate(heads)]
        new_c, new_a = [], []
        for p, (sl, _) in enumerate(heads):
            z = zs[p]
            l = -(jnp.maximum(z, 0.0) + jnp.log1p(jnp.exp(-jnp.abs(z))))
            if diagonal:
                past = colsq < rows
                l = jnp.where(past, l, 0.0)
            hi = l.astype(MXU_DTYPE)
            lo = (l - hi.astype(F32)).astype(MXU_DTYPE)
            sums = _dot(hi, upper) + _dot(lo, upper)
            tail = sums[:, :tq] + jnp.concatenate([carries[p]] * reps, axis=1)
            w = jnp.exp(z + l + tail)
            if diagonal:
                w = jnp.where(past, w, 0.0)
            new_a.append(accs[p] + _dot(w.astype(MXU_DTYPE), vs[sl]))
            new_c.append(carries[p] + sums[:, tq:])
        return tuple(new_c), tuple(new_a)

    zeros = (jnp.zeros((tq, LANES), F32),) * len(heads)
    carries, accs = tile(i, zeros, zeros, True)

    def cond(st):
        kt, carries, _ = st
        worst = functools.reduce(jnp.maximum, carries)
        return (kt >= 0) & (jnp.max(worst) > SB_SKIP)

    def body(st):
        kt, carries, accs = st
        carries, accs = tile(kt, carries, accs, False)
        return kt - 1, carries, accs

    _, _, accs = lax.while_loop(cond, body, (i - 1, carries, accs))
    for sl in range(nslab):
        o_ref[0, :, sl * LANES:(sl + 1) * LANES] = jnp.where(in_half[0], accs[2 * sl],
                                                             accs[2 * sl + 1]).astype(o_ref.dtype)


def _sb_attention(sb, *, tq):
    b, s, _ = sb.shape
    return pl.pallas_call(
        functools.partial(_sb_kernel, tq=tq),
        grid=(b, s // tq),
        in_specs=[pl.BlockSpec((1, tq, SB_W), lambda bi, i: (bi, i, 0)),
                  pl.BlockSpec((1, s, SB_W), lambda bi, i: (bi, 0, 1)),
                  pl.BlockSpec((1, s, SB_W), lambda bi, i: (bi, 0, 2))],
        out_specs=pl.BlockSpec((1, tq, SB_W), lambda bi, i: (bi, i, 0)),
        out_shape=jax.ShapeDtypeStruct((b, s, SB_W), MXU_DTYPE),
        compiler_params=_params(2),
        name="sb_attn",
    )(sb, sb, sb)


def _group_queries(q_ref, tq):
    lane = _lane_iota((tq, LANES))
    qa = []
    for h in range(NSA_HEADS):
        slab, parity, g = h // 2, h % 2, h // NSA_HPG
        qs = q_ref[0, :, slab * LANES:(slab + 1) * LANES].astype(F32)
        mine = jnp.where((lane >> 6) == parity, qs, 0.0)
        if parity != g:
            mine = _swap_halves(mine)
        qa.append(mine.astype(MXU_DTYPE))
    return qa


def _fold_rows(x, op, rows=64):
    while x.shape[0] > rows and x.shape[0] % 16 == 0:
        half = x.shape[0] // 2
        x = op(x[:half], x[half:])
    return x


def _key_max(x):
    return jnp.max(_fold_rows(x, jnp.maximum), axis=0, keepdims=True)


def _key_sum(x):
    return jnp.sum(_fold_rows(x, jnp.add), axis=0, keepdims=True)


def _online_update(m, acc_ref, s, v_t):
    m_new = jnp.maximum(m, _key_max(s))
    p = jnp.exp2(s - m_new)
    acc_ref[...] = jnp.exp2(m - m_new) * acc_ref[...] + _dot(v_t, p.astype(MXU_DTYPE))
    return m_new


def _attend_tiles(n_tiles, scores_fn, softmax_fn, ms, fix_last):
    def step(tiles, ms, last):
        scores = [scores_fn(kt) for kt in tiles]
        if last:
            scores[-1] = fix_last(scores[-1])
        for kt, sc in zip(tiles, scores):
            ms = softmax_fn(kt, ms, sc)
        return ms

    plain = n_tiles - 1
    ms = lax.fori_loop(0, plain // 2, lambda p, ms: step([2 * p, 2 * p + 1], ms, False), ms)
    return lax.cond(plain % 2 == 1,
                    lambda ms: step([n_tiles - 2, n_tiles - 1], ms, True),
                    lambda ms: step([n_tiles - 1], ms, True), ms)


def _attend_pipelined(n_tiles, scores_fn, softmax_fn, ms, fix_last, s_ref):
    heads = s_ref.shape[1]

    def issue(slot, kt):
        for h, sc in enumerate(scores_fn(kt)):
            s_ref[slot, h] = sc

    def fold(slot, kt, ms):
        return softmax_fn(kt, ms, tuple(s_ref[slot, h] for h in range(heads)))

    plain = n_tiles - 1

    @pl.when(plain > 0)
    def _():
        issue(0, 0)

    def body(p, ms):
        issue(1, 2 * p + 1)
        ms = fold(0, 2 * p, ms)
        issue(0, jnp.minimum(2 * p + 2, plain - 1))
        return fold(1, 2 * p + 1, ms)

    ms = lax.fori_loop(0, plain // 2, body, ms)
    ms = lax.cond(plain % 2 == 1, lambda ms: fold(0, plain - 1, ms), lambda ms: ms, ms)
    return softmax_fn(n_tiles - 1, ms, fix_last(scores_fn(n_tiles - 1)))


def _online_init(tq, acc_ref):
    acc_ref[...] = jnp.zeros(acc_ref.shape, F32)
    return tuple(jnp.full((1, tq), NEG, F32) for _ in range(acc_ref.shape[0]))


def _nsa_kernel(q_ref, gate_ref, kc_ref, vct_ref, ks_ref, vst_ref, kw_ref, vwt_ref,
                ovt_ref, oh_ref, o_ref, qaug_ref, acc_ref, s_ref, *, tq, tks):
    i = pl.program_id(1)
    q0 = i * tq
    ncp = kc_ref.shape[1]
    heads = range(NSA_HEADS)
    group = [h // NSA_HPG for h in heads]
    qa = _group_queries(q_ref, tq)

    c_vis = (_row_iota((ncp, tq)) * NSA_CMP_STRIDE + (NSA_CMP_LEN - 1)) <= q0 + _lane_iota((ncp, tq))
    kc = kc_ref[0]
    pc_sum = [jnp.zeros((ncp, tq), F32) for _ in range(NSA_GROUPS)]
    o_cmp = []
    scores = [_dot_nt(kc, qa[h]) for h in heads]
    for h in heads:
        s = jnp.where(c_vis, scores[h], NEG)
        p = jnp.where(c_vis, jnp.exp2(s - _key_max(s)), 0.0)
        den = _key_sum(p)
        p = p * (1.0 / jnp.where(den > 0.0, den, 1.0))
        pc_sum[group[h]] = pc_sum[group[h]] + p
        o_cmp.append(_dot(vct_ref[0, group[h]], p.astype(MXU_DTYPE))[:HEAD_DIM])

    band = tq + NSA_WINDOW
    start = pl.multiple_of(jnp.maximum(q0 - NSA_WINDOW, 0), tq)
    kw = kw_ref[0, pl.ds(start, band), :]
    s_w = start + _row_iota((band, tq))
    t_w = q0 + _lane_iota((band, tq))
    w_ok = (s_w <= t_w) & (s_w > t_w - NSA_WINDOW)
    o_win = []
    scores = [_dot_nt(kw, qa[h]) for h in heads]
    for h in heads:
        s = jnp.where(w_ok, scores[h], NEG)
        p = jnp.exp2(s - _key_max(s)).astype(MXU_DTYPE)
        acc = jnp.zeros((V_ROWS, tq), F32)
        for c in range(band // tq):
            acc = acc + _dot(vwt_ref[0, group[h], start // tq + c], p[c * tq:(c + 1) * tq, :])
        o_win.append(acc[:HEAD_DIM] * (1.0 / acc[HEAD_DIM:HEAD_DIM + 1]))

    blk = _row_iota((LANES, tq))
    t_b = q0 + _lane_iota((LANES, tq))
    cur = t_b >> 6
    visible = blk * NSA_SEL_LEN <= t_b
    forced = (blk == 0) | (blk == cur) | (blk == cur - 1)
    taken = -3.0e38
    work = []
    for g in range(NSA_GROUPS):
        hi = pc_sum[g].astype(MXU_DTYPE)
        lo = (pc_sum[g] - hi.astype(F32)).astype(MXU_DTYPE)
        imp = _dot(ovt_ref[...], hi) + _dot(ovt_ref[...], lo)
        work.append(jnp.where(visible, jnp.where(forced, taken, imp), NEG))
    blk_f = blk.astype(F32)
    for _ in range(NSA_SEL_N - 3):
        for g in range(NSA_GROUPS):
            m = jnp.max(work[g], axis=0, keepdims=True)
            first = jnp.min(jnp.where(work[g] == m, blk_f, float(LANES)), axis=0, keepdims=True)
            work[g] = jnp.where(blk_f == first, taken, work[g])
    sel_bias = [jnp.where(visible, jnp.where(w == taken, 0.0, NEG), NEG).T.astype(MXU_DTYPE) for w in work]

    n_tiles = (q0 + tq + tks - 1) // tks
    t_s = q0 + _lane_iota((tks, tq))
    row_s = _row_iota((tks, tq))
    for h in heads:
        qaug_ref[h] = jnp.concatenate([qa[h], sel_bias[group[h]]], axis=1)

    def sel_scores(kt):
        k_aug = jnp.concatenate([ks_ref[0, pl.ds(kt * tks, tks), :], oh_ref[pl.ds(kt * tks, tks), :]], axis=1)
        return tuple(_dot_nt(k_aug, qaug_ref[h]) for h in heads)

    def sel_softmax(kt, ms, scores):
        v_t = [vst_ref[0, g, kt] for g in range(NSA_GROUPS)]
        return tuple(_online_update(ms[h], acc_ref.at[h], scores[h], v_t[group[h]]) for h in heads)

    def sel_causal(scores):
        causal = (n_tiles - 1) * tks + row_s <= t_s
        return tuple(jnp.where(causal, s, NEG) for s in scores)

    _attend_pipelined(n_tiles, sel_scores, sel_softmax, _online_init(tq, acc_ref), sel_causal, s_ref)
    o_sel = [acc_ref[h, :HEAD_DIM, :] * (1.0 / acc_ref[h, HEAD_DIM:HEAD_DIM + 1, :]) for h in heads]

    gates_t = gate_ref[0].T
    merged = []
    for h in heads:
        tot = jnp.zeros((HEAD_DIM, tq), F32)
        for r, branch in enumerate((o_cmp, o_sel, o_win)):
            tot = tot + gates_t[3 * h + r:3 * h + r + 1, :] * branch[h]
        merged.append(tot)
    for s in range(NSA_QW // LANES):
        pair = jnp.concatenate([merged[2 * s], merged[2 * s + 1]], axis=0)
        o_ref[0, :, s * LANES:(s + 1) * LANES] = pair.T.astype(o_ref.dtype)


def _nsa_attention(nq, ng, kc, vct, ks, vst, kw, vwt, overlap_t, onehot, *, tq, tks):
    b, s, _ = nq.shape
    once = pl.Buffered(1)

    def per_batch(a):
        return pl.BlockSpec((1,) + a.shape[1:], lambda bi, i: (bi,) + (0,) * (a.ndim - 1), pipeline_mode=once)

    def const(a):
        return pl.BlockSpec(a.shape, lambda bi, i: (0,) * a.ndim, pipeline_mode=once)

    return pl.pallas_call(
        functools.partial(_nsa_kernel, tq=tq, tks=tks),
        grid=(b, s // tq),
        in_specs=[pl.BlockSpec((1, tq, NSA_QW), lambda bi, i: (bi, i, 0)),
                  pl.BlockSpec((1, tq, LANES), lambda bi, i: (bi, i, 0)),
                  per_batch(kc), per_batch(vct), per_batch(ks), per_batch(vst), per_batch(kw), per_batch(vwt),
                  const(overlap_t), const(onehot)],
        out_specs=pl.BlockSpec((1, tq, NSA_QW), lambda bi, i: (bi, i, 0)),
        out_shape=jax.ShapeDtypeStruct((b, s, NSA_QW), MXU_DTYPE),
        scratch_shapes=[pltpu.VMEM((NSA_HEADS, tq, 2 * LANES), MXU_DTYPE),
                        pltpu.VMEM((NSA_HEADS, V_ROWS, tq), F32),
                        pltpu.VMEM((2, NSA_HEADS, tks, tq), F32)],
        compiler_params=_params(2),
        name="nsa_attn",
    )(nq, ng, kc, vct, ks, vst, kw, vwt, overlap_t, onehot)


def _dsa_kernel(iq_ref, iwt_ref, ik_ref, q_ref, kv_ref, kvt_ref, o_ref, khi_ref, klo_ref, bias_ref, qa_ref,
                acc_ref, s_ref, *, tq, tk, top_k):
    i = pl.program_id(1)
    q0 = i * tq
    n_tiles = (q0 + tq + tk - 1) // tk
    lane = _lane_iota((tq, LANES))
    t_q = q0 + _lane_iota((tk, tq))
    row_k = _row_iota((tk, tq))

    iqa = []
    for s in range(IDX_QW // LANES):
        qs = iq_ref[0, :, s * LANES:(s + 1) * LANES].astype(F32)
        for p in range(LANES // DSA_IDX_DIM):
            mine = jnp.where((lane >> 5) == p, qs, 0.0)
            if p:
                mine = pltpu.roll(mine, LANES - p * DSA_IDX_DIM, 1)
            iqa.append(mine.astype(MXU_DTYPE))
    w_h = [iwt_ref[0, h:h + 1, :] for h in range(DSA_IDX_HEADS)]

    def score_tiles(tiles, last_is_diagonal):
        logits = []
        for kt in tiles:
            ik = ik_ref[0, pl.ds(kt * tk, tk), :]
            logits.append([_dot_nt(ik, iqa[h]) for h in range(DSA_IDX_HEADS)])
        for n, kt in enumerate(tiles):
            score = jnp.zeros((tk, tq), F32)
            for h in range(DSA_IDX_HEADS):
                score = score + w_h[h] * jnp.maximum(logits[n][h], 0.0)
            score = jnp.where(score == 0.0, 0.0, score)
            bits = pltpu.bitcast(score, jnp.int32)
            key = jnp.where(bits < 0, bits ^ 0x7FFFFFFF, bits)
            if last_is_diagonal and n == len(tiles) - 1:
                key = jnp.where(kt * tk + row_k <= t_q, key, INT_MIN)
            khi_ref[kt] = (key >> 16).astype(jnp.int16)
            klo_ref[kt] = ((key & 0xFFFF) - 0x8000).astype(jnp.int16)

    def score_pair(p, _):
        score_tiles([2 * p, 2 * p + 1], False)
        return 0

    plain = n_tiles - 1
    lax.fori_loop(0, plain // 2, score_pair, 0)
    lax.cond(plain % 2 == 1,
             lambda: score_tiles([n_tiles - 2, n_tiles - 1], True),
             lambda: score_tiles([n_tiles - 1], True))
    one, zero = jnp.ones((), jnp.bfloat16), jnp.zeros((), jnp.bfloat16)

    def count(flags):
        def body(kt, acc):
            return acc + _fold_rows(flags(kt), jnp.add, rows=32).astype(F32)
        acc = lax.fori_loop(0, n_tiles, body, jnp.zeros((32, tq), F32))
        return jnp.sum(acc, axis=0, keepdims=True)

    uncounted = float(2 ** 24)

    def kth_largest(ref):
        def body(b, carry):
            ans_u, reached = carry
            cand_u = ans_u | lax.shift_left(jnp.int32(1), 15 - b)
            cand = (cand_u - 0x8000).astype(jnp.int16)
            cnt = count(lambda kt: jnp.where(ref[kt] >= cand, one, zero))
            ok = cnt >= top_k
            return jnp.where(ok, cand_u, ans_u), jnp.where(ok, cnt, reached)
        ans_u, reached = lax.fori_loop(0, 16, body, (jnp.zeros((1, tq), jnp.int32),
                                                     jnp.full((1, tq), uncounted, F32)))
        return ans_u - 0x8000, reached

    t_hi32, reached_hi = kth_largest(khi_ref)
    t_hi = t_hi32.astype(jnp.int16)

    def pin_body(kt, _):
        hi = khi_ref[kt]
        klo_ref[kt] = jnp.where(hi == t_hi, klo_ref[kt],
                                jnp.where(hi > t_hi, jnp.int16(32767), jnp.int16(-32768)))
        return 0

    lax.fori_loop(0, n_tiles, pin_body, 0)
    t_lo32, reached_lo = kth_largest(klo_ref)
    reached = jnp.where(reached_lo < uncounted, reached_lo, reached_hi)
    t_lo = jnp.where(t_hi32 == -0x8000, jnp.maximum(t_lo32, 1 - 0x8000), t_lo32).astype(jnp.int16)

    def at_least(kt, yes, no):
        hi = khi_ref[kt]
        return jnp.where(hi > t_hi, yes, jnp.where(hi == t_hi, jnp.where(klo_ref[kt] >= t_lo, yes, no), no))

    def above(kt):
        return jnp.where(khi_ref[kt] > t_hi, one, jnp.where(klo_ref[kt] > t_lo, one, zero))

    bias_yes, bias_no = jnp.zeros((), bias_ref.dtype), jnp.full((), NEG, bias_ref.dtype)

    def plain_bias():
        def body(kt, _):
            bias_ref[kt] = at_least(kt, bias_yes, bias_no)
            return 0
        lax.fori_loop(0, n_tiles, body, 0)

    def ranked_bias():
        need = top_k - count(above)
        lower = jnp.where(_lane_iota((tk, tk)) < _row_iota((tk, tk)), 1.0, 0.0).astype(MXU_DTYPE)

        def body(kt, seen):
            tie = jnp.where(khi_ref[kt] == t_hi, jnp.where(klo_ref[kt] == t_lo, one, zero), zero)
            rank = _dot(lower, tie.astype(MXU_DTYPE)) + seen
            tie = tie.astype(F32)
            take = jnp.where(above(kt).astype(F32) > 0.0, 1.0, jnp.where(rank < need, tie, 0.0))
            bias_ref[kt] = ((take - 1.0) * (-NEG)).astype(bias_ref.dtype)
            return seen + _key_sum(tie)

        lax.fori_loop(0, n_tiles, body, jnp.zeros((1, tq), F32))

    lax.cond(jnp.max(reached) > top_k, ranked_bias, plain_bias)

    for s in range(DSA_QW // LANES):
        qs = q_ref[0, :, s * LANES:(s + 1) * LANES].astype(F32)
        lo = jnp.where(lane < HEAD_DIM, qs, 0.0)
        hi = _swap_halves(jnp.where(lane < HEAD_DIM, 0.0, qs))
        qa_ref[2 * s] = lo.astype(MXU_DTYPE)
        qa_ref[2 * s + 1] = hi.astype(MXU_DTYPE)

    def att_scores(kt):
        kv = kv_ref[0, pl.ds(kt * tk, tk), :]
        bias = bias_ref[kt].astype(F32)
        return tuple(_dot_nt(kv, qa_ref[h]) + bias for h in range(DSA_HEADS))

    def att_softmax(kt, ms, scores):
        kv_t = kvt_ref[0, kt]
        return tuple(_online_update(ms[h], acc_ref.at[h], scores[h], kv_t) for h in range(DSA_HEADS))

    _attend_pipelined(n_tiles, att_scores, att_softmax, _online_init(tq, acc_ref), lambda scores: scores, s_ref)
    outs = [acc_ref[h, :HEAD_DIM, :] * (1.0 / acc_ref[h, HEAD_DIM:HEAD_DIM + 1, :]) for h in range(DSA_HEADS)]

    for s in range(DSA_QW // LANES):
        pair = jnp.concatenate([outs[2 * s], outs[2 * s + 1]], axis=0)
        o_ref[0, :, s * LANES:(s + 1) * LANES] = pair.T.astype(o_ref.dtype)


def _dsa_attention(iq, iwt, ik, dq, dkv, dkvt, *, tq, tk, top_k):
    b, s, _ = dq.shape
    tile = lambda width: pl.BlockSpec((1, tq, width), lambda bi, i: (bi, i, 0))
    full = pl.BlockSpec((1, s, LANES), lambda bi, i: (bi, 0, 0))
    return pl.pallas_call(
        functools.partial(_dsa_kernel, tq=tq, tk=tk, top_k=top_k),
        grid=(b, s // tq),
        in_specs=[tile(IDX_QW), pl.BlockSpec((1, DSA_IDX_HEADS, tq), lambda bi, i: (bi, 0, i)), full,
                  tile(DSA_QW), full, pl.BlockSpec((1,) + dkvt.shape[1:], lambda bi, i: (bi, 0, 0, 0))],
        out_specs=tile(DSA_QW),
        out_shape=jax.ShapeDtypeStruct((b, s, DSA_QW), MXU_DTYPE),
        scratch_shapes=[pltpu.VMEM((s // tk, tk, tq), jnp.int16), pltpu.VMEM((s // tk, tk, tq), jnp.int16),
                        pltpu.VMEM((s // tk, tk, tq), MXU_DTYPE),
                        pltpu.VMEM((DSA_HEADS, tq, LANES), MXU_DTYPE), pltpu.VMEM((DSA_HEADS, V_ROWS, tq), F32),
                        pltpu.VMEM((2, DSA_HEADS, tk, tq), F32)],
        compiler_params=_params(2),
        name="dsa_attn",
    )(iq, iwt, ik, dq, dkv, dkvt)


def _merge_kernel(x_ref, g_ref, oa_ref, ob_ref, oc_ref, wg_ref, wa_ref, wb_ref, wc_ref, wo_ref, o_ref):
    x = x_ref[...]
    d = x.shape[1]
    ms = jnp.mean(x * x, axis=-1, keepdims=True)
    h = (x * lax.rsqrt(ms + NORM_EPS) * g_ref[...]).astype(MXU_DTYPE)
    merged = jnp.zeros(x.shape, F32)
    for r, (o_r, w_r) in enumerate(((oa_ref, wa_ref), (ob_ref, wb_ref), (oc_ref, wc_ref))):
        gate = jax.nn.sigmoid(_dot(h, wg_ref[:, r * d:(r + 1) * d]))
        merged = merged + gate * _dot(o_r[...], w_r[...])
    o_ref[...] = x + _dot(merged.astype(MXU_DTYPE), wo_ref[...])


def _merge(x2, g, oa, ob, oc, w_gate, wa, wb, wc, wo, *, tm):
    n, d = x2.shape
    row = lambda i: (i, 0)
    const = lambda i: (0, 0)
    return pl.pallas_call(
        _merge_kernel,
        grid=(n // tm,),
        in_specs=[pl.BlockSpec((tm, d), row), pl.BlockSpec((1, d), const),
                  pl.BlockSpec((tm, SB_W), row), pl.BlockSpec((tm, NSA_QW), row),
                  pl.BlockSpec((tm, DSA_QW), row),
                  pl.BlockSpec(w_gate.shape, const), pl.BlockSpec(wa.shape, const),
                  pl.BlockSpec(wb.shape, const), pl.BlockSpec(wc.shape, const),
                  pl.BlockSpec(wo.shape, const)],
        out_specs=pl.BlockSpec((tm, d), row),
        out_shape=jax.ShapeDtypeStruct((n, d), F32),
        compiler_params=_params(1),
        name="merge",
    )(x2, g, oa, ob, oc, w_gate, wa, wb, wc, wo)


def _rope_tables(pos, rot_dim, head_dim):
    half = rot_dim // 2
    inv = ROPE_THETA ** (-jnp.arange(0, rot_dim, 2, dtype=F32) / rot_dim)
    ang = pos.astype(F32)[:, None] * inv[None, :]
    cos, sin = jnp.cos(ang), jnp.sin(ang)
    n = pos.shape[0]
    rest = head_dim - rot_dim
    zero_h = jnp.zeros((n, half), F32)
    c = jnp.concatenate([cos, cos, jnp.ones((n, rest), F32)], axis=-1)
    sm = jnp.concatenate([-sin, zero_h, jnp.zeros((n, rest), F32)], axis=-1)
    sp = jnp.concatenate([zero_h, sin, jnp.zeros((n, rest), F32)], axis=-1)
    reps = LANES // head_dim
    return tuple(jnp.tile(t, (1, reps)) for t in (c, sm, sp))


def _pad_cols(w, width):
    return jnp.pad(w, ((0, 0), (0, width - w.shape[1])))


def _relayout_w_in(w_in):
    offs = np.cumsum((SB_W, SB_W, SB_W, NSA_QW) + (NSA_KVW,) * 6
                     + (3 * NSA_HEADS, DSA_QW, HEAD_DIM, HEAD_DIM, IDX_QW, DSA_IDX_DIM, DSA_IDX_HEADS))
    o_gate, o_dq, o_dk, o_dv, o_iq, o_ik, o_iw, o_end = offs[9:17].tolist()
    parts = [w_in[:, :o_gate], _pad_cols(w_in[:, o_gate:o_dq], LANES), w_in[:, o_dq:o_dk],
             w_in[:, o_dk:o_iq], w_in[:, o_iq:o_ik], _pad_cols(w_in[:, o_ik:o_iw], LANES),
             _pad_cols(w_in[:, o_iw:o_end], LANES)]
    w = jnp.concatenate(parts, axis=1)
    assert w.shape[1] == _P_END
    return w.astype(MXU_DTYPE)


def _compress_weights(pe, w1, w2):
    r = NSA_CMP_LEN // NSA_CMP_STRIDE
    hid = NSA_CMP_HIDDEN
    w1r = w1.reshape(2, r, NSA_CMP_STRIDE, HEAD_DIM, hid)
    eye = jnp.eye(NSA_GROUPS, dtype=w1.dtype)
    w1e = jnp.einsum('kmldj,gh->kmlgdhj', w1r, eye)
    w1e = w1e.reshape(2, r, NSA_CMP_STRIDE * NSA_KVW, NSA_GROUPS * hid)
    w2e = jnp.einsum('kjd,gh->kgjhd', w2, eye).reshape(2, NSA_GROUPS * hid, NSA_KVW)
    pe_e = jnp.broadcast_to(pe.reshape(2, r, NSA_CMP_STRIDE, 1, HEAD_DIM),
                            (2, r, NSA_CMP_STRIDE, NSA_GROUPS, HEAD_DIM))
    return pe_e.reshape(2, r, NSA_CMP_STRIDE * NSA_KVW), w1e.astype(MXU_DTYPE), w2e.astype(MXU_DTYPE)


def _overlap_matrix_t(ncp, seq):
    c = np.arange(ncp)[None, :] * NSA_CMP_STRIDE
    j = np.arange(LANES)[:, None] * NSA_SEL_LEN
    ov = (c < j + NSA_SEL_LEN) & (c + NSA_CMP_LEN - 1 >= j) & (j < seq)
    ov &= (np.arange(ncp)[None, :] < ncp - (NSA_CMP_LEN // NSA_CMP_STRIDE - 1))
    return jnp.asarray(ov, dtype=MXU_DTYPE)


def _block_onehot(seq):
    return jnp.asarray(np.arange(seq)[:, None] // NSA_SEL_LEN == np.arange(LANES)[None, :], dtype=MXU_DTYPE)


def _key_tiles_t(v, tile):
    b, s, w = v.shape
    return jnp.swapaxes(v.reshape(b, s // tile, tile, w), 2, 3)


def _group_values_t(v, tile):
    b, s, w = v.shape
    groups = w // HEAD_DIM
    vt = _key_tiles_t(v, tile).reshape(b, s // tile, groups, HEAD_DIM, tile)
    vt = jnp.swapaxes(vt, 1, 2)
    pad = jnp.zeros((b, groups, s // tile, V_ROWS - HEAD_DIM, tile), v.dtype).at[:, :, :, 0, :].set(1)
    return jnp.concatenate([vt, pad], axis=3)


def kernel(x, ffn1_norm, ffn1_w_gu, ffn1_w_down, mix_norm, w_in, w_gate, nsa_q_norm, nsa_k_norm,
           nsa_cmp_pe, nsa_cmp_w1, nsa_cmp_w2, dsa_q_norm, dsa_k_norm, w_br_a, w_br_b, w_br_c, w_out,
           ffn2_norm, ffn2_w_gu, ffn2_w_down):
    b, s, d = x.shape
    depth = w_in.shape[0]
    n = b * s
    ncp = s // NSA_CMP_STRIDE
    assert s // NSA_SEL_LEN <= LANES and s % 1024 == 0
    top_k = min(DSA_TOPK, s // 4)
    ffn_tf = ffn1_w_down.shape[1]
    cast = lambda w: w.astype(MXU_DTYPE)
    tile2 = lambda v, reps: jnp.tile(v, reps)[None, :]

    pos = jnp.arange(s)
    rope_main = _rope_tables(pos, ROT_DIM, HEAD_DIM)
    rope_idx = _rope_tables(pos, DSA_IDX_ROT, DSA_IDX_DIM)
    rope_cmp = _rope_tables(jnp.arange(ncp) * NSA_CMP_STRIDE + NSA_CMP_LEN - 1, ROT_DIM, HEAD_DIM)
    overlap_t = _overlap_matrix_t(ncp, s)
    onehot = _block_onehot(s)

    x2 = x.reshape(n, d)
    for l in range(depth):
        x2 = _ffn(x2, ffn1_norm[l][None, :], cast(ffn1_w_gu[l]), cast(ffn1_w_down[l]), tm=FFN_TM, tf=ffn_tf)

        (sb, nq, nkc, nvc, nkv, ng, dq, dkv, iq, ik, iw) = _proj(
            x2, mix_norm[l][None, :], _relayout_w_in(w_in[l]),
            tile2(nsa_q_norm[l], NSA_HEADS),
            jnp.stack([jnp.tile(nsa_k_norm[l, 1], NSA_GROUPS), jnp.tile(nsa_k_norm[l, 2], NSA_GROUPS)]),
            tile2(dsa_q_norm[l], DSA_HEADS), tile2(dsa_k_norm[l], 2),
            rope_main, rope_idx, tm=256, seq=s)
        r3 = lambda t: t.reshape(b, s, t.shape[-1])

        pe_e, w1e, w2e = _compress_weights(nsa_cmp_pe[l], nsa_cmp_w1[l], nsa_cmp_w2[l])
        kc, vc = _compress(nkc.reshape(b, ncp, NSA_CMP_STRIDE * NSA_KVW),
                           nvc.reshape(b, ncp, NSA_CMP_STRIDE * NSA_KVW),
                           pe_e, w1e, w2e, tile2(nsa_k_norm[l, 0], NSA_GROUPS), rope_cmp)

        o_a = _sb_attention(r3(sb), tq=256)

        nkv = r3(nkv)
        ks, vs, kw, vw = (nkv[:, :, c * LANES:(c + 1) * LANES] for c in range(4))
        o_b = _nsa_attention(r3(nq), r3(ng), kc, _group_values_t(vc, ncp)[:, :, 0], ks,
                             _group_values_t(vs, TKS), kw, _group_values_t(vw, TQ_NSA), overlap_t, onehot,
                             tq=TQ_NSA, tks=TKS)

        dkv = r3(dkv)
        iwt = jnp.swapaxes(r3(iw)[:, :, :DSA_IDX_HEADS], 1, 2)
        dkvt = _group_values_t(dkv[:, :, HEAD_DIM:], TKS)[:, 0]
        o_c = _dsa_attention(r3(iq), iwt, r3(ik), r3(dq), dkv, dkvt, tq=TQ, tk=TKS, top_k=top_k)

        x2 = _merge(x2, mix_norm[l][None, :], o_a.reshape(n, SB_W), o_b.reshape(n, NSA_QW),
                    o_c.reshape(n, DSA_QW), cast(w_gate[l]), cast(w_br_a[l]), cast(w_br_b[l]),
                    cast(w_br_c[l]), cast(w_out[l]), tm=256)

        x2 = _ffn(x2, ffn2_norm[l][None, :], cast(ffn2_w_gu[l]), cast(ffn2_w_down[l]), tm=FFN_TM, tf=ffn_tf)
    return x2.reshape(b, s, d)
```

```python
import functools

import numpy as np
import jax
import jax.numpy as jnp
from jax import lax
from jax.experimental import pallas as pl
from jax.experimental.pallas import tpu as pltpu

HEAD_DIM = 64
ROT_DIM = HEAD_DIM // 4
ROPE_THETA = 500000.0
NORM_EPS = 1e-6

SB_HEADS = 4
NSA_HEADS = 8
NSA_GROUPS = 2
NSA_HPG = NSA_HEADS // NSA_GROUPS
NSA_CMP_LEN = 32
NSA_CMP_STRIDE = 16
NSA_CMP_HIDDEN = 2 * HEAD_DIM
NSA_SEL_LEN = 64
NSA_SEL_N = 16
NSA_WINDOW = 512
NSA_FORCED_SCORE = 1.0e4
DSA_HEADS = 4
DSA_IDX_HEADS = 8
DSA_IDX_DIM = 32
DSA_IDX_ROT = DSA_IDX_DIM // 4
DSA_TOPK = 256

SB_W = SB_HEADS * HEAD_DIM
NSA_QW = NSA_HEADS * HEAD_DIM
NSA_KVW = NSA_GROUPS * HEAD_DIM
DSA_QW = DSA_HEADS * HEAD_DIM
IDX_QW = DSA_IDX_HEADS * DSA_IDX_DIM

LANES = 128
VMEM_LIMIT = 56 * 1024 * 1024
MXU_DTYPE = jnp.bfloat16
NEG = -1.0e30
SB_SKIP = -120.0
INT_MIN = -2 ** 31
LOG2E = 1.4426950408889634

F32 = jnp.float32

TQ = 256
TQ_NSA = 256
TKS = 512
SB_TQ = 256
ROW_TM = 256
V_ROWS = LANES
HEAD_SHIFT = HEAD_DIM.bit_length() - 1
IDX_SHIFT = DSA_IDX_DIM.bit_length() - 1
SEL_SHIFT = NSA_SEL_LEN.bit_length() - 1
I16_MIN, I16_MAX = -2 ** 15, 2 ** 15 - 1
FFN_CHUNK = 512
FFN_TM = 512
assert TKS % TQ == 0 and TKS % TQ_NSA == 0 and NSA_WINDOW % TQ_NSA == 0


def _dot(a, b):
    return jnp.dot(a, b, preferred_element_type=F32)


def _dot_nt(a, b):
    return lax.dot_general(a, b, (((1,), (1,)), ((), ())), preferred_element_type=F32)


def _params(n_axes):
    return pltpu.CompilerParams(dimension_semantics=("arbitrary",) * n_axes,
                                vmem_limit_bytes=VMEM_LIMIT)


def _lane_iota(shape):
    return lax.broadcasted_iota(jnp.int32, shape, len(shape) - 1)


def _row_iota(shape):
    return lax.broadcasted_iota(jnp.int32, shape, len(shape) - 2)


def _swap_halves(x):
    return pltpu.roll(x, LANES // 2, 1)


def _head_rms(ys):
    lo = _lane_iota(ys.shape) < HEAD_DIM
    sq = ys * ys
    s_lo = jnp.sum(jnp.where(lo, sq, 0.0), axis=-1, keepdims=True)
    s_hi = jnp.sum(jnp.where(lo, 0.0, sq), axis=-1, keepdims=True)
    ms = jnp.where(lo, s_lo, s_hi) * (1.0 / HEAD_DIM)
    return ys * lax.rsqrt(ms + NORM_EPS)


def _rope(ys, c, sm, sp, half):
    return ys * c + pltpu.roll(ys, LANES - half, 1) * sm + pltpu.roll(ys, half, 1) * sp


def _ffn_kernel(x_ref, g_ref, wg_ref, wu_ref, wd_ref, o_ref, h_ref, acc_ref):
    f = pl.program_id(1)

    @pl.when(f == 0)
    def _():
        x = x_ref[...]
        ms = jnp.mean(x * x, axis=-1, keepdims=True)
        h_ref[...] = (x * lax.rsqrt(ms + NORM_EPS) * g_ref[...]).astype(h_ref.dtype)
        acc_ref[...] = jnp.zeros_like(acc_ref)

    h = h_ref[...]
    tf = wd_ref.shape[0]
    bounds = [(c, min(c + FFN_CHUNK, tf)) for c in range(0, tf, FFN_CHUNK)]

    def gate_up(c0, c1):
        return _dot(h, wg_ref[:, c0:c1]), _dot(h, wu_ref[:, c0:c1])

    ahead = gate_up(*bounds[0])
    for idx, (c0, c1) in enumerate(bounds):
        gate, up = ahead
        if idx + 1 < len(bounds):
            ahead = gate_up(*bounds[idx + 1])
        act = (gate * jax.nn.sigmoid(gate) * up).astype(h_ref.dtype)
        acc_ref[...] += _dot(act, wd_ref[c0:c1, :])

    @pl.when(f == pl.num_programs(1) - 1)
    def _():
        o_ref[...] = x_ref[...] + 0.5 * acc_ref[...]


def _ffn(x2, g, w_gu, w_down, *, tm, tf):
    n, d = x2.shape
    d_ff = w_down.shape[0]
    nf = d_ff // tf
    return pl.pallas_call(
        _ffn_kernel,
        grid=(n // tm, nf),
        in_specs=[
            pl.BlockSpec((tm, d), lambda i, f: (i, 0)),
            pl.BlockSpec((1, d), lambda i, f: (0, 0)),
            pl.BlockSpec((d, tf), lambda i, f: (0, f)),
            pl.BlockSpec((d, tf), lambda i, f: (0, f + nf)),
            pl.BlockSpec((tf, d), lambda i, f: (f, 0)),
        ],
        out_specs=pl.BlockSpec((tm, d), lambda i, f: (i, 0)),
        out_shape=jax.ShapeDtypeStruct((n, d), F32),
        scratch_shapes=[pltpu.VMEM((tm, d), MXU_DTYPE), pltpu.VMEM((tm, d), F32)],
        compiler_params=_params(2),
        name="ffn",
    )(x2, g, w_gu, w_gu, w_down)


_P_SB = 0
_P_NQ = _P_SB + 3 * SB_W
_P_NCV = _P_NQ + NSA_QW
_P_NKV = _P_NCV + 2 * NSA_KVW
_P_NG = _P_NKV + 4 * NSA_KVW
_P_DQ = _P_NG + LANES
_P_DKV = _P_DQ + DSA_QW
_P_IQ = _P_DKV + LANES
_P_IK = _P_IQ + IDX_QW
_P_IW = _P_IK + LANES
_P_END = _P_IW + LANES


def _proj_kernel(x_ref, g_ref, w_ref, nq_g_ref, nk_g_ref, dq_g_ref, dk_g_ref,
                 rc_ref, rm_ref, rp_ref, ic_ref, im_ref, ip_ref,
                 sb_ref, nq_ref, nkc_ref, nvc_ref, nkv_ref, ng_ref, dq_ref, dkv_ref,
                 iq_ref, ik_ref, iw_ref):
    x = x_ref[...]
    ms = jnp.mean(x * x, axis=-1, keepdims=True)
    h = (x * lax.rsqrt(ms + NORM_EPS) * g_ref[...]).astype(MXU_DTYPE)
    rc, rm, rp = rc_ref[...], rm_ref[...], rp_ref[...]
    ic, im, ip = ic_ref[...], im_ref[...], ip_ref[...]
    half, ihalf = ROT_DIM // 2, DSA_IDX_ROT // 2
    scale = HEAD_DIM ** -0.5
    scale2 = scale * LOG2E

    def cols(a, width):
        return _dot(h, w_ref[:, a:a + width])

    def slab(y, s):
        return y[:, s * LANES:(s + 1) * LANES]

    y = cols(_P_SB, 3 * SB_W)
    sb_ref[:, 0:SB_W] = (y[:, 0:SB_W] * scale).astype(sb_ref.dtype)
    sb_ref[:, SB_W:3 * SB_W] = y[:, SB_W:3 * SB_W].astype(sb_ref.dtype)

    y = cols(_P_NQ, NSA_QW)
    for s in range(NSA_QW // LANES):
        ys = _head_rms(slab(y, s)) * slab(nq_g_ref[...], s)
        nq_ref[:, s * LANES:(s + 1) * LANES] = (_rope(ys, rc, rm, rp, half) * scale2).astype(nq_ref.dtype)

    y = cols(_P_NCV, 2 * NSA_KVW)
    nkc_ref[...] = slab(y, 0)
    nvc_ref[...] = slab(y, 1)

    y = cols(_P_NKV, 4 * NSA_KVW)
    for s, gi in ((0, 0), (2, 1)):
        ys = _head_rms(slab(y, s)) * nk_g_ref[gi:gi + 1, :]
        nkv_ref[:, s * LANES:(s + 1) * LANES] = _rope(ys, rc, rm, rp, half).astype(nkv_ref.dtype)
    for s in (1, 3):
        nkv_ref[:, s * LANES:(s + 1) * LANES] = slab(y, s).astype(nkv_ref.dtype)

    ng_ref[...] = jax.nn.sigmoid(cols(_P_NG, LANES))

    y = cols(_P_DQ, DSA_QW)
    for s in range(DSA_QW // LANES):
        ys = _head_rms(slab(y, s)) * slab(dq_g_ref[...], s)
        dq_ref[:, s * LANES:(s + 1) * LANES] = (_rope(ys, rc, rm, rp, half) * scale2).astype(dq_ref.dtype)

    y = cols(_P_DKV, LANES)
    yk = _rope(_head_rms(y) * dk_g_ref[...], rc, rm, rp, half)
    dkv_ref[...] = jnp.where(_lane_iota(y.shape) < HEAD_DIM, yk, y).astype(dkv_ref.dtype)

    y = cols(_P_IQ, IDX_QW)
    for s in range(IDX_QW // LANES):
        iq_ref[:, s * LANES:(s + 1) * LANES] = _rope(slab(y, s), ic, im, ip, ihalf).astype(iq_ref.dtype)
    ik_ref[...] = _rope(cols(_P_IK, LANES), ic, im, ip, ihalf).astype(ik_ref.dtype)
    iw_ref[...] = cols(_P_IW, LANES) * (DSA_IDX_HEADS ** -0.5) * (DSA_IDX_DIM ** -0.5)


def _proj(x2, g, w, nq_g, nk_g, dq_g, dk_g, rope_main, rope_idx, *, tm, seq):
    n, d = x2.shape
    nt = seq // tm
    row = lambda i: (i, 0)
    const = lambda i: (0, 0)
    pos = lambda i: (i % nt, 0)
    widths = [(3 * SB_W, MXU_DTYPE), (NSA_QW, MXU_DTYPE), (NSA_KVW, F32), (NSA_KVW, F32),
              (4 * NSA_KVW, MXU_DTYPE), (LANES, F32), (DSA_QW, MXU_DTYPE), (LANES, MXU_DTYPE),
              (IDX_QW, MXU_DTYPE), (LANES, MXU_DTYPE), (LANES, F32)]
    return pl.pallas_call(
        _proj_kernel,
        grid=(n // tm,),
        in_specs=[pl.BlockSpec((tm, d), row), pl.BlockSpec((1, d), const),
                  pl.BlockSpec((d, _P_END), const),
                  pl.BlockSpec((1, NSA_QW), const), pl.BlockSpec((2, LANES), const),
                  pl.BlockSpec((1, DSA_QW), const), pl.BlockSpec((1, LANES), const)]
                 + [pl.BlockSpec((tm, LANES), pos)] * 6,
        out_specs=[pl.BlockSpec((tm, wd), row) for wd, _ in widths],
        out_shape=[jax.ShapeDtypeStruct((n, wd), dt) for wd, dt in widths],
        compiler_params=_params(1),
        name="proj",
    )(x2, g, w, nq_g, nk_g, dq_g, dk_g, *rope_main, *rope_idx)


def _compress_kernel(xk_ref, xv_ref, pe_ref, w1_ref, w2_ref, g_ref, rc_ref, rm_ref, rp_ref,
                     kc_ref, vc_ref):
    ncp = xk_ref.shape[1]

    def compress(x, kv):
        a0 = _dot((x + pe_ref[kv, 0:1, :]).astype(MXU_DTYPE), w1_ref[kv, 0])
        a1 = _dot((x + pe_ref[kv, 1:2, :]).astype(MXU_DTYPE), w1_ref[kv, 1])
        pre = a0 + pltpu.roll(a1, ncp - 1, 0)
        hid = pre * jax.nn.sigmoid(pre)
        return _dot(hid.astype(MXU_DTYPE), w2_ref[kv])

    yk = _head_rms(compress(xk_ref[0], 0)) * g_ref[...]
    kc_ref[0] = _rope(yk, rc_ref[...], rm_ref[...], rp_ref[...], ROT_DIM // 2).astype(kc_ref.dtype)
    vc_ref[0] = compress(xv_ref[0], 1).astype(vc_ref.dtype)


def _compress(xk, xv, pe, w1, w2, g, rope_c):
    b, ncp, wide = xk.shape
    c3 = lambda i: (0, 0, 0)
    c4 = lambda i: (0, 0, 0, 0)
    c2 = lambda i: (0, 0)
    per_b = lambda i: (i, 0, 0)
    return pl.pallas_call(
        _compress_kernel,
        grid=(b,),
        in_specs=[pl.BlockSpec((1, ncp, wide), per_b), pl.BlockSpec((1, ncp, wide), per_b),
                  pl.BlockSpec(pe.shape, c3), pl.BlockSpec(w1.shape, c4), pl.BlockSpec(w2.shape, c3),
                  pl.BlockSpec((1, LANES), c2)] + [pl.BlockSpec((ncp, LANES), c2)] * 3,
        out_specs=[pl.BlockSpec((1, ncp, LANES), per_b)] * 2,
        out_shape=[jax.ShapeDtypeStruct((b, ncp, LANES), MXU_DTYPE)] * 2,
        compiler_params=_params(1),
        name="compress",
    )(xk, xv, pe, w1, w2, g, *rope_c)


def _sb_kernel(q_ref, k_ref, v_ref, o_ref, *, tq):
    i = pl.program_id(1)
    nslab = SB_W // LANES
    heads = [(sl, p) for sl in range(nslab) for p in range(2)]
    lane = _lane_iota((tq, LANES))
    rows = _row_iota((tq, tq))
    colsq = _lane_iota((tq, tq))
    upper = jnp.where(_row_iota((tq, tq + LANES)) > _lane_iota((tq, tq + LANES)), 1.0,
                      jnp.where(_lane_iota((tq, tq + LANES)) >= tq, 1.0, 0.0)).astype(MXU_DTYPE)
    in_half = [(lane >> HEAD_SHIFT) == p for p in range(2)]
    qh = [jnp.where(in_half[p], q_ref[0, :, sl * LANES:(sl + 1) * LANES].astype(F32), 0.0).astype(MXU_DTYPE)
          for sl, p in heads]
    reps = tq // LANES

    def tile(kt, carries, accs, diagonal):
        ks = [k_ref[0, pl.ds(kt * tq, tq), sl * LANES:(sl + 1) * LANES] for sl in range(nslab)]
        vs = [v_ref[0, pl.ds(kt * tq, tq), sl * LANES:(sl + 1) * LANES] for sl in range(nslab)]
        zs = [_dot_nt(qh[n], ks[sl]) for n, (sl, _) in enumerate(heads)]
        new_c, new_a = [], []
        for p, (sl, _) in enumerate(heads):
            z = zs[p]
            l = -(jnp.maximum(z, 0.0) + jnp.log1p(jnp.exp(-jnp.abs(z))))
            if diagonal:
                past = colsq < rows
                l = jnp.where(past, l, 0.0)
            hi = l.astype(MXU_DTYPE)
            lo = (l - hi.astype(F32)).astype(MXU_DTYPE)
            sums = _dot(hi, upper) + _dot(lo, upper)
            tail = sums[:, :tq] + jnp.concatenate([carries[p]] * reps, axis=1)
            w = jnp.exp(z + l + tail)
            if diagonal:
                w = jnp.where(past, w, 0.0)
            new_a.append(accs[p] + _dot(w.astype(MXU_DTYPE), vs[sl]))
            new_c.append(carries[p] + sums[:, tq:])
        return tuple(new_c), tuple(new_a)

    zeros = (jnp.zeros((tq, LANES), F32),) * len(heads)
    carries, accs = tile(i, zeros, zeros, True)

    def cond(st):
        kt, carries, _ = st
        worst = functools.reduce(jnp.maximum, carries)
        return (kt >= 0) & (jnp.max(worst) > SB_SKIP)

    def body(st):
        kt, carries, accs = st
        carries, accs = tile(kt, carries, accs, False)
        return kt - 1, carries, accs

    _, _, accs = lax.while_loop(cond, body, (i - 1, carries, accs))
    for sl in range(nslab):
        o_ref[0, :, sl * LANES:(sl + 1) * LANES] = jnp.where(in_half[0], accs[2 * sl],
                                                             accs[2 * sl + 1]).astype(o_ref.dtype)


def _sb_attention(sb, *, tq):
    b, s, _ = sb.shape
    return pl.pallas_call(
        functools.partial(_sb_kernel, tq=tq),
        grid=(b, s // tq),
        in_specs=[pl.BlockSpec((1, tq, SB_W), lambda bi, i: (bi, i, 0)),
                  pl.BlockSpec((1, s, SB_W), lambda bi, i: (bi, 0, 1)),
                  pl.BlockSpec((1, s, SB_W), lambda bi, i: (bi, 0, 2))],
        out_specs=pl.BlockSpec((1, tq, SB_W), lambda bi, i: (bi, i, 0)),
        out_shape=jax.ShapeDtypeStruct((b, s, SB_W), MXU_DTYPE),
        compiler_params=_params(2),
        name="sb_attn",
    )(sb, sb, sb)


def _group_queries(q_ref, tq):
    lane = _lane_iota((tq, LANES))
    qa = []
    for h in range(NSA_HEADS):
        slab, parity, g = h // 2, h % 2, h // NSA_HPG
        qs = q_ref[0, :, slab * LANES:(slab + 1) * LANES].astype(F32)
        mine = jnp.where((lane >> HEAD_SHIFT) == parity, qs, 0.0)
        if parity != g:
            mine = _swap_halves(mine)
        qa.append(mine.astype(MXU_DTYPE))
    return qa


def _fold_rows(x, op, rows=64):
    while x.shape[0] > rows and x.shape[0] % 16 == 0:
        half = x.shape[0] // 2
        x = op(x[:half], x[half:])
    return x


def _key_max(x):
    return jnp.max(_fold_rows(x, jnp.maximum), axis=0, keepdims=True)


def _key_sum(x):
    return jnp.sum(_fold_rows(x, jnp.add), axis=0, keepdims=True)


def _online_update(m, acc_ref, s, v_t):
    m_new = jnp.maximum(m, _key_max(s))
    p = jnp.exp2(s - m_new)
    acc_ref[...] = jnp.exp2(m - m_new) * acc_ref[...] + _dot(v_t, p.astype(MXU_DTYPE))
    return m_new


def _attend_pipelined(n_tiles, scores_fn, softmax_fn, ms, fix_last, s_ref):
    heads = s_ref.shape[1]

    def issue(slot, kt):
        for h, sc in enumerate(scores_fn(kt)):
            s_ref[slot, h] = sc

    def fold(slot, kt, ms):
        return softmax_fn(kt, ms, tuple(s_ref[slot, h] for h in range(heads)))

    plain = n_tiles - 1

    @pl.when(plain > 0)
    def _():
        issue(0, 0)

    def body(p, ms):
        issue(1, 2 * p + 1)
        ms = fold(0, 2 * p, ms)
        issue(0, jnp.minimum(2 * p + 2, plain - 1))
        return fold(1, 2 * p + 1, ms)

    ms = lax.fori_loop(0, plain // 2, body, ms)
    ms = lax.cond(plain % 2 == 1, lambda ms: fold(0, plain - 1, ms), lambda ms: ms, ms)
    return softmax_fn(n_tiles - 1, ms, fix_last(scores_fn(n_tiles - 1)))


def _online_init(tq, acc_ref):
    acc_ref[...] = jnp.zeros(acc_ref.shape, F32)
    return tuple(jnp.full((1, tq), NEG, F32) for _ in range(acc_ref.shape[0]))


def _nsa_kernel(q_ref, gate_ref, kc_ref, vct_ref, ks_ref, vst_ref, kw_ref, vwt_ref,
                ovt_ref, oh_ref, o_ref, qaug_ref, acc_ref, s_ref, *, tq, tks):
    i = pl.program_id(1)
    q0 = i * tq
    ncp = kc_ref.shape[1]
    heads = range(NSA_HEADS)
    group = [h // NSA_HPG for h in heads]
    qa = _group_queries(q_ref, tq)

    c_vis = (_row_iota((ncp, tq)) * NSA_CMP_STRIDE + (NSA_CMP_LEN - 1)) <= q0 + _lane_iota((ncp, tq))
    kc = kc_ref[0]
    pc_sum = [jnp.zeros((ncp, tq), F32) for _ in range(NSA_GROUPS)]
    o_cmp = []
    scores = [_dot_nt(kc, qa[h]) for h in heads]
    for h in heads:
        s = jnp.where(c_vis, scores[h], NEG)
        p = jnp.where(c_vis, jnp.exp2(s - _key_max(s)), 0.0)
        den = _key_sum(p)
        p = p * (1.0 / jnp.where(den > 0.0, den, 1.0))
        pc_sum[group[h]] = pc_sum[group[h]] + p
        o_cmp.append(_dot(vct_ref[0, group[h]], p.astype(MXU_DTYPE))[:HEAD_DIM])

    band = tq + NSA_WINDOW
    start = pl.multiple_of(jnp.maximum(q0 - NSA_WINDOW, 0), tq)
    kw = kw_ref[0, pl.ds(start, band), :]
    s_w = start + _row_iota((band, tq))
    t_w = q0 + _lane_iota((band, tq))
    w_ok = (s_w <= t_w) & (s_w > t_w - NSA_WINDOW)
    o_win = []
    scores = [_dot_nt(kw, qa[h]) for h in heads]
    for h in heads:
        s = jnp.where(w_ok, scores[h], NEG)
        p = jnp.exp2(s - _key_max(s)).astype(MXU_DTYPE)
        acc = jnp.zeros((V_ROWS, tq), F32)
        for c in range(band // tq):
            acc = acc + _dot(vwt_ref[0, group[h], start // tq + c], p[c * tq:(c + 1) * tq, :])
        o_win.append(acc[:HEAD_DIM] * (1.0 / acc[HEAD_DIM:HEAD_DIM + 1]))

    blk = _row_iota((LANES, tq))
    t_b = q0 + _lane_iota((LANES, tq))
    cur = t_b >> SEL_SHIFT
    visible = blk * NSA_SEL_LEN <= t_b
    forced = (blk == 0) | (blk == cur) | (blk == cur - 1)
    taken = -3.0e38
    work = []
    for g in range(NSA_GROUPS):
        hi = pc_sum[g].astype(MXU_DTYPE)
        lo = (pc_sum[g] - hi.astype(F32)).astype(MXU_DTYPE)
        imp = _dot(ovt_ref[...], hi) + _dot(ovt_ref[...], lo)
        work.append(jnp.where(visible, jnp.where(forced, taken, imp), NEG))
    blk_f = blk.astype(F32)
    for _ in range(NSA_SEL_N - 3):
        for g in range(NSA_GROUPS):
            m = jnp.max(work[g], axis=0, keepdims=True)
            first = jnp.min(jnp.where(work[g] == m, blk_f, float(LANES)), axis=0, keepdims=True)
            work[g] = jnp.where(blk_f == first, taken, work[g])
    sel_bias = [jnp.where(visible, jnp.where(w == taken, 0.0, NEG), NEG).T.astype(MXU_DTYPE) for w in work]

    n_tiles = (q0 + tq + tks - 1) // tks
    t_s = q0 + _lane_iota((tks, tq))
    row_s = _row_iota((tks, tq))
    for h in heads:
        qaug_ref[h] = jnp.concatenate([qa[h], sel_bias[group[h]]], axis=1)

    def sel_scores(kt):
        k_aug = jnp.concatenate([ks_ref[0, pl.ds(kt * tks, tks), :], oh_ref[pl.ds(kt * tks, tks), :]], axis=1)
        return tuple(_dot_nt(k_aug, qaug_ref[h]) for h in heads)

    def sel_softmax(kt, ms, scores):
        v_t = [vst_ref[0, g, kt] for g in range(NSA_GROUPS)]
        return tuple(_online_update(ms[h], acc_ref.at[h], scores[h], v_t[group[h]]) for h in heads)

    def sel_causal(scores):
        causal = (n_tiles - 1) * tks + row_s <= t_s
        return tuple(jnp.where(causal, s, NEG) for s in scores)

    _attend_pipelined(n_tiles, sel_scores, sel_softmax, _online_init(tq, acc_ref), sel_causal, s_ref)
    o_sel = [acc_ref[h, :HEAD_DIM, :] * (1.0 / acc_ref[h, HEAD_DIM:HEAD_DIM + 1, :]) for h in heads]

    gates_t = gate_ref[0].T
    merged = []
    for h in heads:
        tot = jnp.zeros((HEAD_DIM, tq), F32)
        for r, branch in enumerate((o_cmp, o_sel, o_win)):
            tot = tot + gates_t[3 * h + r:3 * h + r + 1, :] * branch[h]
        merged.append(tot)
    for s in range(NSA_QW // LANES):
        pair = jnp.concatenate([merged[2 * s], merged[2 * s + 1]], axis=0)
        o_ref[0, :, s * LANES:(s + 1) * LANES] = pair.T.astype(o_ref.dtype)


def _nsa_attention(nq, ng, kc, vct, ks, vst, kw, vwt, overlap_t, onehot, *, tq, tks):
    b, s, _ = nq.shape
    once = pl.Buffered(1)

    def per_batch(a):
        return pl.BlockSpec((1,) + a.shape[1:], lambda bi, i: (bi,) + (0,) * (a.ndim - 1), pipeline_mode=once)

    def const(a):
        return pl.BlockSpec(a.shape, lambda bi, i: (0,) * a.ndim, pipeline_mode=once)

    return pl.pallas_call(
        functools.partial(_nsa_kernel, tq=tq, tks=tks),
        grid=(b, s // tq),
        in_specs=[pl.BlockSpec((1, tq, NSA_QW), lambda bi, i: (bi, i, 0)),
                  pl.BlockSpec((1, tq, LANES), lambda bi, i: (bi, i, 0)),
                  per_batch(kc), per_batch(vct), per_batch(ks), per_batch(vst), per_batch(kw), per_batch(vwt),
                  const(overlap_t), const(onehot)],
        out_specs=pl.BlockSpec((1, tq, NSA_QW), lambda bi, i: (bi, i, 0)),
        out_shape=jax.ShapeDtypeStruct((b, s, NSA_QW), MXU_DTYPE),
        scratch_shapes=[pltpu.VMEM((NSA_HEADS, tq, 2 * LANES), MXU_DTYPE),
                        pltpu.VMEM((NSA_HEADS, V_ROWS, tq), F32),
                        pltpu.VMEM((2, NSA_HEADS, tks, tq), F32)],
        compiler_params=_params(2),
        name="nsa_attn",
    )(nq, ng, kc, vct, ks, vst, kw, vwt, overlap_t, onehot)


def _dsa_kernel(iq_ref, iwt_ref, ik_ref, q_ref, kv_ref, kvt_ref, o_ref, khi_ref, klo_ref, bias_ref, qa_ref,
                acc_ref, s_ref, *, tq, tk, top_k):
    i = pl.program_id(1)
    q0 = i * tq
    n_tiles = (q0 + tq + tk - 1) // tk
    lane = _lane_iota((tq, LANES))
    t_q = q0 + _lane_iota((tk, tq))
    row_k = _row_iota((tk, tq))

    iqa = []
    for s in range(IDX_QW // LANES):
        qs = iq_ref[0, :, s * LANES:(s + 1) * LANES].astype(F32)
        for p in range(LANES // DSA_IDX_DIM):
            mine = jnp.where((lane >> IDX_SHIFT) == p, qs, 0.0)
            if p:
                mine = pltpu.roll(mine, LANES - p * DSA_IDX_DIM, 1)
            iqa.append(mine.astype(MXU_DTYPE))
    w_h = [iwt_ref[0, h:h + 1, :] for h in range(DSA_IDX_HEADS)]

    def score_tiles(tiles, last_is_diagonal):
        logits = []
        for kt in tiles:
            ik = ik_ref[0, pl.ds(kt * tk, tk), :]
            logits.append([_dot_nt(ik, iqa[h]) for h in range(DSA_IDX_HEADS)])
        for n, kt in enumerate(tiles):
            score = jnp.zeros((tk, tq), F32)
            for h in range(DSA_IDX_HEADS):
                score = score + w_h[h] * jnp.maximum(logits[n][h], 0.0)
            score = jnp.where(score == 0.0, 0.0, score)
            bits = pltpu.bitcast(score, jnp.int32)
            key = jnp.where(bits < 0, bits ^ 0x7FFFFFFF, bits)
            if last_is_diagonal and n == len(tiles) - 1:
                key = jnp.where(kt * tk + row_k <= t_q, key, INT_MIN)
            khi_ref[kt] = (key >> 16).astype(jnp.int16)
            klo_ref[kt] = ((key & 0xFFFF) + I16_MIN).astype(jnp.int16)

    def score_pair(p, _):
        score_tiles([2 * p, 2 * p + 1], False)
        return 0

    plain = n_tiles - 1
    lax.fori_loop(0, plain // 2, score_pair, 0)
    lax.cond(plain % 2 == 1,
             lambda: score_tiles([n_tiles - 2, n_tiles - 1], True),
             lambda: score_tiles([n_tiles - 1], True))
    one, zero = jnp.ones((), jnp.bfloat16), jnp.zeros((), jnp.bfloat16)

    def count(flags):
        def body(kt, acc):
            return acc + _fold_rows(flags(kt), jnp.add, rows=32).astype(F32)
        acc = lax.fori_loop(0, n_tiles, body, jnp.zeros((32, tq), F32))
        return jnp.sum(acc, axis=0, keepdims=True)

    uncounted = float(2 ** 24)

    def kth_largest(ref, may_stop_early):
        def body(b, carry):
            ans_u, reached = carry
            cand_u = ans_u | lax.shift_left(jnp.int32(1), 15 - b)
            cand = (cand_u + I16_MIN).astype(jnp.int16)
            cnt = count(lambda kt: jnp.where(ref[kt] >= cand, one, zero))
            ok = cnt >= top_k
            return jnp.where(ok, cand_u, ans_u), jnp.where(ok, cnt, reached)

        carry = (jnp.zeros((1, tq), jnp.int32), jnp.full((1, tq), uncounted, F32))
        if may_stop_early:
            carry = lax.fori_loop(0, 8, body, carry)
            for first, last in ((8, 12), (12, 16)):
                unsettled = jnp.max(jnp.abs(carry[1] - top_k)) > 0.0
                carry = lax.cond(unsettled, lambda c, f=first, l=last: lax.fori_loop(f, l, body, c),
                                 lambda c: c, carry)
        else:
            carry = lax.fori_loop(0, 16, body, carry)
        return carry[0] + I16_MIN, carry[1]

    t_hi32, reached_hi = kth_largest(khi_ref, False)
    t_hi = t_hi32.astype(jnp.int16)

    def pin_body(kt, _):
        hi = khi_ref[kt]
        klo_ref[kt] = jnp.where(hi == t_hi, klo_ref[kt],
                                jnp.where(hi > t_hi, jnp.int16(I16_MAX), jnp.int16(I16_MIN)))
        return 0

    lax.fori_loop(0, n_tiles, pin_body, 0)
    t_lo32, reached_lo = kth_largest(klo_ref, True)
    reached = jnp.where(reached_lo < uncounted, reached_lo, reached_hi)
    t_lo = jnp.where(t_hi32 == I16_MIN, jnp.maximum(t_lo32, I16_MIN + 1), t_lo32).astype(jnp.int16)

    def at_least(kt, yes, no):
        hi = khi_ref[kt]
        return jnp.where(hi > t_hi, yes, jnp.where(hi == t_hi, jnp.where(klo_ref[kt] >= t_lo, yes, no), no))

    def above(kt):
        return jnp.where(khi_ref[kt] > t_hi, one, jnp.where(klo_ref[kt] > t_lo, one, zero))

    bias_yes, bias_no = jnp.zeros((), bias_ref.dtype), jnp.full((), NEG, bias_ref.dtype)

    def plain_bias():
        def body(kt, _):
            bias_ref[kt] = at_least(kt, bias_yes, bias_no)
            return 0
        lax.fori_loop(0, n_tiles, body, 0)

    def ranked_bias():
        need = top_k - count(above)
        lower = jnp.where(_lane_iota((tk, tk)) < _row_iota((tk, tk)), 1.0, 0.0).astype(MXU_DTYPE)

        def body(kt, seen):
            tie = jnp.where(khi_ref[kt] == t_hi, jnp.where(klo_ref[kt] == t_lo, one, zero), zero)
            rank = _dot(lower, tie.astype(MXU_DTYPE)) + seen
            tie = tie.astype(F32)
            take = jnp.where(above(kt).astype(F32) > 0.0, 1.0, jnp.where(rank < need, tie, 0.0))
            bias_ref[kt] = ((take - 1.0) * (-NEG)).astype(bias_ref.dtype)
            return seen + _key_sum(tie)

        lax.fori_loop(0, n_tiles, body, jnp.zeros((1, tq), F32))

    lax.cond(jnp.max(reached) > top_k, ranked_bias, plain_bias)

    for s in range(DSA_QW // LANES):
        qs = q_ref[0, :, s * LANES:(s + 1) * LANES].astype(F32)
        lo = jnp.where(lane < HEAD_DIM, qs, 0.0)
        hi = _swap_halves(jnp.where(lane < HEAD_DIM, 0.0, qs))
        qa_ref[2 * s] = lo.astype(MXU_DTYPE)
        qa_ref[2 * s + 1] = hi.astype(MXU_DTYPE)

    def att_scores(kt):
        kv = kv_ref[0, pl.ds(kt * tk, tk), :]
        bias = bias_ref[kt].astype(F32)
        return tuple(_dot_nt(kv, qa_ref[h]) + bias for h in range(DSA_HEADS))

    def att_softmax(kt, ms, scores):
        kv_t = kvt_ref[0, kt]
        return tuple(_online_update(ms[h], acc_ref.at[h], scores[h], kv_t) for h in range(DSA_HEADS))

    _attend_pipelined(n_tiles, att_scores, att_softmax, _online_init(tq, acc_ref), lambda scores: scores, s_ref)
    outs = [acc_ref[h, :HEAD_DIM, :] * (1.0 / acc_ref[h, HEAD_DIM:HEAD_DIM + 1, :]) for h in range(DSA_HEADS)]

    for s in range(DSA_QW // LANES):
        pair = jnp.concatenate([outs[2 * s], outs[2 * s + 1]], axis=0)
        o_ref[0, :, s * LANES:(s + 1) * LANES] = pair.T.astype(o_ref.dtype)


def _dsa_attention(iq, iwt, ik, dq, dkv, dkvt, *, tq, tk, top_k):
    b, s, _ = dq.shape
    tile = lambda width: pl.BlockSpec((1, tq, width), lambda bi, i: (bi, i, 0))
    full = pl.BlockSpec((1, s, LANES), lambda bi, i: (bi, 0, 0))
    return pl.pallas_call(
        functools.partial(_dsa_kernel, tq=tq, tk=tk, top_k=top_k),
        grid=(b, s // tq),
        in_specs=[tile(IDX_QW), pl.BlockSpec((1, DSA_IDX_HEADS, tq), lambda bi, i: (bi, 0, i)), full,
                  tile(DSA_QW), full, pl.BlockSpec((1,) + dkvt.shape[1:], lambda bi, i: (bi, 0, 0, 0))],
        out_specs=tile(DSA_QW),
        out_shape=jax.ShapeDtypeStruct((b, s, DSA_QW), MXU_DTYPE),
        scratch_shapes=[pltpu.VMEM((s // tk, tk, tq), jnp.int16), pltpu.VMEM((s // tk, tk, tq), jnp.int16),
                        pltpu.VMEM((s // tk, tk, tq), MXU_DTYPE),
                        pltpu.VMEM((DSA_HEADS, tq, LANES), MXU_DTYPE), pltpu.VMEM((DSA_HEADS, V_ROWS, tq), F32),
                        pltpu.VMEM((2, DSA_HEADS, tk, tq), F32)],
        compiler_params=_params(2),
        name="dsa_attn",
    )(iq, iwt, ik, dq, dkv, dkvt)


def _merge_kernel(x_ref, g_ref, oa_ref, ob_ref, oc_ref, wg_ref, wa_ref, wb_ref, wc_ref, wo_ref, o_ref):
    x = x_ref[...]
    d = x.shape[1]
    ms = jnp.mean(x * x, axis=-1, keepdims=True)
    h = (x * lax.rsqrt(ms + NORM_EPS) * g_ref[...]).astype(MXU_DTYPE)
    merged = jnp.zeros(x.shape, F32)
    for r, (o_r, w_r) in enumerate(((oa_ref, wa_ref), (ob_ref, wb_ref), (oc_ref, wc_ref))):
        gate = jax.nn.sigmoid(_dot(h, wg_ref[:, r * d:(r + 1) * d]))
        merged = merged + gate * _dot(o_r[...], w_r[...])
    o_ref[...] = x + _dot(merged.astype(MXU_DTYPE), wo_ref[...])


def _merge(x2, g, oa, ob, oc, w_gate, wa, wb, wc, wo, *, tm):
    n, d = x2.shape
    row = lambda i: (i, 0)
    const = lambda i: (0, 0)
    return pl.pallas_call(
        _merge_kernel,
        grid=(n // tm,),
        in_specs=[pl.BlockSpec((tm, d), row), pl.BlockSpec((1, d), const),
                  pl.BlockSpec((tm, SB_W), row), pl.BlockSpec((tm, NSA_QW), row),
                  pl.BlockSpec((tm, DSA_QW), row),
                  pl.BlockSpec(w_gate.shape, const), pl.BlockSpec(wa.shape, const),
                  pl.BlockSpec(wb.shape, const), pl.BlockSpec(wc.shape, const),
                  pl.BlockSpec(wo.shape, const)],
        out_specs=pl.BlockSpec((tm, d), row),
        out_shape=jax.ShapeDtypeStruct((n, d), F32),
        compiler_params=_params(1),
        name="merge",
    )(x2, g, oa, ob, oc, w_gate, wa, wb, wc, wo)


def _rope_tables(pos, rot_dim, head_dim):
    half = rot_dim // 2
    inv = ROPE_THETA ** (-jnp.arange(0, rot_dim, 2, dtype=F32) / rot_dim)
    ang = pos.astype(F32)[:, None] * inv[None, :]
    cos, sin = jnp.cos(ang), jnp.sin(ang)
    n = pos.shape[0]
    rest = head_dim - rot_dim
    zero_h = jnp.zeros((n, half), F32)
    c = jnp.concatenate([cos, cos, jnp.ones((n, rest), F32)], axis=-1)
    sm = jnp.concatenate([-sin, zero_h, jnp.zeros((n, rest), F32)], axis=-1)
    sp = jnp.concatenate([zero_h, sin, jnp.zeros((n, rest), F32)], axis=-1)
    reps = LANES // head_dim
    return tuple(jnp.tile(t, (1, reps)) for t in (c, sm, sp))


def _pad_cols(w, width):
    return jnp.pad(w, ((0, 0), (0, width - w.shape[1])))


def _relayout_w_in(w_in):
    offs = np.cumsum((SB_W, SB_W, SB_W, NSA_QW) + (NSA_KVW,) * 6
                     + (3 * NSA_HEADS, DSA_QW, HEAD_DIM, HEAD_DIM, IDX_QW, DSA_IDX_DIM, DSA_IDX_HEADS))
    o_gate, o_dq, o_dk, o_dv, o_iq, o_ik, o_iw, o_end = offs[9:17].tolist()
    parts = [w_in[:, :o_gate], _pad_cols(w_in[:, o_gate:o_dq], LANES), w_in[:, o_dq:o_dk],
             w_in[:, o_dk:o_iq], w_in[:, o_iq:o_ik], _pad_cols(w_in[:, o_ik:o_iw], LANES),
             _pad_cols(w_in[:, o_iw:o_end], LANES)]
    w = jnp.concatenate(parts, axis=1)
    assert w.shape[1] == _P_END
    return w.astype(MXU_DTYPE)


def _compress_weights(pe, w1, w2):
    r = NSA_CMP_LEN // NSA_CMP_STRIDE
    hid = NSA_CMP_HIDDEN
    w1r = w1.reshape(2, r, NSA_CMP_STRIDE, HEAD_DIM, hid)
    eye = jnp.eye(NSA_GROUPS, dtype=w1.dtype)
    w1e = jnp.einsum('kmldj,gh->kmlgdhj', w1r, eye)
    w1e = w1e.reshape(2, r, NSA_CMP_STRIDE * NSA_KVW, NSA_GROUPS * hid)
    w2e = jnp.einsum('kjd,gh->kgjhd', w2, eye).reshape(2, NSA_GROUPS * hid, NSA_KVW)
    pe_e = jnp.broadcast_to(pe.reshape(2, r, NSA_CMP_STRIDE, 1, HEAD_DIM),
                            (2, r, NSA_CMP_STRIDE, NSA_GROUPS, HEAD_DIM))
    return pe_e.reshape(2, r, NSA_CMP_STRIDE * NSA_KVW), w1e.astype(MXU_DTYPE), w2e.astype(MXU_DTYPE)


def _overlap_matrix_t(ncp, seq):
    c = np.arange(ncp)[None, :] * NSA_CMP_STRIDE
    j = np.arange(LANES)[:, None] * NSA_SEL_LEN
    ov = (c < j + NSA_SEL_LEN) & (c + NSA_CMP_LEN - 1 >= j) & (j < seq)
    ov &= (np.arange(ncp)[None, :] < ncp - (NSA_CMP_LEN // NSA_CMP_STRIDE - 1))
    return jnp.asarray(ov, dtype=MXU_DTYPE)


def _block_onehot(seq):
    return jnp.asarray(np.arange(seq)[:, None] // NSA_SEL_LEN == np.arange(LANES)[None, :], dtype=MXU_DTYPE)


def _key_tiles_t(v, tile):
    b, s, w = v.shape
    return jnp.swapaxes(v.reshape(b, s // tile, tile, w), 2, 3)


def _group_values_t(v, tile):
    b, s, w = v.shape
    groups = w // HEAD_DIM
    vt = _key_tiles_t(v, tile).reshape(b, s // tile, groups, HEAD_DIM, tile)
    vt = jnp.swapaxes(vt, 1, 2)
    pad = jnp.zeros((b, groups, s // tile, V_ROWS - HEAD_DIM, tile), v.dtype).at[:, :, :, 0, :].set(1)
    return jnp.concatenate([vt, pad], axis=3)


def kernel(x, ffn1_norm, ffn1_w_gu, ffn1_w_down, mix_norm, w_in, w_gate, nsa_q_norm, nsa_k_norm,
           nsa_cmp_pe, nsa_cmp_w1, nsa_cmp_w2, dsa_q_norm, dsa_k_norm, w_br_a, w_br_b, w_br_c, w_out,
           ffn2_norm, ffn2_w_gu, ffn2_w_down):
    b, s, d = x.shape
    depth = w_in.shape[0]
    n = b * s
    ncp = s // NSA_CMP_STRIDE
    assert s // NSA_SEL_LEN <= LANES and s % 1024 == 0
    top_k = min(DSA_TOPK, s // 4)
    ffn_tf = ffn1_w_down.shape[1]
    cast = lambda w: w.astype(MXU_DTYPE)
    tile2 = lambda v, reps: jnp.tile(v, reps)[None, :]

    pos = jnp.arange(s)
    rope_main = _rope_tables(pos, ROT_DIM, HEAD_DIM)
    rope_idx = _rope_tables(pos, DSA_IDX_ROT, DSA_IDX_DIM)
    rope_cmp = _rope_tables(jnp.arange(ncp) * NSA_CMP_STRIDE + NSA_CMP_LEN - 1, ROT_DIM, HEAD_DIM)
    overlap_t = _overlap_matrix_t(ncp, s)
    onehot = _block_onehot(s)

    x2 = x.reshape(n, d)
    for l in range(depth):
        x2 = _ffn(x2, ffn1_norm[l][None, :], cast(ffn1_w_gu[l]), cast(ffn1_w_down[l]), tm=FFN_TM, tf=ffn_tf)

        (sb, nq, nkc, nvc, nkv, ng, dq, dkv, iq, ik, iw) = _proj(
            x2, mix_norm[l][None, :], _relayout_w_in(w_in[l]),
            tile2(nsa_q_norm[l], NSA_HEADS),
            jnp.stack([jnp.tile(nsa_k_norm[l, 1], NSA_GROUPS), jnp.tile(nsa_k_norm[l, 2], NSA_GROUPS)]),
            tile2(dsa_q_norm[l], DSA_HEADS), tile2(dsa_k_norm[l], 2),
            rope_main, rope_idx, tm=ROW_TM, seq=s)
        r3 = lambda t: t.reshape(b, s, t.shape[-1])

        pe_e, w1e, w2e = _compress_weights(nsa_cmp_pe[l], nsa_cmp_w1[l], nsa_cmp_w2[l])
        kc, vc = _compress(nkc.reshape(b, ncp, NSA_CMP_STRIDE * NSA_KVW),
                           nvc.reshape(b, ncp, NSA_CMP_STRIDE * NSA_KVW),
                           pe_e, w1e, w2e, tile2(nsa_k_norm[l, 0], NSA_GROUPS), rope_cmp)

        o_a = _sb_attention(r3(sb), tq=SB_TQ)

        nkv = r3(nkv)
        ks, vs, kw, vw = (nkv[:, :, c * LANES:(c + 1) * LANES] for c in range(4))
        o_b = _nsa_attention(r3(nq), r3(ng), kc, _group_values_t(vc, ncp)[:, :, 0], ks,
                             _group_values_t(vs, TKS), kw, _group_values_t(vw, TQ_NSA), overlap_t, onehot,
                             tq=TQ_NSA, tks=TKS)

        dkv = r3(dkv)
        iwt = jnp.swapaxes(r3(iw)[:, :, :DSA_IDX_HEADS], 1, 2)
        dkvt = _group_values_t(dkv[:, :, HEAD_DIM:], TKS)[:, 0]
        o_c = _dsa_attention(r3(iq), iwt, r3(ik), r3(dq), dkv, dkvt, tq=TQ, tk=TKS, top_k=top_k)

        x2 = _merge(x2, mix_norm[l][None, :], o_a.reshape(n, SB_W), o_b.reshape(n, NSA_QW),
                    o_c.reshape(n, DSA_QW), cast(w_gate[l]), cast(w_br_a[l]), cast(w_br_b[l]),
                    cast(w_br_c[l]), cast(w_out[l]), tm=ROW_TM)

        x2 = _ffn(x2, ffn2_norm[l][None, :], cast(ffn2_w_gu[l]), cast(ffn2_w_down[l]), tm=FFN_TM, tf=ffn_tf)
    return x2.reshape(b, s, d)
```

```python
import functools

import numpy as np
import jax
import jax.numpy as jnp
from jax import lax
from jax.experimental import pallas as pl
from jax.experimental.pallas import tpu as pltpu

HEAD_DIM = 64
ROT_DIM = HEAD_DIM // 4
ROPE_THETA = 500000.0
NORM_EPS = 1e-6

SB_HEADS = 4
NSA_HEADS = 8
NSA_GROUPS = 2
NSA_HPG = NSA_HEADS // NSA_GROUPS
NSA_CMP_LEN = 32
NSA_CMP_STRIDE = 16
NSA_CMP_HIDDEN = 2 * HEAD_DIM
NSA_SEL_LEN = 64
NSA_SEL_N = 16
NSA_WINDOW = 512
NSA_FORCED_SCORE = 1.0e4
DSA_HEADS = 4
DSA_IDX_HEADS = 8
DSA_IDX_DIM = 32
DSA_IDX_ROT = DSA_IDX_DIM // 4
DSA_TOPK = 256

SB_W = SB_HEADS * HEAD_DIM
NSA_QW = NSA_HEADS * HEAD_DIM
NSA_KVW = NSA_GROUPS * HEAD_DIM
DSA_QW = DSA_HEADS * HEAD_DIM
IDX_QW = DSA_IDX_HEADS * DSA_IDX_DIM

LANES = 128
VMEM_LIMIT = 56 * 1024 * 1024
MXU_DTYPE = jnp.bfloat16
NEG = -1.0e30
SB_SKIP = -120.0
INT_MIN = -2 ** 31
LOG2E = 1.4426950408889634

F32 = jnp.float32

TQ = 256
TQ_NSA = 256
TKS = 512
SB_TQ = 256
ROW_TM = 256
V_ROWS = LANES
HEAD_SHIFT = HEAD_DIM.bit_length() - 1
IDX_SHIFT = DSA_IDX_DIM.bit_length() - 1
SEL_SHIFT = NSA_SEL_LEN.bit_length() - 1
I16_MIN, I16_MAX = -2 ** 15, 2 ** 15 - 1
FFN_CHUNK = 512
FFN_TM = 512
assert TKS % TQ == 0 and TKS % TQ_NSA == 0 and NSA_WINDOW % TQ_NSA == 0


def _dot(a, b):
    return jnp.dot(a, b, preferred_element_type=F32)


def _dot_nt(a, b):
    return lax.dot_general(a, b, (((1,), (1,)), ((), ())), preferred_element_type=F32)


def _params(n_axes):
    return pltpu.CompilerParams(dimension_semantics=("arbitrary",) * n_axes,
                                vmem_limit_bytes=VMEM_LIMIT)


def _lane_iota(shape):
    return lax.broadcasted_iota(jnp.int32, shape, len(shape) - 1)


def _row_iota(shape):
    return lax.broadcasted_iota(jnp.int32, shape, len(shape) - 2)


def _swap_halves(x):
    return pltpu.roll(x, LANES // 2, 1)


def _head_rms(ys):
    lo = _lane_iota(ys.shape) < HEAD_DIM
    sq = ys * ys
    s_lo = jnp.sum(jnp.where(lo, sq, 0.0), axis=-1, keepdims=True)
    s_hi = jnp.sum(jnp.where(lo, 0.0, sq), axis=-1, keepdims=True)
    ms = jnp.where(lo, s_lo, s_hi) * (1.0 / HEAD_DIM)
    return ys * lax.rsqrt(ms + NORM_EPS)


def _rope(ys, c, sm, sp, half):
    return ys * c + pltpu.roll(ys, LANES - half, 1) * sm + pltpu.roll(ys, half, 1) * sp


def _ffn_kernel(x_ref, g_ref, wg_ref, wu_ref, wd_ref, o_ref, h_ref, acc_ref):
    f = pl.program_id(1)

    @pl.when(f == 0)
    def _():
        x = x_ref[...]
        ms = jnp.mean(x * x, axis=-1, keepdims=True)
        h_ref[...] = (x * lax.rsqrt(ms + NORM_EPS) * g_ref[...]).astype(h_ref.dtype)
        acc_ref[...] = jnp.zeros_like(acc_ref)

    h = h_ref[...]
    tf = wd_ref.shape[0]
    bounds = [(c, min(c + FFN_CHUNK, tf)) for c in range(0, tf, FFN_CHUNK)]

    def gate_up(c0, c1):
        return _dot(h, wg_ref[:, c0:c1]), _dot(h, wu_ref[:, c0:c1])

    ahead = gate_up(*bounds[0])
    for idx, (c0, c1) in enumerate(bounds):
        gate, up = ahead
        if idx + 1 < len(bounds):
            ahead = gate_up(*bounds[idx + 1])
        act = (gate * jax.nn.sigmoid(gate) * up).astype(h_ref.dtype)
        acc_ref[...] += _dot(act, wd_ref[c0:c1, :])

    @pl.when(f == pl.num_programs(1) - 1)
    def _():
        o_ref[...] = x_ref[...] + 0.5 * acc_ref[...]


def _ffn(x2, g, w_gu, w_down, *, tm, tf):
    n, d = x2.shape
    d_ff = w_down.shape[0]
    nf = d_ff // tf
    return pl.pallas_call(
        _ffn_kernel,
        grid=(n // tm, nf),
        in_specs=[
            pl.BlockSpec((tm, d), lambda i, f: (i, 0)),
            pl.BlockSpec((1, d), lambda i, f: (0, 0)),
            pl.BlockSpec((d, tf), lambda i, f: (0, f)),
            pl.BlockSpec((d, tf), lambda i, f: (0, f + nf)),
            pl.BlockSpec((tf, d), lambda i, f: (f, 0)),
        ],
        out_specs=pl.BlockSpec((tm, d), lambda i, f: (i, 0)),
        out_shape=jax.ShapeDtypeStruct((n, d), F32),
        scratch_shapes=[pltpu.VMEM((tm, d), MXU_DTYPE), pltpu.VMEM((tm, d), F32)],
        compiler_params=_params(2),
        name="ffn",
    )(x2, g, w_gu, w_gu, w_down)


_P_SB = 0
_P_NQ = _P_SB + 3 * SB_W
_P_NCV = _P_NQ + NSA_QW
_P_NKV = _P_NCV + 2 * NSA_KVW
_P_NG = _P_NKV + 4 * NSA_KVW
_P_DQ = _P_NG + LANES
_P_DKV = _P_DQ + DSA_QW
_P_IQ = _P_DKV + LANES
_P_IK = _P_IQ + IDX_QW
_P_IW = _P_IK + LANES
_P_END = _P_IW + LANES


def _proj_kernel(x_ref, g_ref, w_ref, nq_g_ref, nk_g_ref, dq_g_ref, dk_g_ref,
                 rc_ref, rm_ref, rp_ref, ic_ref, im_ref, ip_ref,
                 sb_ref, nq_ref, nkc_ref, nvc_ref, nkv_ref, ng_ref, dq_ref, dkv_ref,
                 iq_ref, ik_ref, iw_ref):
    x = x_ref[...]
    ms = jnp.mean(x * x, axis=-1, keepdims=True)
    h = (x * lax.rsqrt(ms + NORM_EPS) * g_ref[...]).astype(MXU_DTYPE)
    rc, rm, rp = rc_ref[...], rm_ref[...], rp_ref[...]
    ic, im, ip = ic_ref[...], im_ref[...], ip_ref[...]
    half, ihalf = ROT_DIM // 2, DSA_IDX_ROT // 2
    scale = HEAD_DIM ** -0.5
    scale2 = scale * LOG2E

    def cols(a, width):
        return _dot(h, w_ref[:, a:a + width])

    def slab(y, s):
        return y[:, s * LANES:(s + 1) * LANES]

    y = cols(_P_SB, 3 * SB_W)
    sb_ref[:, 0:SB_W] = (y[:, 0:SB_W] * scale).astype(sb_ref.dtype)
    sb_ref[:, SB_W:3 * SB_W] = y[:, SB_W:3 * SB_W].astype(sb_ref.dtype)

    y = cols(_P_NQ, NSA_QW)
    for s in range(NSA_QW // LANES):
        ys = _head_rms(slab(y, s)) * slab(nq_g_ref[...], s)
        nq_ref[:, s * LANES:(s + 1) * LANES] = (_rope(ys, rc, rm, rp, half) * scale2).astype(nq_ref.dtype)

    y = cols(_P_NCV, 2 * NSA_KVW)
    nkc_ref[...] = slab(y, 0)
    nvc_ref[...] = slab(y, 1)

    y = cols(_P_NKV, 4 * NSA_KVW)
    for s, gi in ((0, 0), (2, 1)):
        ys = _head_rms(slab(y, s)) * nk_g_ref[gi:gi + 1, :]
        nkv_ref[:, s * LANES:(s + 1) * LANES] = _rope(ys, rc, rm, rp, half).astype(nkv_ref.dtype)
    for s in (1, 3):
        nkv_ref[:, s * LANES:(s + 1) * LANES] = slab(y, s).astype(nkv_ref.dtype)

    ng_ref[...] = jax.nn.sigmoid(cols(_P_NG, LANES))

    y = cols(_P_DQ, DSA_QW)
    for s in range(DSA_QW // LANES):
        ys = _head_rms(slab(y, s)) * slab(dq_g_ref[...], s)
        dq_ref[:, s * LANES:(s + 1) * LANES] = (_rope(ys, rc, rm, rp, half) * scale2).astype(dq_ref.dtype)

    y = cols(_P_DKV, LANES)
    yk = _rope(_head_rms(y) * dk_g_ref[...], rc, rm, rp, half)
    dkv_ref[...] = jnp.where(_lane_iota(y.shape) < HEAD_DIM, yk, y).astype(dkv_ref.dtype)

    y = cols(_P_IQ, IDX_QW)
    for s in range(IDX_QW // LANES):
        iq_ref[:, s * LANES:(s + 1) * LANES] = _rope(slab(y, s), ic, im, ip, ihalf).astype(iq_ref.dtype)
    ik_ref[...] = _rope(cols(_P_IK, LANES), ic, im, ip, ihalf).astype(ik_ref.dtype)
    iw_ref[...] = cols(_P_IW, LANES) * (DSA_IDX_HEADS ** -0.5) * (DSA_IDX_DIM ** -0.5)


def _proj(x2, g, w, nq_g, nk_g, dq_g, dk_g, rope_main, rope_idx, *, tm, seq):
    n, d = x2.shape
    nt = seq // tm
    row = lambda i: (i, 0)
    const = lambda i: (0, 0)
    pos = lambda i: (i % nt, 0)
    widths = [(3 * SB_W, MXU_DTYPE), (NSA_QW, MXU_DTYPE), (NSA_KVW, F32), (NSA_KVW, F32),
              (4 * NSA_KVW, MXU_DTYPE), (LANES, F32), (DSA_QW, MXU_DTYPE), (LANES, MXU_DTYPE),
              (IDX_QW, MXU_DTYPE), (LANES, MXU_DTYPE), (LANES, F32)]
    return pl.pallas_call(
        _proj_kernel,
        grid=(n // tm,),
        in_specs=[pl.BlockSpec((tm, d), row), pl.BlockSpec((1, d), const),
                  pl.BlockSpec((d, _P_END), const),
                  pl.BlockSpec((1, NSA_QW), const), pl.BlockSpec((2, LANES), const),
                  pl.BlockSpec((1, DSA_QW), const), pl.BlockSpec((1, LANES), const)]
                 + [pl.BlockSpec((tm, LANES), pos)] * 6,
        out_specs=[pl.BlockSpec((tm, wd), row) for wd, _ in widths],
        out_shape=[jax.ShapeDtypeStruct((n, wd), dt) for wd, dt in widths],
        compiler_params=_params(1),
        name="proj",
    )(x2, g, w, nq_g, nk_g, dq_g, dk_g, *rope_main, *rope_idx)


def _compress_kernel(xk_ref, xv_ref, pe_ref, w1_ref, w2_ref, g_ref, rc_ref, rm_ref, rp_ref,
                     kc_ref, vc_ref):
    ncp = xk_ref.shape[1]

    def compress(x, kv):
        a0 = _dot((x + pe_ref[kv, 0:1, :]).astype(MXU_DTYPE), w1_ref[kv, 0])
        a1 = _dot((x + pe_ref[kv, 1:2, :]).astype(MXU_DTYPE), w1_ref[kv, 1])
        pre = a0 + pltpu.roll(a1, ncp - 1, 0)
        hid = pre * jax.nn.sigmoid(pre)
        return _dot(hid.astype(MXU_DTYPE), w2_ref[kv])

    yk = _head_rms(compress(xk_ref[0], 0)) * g_ref[...]
    kc_ref[0] = _rope(yk, rc_ref[...], rm_ref[...], rp_ref[...], ROT_DIM // 2).astype(kc_ref.dtype)
    vc_ref[0] = compress(xv_ref[0], 1).astype(vc_ref.dtype)


def _compress(xk, xv, pe, w1, w2, g, rope_c):
    b, ncp, wide = xk.shape
    c3 = lambda i: (0, 0, 0)
    c4 = lambda i: (0, 0, 0, 0)
    c2 = lambda i: (0, 0)
    per_b = lambda i: (i, 0, 0)
    return pl.pallas_call(
        _compress_kernel,
        grid=(b,),
        in_specs=[pl.BlockSpec((1, ncp, wide), per_b), pl.BlockSpec((1, ncp, wide), per_b),
                  pl.BlockSpec(pe.shape, c3), pl.BlockSpec(w1.shape, c4), pl.BlockSpec(w2.shape, c3),
                  pl.BlockSpec((1, LANES), c2)] + [pl.BlockSpec((ncp, LANES), c2)] * 3,
        out_specs=[pl.BlockSpec((1, ncp, LANES), per_b)] * 2,
        out_shape=[jax.ShapeDtypeStruct((b, ncp, LANES), MXU_DTYPE)] * 2,
        compiler_params=_params(1),
        name="compress",
    )(xk, xv, pe, w1, w2, g, *rope_c)


def _sb_kernel(q_ref, k_ref, v_ref, o_ref, *, tq):
    i = pl.program_id(1)
    nslab = SB_W // LANES
    heads = [(sl, p) for sl in range(nslab) for p in range(2)]
    lane = _lane_iota((tq, LANES))
    rows = _row_iota((tq, tq))
    colsq = _lane_iota((tq, tq))
    upper = jnp.where(_row_iota((tq, tq + LANES)) > _lane_iota((tq, tq + LANES)), 1.0,
                      jnp.where(_lane_iota((tq, tq + LANES)) >= tq, 1.0, 0.0)).astype(MXU_DTYPE)
    in_half = [(lane >> HEAD_SHIFT) == p for p in range(2)]
    qh = [jnp.where(in_half[p], q_ref[0, :, sl * LANES:(sl + 1) * LANES].astype(F32), 0.0).astype(MXU_DTYPE)
          for sl, p in heads]
    reps = tq // LANES

    def tile(kt, carries, accs, diagonal):
        ks = [k_ref[0, pl.ds(kt * tq, tq), sl * LANES:(sl + 1) * LANES] for sl in range(nslab)]
        vs = [v_ref[0, pl.ds(kt * tq, tq), sl * LANES:(sl + 1) * LANES] for sl in range(nslab)]
        zs = [_dot_nt(qh[n], ks[sl]) for n, (sl, _) in enumerate(heads)]
        new_c, new_a = [], []
        for p, (sl, _) in enumerate(heads):
            z = zs[p]
            l = -(jnp.maximum(z, 0.0) + jnp.log1p(jnp.exp(-jnp.abs(z))))
            if diagonal:
                past = colsq < rows
                l = jnp.where(past, l, 0.0)
            hi = l.astype(MXU_DTYPE)
            lo = (l - hi.astype(F32)).astype(MXU_DTYPE)
            sums = _dot(hi, upper) + _dot(lo, upper)
            tail = sums[:, :tq] + jnp.concatenate([carries[p]] * reps, axis=1)
            w = jnp.exp(z + l + tail)
            if diagonal:
                w = jnp.where(past, w, 0.0)
            new_a.append(accs[p] + _dot(w.astype(MXU_DTYPE), vs[sl]))
            new_c.append(carries[p] + sums[:, tq:])
        return tuple(new_c), tuple(new_a)

    zeros = (jnp.zeros((tq, LANES), F32),) * len(heads)
    carries, accs = tile(i, zeros, zeros, True)

    def cond(st):
        kt, carries, _ = st
        worst = functools.reduce(jnp.maximum, carries)
        return (kt >= 0) & (jnp.max(worst) > SB_SKIP)

    def body(st):
        kt, carries, accs = st
        carries, accs = tile(kt, carries, accs, False)
        return kt - 1, carries, accs

    _, _, accs = lax.while_loop(cond, body, (i - 1, carries, accs))
    for sl in range(nslab):
        o_ref[0, :, sl * LANES:(sl + 1) * LANES] = jnp.where(in_half[0], accs[2 * sl],
                                                             accs[2 * sl + 1]).astype(o_ref.dtype)


def _sb_attention(sb, *, tq):
    b, s, _ = sb.shape
    return pl.pallas_call(
        functools.partial(_sb_kernel, tq=tq),
        grid=(b, s // tq),
        in_specs=[pl.BlockSpec((1, tq, SB_W), lambda bi, i: (bi, i, 0)),
                  pl.BlockSpec((1, s, SB_W), lambda bi, i: (bi, 0, 1)),
                  pl.BlockSpec((1, s, SB_W), lambda bi, i: (bi, 0, 2))],
        out_specs=pl.BlockSpec((1, tq, SB_W), lambda bi, i: (bi, i, 0)),
        out_shape=jax.ShapeDtypeStruct((b, s, SB_W), MXU_DTYPE),
        compiler_params=_params(2),
        name="sb_attn",
    )(sb, sb, sb)


def _group_queries(q_ref, tq):
    lane = _lane_iota((tq, LANES))
    qa = []
    for h in range(NSA_HEADS):
        slab, parity, g = h // 2, h % 2, h // NSA_HPG
        qs = q_ref[0, :, slab * LANES:(slab + 1) * LANES].astype(F32)
        mine = jnp.where((lane >> HEAD_SHIFT) == parity, qs, 0.0)
        if parity != g:
            mine = _swap_halves(mine)
        qa.append(mine.astype(MXU_DTYPE))
    return qa


def _fold_rows(x, op, rows=64):
    while x.shape[0] > rows and x.shape[0] % 16 == 0:
        half = x.shape[0] // 2
        x = op(x[:half], x[half:])
    return x


def _key_max(x):
    return jnp.max(_fold_rows(x, jnp.maximum), axis=0, keepdims=True)


def _key_sum(x):
    return jnp.sum(_fold_rows(x, jnp.add), axis=0, keepdims=True)


def _online_update(m, acc_ref, s, v_t):
    m_new = jnp.maximum(m, _key_max(s))
    p = jnp.exp2(s - m_new)
    acc_ref[...] = jnp.exp2(m - m_new) * acc_ref[...] + _dot(v_t, p.astype(MXU_DTYPE))
    return m_new


def _attend_pipelined(n_tiles, scores_fn, softmax_fn, ms, fix_last, s_ref):
    heads = s_ref.shape[1]

    def issue(slot, kt):
        for h, sc in enumerate(scores_fn(kt)):
            s_ref[slot, h] = sc

    def fold(slot, kt, ms):
        return softmax_fn(kt, ms, tuple(s_ref[slot, h] for h in range(heads)))

    plain = n_tiles - 1

    @pl.when(plain > 0)
    def _():
        issue(0, 0)

    def body(p, ms):
        issue(1, 2 * p + 1)
        ms = fold(0, 2 * p, ms)
        issue(0, jnp.minimum(2 * p + 2, plain - 1))
        return fold(1, 2 * p + 1, ms)

    ms = lax.fori_loop(0, plain // 2, body, ms)
    ms = lax.cond(plain % 2 == 1, lambda ms: fold(0, plain - 1, ms), lambda ms: ms, ms)
    return softmax_fn(n_tiles - 1, ms, fix_last(scores_fn(n_tiles - 1)))


def _online_init(tq, acc_ref):
    acc_ref[...] = jnp.zeros(acc_ref.shape, F32)
    return tuple(jnp.full((1, tq), NEG, F32) for _ in range(acc_ref.shape[0]))


def _nsa_kernel(q_ref, gate_ref, kc_ref, vct_ref, ks_ref, vst_ref, kw_ref, vwt_ref,
                ovt_ref, oh_ref, o_ref, qaug_ref, acc_ref, s_ref, *, tq, tks):
    i = pl.program_id(1)
    q0 = i * tq
    ncp = kc_ref.shape[1]
    heads = range(NSA_HEADS)
    group = [h // NSA_HPG for h in heads]
    qa = _group_queries(q_ref, tq)

    c_vis = (_row_iota((ncp, tq)) * NSA_CMP_STRIDE + (NSA_CMP_LEN - 1)) <= q0 + _lane_iota((ncp, tq))
    kc = kc_ref[0]
    pc_sum = [jnp.zeros((ncp, tq), F32) for _ in range(NSA_GROUPS)]
    o_cmp = []
    scores = [_dot_nt(kc, qa[h]) for h in heads]
    for h in heads:
        s = jnp.where(c_vis, scores[h], NEG)
        p = jnp.where(c_vis, jnp.exp2(s - _key_max(s)), 0.0)
        den = _key_sum(p)
        p = p * (1.0 / jnp.where(den > 0.0, den, 1.0))
        pc_sum[group[h]] = pc_sum[group[h]] + p
        o_cmp.append(_dot(vct_ref[0, group[h]], p.astype(MXU_DTYPE))[:HEAD_DIM])

    band = tq + NSA_WINDOW
    start = pl.multiple_of(jnp.maximum(q0 - NSA_WINDOW, 0), tq)
    kw = kw_ref[0, pl.ds(start, band), :]
    s_w = start + _row_iota((band, tq))
    t_w = q0 + _lane_iota((band, tq))
    w_ok = (s_w <= t_w) & (s_w > t_w - NSA_WINDOW)
    o_win = []
    scores = [_dot_nt(kw, qa[h]) for h in heads]
    for h in heads:
        s = jnp.where(w_ok, scores[h], NEG)
        p = jnp.exp2(s - _key_max(s)).astype(MXU_DTYPE)
        acc = jnp.zeros((V_ROWS, tq), F32)
        for c in range(band // tq):
            acc = acc + _dot(vwt_ref[0, group[h], start // tq + c], p[c * tq:(c + 1) * tq, :])
        o_win.append(acc[:HEAD_DIM] * (1.0 / acc[HEAD_DIM:HEAD_DIM + 1]))

    blk = _row_iota((LANES, tq))
    t_b = q0 + _lane_iota((LANES, tq))
    cur = t_b >> SEL_SHIFT
    visible = blk * NSA_SEL_LEN <= t_b
    forced = (blk == 0) | (blk == cur) | (blk == cur - 1)
    taken = -3.0e38
    work = []
    for g in range(NSA_GROUPS):
        hi = pc_sum[g].astype(MXU_DTYPE)
        lo = (pc_sum[g] - hi.astype(F32)).astype(MXU_DTYPE)
        imp = _dot(ovt_ref[...], hi) + _dot(ovt_ref[...], lo)
        work.append(jnp.where(visible, jnp.where(forced, taken, imp), NEG))
    blk_f = blk.astype(F32)
    for _ in range(NSA_SEL_N - 3):
        for g in range(NSA_GROUPS):
            m = jnp.max(work[g], axis=0, keepdims=True)
            first = jnp.min(jnp.where(work[g] == m, blk_f, float(LANES)), axis=0, keepdims=True)
            work[g] = jnp.where(blk_f == first, taken, work[g])
    sel_bias = [jnp.where(visible, jnp.where(w == taken, 0.0, NEG), NEG).T.astype(MXU_DTYPE) for w in work]

    n_tiles = (q0 + tq + tks - 1) // tks
    t_s = q0 + _lane_iota((tks, tq))
    row_s = _row_iota((tks, tq))
    for h in heads:
        qaug_ref[h] = jnp.concatenate([qa[h], sel_bias[group[h]]], axis=1)

    def sel_scores(kt):
        k_aug = jnp.concatenate([ks_ref[0, pl.ds(kt * tks, tks), :], oh_ref[pl.ds(kt * tks, tks), :]], axis=1)
        return tuple(_dot_nt(k_aug, qaug_ref[h]) for h in heads)

    def sel_softmax(kt, ms, scores):
        v_t = [vst_ref[0, g, kt] for g in range(NSA_GROUPS)]
        return tuple(_online_update(ms[h], acc_ref.at[h], scores[h], v_t[group[h]]) for h in heads)

    def sel_causal(scores):
        causal = (n_tiles - 1) * tks + row_s <= t_s
        return tuple(jnp.where(causal, s, NEG) for s in scores)

    _attend_pipelined(n_tiles, sel_scores, sel_softmax, _online_init(tq, acc_ref), sel_causal, s_ref)
    o_sel = [acc_ref[h, :HEAD_DIM, :] * (1.0 / acc_ref[h, HEAD_DIM:HEAD_DIM + 1, :]) for h in heads]

    gates_t = gate_ref[0].T
    merged = []
    for h in heads:
        tot = jnp.zeros((HEAD_DIM, tq), F32)
        for r, branch in enumerate((o_cmp, o_sel, o_win)):
            tot = tot + gates_t[3 * h + r:3 * h + r + 1, :] * branch[h]
        merged.append(tot)
    for s in range(NSA_QW // LANES):
        pair = jnp.concatenate([merged[2 * s], merged[2 * s + 1]], axis=0)
        o_ref[0, :, s * LANES:(s + 1) * LANES] = pair.T.astype(o_ref.dtype)


def _nsa_attention(nq, ng, kc, vct, ks, vst, kw, vwt, overlap_t, onehot, *, tq, tks):
    b, s, _ = nq.shape
    once = pl.Buffered(1)

    def per_batch(a):
        return pl.BlockSpec((1,) + a.shape[1:], lambda bi, i: (bi,) + (0,) * (a.ndim - 1), pipeline_mode=once)

    def const(a):
        return pl.BlockSpec(a.shape, lambda bi, i: (0,) * a.ndim, pipeline_mode=once)

    return pl.pallas_call(
        functools.partial(_nsa_kernel, tq=tq, tks=tks),
        grid=(b, s // tq),
        in_specs=[pl.BlockSpec((1, tq, NSA_QW), lambda bi, i: (bi, i, 0)),
                  pl.BlockSpec((1, tq, LANES), lambda bi, i: (bi, i, 0)),
                  per_batch(kc), per_batch(vct), per_batch(ks), per_batch(vst), per_batch(kw), per_batch(vwt),
                  const(overlap_t), const(onehot)],
        out_specs=pl.BlockSpec((1, tq, NSA_QW), lambda bi, i: (bi, i, 0)),
        out_shape=jax.ShapeDtypeStruct((b, s, NSA_QW), MXU_DTYPE),
        scratch_shapes=[pltpu.VMEM((NSA_HEADS, tq, 2 * LANES), MXU_DTYPE),
                        pltpu.VMEM((NSA_HEADS, V_ROWS, tq), F32),
                        pltpu.VMEM((2, NSA_HEADS, tks, tq), F32)],
        compiler_params=_params(2),
        name="nsa_attn",
    )(nq, ng, kc, vct, ks, vst, kw, vwt, overlap_t, onehot)


def _dsa_kernel(iq_ref, iwt_ref, ik_ref, q_ref, kv_ref, kvt_ref, o_ref, khi_ref, klo_ref, bias_ref, qa_ref,
                acc_ref, s_ref, *, tq, tk, top_k):
    i = pl.program_id(1)
    q0 = i * tq
    n_tiles = (q0 + tq + tk - 1) // tk
    lane = _lane_iota((tq, LANES))
    t_q = q0 + _lane_iota((tk, tq))
    row_k = _row_iota((tk, tq))

    iqa = []
    for s in range(IDX_QW // LANES):
        qs = iq_ref[0, :, s * LANES:(s + 1) * LANES].astype(F32)
        for p in range(LANES // DSA_IDX_DIM):
            mine = jnp.where((lane >> IDX_SHIFT) == p, qs, 0.0)
            if p:
                mine = pltpu.roll(mine, LANES - p * DSA_IDX_DIM, 1)
            iqa.append(mine.astype(MXU_DTYPE))
    w_h = [iwt_ref[0, h:h + 1, :] for h in range(DSA_IDX_HEADS)]

    def score_tiles(tiles, last_is_diagonal):
        logits = []
        for kt in tiles:
            ik = ik_ref[0, pl.ds(kt * tk, tk), :]
            logits.append([_dot_nt(ik, iqa[h]) for h in range(DSA_IDX_HEADS)])
        for n, kt in enumerate(tiles):
            score = jnp.zeros((tk, tq), F32)
            for h in range(DSA_IDX_HEADS):
                score = score + w_h[h] * jnp.maximum(logits[n][h], 0.0)
            score = jnp.where(score == 0.0, 0.0, score)
            bits = pltpu.bitcast(score, jnp.int32)
            key = jnp.where(bits < 0, bits ^ 0x7FFFFFFF, bits)
            if last_is_diagonal and n == len(tiles) - 1:
                key = jnp.where(kt * tk + row_k <= t_q, key, INT_MIN)
            khi_ref[kt] = (key >> 16).astype(jnp.int16)
            klo_ref[kt] = ((key & 0xFFFF) + I16_MIN).astype(jnp.int16)

    def score_pair(p, _):
        score_tiles([2 * p, 2 * p + 1], False)
        return 0

    plain = n_tiles - 1
    lax.fori_loop(0, plain // 2, score_pair, 0)
    lax.cond(plain % 2 == 1,
             lambda: score_tiles([n_tiles - 2, n_tiles - 1], True),
             lambda: score_tiles([n_tiles - 1], True))
    one, zero = jnp.ones((), jnp.bfloat16), jnp.zeros((), jnp.bfloat16)

    def count(flags):
        def body(kt, acc):
            return acc + _fold_rows(flags(kt), jnp.add, rows=32).astype(F32)
        acc = lax.fori_loop(0, n_tiles, body, jnp.zeros((32, tq), F32))
        return jnp.sum(acc, axis=0, keepdims=True)

    uncounted = float(2 ** 24)

    def kth_largest(ref, may_stop_early):
        def body(b, carry):
            ans_u, reached = carry
            cand_u = ans_u | lax.shift_left(jnp.int32(1), 15 - b)
            cand = (cand_u + I16_MIN).astype(jnp.int16)
            cnt = count(lambda kt: jnp.where(ref[kt] >= cand, one, zero))
            ok = cnt >= top_k
            return jnp.where(ok, cand_u, ans_u), jnp.where(ok, cnt, reached)

        carry = (jnp.zeros((1, tq), jnp.int32), jnp.full((1, tq), uncounted, F32))
        if may_stop_early:
            carry = lax.fori_loop(0, 8, body, carry)
            for first, last in ((8, 10), (10, 12), (12, 14), (14, 16)):
                unsettled = jnp.max(jnp.abs(carry[1] - top_k)) > 0.0
                carry = lax.cond(unsettled, lambda c, f=first, l=last: lax.fori_loop(f, l, body, c),
                                 lambda c: c, carry)
        else:
            carry = lax.fori_loop(0, 16, body, carry)
        return carry[0] + I16_MIN, carry[1]

    t_hi32, reached_hi = kth_largest(khi_ref, False)
    t_hi = t_hi32.astype(jnp.int16)

    def pin_body(kt, _):
        hi = khi_ref[kt]
        klo_ref[kt] = jnp.where(hi == t_hi, klo_ref[kt],
                                jnp.where(hi > t_hi, jnp.int16(I16_MAX), jnp.int16(I16_MIN)))
        return 0

    lax.fori_loop(0, n_tiles, pin_body, 0)
    t_lo32, reached_lo = kth_largest(klo_ref, True)
    reached = jnp.where(reached_lo < uncounted, reached_lo, reached_hi)
    t_lo = jnp.where(t_hi32 == I16_MIN, jnp.maximum(t_lo32, I16_MIN + 1), t_lo32).astype(jnp.int16)

    def at_least(kt, yes, no):
        hi = khi_ref[kt]
        return jnp.where(hi > t_hi, yes, jnp.where(hi == t_hi, jnp.where(klo_ref[kt] >= t_lo, yes, no), no))

    def above(kt):
        return jnp.where(khi_ref[kt] > t_hi, one, jnp.where(klo_ref[kt] > t_lo, one, zero))

    bias_yes, bias_no = jnp.zeros((), bias_ref.dtype), jnp.full((), NEG, bias_ref.dtype)

    def plain_bias():
        def body(kt, _):
            bias_ref[kt] = at_least(kt, bias_yes, bias_no)
            return 0
        lax.fori_loop(0, n_tiles, body, 0)

    def ranked_bias():
        need = top_k - count(above)
        lower = jnp.where(_lane_iota((tk, tk)) < _row_iota((tk, tk)), 1.0, 0.0).astype(MXU_DTYPE)

        def body(kt, seen):
            tie = jnp.where(khi_ref[kt] == t_hi, jnp.where(klo_ref[kt] == t_lo, one, zero), zero)
            rank = _dot(lower, tie.astype(MXU_DTYPE)) + seen
            tie = tie.astype(F32)
            take = jnp.where(above(kt).astype(F32) > 0.0, 1.0, jnp.where(rank < need, tie, 0.0))
            bias_ref[kt] = ((take - 1.0) * (-NEG)).astype(bias_ref.dtype)
            return seen + _key_sum(tie)

        lax.fori_loop(0, n_tiles, body, jnp.zeros((1, tq), F32))

    lax.cond(jnp.max(reached) > top_k, ranked_bias, plain_bias)

    for s in range(DSA_QW // LANES):
        qs = q_ref[0, :, s * LANES:(s + 1) * LANES].astype(F32)
        lo = jnp.where(lane < HEAD_DIM, qs, 0.0)
        hi = _swap_halves(jnp.where(lane < HEAD_DIM, 0.0, qs))
        qa_ref[2 * s] = lo.astype(MXU_DTYPE)
        qa_ref[2 * s + 1] = hi.astype(MXU_DTYPE)

    def att_scores(kt):
        kv = kv_ref[0, pl.ds(kt * tk, tk), :]
        bias = bias_ref[kt].astype(F32)
        return tuple(_dot_nt(kv, qa_ref[h]) + bias for h in range(DSA_HEADS))

    def att_softmax(kt, ms, scores):
        kv_t = kvt_ref[0, kt]
        return tuple(_online_update(ms[h], acc_ref.at[h], scores[h], kv_t) for h in range(DSA_HEADS))

    _attend_pipelined(n_tiles, att_scores, att_softmax, _online_init(tq, acc_ref), lambda scores: scores, s_ref)
    outs = [acc_ref[h, :HEAD_DIM, :] * (1.0 / acc_ref[h, HEAD_DIM:HEAD_DIM + 1, :]) for h in range(DSA_HEADS)]

    for s in range(DSA_QW // LANES):
        pair = jnp.concatenate([outs[2 * s], outs[2 * s + 1]], axis=0)
        o_ref[0, :, s * LANES:(s + 1) * LANES] = pair.T.astype(o_ref.dtype)


def _dsa_attention(iq, iwt, ik, dq, dkv, dkvt, *, tq, tk, top_k):
    b, s, _ = dq.shape
    tile = lambda width: pl.BlockSpec((1, tq, width), lambda bi, i: (bi, i, 0))
    full = pl.BlockSpec((1, s, LANES), lambda bi, i: (bi, 0, 0))
    return pl.pallas_call(
        functools.partial(_dsa_kernel, tq=tq, tk=tk, top_k=top_k),
        grid=(b, s // tq),
        in_specs=[tile(IDX_QW), pl.BlockSpec((1, DSA_IDX_HEADS, tq), lambda bi, i: (bi, 0, i)), full,
                  tile(DSA_QW), full, pl.BlockSpec((1,) + dkvt.shape[1:], lambda bi, i: (bi, 0, 0, 0))],
        out_specs=tile(DSA_QW),
        out_shape=jax.ShapeDtypeStruct((b, s, DSA_QW), MXU_DTYPE),
        scratch_shapes=[pltpu.VMEM((s // tk, tk, tq), jnp.int16), pltpu.VMEM((s // tk, tk, tq), jnp.int16),
                        pltpu.VMEM((s // tk, tk, tq), MXU_DTYPE),
                        pltpu.VMEM((DSA_HEADS, tq, LANES), MXU_DTYPE), pltpu.VMEM((DSA_HEADS, V_ROWS, tq), F32),
                        pltpu.VMEM((2, DSA_HEADS, tk, tq), F32)],
        compiler_params=_params(2),
        name="dsa_attn",
    )(iq, iwt, ik, dq, dkv, dkvt)


def _merge_kernel(x_ref, g_ref, oa_ref, ob_ref, oc_ref, wg_ref, wa_ref, wb_ref, wc_ref, wo_ref, o_ref):
    x = x_ref[...]
    d = x.shape[1]
    ms = jnp.mean(x * x, axis=-1, keepdims=True)
    h = (x * lax.rsqrt(ms + NORM_EPS) * g_ref[...]).astype(MXU_DTYPE)
    merged = jnp.zeros(x.shape, F32)
    for r, (o_r, w_r) in enumerate(((oa_ref, wa_ref), (ob_ref, wb_ref), (oc_ref, wc_ref))):
        gate = jax.nn.sigmoid(_dot(h, wg_ref[:, r * d:(r + 1) * d]))
        merged = merged + gate * _dot(o_r[...], w_r[...])
    o_ref[...] = x + _dot(merged.astype(MXU_DTYPE), wo_ref[...])


def _merge(x2, g, oa, ob, oc, w_gate, wa, wb, wc, wo, *, tm):
    n, d = x2.shape
    row = lambda i: (i, 0)
    const = lambda i: (0, 0)
    return pl.pallas_call(
        _merge_kernel,
        grid=(n // tm,),
        in_specs=[pl.BlockSpec((tm, d), row), pl.BlockSpec((1, d), const),
                  pl.BlockSpec((tm, SB_W), row), pl.BlockSpec((tm, NSA_QW), row),
                  pl.BlockSpec((tm, DSA_QW), row),
                  pl.BlockSpec(w_gate.shape, const), pl.BlockSpec(wa.shape, const),
                  pl.BlockSpec(wb.shape, const), pl.BlockSpec(wc.shape, const),
                  pl.BlockSpec(wo.shape, const)],
        out_specs=pl.BlockSpec((tm, d), row),
        out_shape=jax.ShapeDtypeStruct((n, d), F32),
        compiler_params=_params(1),
        name="merge",
    )(x2, g, oa, ob, oc, w_gate, wa, wb, wc, wo)


def _rope_tables(pos, rot_dim, head_dim):
    half = rot_dim // 2
    inv = ROPE_THETA ** (-jnp.arange(0, rot_dim, 2, dtype=F32) / rot_dim)
    ang = pos.astype(F32)[:, None] * inv[None, :]
    cos, sin = jnp.cos(ang), jnp.sin(ang)
    n = pos.shape[0]
    rest = head_dim - rot_dim
    zero_h = jnp.zeros((n, half), F32)
    c = jnp.concatenate([cos, cos, jnp.ones((n, rest), F32)], axis=-1)
    sm = jnp.concatenate([-sin, zero_h, jnp.zeros((n, rest), F32)], axis=-1)
    sp = jnp.concatenate([zero_h, sin, jnp.zeros((n, rest), F32)], axis=-1)
    reps = LANES // head_dim
    return tuple(jnp.tile(t, (1, reps)) for t in (c, sm, sp))


def _pad_cols(w, width):
    return jnp.pad(w, ((0, 0), (0, width - w.shape[1])))


def _relayout_w_in(w_in):
    offs = np.cumsum((SB_W, SB_W, SB_W, NSA_QW) + (NSA_KVW,) * 6
                     + (3 * NSA_HEADS, DSA_QW, HEAD_DIM, HEAD_DIM, IDX_QW, DSA_IDX_DIM, DSA_IDX_HEADS))
    o_gate, o_dq, o_dk, o_dv, o_iq, o_ik, o_iw, o_end = offs[9:17].tolist()
    parts = [w_in[:, :o_gate], _pad_cols(w_in[:, o_gate:o_dq], LANES), w_in[:, o_dq:o_dk],
             w_in[:, o_dk:o_iq], w_in[:, o_iq:o_ik], _pad_cols(w_in[:, o_ik:o_iw], LANES),
             _pad_cols(w_in[:, o_iw:o_end], LANES)]
    w = jnp.concatenate(parts, axis=1)
    assert w.shape[1] == _P_END
    return w.astype(MXU_DTYPE)


def _compress_weights(pe, w1, w2):
    r = NSA_CMP_LEN // NSA_CMP_STRIDE
    hid = NSA_CMP_HIDDEN
    w1r = w1.reshape(2, r, NSA_CMP_STRIDE, HEAD_DIM, hid)
    eye = jnp.eye(NSA_GROUPS, dtype=w1.dtype)
    w1e = jnp.einsum('kmldj,gh->kmlgdhj', w1r, eye)
    w1e = w1e.reshape(2, r, NSA_CMP_STRIDE * NSA_KVW, NSA_GROUPS * hid)
    w2e = jnp.einsum('kjd,gh->kgjhd', w2, eye).reshape(2, NSA_GROUPS * hid, NSA_KVW)
    pe_e = jnp.broadcast_to(pe.reshape(2, r, NSA_CMP_STRIDE, 1, HEAD_DIM),
                            (2, r, NSA_CMP_STRIDE, NSA_GROUPS, HEAD_DIM))
    return pe_e.reshape(2, r, NSA_CMP_STRIDE * NSA_KVW), w1e.astype(MXU_DTYPE), w2e.astype(MXU_DTYPE)


def _overlap_matrix_t(ncp, seq):
    c = np.arange(ncp)[None, :] * NSA_CMP_STRIDE
    j = np.arange(LANES)[:, None] * NSA_SEL_LEN
    ov = (c < j + NSA_SEL_LEN) & (c + NSA_CMP_LEN - 1 >= j) & (j < seq)
    ov &= (np.arange(ncp)[None, :] < ncp - (NSA_CMP_LEN // NSA_CMP_STRIDE - 1))
    return jnp.asarray(ov, dtype=MXU_DTYPE)


def _block_onehot(seq):
    return jnp.asarray(np.arange(seq)[:, None] // NSA_SEL_LEN == np.arange(LANES)[None, :], dtype=MXU_DTYPE)


def _key_tiles_t(v, tile):
    b, s, w = v.shape
    return jnp.swapaxes(v.reshape(b, s // tile, tile, w), 2, 3)


def _group_values_t(v, tile):
    b, s, w = v.shape
    groups = w // HEAD_DIM
    vt = _key_tiles_t(v, tile).reshape(b, s // tile, groups, HEAD_DIM, tile)
    vt = jnp.swapaxes(vt, 1, 2)
    pad = jnp.zeros((b, groups, s // tile, V_ROWS - HEAD_DIM, tile), v.dtype).at[:, :, :, 0, :].set(1)
    return jnp.concatenate([vt, pad], axis=3)


def kernel(x, ffn1_norm, ffn1_w_gu, ffn1_w_down, mix_norm, w_in, w_gate, nsa_q_norm, nsa_k_norm,
           nsa_cmp_pe, nsa_cmp_w1, nsa_cmp_w2, dsa_q_norm, dsa_k_norm, w_br_a, w_br_b, w_br_c, w_out,
           ffn2_norm, ffn2_w_gu, ffn2_w_down):
    b, s, d = x.shape
    depth = w_in.shape[0]
    n = b * s
    ncp = s // NSA_CMP_STRIDE
    assert s // NSA_SEL_LEN <= LANES and s % 1024 == 0
    top_k = min(DSA_TOPK, s // 4)
    ffn_tf = ffn1_w_down.shape[1]
    cast = lambda w: w.astype(MXU_DTYPE)
    tile2 = lambda v, reps: jnp.tile(v, reps)[None, :]

    pos = jnp.arange(s)
    rope_main = _rope_tables(pos, ROT_DIM, HEAD_DIM)
    rope_idx = _rope_tables(pos, DSA_IDX_ROT, DSA_IDX_DIM)
    rope_cmp = _rope_tables(jnp.arange(ncp) * NSA_CMP_STRIDE + NSA_CMP_LEN - 1, ROT_DIM, HEAD_DIM)
    overlap_t = _overlap_matrix_t(ncp, s)
    onehot = _block_onehot(s)

    x2 = x.reshape(n, d)
    for l in range(depth):
        x2 = _ffn(x2, ffn1_norm[l][None, :], cast(ffn1_w_gu[l]), cast(ffn1_w_down[l]), tm=FFN_TM, tf=ffn_tf)

        (sb, nq, nkc, nvc, nkv, ng, dq, dkv, iq, ik, iw) = _proj(
            x2, mix_norm[l][None, :], _relayout_w_in(w_in[l]),
            tile2(nsa_q_norm[l], NSA_HEADS),
            jnp.stack([jnp.tile(nsa_k_norm[l, 1], NSA_GROUPS), jnp.tile(nsa_k_norm[l, 2], NSA_GROUPS)]),
            tile2(dsa_q_norm[l], DSA_HEADS), tile2(dsa_k_norm[l], 2),
            rope_main, rope_idx, tm=ROW_TM, seq=s)
        r3 = lambda t: t.reshape(b, s, t.shape[-1])

        pe_e, w1e, w2e = _compress_weights(nsa_cmp_pe[l], nsa_cmp_w1[l], nsa_cmp_w2[l])
        kc, vc = _compress(nkc.reshape(b, ncp, NSA_CMP_STRIDE * NSA_KVW),
                           nvc.reshape(b, ncp, NSA_CMP_STRIDE * NSA_KVW),
                           pe_e, w1e, w2e, tile2(nsa_k_norm[l, 0], NSA_GROUPS), rope_cmp)

        o_a = _sb_attention(r3(sb), tq=SB_TQ)

        nkv = r3(nkv)
        ks, vs, kw, vw = (nkv[:, :, c * LANES:(c + 1) * LANES] for c in range(4))
        o_b = _nsa_attention(r3(nq), r3(ng), kc, _group_values_t(vc, ncp)[:, :, 0], ks,
                             _group_values_t(vs, TKS), kw, _group_values_t(vw, TQ_NSA), overlap_t, onehot,
                             tq=TQ_NSA, tks=TKS)

        dkv = r3(dkv)
        iwt = jnp.swapaxes(r3(iw)[:, :, :DSA_IDX_HEADS], 1, 2)
        dkvt = _group_values_t(dkv[:, :, HEAD_DIM:], TKS)[:, 0]
        o_c = _dsa_attention(r3(iq), iwt, r3(ik), r3(dq), dkv, dkvt, tq=TQ, tk=TKS, top_k=top_k)

        x2 = _merge(x2, mix_norm[l][None, :], o_a.reshape(n, SB_W), o_b.reshape(n, NSA_QW),
                    o_c.reshape(n, DSA_QW), cast(w_gate[l]), cast(w_br_a[l]), cast(w_br_b[l]),
                    cast(w_br_c[l]), cast(w_out[l]), tm=ROW_TM)

        x2 = _ffn(x2, ffn2_norm[l][None, :], cast(ffn2_w_gu[l]), cast(ffn2_w_down[l]), tm=FFN_TM, tf=ffn_tf)
    return x2.reshape(b, s, d)
```

```python
import functools

import numpy as np
import jax
import jax.numpy as jnp
from jax import lax
from jax.experimental import pallas as pl
from jax.experimental.pallas import tpu as pltpu

HEAD_DIM = 64
ROT_DIM = HEAD_DIM // 4
ROPE_THETA = 500000.0
NORM_EPS = 1e-6

SB_HEADS = 4
NSA_HEADS = 8
NSA_GROUPS = 2
NSA_HPG = NSA_HEADS // NSA_GROUPS
NSA_CMP_LEN = 32
NSA_CMP_STRIDE = 16
NSA_CMP_HIDDEN = 2 * HEAD_DIM
NSA_SEL_LEN = 64
NSA_SEL_N = 16
NSA_WINDOW = 512
NSA_FORCED_SCORE = 1.0e4
DSA_HEADS = 4
DSA_IDX_HEADS = 8
DSA_IDX_DIM = 32
DSA_IDX_ROT = DSA_IDX_DIM // 4
DSA_TOPK = 256

SB_W = SB_HEADS * HEAD_DIM
NSA_QW = NSA_HEADS * HEAD_DIM
NSA_KVW = NSA_GROUPS * HEAD_DIM
DSA_QW = DSA_HEADS * HEAD_DIM
IDX_QW = DSA_IDX_HEADS * DSA_IDX_DIM

LANES = 128
VMEM_LIMIT = 56 * 1024 * 1024
MXU_DTYPE = jnp.bfloat16
NEG = -1.0e30
SB_SKIP = -120.0
INT_MIN = -2 ** 31
LOG2E = 1.4426950408889634

F32 = jnp.float32

TQ = 256
TQ_NSA = 256
TKS = 512
SB_TQ = 256
ROW_TM = 256
MERGE_TM = 512
V_ROWS = LANES
HEAD_SHIFT = HEAD_DIM.bit_length() - 1
IDX_SHIFT = DSA_IDX_DIM.bit_length() - 1
SEL_SHIFT = NSA_SEL_LEN.bit_length() - 1
I16_MIN, I16_MAX = -2 ** 15, 2 ** 15 - 1
FFN_CHUNK = 512
FFN_TM = 512
assert TKS % TQ == 0 and TKS % TQ_NSA == 0 and NSA_WINDOW % TQ_NSA == 0


def _dot(a, b):
    return jnp.dot(a, b, preferred_element_type=F32)


def _dot_nt(a, b):
    return lax.dot_general(a, b, (((1,), (1,)), ((), ())), preferred_element_type=F32)


def _params(n_axes):
    return pltpu.CompilerParams(dimension_semantics=("arbitrary",) * n_axes,
                                vmem_limit_bytes=VMEM_LIMIT)


def _lane_iota(shape):
    return lax.broadcasted_iota(jnp.int32, shape, len(shape) - 1)


def _row_iota(shape):
    return lax.broadcasted_iota(jnp.int32, shape, len(shape) - 2)


def _swap_halves(x):
    return pltpu.roll(x, LANES // 2, 1)


def _head_rms(ys):
    lo = _lane_iota(ys.shape) < HEAD_DIM
    sq = ys * ys
    s_lo = jnp.sum(jnp.where(lo, sq, 0.0), axis=-1, keepdims=True)
    s_hi = jnp.sum(jnp.where(lo, 0.0, sq), axis=-1, keepdims=True)
    ms = jnp.where(lo, s_lo, s_hi) * (1.0 / HEAD_DIM)
    return ys * lax.rsqrt(ms + NORM_EPS)


def _rope(ys, c, sm, sp, half):
    return ys * c + pltpu.roll(ys, LANES - half, 1) * sm + pltpu.roll(ys, half, 1) * sp


def _ffn_kernel(x_ref, g_ref, wg_ref, wu_ref, wd_ref, o_ref, h_ref, acc_ref):
    f = pl.program_id(1)

    @pl.when(f == 0)
    def _():
        x = x_ref[...]
        ms = jnp.mean(x * x, axis=-1, keepdims=True)
        h_ref[...] = (x * lax.rsqrt(ms + NORM_EPS) * g_ref[...]).astype(h_ref.dtype)
        acc_ref[...] = jnp.zeros_like(acc_ref)

    h = h_ref[...]
    tf = wd_ref.shape[0]
    bounds = [(c, min(c + FFN_CHUNK, tf)) for c in range(0, tf, FFN_CHUNK)]

    def gate_up(c0, c1):
        return _dot(h, wg_ref[:, c0:c1]), _dot(h, wu_ref[:, c0:c1])

    ahead = gate_up(*bounds[0])
    for idx, (c0, c1) in enumerate(bounds):
        gate, up = ahead
        if idx + 1 < len(bounds):
            ahead = gate_up(*bounds[idx + 1])
        act = (gate * jax.nn.sigmoid(gate) * up).astype(h_ref.dtype)
        acc_ref[...] += _dot(act, wd_ref[c0:c1, :])

    @pl.when(f == pl.num_programs(1) - 1)
    def _():
        o_ref[...] = x_ref[...] + 0.5 * acc_ref[...]


def _ffn(x2, g, w_gu, w_down, *, tm, tf):
    n, d = x2.shape
    d_ff = w_down.shape[0]
    nf = d_ff // tf
    return pl.pallas_call(
        _ffn_kernel,
        grid=(n // tm, nf),
        in_specs=[
            pl.BlockSpec((tm, d), lambda i, f: (i, 0)),
            pl.BlockSpec((1, d), lambda i, f: (0, 0)),
            pl.BlockSpec((d, tf), lambda i, f: (0, f)),
            pl.BlockSpec((d, tf), lambda i, f: (0, f + nf)),
            pl.BlockSpec((tf, d), lambda i, f: (f, 0)),
        ],
        out_specs=pl.BlockSpec((tm, d), lambda i, f: (i, 0)),
        out_shape=jax.ShapeDtypeStruct((n, d), F32),
        scratch_shapes=[pltpu.VMEM((tm, d), MXU_DTYPE), pltpu.VMEM((tm, d), F32)],
        compiler_params=_params(2),
        name="ffn",
    )(x2, g, w_gu, w_gu, w_down)


_P_SB = 0
_P_NQ = _P_SB + 3 * SB_W
_P_NCV = _P_NQ + NSA_QW
_P_NKV = _P_NCV + 2 * NSA_KVW
_P_NG = _P_NKV + 4 * NSA_KVW
_P_DQ = _P_NG + LANES
_P_DKV = _P_DQ + DSA_QW
_P_IQ = _P_DKV + LANES
_P_IK = _P_IQ + IDX_QW
_P_IW = _P_IK + LANES
_P_END = _P_IW + LANES


def _proj_kernel(x_ref, g_ref, w_ref, nq_g_ref, nk_g_ref, dq_g_ref, dk_g_ref,
                 rc_ref, rm_ref, rp_ref, ic_ref, im_ref, ip_ref,
                 sb_ref, nq_ref, nkc_ref, nvc_ref, nkv_ref, ng_ref, dq_ref, dkv_ref,
                 iq_ref, ik_ref, iw_ref):
    x = x_ref[...]
    ms = jnp.mean(x * x, axis=-1, keepdims=True)
    h = (x * lax.rsqrt(ms + NORM_EPS) * g_ref[...]).astype(MXU_DTYPE)
    rc, rm, rp = rc_ref[...], rm_ref[...], rp_ref[...]
    ic, im, ip = ic_ref[...], im_ref[...], ip_ref[...]
    half, ihalf = ROT_DIM // 2, DSA_IDX_ROT // 2
    scale = HEAD_DIM ** -0.5
    scale2 = scale * LOG2E

    def cols(a, width):
        return _dot(h, w_ref[:, a:a + width])

    def slab(y, s):
        return y[:, s * LANES:(s + 1) * LANES]

    y = cols(_P_SB, 3 * SB_W)
    sb_ref[:, 0:SB_W] = (y[:, 0:SB_W] * scale).astype(sb_ref.dtype)
    sb_ref[:, SB_W:3 * SB_W] = y[:, SB_W:3 * SB_W].astype(sb_ref.dtype)

    y = cols(_P_NQ, NSA_QW)
    for s in range(NSA_QW // LANES):
        ys = _head_rms(slab(y, s)) * slab(nq_g_ref[...], s)
        nq_ref[:, s * LANES:(s + 1) * LANES] = (_rope(ys, rc, rm, rp, half) * scale2).astype(nq_ref.dtype)

    y = cols(_P_NCV, 2 * NSA_KVW)
    nkc_ref[...] = slab(y, 0)
    nvc_ref[...] = slab(y, 1)

    y = cols(_P_NKV, 4 * NSA_KVW)
    for s, gi in ((0, 0), (2, 1)):
        ys = _head_rms(slab(y, s)) * nk_g_ref[gi:gi + 1, :]
        nkv_ref[:, s * LANES:(s + 1) * LANES] = _rope(ys, rc, rm, rp, half).astype(nkv_ref.dtype)
    for s in (1, 3):
        nkv_ref[:, s * LANES:(s + 1) * LANES] = slab(y, s).astype(nkv_ref.dtype)

    ng_ref[...] = jax.nn.sigmoid(cols(_P_NG, LANES))

    y = cols(_P_DQ, DSA_QW)
    for s in range(DSA_QW // LANES):
        ys = _head_rms(slab(y, s)) * slab(dq_g_ref[...], s)
        dq_ref[:, s * LANES:(s + 1) * LANES] = (_rope(ys, rc, rm, rp, half) * scale2).astype(dq_ref.dtype)

    y = cols(_P_DKV, LANES)
    yk = _rope(_head_rms(y) * dk_g_ref[...], rc, rm, rp, half)
    dkv_ref[...] = jnp.where(_lane_iota(y.shape) < HEAD_DIM, yk, y).astype(dkv_ref.dtype)

    y = cols(_P_IQ, IDX_QW)
    for s in range(IDX_QW // LANES):
        iq_ref[:, s * LANES:(s + 1) * LANES] = _rope(slab(y, s), ic, im, ip, ihalf).astype(iq_ref.dtype)
    ik_ref[...] = _rope(cols(_P_IK, LANES), ic, im, ip, ihalf).astype(ik_ref.dtype)
    iw_ref[...] = cols(_P_IW, LANES) * (DSA_IDX_HEADS ** -0.5) * (DSA_IDX_DIM ** -0.5)


def _proj(x2, g, w, nq_g, nk_g, dq_g, dk_g, rope_main, rope_idx, *, tm, seq):
    n, d = x2.shape
    nt = seq // tm
    row = lambda i: (i, 0)
    const = lambda i: (0, 0)
    pos = lambda i: (i % nt, 0)
    widths = [(3 * SB_W, MXU_DTYPE), (NSA_QW, MXU_DTYPE), (NSA_KVW, F32), (NSA_KVW, F32),
              (4 * NSA_KVW, MXU_DTYPE), (LANES, F32), (DSA_QW, MXU_DTYPE), (LANES, MXU_DTYPE),
              (IDX_QW, MXU_DTYPE), (LANES, MXU_DTYPE), (LANES, F32)]
    return pl.pallas_call(
        _proj_kernel,
        grid=(n // tm,),
        in_specs=[pl.BlockSpec((tm, d), row), pl.BlockSpec((1, d), const),
                  pl.BlockSpec((d, _P_END), const),
                  pl.BlockSpec((1, NSA_QW), const), pl.BlockSpec((2, LANES), const),
                  pl.BlockSpec((1, DSA_QW), const), pl.BlockSpec((1, LANES), const)]
                 + [pl.BlockSpec((tm, LANES), pos)] * 6,
        out_specs=[pl.BlockSpec((tm, wd), row) for wd, _ in widths],
        out_shape=[jax.ShapeDtypeStruct((n, wd), dt) for wd, dt in widths],
        compiler_params=_params(1),
        name="proj",
    )(x2, g, w, nq_g, nk_g, dq_g, dk_g, *rope_main, *rope_idx)


def _compress_kernel(xk_ref, xv_ref, pe_ref, w1_ref, w2_ref, g_ref, rc_ref, rm_ref, rp_ref,
                     kc_ref, vc_ref):
    ncp = xk_ref.shape[1]

    def compress(x, kv):
        a0 = _dot((x + pe_ref[kv, 0:1, :]).astype(MXU_DTYPE), w1_ref[kv, 0])
        a1 = _dot((x + pe_ref[kv, 1:2, :]).astype(MXU_DTYPE), w1_ref[kv, 1])
        pre = a0 + pltpu.roll(a1, ncp - 1, 0)
        hid = pre * jax.nn.sigmoid(pre)
        return _dot(hid.astype(MXU_DTYPE), w2_ref[kv])

    yk = _head_rms(compress(xk_ref[0], 0)) * g_ref[...]
    kc_ref[0] = _rope(yk, rc_ref[...], rm_ref[...], rp_ref[...], ROT_DIM // 2).astype(kc_ref.dtype)
    vc_ref[0] = compress(xv_ref[0], 1).astype(vc_ref.dtype)


def _compress(xk, xv, pe, w1, w2, g, rope_c):
    b, ncp, wide = xk.shape
    c3 = lambda i: (0, 0, 0)
    c4 = lambda i: (0, 0, 0, 0)
    c2 = lambda i: (0, 0)
    per_b = lambda i: (i, 0, 0)
    return pl.pallas_call(
        _compress_kernel,
        grid=(b,),
        in_specs=[pl.BlockSpec((1, ncp, wide), per_b), pl.BlockSpec((1, ncp, wide), per_b),
                  pl.BlockSpec(pe.shape, c3), pl.BlockSpec(w1.shape, c4), pl.BlockSpec(w2.shape, c3),
                  pl.BlockSpec((1, LANES), c2)] + [pl.BlockSpec((ncp, LANES), c2)] * 3,
        out_specs=[pl.BlockSpec((1, ncp, LANES), per_b)] * 2,
        out_shape=[jax.ShapeDtypeStruct((b, ncp, LANES), MXU_DTYPE)] * 2,
        compiler_params=_params(1),
        name="compress",
    )(xk, xv, pe, w1, w2, g, *rope_c)


def _sb_kernel(q_ref, k_ref, v_ref, o_ref, *, tq):
    i = pl.program_id(1)
    nslab = SB_W // LANES
    heads = [(sl, p) for sl in range(nslab) for p in range(2)]
    lane = _lane_iota((tq, LANES))
    rows = _row_iota((tq, tq))
    colsq = _lane_iota((tq, tq))
    upper = jnp.where(_row_iota((tq, tq + LANES)) > _lane_iota((tq, tq + LANES)), 1.0,
                      jnp.where(_lane_iota((tq, tq + LANES)) >= tq, 1.0, 0.0)).astype(MXU_DTYPE)
    in_half = [(lane >> HEAD_SHIFT) == p for p in range(2)]
    qh = [jnp.where(in_half[p], q_ref[0, :, sl * LANES:(sl + 1) * LANES].astype(F32), 0.0).astype(MXU_DTYPE)
          for sl, p in heads]
    reps = tq // LANES

    def tile(kt, carries, accs, diagonal):
        ks = [k_ref[0, pl.ds(kt * tq, tq), sl * LANES:(sl + 1) * LANES] for sl in range(nslab)]
        vs = [v_ref[0, pl.ds(kt * tq, tq), sl * LANES:(sl + 1) * LANES] for sl in range(nslab)]
        zs = [_dot_nt(qh[n], ks[sl]) for n, (sl, _) in enumerate(heads)]
        new_c, new_a = [], []
        for p, (sl, _) in enumerate(heads):
            z = zs[p]
            l = -(jnp.maximum(z, 0.0) + jnp.log1p(jnp.exp(-jnp.abs(z))))
            if diagonal:
                past = colsq < rows
                l = jnp.where(past, l, 0.0)
            hi = l.astype(MXU_DTYPE)
            lo = (l - hi.astype(F32)).astype(MXU_DTYPE)
            sums = _dot(hi, upper) + _dot(lo, upper)
            tail = sums[:, :tq] + jnp.concatenate([carries[p]] * reps, axis=1)
            w = jnp.exp(z + l + tail)
            if diagonal:
                w = jnp.where(past, w, 0.0)
            new_a.append(accs[p] + _dot(w.astype(MXU_DTYPE), vs[sl]))
            new_c.append(carries[p] + sums[:, tq:])
        return tuple(new_c), tuple(new_a)

    zeros = (jnp.zeros((tq, LANES), F32),) * len(heads)
    carries, accs = tile(i, zeros, zeros, True)

    def cond(st):
        kt, carries, _ = st
        worst = functools.reduce(jnp.maximum, carries)
        return (kt >= 0) & (jnp.max(worst) > SB_SKIP)

    def body(st):
        kt, carries, accs = st
        carries, accs = tile(kt, carries, accs, False)
        return kt - 1, carries, accs

    _, _, accs = lax.while_loop(cond, body, (i - 1, carries, accs))
    for sl in range(nslab):
        o_ref[0, :, sl * LANES:(sl + 1) * LANES] = jnp.where(in_half[0], accs[2 * sl],
                                                             accs[2 * sl + 1]).astype(o_ref.dtype)


def _sb_attention(sb, *, tq):
    b, s, _ = sb.shape
    return pl.pallas_call(
        functools.partial(_sb_kernel, tq=tq),
        grid=(b, s // tq),
        in_specs=[pl.BlockSpec((1, tq, SB_W), lambda bi, i: (bi, i, 0)),
                  pl.BlockSpec((1, s, SB_W), lambda bi, i: (bi, 0, 1)),
                  pl.BlockSpec((1, s, SB_W), lambda bi, i: (bi, 0, 2))],
        out_specs=pl.BlockSpec((1, tq, SB_W), lambda bi, i: (bi, i, 0)),
        out_shape=jax.ShapeDtypeStruct((b, s, SB_W), MXU_DTYPE),
        compiler_params=_params(2),
        name="sb_attn",
    )(sb, sb, sb)


def _group_queries(q_ref, tq):
    lane = _lane_iota((tq, LANES))
    qa = []
    for h in range(NSA_HEADS):
        slab, parity, g = h // 2, h % 2, h // NSA_HPG
        qs = q_ref[0, :, slab * LANES:(slab + 1) * LANES].astype(F32)
        mine = jnp.where((lane >> HEAD_SHIFT) == parity, qs, 0.0)
        if parity != g:
            mine = _swap_halves(mine)
        qa.append(mine.astype(MXU_DTYPE))
    return qa


def _fold_rows(x, op, rows=64):
    while x.shape[0] > rows and x.shape[0] % 16 == 0:
        half = x.shape[0] // 2
        x = op(x[:half], x[half:])
    return x


def _key_max(x):
    return jnp.max(_fold_rows(x, jnp.maximum), axis=0, keepdims=True)


def _key_sum(x):
    return jnp.sum(_fold_rows(x, jnp.add), axis=0, keepdims=True)


def _online_update(m, acc_ref, s, v_t):
    m_new = jnp.maximum(m, _key_max(s))
    p = jnp.exp2(s - m_new)
    acc_ref[...] = jnp.exp2(m - m_new) * acc_ref[...] + _dot(v_t, p.astype(MXU_DTYPE))
    return m_new


def _attend_pipelined(n_tiles, scores_fn, softmax_fn, ms, fix_last, s_ref):
    heads = s_ref.shape[1]

    def issue(slot, kt):
        for h, sc in enumerate(scores_fn(kt)):
            s_ref[slot, h] = sc

    def fold(slot, kt, ms):
        return softmax_fn(kt, ms, tuple(s_ref[slot, h] for h in range(heads)))

    plain = n_tiles - 1

    @pl.when(plain > 0)
    def _():
        issue(0, 0)

    def body(p, ms):
        issue(1, 2 * p + 1)
        ms = fold(0, 2 * p, ms)
        issue(0, jnp.minimum(2 * p + 2, plain - 1))
        return fold(1, 2 * p + 1, ms)

    ms = lax.fori_loop(0, plain // 2, body, ms)
    ms = lax.cond(plain % 2 == 1, lambda ms: fold(0, plain - 1, ms), lambda ms: ms, ms)
    return softmax_fn(n_tiles - 1, ms, fix_last(scores_fn(n_tiles - 1)))


def _online_init(tq, acc_ref):
    acc_ref[...] = jnp.zeros(acc_ref.shape, F32)
    return tuple(jnp.full((1, tq), NEG, F32) for _ in range(acc_ref.shape[0]))


def _nsa_kernel(q_ref, gate_ref, kc_ref, vct_ref, ks_ref, vst_ref, kw_ref, vwt_ref,
                ovt_ref, oh_ref, o_ref, qaug_ref, acc_ref, s_ref, *, tq, tks):
    i = pl.program_id(1)
    q0 = i * tq
    ncp = kc_ref.shape[1]
    heads = range(NSA_HEADS)
    group = [h // NSA_HPG for h in heads]
    qa = _group_queries(q_ref, tq)

    c_vis = (_row_iota((ncp, tq)) * NSA_CMP_STRIDE + (NSA_CMP_LEN - 1)) <= q0 + _lane_iota((ncp, tq))
    kc = kc_ref[0]
    pc_sum = [jnp.zeros((ncp, tq), F32) for _ in range(NSA_GROUPS)]
    o_cmp = []
    any_vis = q0 + _lane_iota((1, tq)) >= NSA_CMP_LEN - 1
    scores = [_dot_nt(kc, qa[h]) for h in heads]
    for h in heads:
        s = jnp.where(c_vis, scores[h], NEG)
        p = jnp.exp2(s - _key_max(s))
        p = p * jnp.where(any_vis, 1.0 / _key_sum(p), 0.0)
        pc_sum[group[h]] = pc_sum[group[h]] + p
        o_cmp.append(_dot(vct_ref[0, group[h]], p.astype(MXU_DTYPE))[:HEAD_DIM])

    band = tq + NSA_WINDOW
    start = pl.multiple_of(jnp.maximum(q0 - NSA_WINDOW, 0), tq)
    kw = kw_ref[0, pl.ds(start, band), :]
    s_w = start + _row_iota((band, tq))
    t_w = q0 + _lane_iota((band, tq))
    w_ok = (s_w <= t_w) & (s_w > t_w - NSA_WINDOW)
    o_win = []
    scores = [_dot_nt(kw, qa[h]) for h in heads]
    for h in heads:
        s = jnp.where(w_ok, scores[h], NEG)
        p = jnp.exp2(s - _key_max(s)).astype(MXU_DTYPE)
        acc = jnp.zeros((V_ROWS, tq), F32)
        for c in range(band // tq):
            acc = acc + _dot(vwt_ref[0, group[h], start // tq + c], p[c * tq:(c + 1) * tq, :])
        o_win.append(acc[:HEAD_DIM] * (1.0 / acc[HEAD_DIM:HEAD_DIM + 1]))

    blk = _row_iota((LANES, tq))
    t_b = q0 + _lane_iota((LANES, tq))
    cur = t_b >> SEL_SHIFT
    visible = blk * NSA_SEL_LEN <= t_b
    forced = (blk == 0) | (blk == cur) | (blk == cur - 1)
    taken = -3.0e38
    work = []
    for g in range(NSA_GROUPS):
        hi = pc_sum[g].astype(MXU_DTYPE)
        lo = (pc_sum[g] - hi.astype(F32)).astype(MXU_DTYPE)
        imp = _dot(ovt_ref[...], hi) + _dot(ovt_ref[...], lo)
        work.append(jnp.where(visible, jnp.where(forced, taken, imp), NEG))
    blk_f = blk.astype(F32)
    for _ in range(NSA_SEL_N - 3):
        for g in range(NSA_GROUPS):
            m = jnp.max(work[g], axis=0, keepdims=True)
            first = jnp.min(jnp.where(work[g] == m, blk_f, float(LANES)), axis=0, keepdims=True)
            work[g] = jnp.where(blk_f == first, taken, work[g])
    sel_bias = [jnp.where(visible, jnp.where(w == taken, 0.0, NEG), NEG).T.astype(MXU_DTYPE) for w in work]

    n_tiles = (q0 + tq + tks - 1) // tks
    t_s = q0 + _lane_iota((tks, tq))
    row_s = _row_iota((tks, tq))
    for h in heads:
        qaug_ref[h] = jnp.concatenate([qa[h], sel_bias[group[h]]], axis=1)

    def sel_scores(kt):
        k_aug = jnp.concatenate([ks_ref[0, pl.ds(kt * tks, tks), :], oh_ref[pl.ds(kt * tks, tks), :]], axis=1)
        return tuple(_dot_nt(k_aug, qaug_ref[h]) for h in heads)

    def sel_softmax(kt, ms, scores):
        v_t = [vst_ref[0, g, kt] for g in range(NSA_GROUPS)]
        return tuple(_online_update(ms[h], acc_ref.at[h], scores[h], v_t[group[h]]) for h in heads)

    def sel_causal(scores):
        causal = (n_tiles - 1) * tks + row_s <= t_s
        return tuple(jnp.where(causal, s, NEG) for s in scores)

    _attend_pipelined(n_tiles, sel_scores, sel_softmax, _online_init(tq, acc_ref), sel_causal, s_ref)
    o_sel = [acc_ref[h, :HEAD_DIM, :] * (1.0 / acc_ref[h, HEAD_DIM:HEAD_DIM + 1, :]) for h in heads]

    gates_t = gate_ref[0].T
    merged = []
    for h in heads:
        tot = jnp.zeros((HEAD_DIM, tq), F32)
        for r, branch in enumerate((o_cmp, o_sel, o_win)):
            tot = tot + gates_t[3 * h + r:3 * h + r + 1, :] * branch[h]
        merged.append(tot)
    for s in range(NSA_QW // LANES):
        pair = jnp.concatenate([merged[2 * s], merged[2 * s + 1]], axis=0)
        o_ref[0, :, s * LANES:(s + 1) * LANES] = pair.T.astype(o_ref.dtype)


def _nsa_attention(nq, ng, kc, vct, ks, vst, kw, vwt, overlap_t, onehot, *, tq, tks):
    b, s, _ = nq.shape
    once = pl.Buffered(1)

    def per_batch(a):
        return pl.BlockSpec((1,) + a.shape[1:], lambda bi, i: (bi,) + (0,) * (a.ndim - 1), pipeline_mode=once)

    def const(a):
        return pl.BlockSpec(a.shape, lambda bi, i: (0,) * a.ndim, pipeline_mode=once)

    return pl.pallas_call(
        functools.partial(_nsa_kernel, tq=tq, tks=tks),
        grid=(b, s // tq),
        in_specs=[pl.BlockSpec((1, tq, NSA_QW), lambda bi, i: (bi, i, 0)),
                  pl.BlockSpec((1, tq, LANES), lambda bi, i: (bi, i, 0)),
                  per_batch(kc), per_batch(vct), per_batch(ks), per_batch(vst), per_batch(kw), per_batch(vwt),
                  const(overlap_t), const(onehot)],
        out_specs=pl.BlockSpec((1, tq, NSA_QW), lambda bi, i: (bi, i, 0)),
        out_shape=jax.ShapeDtypeStruct((b, s, NSA_QW), MXU_DTYPE),
        scratch_shapes=[pltpu.VMEM((NSA_HEADS, tq, 2 * LANES), MXU_DTYPE),
                        pltpu.VMEM((NSA_HEADS, V_ROWS, tq), F32),
                        pltpu.VMEM((2, NSA_HEADS, tks, tq), F32)],
        compiler_params=_params(2),
        name="nsa_attn",
    )(nq, ng, kc, vct, ks, vst, kw, vwt, overlap_t, onehot)


def _dsa_kernel(iq_ref, iwt_ref, ik_ref, q_ref, kv_ref, kvt_ref, o_ref, khi_ref, klo_ref, bias_ref, qa_ref,
                acc_ref, s_ref, *, tq, tk, top_k):
    i = pl.program_id(1)
    q0 = i * tq
    n_tiles = (q0 + tq + tk - 1) // tk
    lane = _lane_iota((tq, LANES))
    t_q = q0 + _lane_iota((tk, tq))
    row_k = _row_iota((tk, tq))

    iqa = []
    for s in range(IDX_QW // LANES):
        qs = iq_ref[0, :, s * LANES:(s + 1) * LANES].astype(F32)
        for p in range(LANES // DSA_IDX_DIM):
            mine = jnp.where((lane >> IDX_SHIFT) == p, qs, 0.0)
            if p:
                mine = pltpu.roll(mine, LANES - p * DSA_IDX_DIM, 1)
            iqa.append(mine.astype(MXU_DTYPE))
    w_h = [iwt_ref[0, h:h + 1, :] for h in range(DSA_IDX_HEADS)]

    def score_tiles(tiles, last_is_diagonal):
        logits = []
        for kt in tiles:
            ik = ik_ref[0, pl.ds(kt * tk, tk), :]
            logits.append([_dot_nt(ik, iqa[h]) for h in range(DSA_IDX_HEADS)])
        for n, kt in enumerate(tiles):
            score = jnp.zeros((tk, tq), F32)
            for h in range(DSA_IDX_HEADS):
                score = score + w_h[h] * jnp.maximum(logits[n][h], 0.0)
            score = jnp.where(score == 0.0, 0.0, score)
            bits = pltpu.bitcast(score, jnp.int32)
            key = jnp.where(bits < 0, bits ^ 0x7FFFFFFF, bits)
            if last_is_diagonal and n == len(tiles) - 1:
                key = jnp.where(kt * tk + row_k <= t_q, key, INT_MIN)
            khi_ref[kt] = (key >> 16).astype(jnp.int16)
            klo_ref[kt] = ((key & 0xFFFF) + I16_MIN).astype(jnp.int16)

    def score_pair(p, _):
        score_tiles([2 * p, 2 * p + 1], False)
        return 0

    plain = n_tiles - 1
    lax.fori_loop(0, plain // 2, score_pair, 0)
    lax.cond(plain % 2 == 1,
             lambda: score_tiles([n_tiles - 2, n_tiles - 1], True),
             lambda: score_tiles([n_tiles - 1], True))
    one, zero = jnp.ones((), jnp.bfloat16), jnp.zeros((), jnp.bfloat16)

    def count(flags):
        def body(kt, acc):
            return acc + _fold_rows(flags(kt), jnp.add, rows=32).astype(F32)
        acc = lax.fori_loop(0, n_tiles, body, jnp.zeros((32, tq), F32))
        return jnp.sum(acc, axis=0, keepdims=True)

    uncounted = float(2 ** 24)

    def kth_largest(ref, may_stop_early):
        def body(b, carry):
            ans_u, reached = carry
            cand_u = ans_u | lax.shift_left(jnp.int32(1), 15 - b)
            cand = (cand_u + I16_MIN).astype(jnp.int16)
            cnt = count(lambda kt: jnp.where(ref[kt] >= cand, one, zero))
            ok = cnt >= top_k
            return jnp.where(ok, cand_u, ans_u), jnp.where(ok, cnt, reached)

        carry = (jnp.zeros((1, tq), jnp.int32), jnp.full((1, tq), uncounted, F32))
        if may_stop_early:
            carry = lax.fori_loop(0, 8, body, carry)
            for first, last in ((8, 12), (12, 16)):
                unsettled = jnp.max(jnp.abs(carry[1] - top_k)) > 0.0
                carry = lax.cond(unsettled, lambda c, f=first, l=last: lax.fori_loop(f, l, body, c),
                                 lambda c: c, carry)
        else:
            carry = lax.fori_loop(0, 16, body, carry)
        return carry[0] + I16_MIN, carry[1]

    t_hi32, reached_hi = kth_largest(khi_ref, False)
    t_hi = t_hi32.astype(jnp.int16)

    def pin_body(kt, _):
        hi = khi_ref[kt]
        klo_ref[kt] = jnp.where(hi == t_hi, klo_ref[kt],
                                jnp.where(hi > t_hi, jnp.int16(I16_MAX), jnp.int16(I16_MIN)))
        return 0

    lax.fori_loop(0, n_tiles, pin_body, 0)
    t_lo32, reached_lo = kth_largest(klo_ref, True)
    reached = jnp.where(reached_lo < uncounted, reached_lo, reached_hi)
    t_lo = jnp.where(t_hi32 == I16_MIN, jnp.maximum(t_lo32, I16_MIN + 1), t_lo32).astype(jnp.int16)

    def at_least(kt, yes, no):
        hi = khi_ref[kt]
        return jnp.where(hi > t_hi, yes, jnp.where(hi == t_hi, jnp.where(klo_ref[kt] >= t_lo, yes, no), no))

    def above(kt):
        return jnp.where(khi_ref[kt] > t_hi, one, jnp.where(klo_ref[kt] > t_lo, one, zero))

    bias_yes, bias_no = jnp.zeros((), bias_ref.dtype), jnp.full((), NEG, bias_ref.dtype)

    def plain_bias():
        def body(kt, _):
            bias_ref[kt] = at_least(kt, bias_yes, bias_no)
            return 0
        lax.fori_loop(0, n_tiles, body, 0)

    def ranked_bias():
        need = top_k - count(above)
        lower = jnp.where(_lane_iota((tk, tk)) < _row_iota((tk, tk)), 1.0, 0.0).astype(MXU_DTYPE)

        def body(kt, seen):
            tie = jnp.where(khi_ref[kt] == t_hi, jnp.where(klo_ref[kt] == t_lo, one, zero), zero)
            rank = _dot(lower, tie.astype(MXU_DTYPE)) + seen
            tie = tie.astype(F32)
            take = jnp.where(above(kt).astype(F32) > 0.0, 1.0, jnp.where(rank < need, tie, 0.0))
            bias_ref[kt] = ((take - 1.0) * (-NEG)).astype(bias_ref.dtype)
            return seen + _key_sum(tie)

        lax.fori_loop(0, n_tiles, body, jnp.zeros((1, tq), F32))

    lax.cond(jnp.max(reached) > top_k, ranked_bias, plain_bias)

    for s in range(DSA_QW // LANES):
        qs = q_ref[0, :, s * LANES:(s + 1) * LANES].astype(F32)
        lo = jnp.where(lane < HEAD_DIM, qs, 0.0)
        hi = _swap_halves(jnp.where(lane < HEAD_DIM, 0.0, qs))
        qa_ref[2 * s] = lo.astype(MXU_DTYPE)
        qa_ref[2 * s + 1] = hi.astype(MXU_DTYPE)

    def att_scores(kt):
        kv = kv_ref[0, pl.ds(kt * tk, tk), :]
        bias = bias_ref[kt].astype(F32)
        return tuple(_dot_nt(kv, qa_ref[h]) + bias for h in range(DSA_HEADS))

    def att_softmax(kt, ms, scores):
        kv_t = kvt_ref[0, kt]
        return tuple(_online_update(ms[h], acc_ref.at[h], scores[h], kv_t) for h in range(DSA_HEADS))

    _attend_pipelined(n_tiles, att_scores, att_softmax, _online_init(tq, acc_ref), lambda scores: scores, s_ref)
    outs = [acc_ref[h, :HEAD_DIM, :] * (1.0 / acc_ref[h, HEAD_DIM:HEAD_DIM + 1, :]) for h in range(DSA_HEADS)]

    for s in range(DSA_QW // LANES):
        pair = jnp.concatenate([outs[2 * s], outs[2 * s + 1]], axis=0)
        o_ref[0, :, s * LANES:(s + 1) * LANES] = pair.T.astype(o_ref.dtype)


def _dsa_attention(iq, iwt, ik, dq, dkv, dkvt, *, tq, tk, top_k):
    b, s, _ = dq.shape
    tile = lambda width: pl.BlockSpec((1, tq, width), lambda bi, i: (bi, i, 0))
    full = pl.BlockSpec((1, s, LANES), lambda bi, i: (bi, 0, 0))
    return pl.pallas_call(
        functools.partial(_dsa_kernel, tq=tq, tk=tk, top_k=top_k),
        grid=(b, s // tq),
        in_specs=[tile(IDX_QW), pl.BlockSpec((1, DSA_IDX_HEADS, tq), lambda bi, i: (bi, 0, i)), full,
                  tile(DSA_QW), full, pl.BlockSpec((1,) + dkvt.shape[1:], lambda bi, i: (bi, 0, 0, 0))],
        out_specs=tile(DSA_QW),
        out_shape=jax.ShapeDtypeStruct((b, s, DSA_QW), MXU_DTYPE),
        scratch_shapes=[pltpu.VMEM((s // tk, tk, tq), jnp.int16), pltpu.VMEM((s // tk, tk, tq), jnp.int16),
                        pltpu.VMEM((s // tk, tk, tq), MXU_DTYPE),
                        pltpu.VMEM((DSA_HEADS, tq, LANES), MXU_DTYPE), pltpu.VMEM((DSA_HEADS, V_ROWS, tq), F32),
                        pltpu.VMEM((2, DSA_HEADS, tk, tq), F32)],
        compiler_params=_params(2),
        name="dsa_attn",
    )(iq, iwt, ik, dq, dkv, dkvt)


def _merge_kernel(x_ref, g_ref, oa_ref, ob_ref, oc_ref, wg_ref, wa_ref, wb_ref, wc_ref, wo_ref, o_ref):
    x = x_ref[...]
    d = x.shape[1]
    ms = jnp.mean(x * x, axis=-1, keepdims=True)
    h = (x * lax.rsqrt(ms + NORM_EPS) * g_ref[...]).astype(MXU_DTYPE)
    merged = jnp.zeros(x.shape, F32)
    for r, (o_r, w_r) in enumerate(((oa_ref, wa_ref), (ob_ref, wb_ref), (oc_ref, wc_ref))):
        gate = jax.nn.sigmoid(_dot(h, wg_ref[:, r * d:(r + 1) * d]))
        merged = merged + gate * _dot(o_r[...], w_r[...])
    o_ref[...] = x + _dot(merged.astype(MXU_DTYPE), wo_ref[...])


def _merge(x2, g, oa, ob, oc, w_gate, wa, wb, wc, wo, *, tm):
    n, d = x2.shape
    row = lambda i: (i, 0)
    const = lambda i: (0, 0)
    return pl.pallas_call(
        _merge_kernel,
        grid=(n // tm,),
        in_specs=[pl.BlockSpec((tm, d), row), pl.BlockSpec((1, d), const),
                  pl.BlockSpec((tm, SB_W), row), pl.BlockSpec((tm, NSA_QW), row),
                  pl.BlockSpec((tm, DSA_QW), row),
                  pl.BlockSpec(w_gate.shape, const), pl.BlockSpec(wa.shape, const),
                  pl.BlockSpec(wb.shape, const), pl.BlockSpec(wc.shape, const),
                  pl.BlockSpec(wo.shape, const)],
        out_specs=pl.BlockSpec((tm, d), row),
        out_shape=jax.ShapeDtypeStruct((n, d), F32),
        compiler_params=_params(1),
        name="merge",
    )(x2, g, oa, ob, oc, w_gate, wa, wb, wc, wo)


def _rope_tables(pos, rot_dim, head_dim):
    half = rot_dim // 2
    inv = ROPE_THETA ** (-jnp.arange(0, rot_dim, 2, dtype=F32) / rot_dim)
    ang = pos.astype(F32)[:, None] * inv[None, :]
    cos, sin = jnp.cos(ang), jnp.sin(ang)
    n = pos.shape[0]
    rest = head_dim - rot_dim
    zero_h = jnp.zeros((n, half), F32)
    c = jnp.concatenate([cos, cos, jnp.ones((n, rest), F32)], axis=-1)
    sm = jnp.concatenate([-sin, zero_h, jnp.zeros((n, rest), F32)], axis=-1)
    sp = jnp.concatenate([zero_h, sin, jnp.zeros((n, rest), F32)], axis=-1)
    reps = LANES // head_dim
    return tuple(jnp.tile(t, (1, reps)) for t in (c, sm, sp))


def _pad_cols(w, width):
    return jnp.pad(w, ((0, 0), (0, width - w.shape[1])))


def _relayout_w_in(w_in):
    offs = np.cumsum((SB_W, SB_W, SB_W, NSA_QW) + (NSA_KVW,) * 6
                     + (3 * NSA_HEADS, DSA_QW, HEAD_DIM, HEAD_DIM, IDX_QW, DSA_IDX_DIM, DSA_IDX_HEADS))
    o_gate, o_dq, o_dk, o_dv, o_iq, o_ik, o_iw, o_end = offs[9:17].tolist()
    parts = [w_in[:, :o_gate], _pad_cols(w_in[:, o_gate:o_dq], LANES), w_in[:, o_dq:o_dk],
             w_in[:, o_dk:o_iq], w_in[:, o_iq:o_ik], _pad_cols(w_in[:, o_ik:o_iw], LANES),
             _pad_cols(w_in[:, o_iw:o_end], LANES)]
    w = jnp.concatenate(parts, axis=1)
    assert w.shape[1] == _P_END
    return w.astype(MXU_DTYPE)


def _compress_weights(pe, w1, w2):
    r = NSA_CMP_LEN // NSA_CMP_STRIDE
    hid = NSA_CMP_HIDDEN
    w1r = w1.reshape(2, r, NSA_CMP_STRIDE, HEAD_DIM, hid)
    eye = jnp.eye(NSA_GROUPS, dtype=w1.dtype)
    w1e = jnp.einsum('kmldj,gh->kmlgdhj', w1r, eye)
    w1e = w1e.reshape(2, r, NSA_CMP_STRIDE * NSA_KVW, NSA_GROUPS * hid)
    w2e = jnp.einsum('kjd,gh->kgjhd', w2, eye).reshape(2, NSA_GROUPS * hid, NSA_KVW)
    pe_e = jnp.broadcast_to(pe.reshape(2, r, NSA_CMP_STRIDE, 1, HEAD_DIM),
                            (2, r, NSA_CMP_STRIDE, NSA_GROUPS, HEAD_DIM))
    return pe_e.reshape(2, r, NSA_CMP_STRIDE * NSA_KVW), w1e.astype(MXU_DTYPE), w2e.astype(MXU_DTYPE)


def _overlap_matrix_t(ncp, seq):
    c = np.arange(ncp)[None, :] * NSA_CMP_STRIDE
    j = np.arange(LANES)[:, None] * NSA_SEL_LEN
    ov = (c < j + NSA_SEL_LEN) & (c + NSA_CMP_LEN - 1 >= j) & (j < seq)
    ov &= (np.arange(ncp)[None, :] < ncp - (NSA_CMP_LEN // NSA_CMP_STRIDE - 1))
    return jnp.asarray(ov, dtype=MXU_DTYPE)


def _block_onehot(seq):
    return jnp.asarray(np.arange(seq)[:, None] // NSA_SEL_LEN == np.arange(LANES)[None, :], dtype=MXU_DTYPE)


def _key_tiles_t(v, tile):
    b, s, w = v.shape
    return jnp.swapaxes(v.reshape(b, s // tile, tile, w), 2, 3)


def _group_values_t(v, tile):
    b, s, w = v.shape
    groups = w // HEAD_DIM
    vt = _key_tiles_t(v, tile).reshape(b, s // tile, groups, HEAD_DIM, tile)
    vt = jnp.swapaxes(vt, 1, 2)
    pad = jnp.zeros((b, groups, s // tile, V_ROWS - HEAD_DIM, tile), v.dtype).at[:, :, :, 0, :].set(1)
    return jnp.concatenate([vt, pad], axis=3)


def kernel(x, ffn1_norm, ffn1_w_gu, ffn1_w_down, mix_norm, w_in, w_gate, nsa_q_norm, nsa_k_norm,
           nsa_cmp_pe, nsa_cmp_w1, nsa_cmp_w2, dsa_q_norm, dsa_k_norm, w_br_a, w_br_b, w_br_c, w_out,
           ffn2_norm, ffn2_w_gu, ffn2_w_down):
    b, s, d = x.shape
    depth = w_in.shape[0]
    n = b * s
    ncp = s // NSA_CMP_STRIDE
    assert s // NSA_SEL_LEN <= LANES and s % 1024 == 0
    top_k = min(DSA_TOPK, s // 4)
    ffn_tf = ffn1_w_down.shape[1]
    cast = lambda w: w.astype(MXU_DTYPE)
    tile2 = lambda v, reps: jnp.tile(v, reps)[None, :]

    pos = jnp.arange(s)
    rope_main = _rope_tables(pos, ROT_DIM, HEAD_DIM)
    rope_idx = _rope_tables(pos, DSA_IDX_ROT, DSA_IDX_DIM)
    rope_cmp = _rope_tables(jnp.arange(ncp) * NSA_CMP_STRIDE + NSA_CMP_LEN - 1, ROT_DIM, HEAD_DIM)
    overlap_t = _overlap_matrix_t(ncp, s)
    onehot = _block_onehot(s)

    x2 = x.reshape(n, d)
    for l in range(depth):
        x2 = _ffn(x2, ffn1_norm[l][None, :], cast(ffn1_w_gu[l]), cast(ffn1_w_down[l]), tm=FFN_TM, tf=ffn_tf)

        (sb, nq, nkc, nvc, nkv, ng, dq, dkv, iq, ik, iw) = _proj(
            x2, mix_norm[l][None, :], _relayout_w_in(w_in[l]),
            tile2(nsa_q_norm[l], NSA_HEADS),
            jnp.stack([jnp.tile(nsa_k_norm[l, 1], NSA_GROUPS), jnp.tile(nsa_k_norm[l, 2], NSA_GROUPS)]),
            tile2(dsa_q_norm[l], DSA_HEADS), tile2(dsa_k_norm[l], 2),
            rope_main, rope_idx, tm=ROW_TM, seq=s)
        r3 = lambda t: t.reshape(b, s, t.shape[-1])

        pe_e, w1e, w2e = _compress_weights(nsa_cmp_pe[l], nsa_cmp_w1[l], nsa_cmp_w2[l])
        kc, vc = _compress(nkc.reshape(b, ncp, NSA_CMP_STRIDE * NSA_KVW),
                           nvc.reshape(b, ncp, NSA_CMP_STRIDE * NSA_KVW),
                           pe_e, w1e, w2e, tile2(nsa_k_norm[l, 0], NSA_GROUPS), rope_cmp)

        o_a = _sb_attention(r3(sb), tq=SB_TQ)

        nkv = r3(nkv)
        ks, vs, kw, vw = (nkv[:, :, c * LANES:(c + 1) * LANES] for c in range(4))
        o_b = _nsa_attention(r3(nq), r3(ng), kc, _group_values_t(vc, ncp)[:, :, 0], ks,
                             _group_values_t(vs, TKS), kw, _group_values_t(vw, TQ_NSA), overlap_t, onehot,
                             tq=TQ_NSA, tks=TKS)

        dkv = r3(dkv)
        iwt = jnp.swapaxes(r3(iw)[:, :, :DSA_IDX_HEADS], 1, 2)
        dkvt = _group_values_t(dkv[:, :, HEAD_DIM:], TKS)[:, 0]
        o_c = _dsa_attention(r3(iq), iwt, r3(ik), r3(dq), dkv, dkvt, tq=TQ, tk=TKS, top_k=top_k)

        x2 = _merge(x2, mix_norm[l][None, :], o_a.reshape(n, SB_W), o_b.reshape(n, NSA_QW),
                    o_c.reshape(n, DSA_QW), cast(w_gate[l]), cast(w_br_a[l]), cast(w_br_b[l]),
                    cast(w_br_c[l]), cast(w_out[l]), tm=MERGE_TM)

        x2 = _ffn(x2, ffn2_norm[l][None, :], cast(ffn2_w_gu[l]), cast(ffn2_w_down[l]), tm=FFN_TM, tf=ffn_tf)
    return x2.reshape(b, s, d)
```

```python
import functools

import numpy as np
import jax
import jax.numpy as jnp
from jax import lax
from jax.experimental import pallas as pl
from jax.experimental.pallas import tpu as pltpu

HEAD_DIM = 64
ROT_DIM = HEAD_DIM // 4
ROPE_THETA = 500000.0
NORM_EPS = 1e-6

SB_HEADS = 4
NSA_HEADS = 8
NSA_GROUPS = 2
NSA_HPG = NSA_HEADS // NSA_GROUPS
NSA_CMP_LEN = 32
NSA_CMP_STRIDE = 16
NSA_CMP_HIDDEN = 2 * HEAD_DIM
NSA_SEL_LEN = 64
NSA_SEL_N = 16
NSA_WINDOW = 512
NSA_FORCED_SCORE = 1.0e4
DSA_HEADS = 4
DSA_IDX_HEADS = 8
DSA_IDX_DIM = 32
DSA_IDX_ROT = DSA_IDX_DIM // 4
DSA_TOPK = 256

SB_W = SB_HEADS * HEAD_DIM
NSA_QW = NSA_HEADS * HEAD_DIM
NSA_KVW = NSA_GROUPS * HEAD_DIM
DSA_QW = DSA_HEADS * HEAD_DIM
IDX_QW = DSA_IDX_HEADS * DSA_IDX_DIM

LANES = 128
VMEM_LIMIT = 56 * 1024 * 1024
MXU_DTYPE = jnp.bfloat16
NEG = -1.0e30
SB_SKIP = -120.0
INT_MIN = -2 ** 31
LOG2E = 1.4426950408889634

F32 = jnp.float32

TQ = 256
TQ_NSA = 256
TKS = 512
SB_TQ = 256
ROW_TM = 256
MERGE_TM = 512
V_ROWS = LANES
HEAD_SHIFT = HEAD_DIM.bit_length() - 1
IDX_SHIFT = DSA_IDX_DIM.bit_length() - 1
SEL_SHIFT = NSA_SEL_LEN.bit_length() - 1
I16_MIN, I16_MAX = -2 ** 15, 2 ** 15 - 1
FFN_CHUNK = 512
FFN_TM = 512
assert TKS % TQ == 0 and TKS % TQ_NSA == 0 and NSA_WINDOW % TQ_NSA == 0


def _dot(a, b):
    return jnp.dot(a, b, preferred_element_type=F32)


def _dot_nt(a, b):
    return lax.dot_general(a, b, (((1,), (1,)), ((), ())), preferred_element_type=F32)


def _params(n_axes):
    return pltpu.CompilerParams(dimension_semantics=("arbitrary",) * n_axes,
                                vmem_limit_bytes=VMEM_LIMIT)


def _lane_iota(shape):
    return lax.broadcasted_iota(jnp.int32, shape, len(shape) - 1)


def _row_iota(shape):
    return lax.broadcasted_iota(jnp.int32, shape, len(shape) - 2)


def _swap_halves(x):
    return pltpu.roll(x, LANES // 2, 1)


def _head_rms(ys):
    lo = _lane_iota(ys.shape) < HEAD_DIM
    sq = ys * ys
    s_lo = jnp.sum(jnp.where(lo, sq, 0.0), axis=-1, keepdims=True)
    s_hi = jnp.sum(jnp.where(lo, 0.0, sq), axis=-1, keepdims=True)
    ms = jnp.where(lo, s_lo, s_hi) * (1.0 / HEAD_DIM)
    return ys * lax.rsqrt(ms + NORM_EPS)


def _rope(ys, c, sm, sp, half):
    return ys * c + pltpu.roll(ys, LANES - half, 1) * sm + pltpu.roll(ys, half, 1) * sp


def _ffn_kernel(x_ref, g_ref, wg_ref, wu_ref, wd_ref, o_ref, h_ref, acc_ref):
    f = pl.program_id(1)

    @pl.when(f == 0)
    def _():
        x = x_ref[...]
        ms = jnp.mean(x * x, axis=-1, keepdims=True)
        h_ref[...] = (x * lax.rsqrt(ms + NORM_EPS) * g_ref[...]).astype(h_ref.dtype)
        acc_ref[...] = jnp.zeros_like(acc_ref)

    h = h_ref[...]
    tf = wd_ref.shape[0]
    bounds = [(c, min(c + FFN_CHUNK, tf)) for c in range(0, tf, FFN_CHUNK)]

    def gate_up(c0, c1):
        return _dot(h, wg_ref[:, c0:c1]), _dot(h, wu_ref[:, c0:c1])

    ahead = gate_up(*bounds[0])
    for idx, (c0, c1) in enumerate(bounds):
        gate, up = ahead
        if idx + 1 < len(bounds):
            ahead = gate_up(*bounds[idx + 1])
        act = (gate * jax.nn.sigmoid(gate) * up).astype(h_ref.dtype)
        acc_ref[...] += _dot(act, wd_ref[c0:c1, :])

    @pl.when(f == pl.num_programs(1) - 1)
    def _():
        o_ref[...] = x_ref[...] + 0.5 * acc_ref[...]


def _ffn(x2, g, w_gu, w_down, *, tm, tf):
    n, d = x2.shape
    d_ff = w_down.shape[0]
    nf = d_ff // tf
    return pl.pallas_call(
        _ffn_kernel,
        grid=(n // tm, nf),
        in_specs=[
            pl.BlockSpec((tm, d), lambda i, f: (i, 0)),
            pl.BlockSpec((1, d), lambda i, f: (0, 0)),
            pl.BlockSpec((d, tf), lambda i, f: (0, f)),
            pl.BlockSpec((d, tf), lambda i, f: (0, f + nf)),
            pl.BlockSpec((tf, d), lambda i, f: (f, 0)),
        ],
        out_specs=pl.BlockSpec((tm, d), lambda i, f: (i, 0)),
        out_shape=jax.ShapeDtypeStruct((n, d), F32),
        scratch_shapes=[pltpu.VMEM((tm, d), MXU_DTYPE), pltpu.VMEM((tm, d), F32)],
        compiler_params=pltpu.CompilerParams(dimension_semantics=("arbitrary",) * 2, vmem_limit_bytes=VMEM_LIMIT,
                                             allow_input_fusion=[False, False, True, True, True]),
        name="ffn",
    )(x2, g, w_gu, w_gu, w_down)


_P_SB = 0
_P_NQ = _P_SB + 3 * SB_W
_P_NCV = _P_NQ + NSA_QW
_P_NKV = _P_NCV + 2 * NSA_KVW
_P_NG = _P_NKV + 4 * NSA_KVW
_P_DQ = _P_NG + LANES
_P_DKV = _P_DQ + DSA_QW
_P_IQ = _P_DKV + LANES
_P_IK = _P_IQ + IDX_QW
_P_IW = _P_IK + LANES
_P_END = _P_IW + LANES


def _proj_kernel(x_ref, g_ref, w_ref, nq_g_ref, nk_g_ref, dq_g_ref, dk_g_ref,
                 rc_ref, rm_ref, rp_ref, ic_ref, im_ref, ip_ref,
                 sb_ref, nq_ref, nkc_ref, nvc_ref, nkv_ref, ng_ref, dq_ref, dkv_ref,
                 iq_ref, ik_ref, iw_ref):
    x = x_ref[...]
    ms = jnp.mean(x * x, axis=-1, keepdims=True)
    h = (x * lax.rsqrt(ms + NORM_EPS) * g_ref[...]).astype(MXU_DTYPE)
    rc, rm, rp = rc_ref[...], rm_ref[...], rp_ref[...]
    ic, im, ip = ic_ref[...], im_ref[...], ip_ref[...]
    half, ihalf = ROT_DIM // 2, DSA_IDX_ROT // 2
    scale = HEAD_DIM ** -0.5
    scale2 = scale * LOG2E

    def cols(a, width):
        return _dot(h, w_ref[:, a:a + width])

    def slab(y, s):
        return y[:, s * LANES:(s + 1) * LANES]

    y = cols(_P_SB, 3 * SB_W)
    sb_ref[:, 0:SB_W] = (y[:, 0:SB_W] * scale).astype(sb_ref.dtype)
    sb_ref[:, SB_W:3 * SB_W] = y[:, SB_W:3 * SB_W].astype(sb_ref.dtype)

    y = cols(_P_NQ, NSA_QW)
    for s in range(NSA_QW // LANES):
        ys = _head_rms(slab(y, s)) * slab(nq_g_ref[...], s)
        nq_ref[:, s * LANES:(s + 1) * LANES] = (_rope(ys, rc, rm, rp, half) * scale2).astype(nq_ref.dtype)

    y = cols(_P_NCV, 2 * NSA_KVW)
    nkc_ref[...] = slab(y, 0)
    nvc_ref[...] = slab(y, 1)

    y = cols(_P_NKV, 4 * NSA_KVW)
    for s, gi in ((0, 0), (2, 1)):
        ys = _head_rms(slab(y, s)) * nk_g_ref[gi:gi + 1, :]
        nkv_ref[:, s * LANES:(s + 1) * LANES] = _rope(ys, rc, rm, rp, half).astype(nkv_ref.dtype)
    for s in (1, 3):
        nkv_ref[:, s * LANES:(s + 1) * LANES] = slab(y, s).astype(nkv_ref.dtype)

    ng_ref[...] = jax.nn.sigmoid(cols(_P_NG, LANES))

    y = cols(_P_DQ, DSA_QW)
    for s in range(DSA_QW // LANES):
        ys = _head_rms(slab(y, s)) * slab(dq_g_ref[...], s)
        dq_ref[:, s * LANES:(s + 1) * LANES] = (_rope(ys, rc, rm, rp, half) * scale2).astype(dq_ref.dtype)

    y = cols(_P_DKV, LANES)
    yk = _rope(_head_rms(y) * dk_g_ref[...], rc, rm, rp, half)
    dkv_ref[...] = jnp.where(_lane_iota(y.shape) < HEAD_DIM, yk, y).astype(dkv_ref.dtype)

    y = cols(_P_IQ, IDX_QW)
    for s in range(IDX_QW // LANES):
        iq_ref[:, s * LANES:(s + 1) * LANES] = _rope(slab(y, s), ic, im, ip, ihalf).astype(iq_ref.dtype)
    ik_ref[...] = _rope(cols(_P_IK, LANES), ic, im, ip, ihalf).astype(ik_ref.dtype)
    iw_ref[...] = cols(_P_IW, LANES) * (DSA_IDX_HEADS ** -0.5) * (DSA_IDX_DIM ** -0.5)


def _proj(x2, g, w, nq_g, nk_g, dq_g, dk_g, rope_main, rope_idx, *, tm, seq):
    n, d = x2.shape
    nt = seq // tm
    row = lambda i: (i, 0)
    const = lambda i: (0, 0)
    pos = lambda i: (i % nt, 0)
    widths = [(3 * SB_W, MXU_DTYPE), (NSA_QW, MXU_DTYPE), (NSA_KVW, F32), (NSA_KVW, F32),
              (4 * NSA_KVW, MXU_DTYPE), (LANES, F32), (DSA_QW, MXU_DTYPE), (LANES, MXU_DTYPE),
              (IDX_QW, MXU_DTYPE), (LANES, MXU_DTYPE), (LANES, F32)]
    return pl.pallas_call(
        _proj_kernel,
        grid=(n // tm,),
        in_specs=[pl.BlockSpec((tm, d), row), pl.BlockSpec((1, d), const),
                  pl.BlockSpec((d, _P_END), const),
                  pl.BlockSpec((1, NSA_QW), const), pl.BlockSpec((2, LANES), const),
                  pl.BlockSpec((1, DSA_QW), const), pl.BlockSpec((1, LANES), const)]
                 + [pl.BlockSpec((tm, LANES), pos)] * 6,
        out_specs=[pl.BlockSpec((tm, wd), row) for wd, _ in widths],
        out_shape=[jax.ShapeDtypeStruct((n, wd), dt) for wd, dt in widths],
        compiler_params=_params(1),
        name="proj",
    )(x2, g, w, nq_g, nk_g, dq_g, dk_g, *rope_main, *rope_idx)


def _compress_kernel(xk_ref, xv_ref, pe_ref, w1_ref, w2_ref, g_ref, rc_ref, rm_ref, rp_ref,
                     kc_ref, vc_ref):
    ncp = xk_ref.shape[1]

    def compress(x, kv):
        a0 = _dot((x + pe_ref[kv, 0:1, :]).astype(MXU_DTYPE), w1_ref[kv, 0])
        a1 = _dot((x + pe_ref[kv, 1:2, :]).astype(MXU_DTYPE), w1_ref[kv, 1])
        pre = a0 + pltpu.roll(a1, ncp - 1, 0)
        hid = pre * jax.nn.sigmoid(pre)
        return _dot(hid.astype(MXU_DTYPE), w2_ref[kv])

    yk = _head_rms(compress(xk_ref[0], 0)) * g_ref[...]
    kc_ref[0] = _rope(yk, rc_ref[...], rm_ref[...], rp_ref[...], ROT_DIM // 2).astype(kc_ref.dtype)
    vc_ref[0] = compress(xv_ref[0], 1).astype(vc_ref.dtype)


def _compress(xk, xv, pe, w1, w2, g, rope_c):
    b, ncp, wide = xk.shape
    c3 = lambda i: (0, 0, 0)
    c4 = lambda i: (0, 0, 0, 0)
    c2 = lambda i: (0, 0)
    per_b = lambda i: (i, 0, 0)
    return pl.pallas_call(
        _compress_kernel,
        grid=(b,),
        in_specs=[pl.BlockSpec((1, ncp, wide), per_b), pl.BlockSpec((1, ncp, wide), per_b),
                  pl.BlockSpec(pe.shape, c3), pl.BlockSpec(w1.shape, c4), pl.BlockSpec(w2.shape, c3),
                  pl.BlockSpec((1, LANES), c2)] + [pl.BlockSpec((ncp, LANES), c2)] * 3,
        out_specs=[pl.BlockSpec((1, ncp, LANES), per_b)] * 2,
        out_shape=[jax.ShapeDtypeStruct((b, ncp, LANES), MXU_DTYPE)] * 2,
        compiler_params=_params(1),
        name="compress",
    )(xk, xv, pe, w1, w2, g, *rope_c)


def _sb_kernel(q_ref, k_ref, v_ref, o_ref, *, tq):
    i = pl.program_id(1)
    nslab = SB_W // LANES
    heads = [(sl, p) for sl in range(nslab) for p in range(2)]
    lane = _lane_iota((tq, LANES))
    rows = _row_iota((tq, tq))
    colsq = _lane_iota((tq, tq))
    upper = jnp.where(_row_iota((tq, tq + LANES)) > _lane_iota((tq, tq + LANES)), 1.0,
                      jnp.where(_lane_iota((tq, tq + LANES)) >= tq, 1.0, 0.0)).astype(MXU_DTYPE)
    in_half = [(lane >> HEAD_SHIFT) == p for p in range(2)]
    qh = [jnp.where(in_half[p], q_ref[0, :, sl * LANES:(sl + 1) * LANES].astype(F32), 0.0).astype(MXU_DTYPE)
          for sl, p in heads]
    reps = tq // LANES

    def tile(kt, carries, accs, diagonal):
        ks = [k_ref[0, pl.ds(kt * tq, tq), sl * LANES:(sl + 1) * LANES] for sl in range(nslab)]
        vs = [v_ref[0, pl.ds(kt * tq, tq), sl * LANES:(sl + 1) * LANES] for sl in range(nslab)]
        zs = [_dot_nt(qh[n], ks[sl]) for n, (sl, _) in enumerate(heads)]
        new_c, new_a = [], []
        for p, (sl, _) in enumerate(heads):
            z = zs[p]
            l = -(jnp.maximum(z, 0.0) + jnp.log1p(jnp.exp(-jnp.abs(z))))
            if diagonal:
                past = colsq < rows
                l = jnp.where(past, l, 0.0)
            hi = l.astype(MXU_DTYPE)
            lo = (l - hi.astype(F32)).astype(MXU_DTYPE)
            sums = _dot(hi, upper) + _dot(lo, upper)
            tail = sums[:, :tq] + jnp.concatenate([carries[p]] * reps, axis=1)
            w = jnp.exp(z + l + tail)
            if diagonal:
                w = jnp.where(past, w, 0.0)
            new_a.append(accs[p] + _dot(w.astype(MXU_DTYPE), vs[sl]))
            new_c.append(carries[p] + sums[:, tq:])
        return tuple(new_c), tuple(new_a)

    zeros = (jnp.zeros((tq, LANES), F32),) * len(heads)
    carries, accs = tile(i, zeros, zeros, True)

    def cond(st):
        kt, carries, _ = st
        worst = functools.reduce(jnp.maximum, carries)
        return (kt >= 0) & (jnp.max(worst) > SB_SKIP)

    def body(st):
        kt, carries, accs = st
        carries, accs = tile(kt, carries, accs, False)
        return kt - 1, carries, accs

    _, _, accs = lax.while_loop(cond, body, (i - 1, carries, accs))
    for sl in range(nslab):
        o_ref[0, :, sl * LANES:(sl + 1) * LANES] = jnp.where(in_half[0], accs[2 * sl],
                                                             accs[2 * sl + 1]).astype(o_ref.dtype)


def _sb_attention(sb, *, tq):
    b, s, _ = sb.shape
    return pl.pallas_call(
        functools.partial(_sb_kernel, tq=tq),
        grid=(b, s // tq),
        in_specs=[pl.BlockSpec((1, tq, SB_W), lambda bi, i: (bi, i, 0)),
                  pl.BlockSpec((1, s, SB_W), lambda bi, i: (bi, 0, 1)),
                  pl.BlockSpec((1, s, SB_W), lambda bi, i: (bi, 0, 2))],
        out_specs=pl.BlockSpec((1, tq, SB_W), lambda bi, i: (bi, i, 0)),
        out_shape=jax.ShapeDtypeStruct((b, s, SB_W), MXU_DTYPE),
        compiler_params=_params(2),
        name="sb_attn",
    )(sb, sb, sb)


def _group_queries(q_ref, tq):
    lane = _lane_iota((tq, LANES))
    qa = []
    for h in range(NSA_HEADS):
        slab, parity, g = h // 2, h % 2, h // NSA_HPG
        qs = q_ref[0, :, slab * LANES:(slab + 1) * LANES].astype(F32)
        mine = jnp.where((lane >> HEAD_SHIFT) == parity, qs, 0.0)
        if parity != g:
            mine = _swap_halves(mine)
        qa.append(mine.astype(MXU_DTYPE))
    return qa


def _fold_rows(x, op, rows=64):
    while x.shape[0] > rows and x.shape[0] % 16 == 0:
        half = x.shape[0] // 2
        x = op(x[:half], x[half:])
    return x


def _key_max(x):
    return jnp.max(_fold_rows(x, jnp.maximum), axis=0, keepdims=True)


def _key_sum(x):
    return jnp.sum(_fold_rows(x, jnp.add), axis=0, keepdims=True)


def _online_update(m, acc_ref, s, v_t):
    m_new = jnp.maximum(m, _key_max(s))
    p = jnp.exp2(s - m_new)
    acc_ref[...] = jnp.exp2(m - m_new) * acc_ref[...] + _dot(v_t, p.astype(MXU_DTYPE))
    return m_new


def _attend_pipelined(n_tiles, scores_fn, softmax_fn, ms, fix_last, s_ref):
    heads = s_ref.shape[1]

    def issue(slot, kt):
        for h, sc in enumerate(scores_fn(kt)):
            s_ref[slot, h] = sc

    def fold(slot, kt, ms):
        return softmax_fn(kt, ms, tuple(s_ref[slot, h] for h in range(heads)))

    plain = n_tiles - 1

    @pl.when(plain > 0)
    def _():
        issue(0, 0)

    def body(p, ms):
        issue(1, 2 * p + 1)
        ms = fold(0, 2 * p, ms)
        issue(0, jnp.minimum(2 * p + 2, plain - 1))
        return fold(1, 2 * p + 1, ms)

    ms = lax.fori_loop(0, plain // 2, body, ms)
    ms = lax.cond(plain % 2 == 1, lambda ms: fold(0, plain - 1, ms), lambda ms: ms, ms)
    return softmax_fn(n_tiles - 1, ms, fix_last(scores_fn(n_tiles - 1)))


def _online_init(tq, acc_ref):
    acc_ref[...] = jnp.zeros(acc_ref.shape, F32)
    return tuple(jnp.full((1, tq), NEG, F32) for _ in range(acc_ref.shape[0]))


def _nsa_kernel(q_ref, gate_ref, kc_ref, vct_ref, ks_ref, vst_ref, kw_ref, vwt_ref,
                ovt_ref, oh_ref, o_ref, qaug_ref, acc_ref, s_ref, *, tq, tks):
    i = pl.program_id(1)
    q0 = i * tq
    ncp = kc_ref.shape[1]
    heads = range(NSA_HEADS)
    group = [h // NSA_HPG for h in heads]
    qa = _group_queries(q_ref, tq)

    c_vis = (_row_iota((ncp, tq)) * NSA_CMP_STRIDE + (NSA_CMP_LEN - 1)) <= q0 + _lane_iota((ncp, tq))
    kc = kc_ref[0]
    pc_sum = [jnp.zeros((ncp, tq), F32) for _ in range(NSA_GROUPS)]
    o_cmp = []
    any_vis = q0 + _lane_iota((1, tq)) >= NSA_CMP_LEN - 1
    scores = [_dot_nt(kc, qa[h]) for h in heads]
    for h in heads:
        s = jnp.where(c_vis, scores[h], NEG)
        p = jnp.exp2(s - _key_max(s))
        p = p * jnp.where(any_vis, 1.0 / _key_sum(p), 0.0)
        pc_sum[group[h]] = pc_sum[group[h]] + p
        o_cmp.append(_dot(vct_ref[0, group[h]], p.astype(MXU_DTYPE))[:HEAD_DIM])

    band = tq + NSA_WINDOW
    start = pl.multiple_of(jnp.maximum(q0 - NSA_WINDOW, 0), tq)
    kw = kw_ref[0, pl.ds(start, band), :]
    s_w = start + _row_iota((band, tq))
    t_w = q0 + _lane_iota((band, tq))
    w_ok = (s_w <= t_w) & (s_w > t_w - NSA_WINDOW)
    o_win = []
    scores = [_dot_nt(kw, qa[h]) for h in heads]
    for h in heads:
        s = jnp.where(w_ok, scores[h], NEG)
        p = jnp.exp2(s - _key_max(s)).astype(MXU_DTYPE)
        acc = jnp.zeros((V_ROWS, tq), F32)
        for c in range(band // tq):
            acc = acc + _dot(vwt_ref[0, group[h], start // tq + c], p[c * tq:(c + 1) * tq, :])
        o_win.append(acc[:HEAD_DIM] * (1.0 / acc[HEAD_DIM:HEAD_DIM + 1]))

    blk = _row_iota((LANES, tq))
    t_b = q0 + _lane_iota((LANES, tq))
    cur = t_b >> SEL_SHIFT
    visible = blk * NSA_SEL_LEN <= t_b
    forced = (blk == 0) | (blk == cur) | (blk == cur - 1)
    taken = -3.0e38
    work = []
    for g in range(NSA_GROUPS):
        hi = pc_sum[g].astype(MXU_DTYPE)
        lo = (pc_sum[g] - hi.astype(F32)).astype(MXU_DTYPE)
        imp = _dot(ovt_ref[...], hi) + _dot(ovt_ref[...], lo)
        work.append(jnp.where(visible, jnp.where(forced, taken, imp), NEG))
    blk_f = blk.astype(F32)
    for _ in range(NSA_SEL_N - 3):
        for g in range(NSA_GROUPS):
            m = jnp.max(work[g], axis=0, keepdims=True)
            first = jnp.min(jnp.where(work[g] == m, blk_f, float(LANES)), axis=0, keepdims=True)
            work[g] = jnp.where(blk_f == first, taken, work[g])
    sel_bias = [jnp.where(visible, jnp.where(w == taken, 0.0, NEG), NEG).T.astype(MXU_DTYPE) for w in work]

    n_tiles = (q0 + tq + tks - 1) // tks
    t_s = q0 + _lane_iota((tks, tq))
    row_s = _row_iota((tks, tq))
    for h in heads:
        qaug_ref[h] = jnp.concatenate([qa[h], sel_bias[group[h]]], axis=1)

    def sel_scores(kt):
        k_aug = jnp.concatenate([ks_ref[0, pl.ds(kt * tks, tks), :], oh_ref[pl.ds(kt * tks, tks), :]], axis=1)
        return tuple(_dot_nt(k_aug, qaug_ref[h]) for h in heads)

    def sel_softmax(kt, ms, scores):
        v_t = [vst_ref[0, g, kt] for g in range(NSA_GROUPS)]
        return tuple(_online_update(ms[h], acc_ref.at[h], scores[h], v_t[group[h]]) for h in heads)

    def sel_causal(scores):
        causal = (n_tiles - 1) * tks + row_s <= t_s
        return tuple(jnp.where(causal, s, NEG) for s in scores)

    _attend_pipelined(n_tiles, sel_scores, sel_softmax, _online_init(tq, acc_ref), sel_causal, s_ref)
    o_sel = [acc_ref[h, :HEAD_DIM, :] * (1.0 / acc_ref[h, HEAD_DIM:HEAD_DIM + 1, :]) for h in heads]

    gates_t = gate_ref[0].T
    merged = []
    for h in heads:
        tot = jnp.zeros((HEAD_DIM, tq), F32)
        for r, branch in enumerate((o_cmp, o_sel, o_win)):
            tot = tot + gates_t[3 * h + r:3 * h + r + 1, :] * branch[h]
        merged.append(tot)
    for s in range(NSA_QW // LANES):
        pair = jnp.concatenate([merged[2 * s], merged[2 * s + 1]], axis=0)
        o_ref[0, :, s * LANES:(s + 1) * LANES] = pair.T.astype(o_ref.dtype)


def _nsa_attention(nq, ng, kc, vct, ks, vst, kw, vwt, overlap_t, onehot, *, tq, tks):
    b, s, _ = nq.shape
    once = pl.Buffered(1)

    def per_batch(a):
        return pl.BlockSpec((1,) + a.shape[1:], lambda bi, i: (bi,) + (0,) * (a.ndim - 1), pipeline_mode=once)

    def const(a):
        return pl.BlockSpec(a.shape, lambda bi, i: (0,) * a.ndim, pipeline_mode=once)

    return pl.pallas_call(
        functools.partial(_nsa_kernel, tq=tq, tks=tks),
        grid=(b, s // tq),
        in_specs=[pl.BlockSpec((1, tq, NSA_QW), lambda bi, i: (bi, i, 0)),
                  pl.BlockSpec((1, tq, LANES), lambda bi, i: (bi, i, 0)),
                  per_batch(kc), per_batch(vct), per_batch(ks), per_batch(vst), per_batch(kw), per_batch(vwt),
                  const(overlap_t), const(onehot)],
        out_specs=pl.BlockSpec((1, tq, NSA_QW), lambda bi, i: (bi, i, 0)),
        out_shape=jax.ShapeDtypeStruct((b, s, NSA_QW), MXU_DTYPE),
        scratch_shapes=[pltpu.VMEM((NSA_HEADS, tq, 2 * LANES), MXU_DTYPE),
                        pltpu.VMEM((NSA_HEADS, V_ROWS, tq), F32),
                        pltpu.VMEM((2, NSA_HEADS, tks, tq), F32)],
        compiler_params=_params(2),
        name="nsa_attn",
    )(nq, ng, kc, vct, ks, vst, kw, vwt, overlap_t, onehot)


def _dsa_kernel(iq_ref, iwt_ref, ik_ref, q_ref, kv_ref, kvt_ref, o_ref, khi_ref, klo_ref, bias_ref, qa_ref,
                acc_ref, s_ref, *, tq, tk, top_k):
    i = pl.program_id(1)
    q0 = i * tq
    n_tiles = (q0 + tq + tk - 1) // tk
    lane = _lane_iota((tq, LANES))
    t_q = q0 + _lane_iota((tk, tq))
    row_k = _row_iota((tk, tq))

    iqa = []
    for s in range(IDX_QW // LANES):
        qs = iq_ref[0, :, s * LANES:(s + 1) * LANES].astype(F32)
        for p in range(LANES // DSA_IDX_DIM):
            mine = jnp.where((lane >> IDX_SHIFT) == p, qs, 0.0)
            if p:
                mine = pltpu.roll(mine, LANES - p * DSA_IDX_DIM, 1)
            iqa.append(mine.astype(MXU_DTYPE))
    w_h = [iwt_ref[0, h:h + 1, :] for h in range(DSA_IDX_HEADS)]

    def score_tiles(tiles, last_is_diagonal):
        logits = []
        for kt in tiles:
            ik = ik_ref[0, pl.ds(kt * tk, tk), :]
            logits.append([_dot_nt(ik, iqa[h]) for h in range(DSA_IDX_HEADS)])
        for n, kt in enumerate(tiles):
            score = jnp.zeros((tk, tq), F32)
            for h in range(DSA_IDX_HEADS):
                score = score + w_h[h] * jnp.maximum(logits[n][h], 0.0)
            score = jnp.where(score == 0.0, 0.0, score)
            bits = pltpu.bitcast(score, jnp.int32)
            key = jnp.where(bits < 0, bits ^ 0x7FFFFFFF, bits)
            if last_is_diagonal and n == len(tiles) - 1:
                key = jnp.where(kt * tk + row_k <= t_q, key, INT_MIN)
            khi_ref[kt] = (key >> 16).astype(jnp.int16)
            klo_ref[kt] = ((key & 0xFFFF) + I16_MIN).astype(jnp.int16)

    def score_pair(p, _):
        score_tiles([2 * p, 2 * p + 1], False)
        return 0

    plain = n_tiles - 1
    lax.fori_loop(0, plain // 2, score_pair, 0)
    lax.cond(plain % 2 == 1,
             lambda: score_tiles([n_tiles - 2, n_tiles - 1], True),
             lambda: score_tiles([n_tiles - 1], True))
    one, zero = jnp.ones((), jnp.bfloat16), jnp.zeros((), jnp.bfloat16)

    def count(flags):
        def body(kt, acc):
            return acc + _fold_rows(flags(kt), jnp.add, rows=32).astype(F32)
        acc = lax.fori_loop(0, n_tiles, body, jnp.zeros((32, tq), F32))
        return jnp.sum(acc, axis=0, keepdims=True)

    uncounted = float(2 ** 24)

    def kth_largest(ref, may_stop_early):
        def body(b, carry):
            ans_u, reached = carry
            cand_u = ans_u | lax.shift_left(jnp.int32(1), 15 - b)
            cand = (cand_u + I16_MIN).astype(jnp.int16)
            cnt = count(lambda kt: jnp.where(ref[kt] >= cand, one, zero))
            ok = cnt >= top_k
            return jnp.where(ok, cand_u, ans_u), jnp.where(ok, cnt, reached)

        carry = (jnp.zeros((1, tq), jnp.int32), jnp.full((1, tq), uncounted, F32))
        if may_stop_early:
            carry = lax.fori_loop(0, 8, body, carry)
            for first, last in ((8, 12), (12, 16)):
                unsettled = jnp.max(jnp.abs(carry[1] - top_k)) > 0.0
                carry = lax.cond(unsettled, lambda c, f=first, l=last: lax.fori_loop(f, l, body, c),
                                 lambda c: c, carry)
        else:
            carry = lax.fori_loop(0, 16, body, carry)
        return carry[0] + I16_MIN, carry[1]

    t_hi32, reached_hi = kth_largest(khi_ref, False)
    t_hi = t_hi32.astype(jnp.int16)

    def pin_body(kt, _):
        hi = khi_ref[kt]
        klo_ref[kt] = jnp.where(hi == t_hi, klo_ref[kt],
                                jnp.where(hi > t_hi, jnp.int16(I16_MAX), jnp.int16(I16_MIN)))
        return 0

    lax.fori_loop(0, n_tiles, pin_body, 0)
    t_lo32, reached_lo = kth_largest(klo_ref, True)
    reached = jnp.where(reached_lo < uncounted, reached_lo, reached_hi)
    t_lo = jnp.where(t_hi32 == I16_MIN, jnp.maximum(t_lo32, I16_MIN + 1), t_lo32).astype(jnp.int16)

    def at_least(kt, yes, no):
        hi = khi_ref[kt]
        return jnp.where(hi > t_hi, yes, jnp.where(hi == t_hi, jnp.where(klo_ref[kt] >= t_lo, yes, no), no))

    def above(kt):
        return jnp.where(khi_ref[kt] > t_hi, one, jnp.where(klo_ref[kt] > t_lo, one, zero))

    bias_yes, bias_no = jnp.zeros((), bias_ref.dtype), jnp.full((), NEG, bias_ref.dtype)

    def plain_bias():
        def body(kt, _):
            bias_ref[kt] = at_least(kt, bias_yes, bias_no)
            return 0
        lax.fori_loop(0, n_tiles, body, 0)

    def ranked_bias():
        need = top_k - count(above)
        lower = jnp.where(_lane_iota((tk, tk)) < _row_iota((tk, tk)), 1.0, 0.0).astype(MXU_DTYPE)

        def body(kt, seen):
            tie = jnp.where(khi_ref[kt] == t_hi, jnp.where(klo_ref[kt] == t_lo, one, zero), zero)
            rank = _dot(lower, tie.astype(MXU_DTYPE)) + seen
            tie = tie.astype(F32)
            take = jnp.where(above(kt).astype(F32) > 0.0, 1.0, jnp.where(rank < need, tie, 0.0))
            bias_ref[kt] = ((take - 1.0) * (-NEG)).astype(bias_ref.dtype)
            return seen + _key_sum(tie)

        lax.fori_loop(0, n_tiles, body, jnp.zeros((1, tq), F32))

    lax.cond(jnp.max(reached) > top_k, ranked_bias, plain_bias)

    for s in range(DSA_QW // LANES):
        qs = q_ref[0, :, s * LANES:(s + 1) * LANES].astype(F32)
        lo = jnp.where(lane < HEAD_DIM, qs, 0.0)
        hi = _swap_halves(jnp.where(lane < HEAD_DIM, 0.0, qs))
        qa_ref[2 * s] = lo.astype(MXU_DTYPE)
        qa_ref[2 * s + 1] = hi.astype(MXU_DTYPE)

    def att_scores(kt):
        kv = kv_ref[0, pl.ds(kt * tk, tk), :]
        bias = bias_ref[kt].astype(F32)
        return tuple(_dot_nt(kv, qa_ref[h]) + bias for h in range(DSA_HEADS))

    def att_softmax(kt, ms, scores):
        kv_t = kvt_ref[0, kt]
        return tuple(_online_update(ms[h], acc_ref.at[h], scores[h], kv_t) for h in range(DSA_HEADS))

    _attend_pipelined(n_tiles, att_scores, att_softmax, _online_init(tq, acc_ref), lambda scores: scores, s_ref)
    outs = [acc_ref[h, :HEAD_DIM, :] * (1.0 / acc_ref[h, HEAD_DIM:HEAD_DIM + 1, :]) for h in range(DSA_HEADS)]

    for s in range(DSA_QW // LANES):
        pair = jnp.concatenate([outs[2 * s], outs[2 * s + 1]], axis=0)
        o_ref[0, :, s * LANES:(s + 1) * LANES] = pair.T.astype(o_ref.dtype)


def _dsa_attention(iq, iwt, ik, dq, dkv, dkvt, *, tq, tk, top_k):
    b, s, _ = dq.shape
    tile = lambda width: pl.BlockSpec((1, tq, width), lambda bi, i: (bi, i, 0))
    full = pl.BlockSpec((1, s, LANES), lambda bi, i: (bi, 0, 0))
    return pl.pallas_call(
        functools.partial(_dsa_kernel, tq=tq, tk=tk, top_k=top_k),
        grid=(b, s // tq),
        in_specs=[tile(IDX_QW), pl.BlockSpec((1, DSA_IDX_HEADS, tq), lambda bi, i: (bi, 0, i)), full,
                  tile(DSA_QW), full, pl.BlockSpec((1,) + dkvt.shape[1:], lambda bi, i: (bi, 0, 0, 0))],
        out_specs=tile(DSA_QW),
        out_shape=jax.ShapeDtypeStruct((b, s, DSA_QW), MXU_DTYPE),
        scratch_shapes=[pltpu.VMEM((s // tk, tk, tq), jnp.int16), pltpu.VMEM((s // tk, tk, tq), jnp.int16),
                        pltpu.VMEM((s // tk, tk, tq), MXU_DTYPE),
                        pltpu.VMEM((DSA_HEADS, tq, LANES), MXU_DTYPE), pltpu.VMEM((DSA_HEADS, V_ROWS, tq), F32),
                        pltpu.VMEM((2, DSA_HEADS, tk, tq), F32)],
        compiler_params=_params(2),
        name="dsa_attn",
    )(iq, iwt, ik, dq, dkv, dkvt)


def _merge_kernel(x_ref, g_ref, oa_ref, ob_ref, oc_ref, wg_ref, wa_ref, wb_ref, wc_ref, wo_ref, o_ref):
    x = x_ref[...]
    d = x.shape[1]
    ms = jnp.mean(x * x, axis=-1, keepdims=True)
    h = (x * lax.rsqrt(ms + NORM_EPS) * g_ref[...]).astype(MXU_DTYPE)
    merged = jnp.zeros(x.shape, F32)
    for r, (o_r, w_r) in enumerate(((oa_ref, wa_ref), (ob_ref, wb_ref), (oc_ref, wc_ref))):
        gate = jax.nn.sigmoid(_dot(h, wg_ref[:, r * d:(r + 1) * d]))
        merged = merged + gate * _dot(o_r[...], w_r[...])
    o_ref[...] = x + _dot(merged.astype(MXU_DTYPE), wo_ref[...])


def _merge(x2, g, oa, ob, oc, w_gate, wa, wb, wc, wo, *, tm):
    n, d = x2.shape
    row = lambda i: (i, 0)
    const = lambda i: (0, 0)
    return pl.pallas_call(
        _merge_kernel,
        grid=(n // tm,),
        in_specs=[pl.BlockSpec((tm, d), row), pl.BlockSpec((1, d), const),
                  pl.BlockSpec((tm, SB_W), row), pl.BlockSpec((tm, NSA_QW), row),
                  pl.BlockSpec((tm, DSA_QW), row),
                  pl.BlockSpec(w_gate.shape, const), pl.BlockSpec(wa.shape, const),
                  pl.BlockSpec(wb.shape, const), pl.BlockSpec(wc.shape, const),
                  pl.BlockSpec(wo.shape, const)],
        out_specs=pl.BlockSpec((tm, d), row),
        out_shape=jax.ShapeDtypeStruct((n, d), F32),
        compiler_params=_params(1),
        name="merge",
    )(x2, g, oa, ob, oc, w_gate, wa, wb, wc, wo)


def _rope_tables(pos, rot_dim, head_dim):
    half = rot_dim // 2
    inv = ROPE_THETA ** (-jnp.arange(0, rot_dim, 2, dtype=F32) / rot_dim)
    ang = pos.astype(F32)[:, None] * inv[None, :]
    cos, sin = jnp.cos(ang), jnp.sin(ang)
    n = pos.shape[0]
    rest = head_dim - rot_dim
    zero_h = jnp.zeros((n, half), F32)
    c = jnp.concatenate([cos, cos, jnp.ones((n, rest), F32)], axis=-1)
    sm = jnp.concatenate([-sin, zero_h, jnp.zeros((n, rest), F32)], axis=-1)
    sp = jnp.concatenate([zero_h, sin, jnp.zeros((n, rest), F32)], axis=-1)
    reps = LANES // head_dim
    return tuple(jnp.tile(t, (1, reps)) for t in (c, sm, sp))


def _pad_cols(w, width):
    return jnp.pad(w, ((0, 0), (0, width - w.shape[1])))


def _relayout_w_in(w_in):
    offs = np.cumsum((SB_W, SB_W, SB_W, NSA_QW) + (NSA_KVW,) * 6
                     + (3 * NSA_HEADS, DSA_QW, HEAD_DIM, HEAD_DIM, IDX_QW, DSA_IDX_DIM, DSA_IDX_HEADS))
    o_gate, o_dq, o_dk, o_dv, o_iq, o_ik, o_iw, o_end = offs[9:17].tolist()
    parts = [w_in[:, :o_gate], _pad_cols(w_in[:, o_gate:o_dq], LANES), w_in[:, o_dq:o_dk],
             w_in[:, o_dk:o_iq], w_in[:, o_iq:o_ik], _pad_cols(w_in[:, o_ik:o_iw], LANES),
             _pad_cols(w_in[:, o_iw:o_end], LANES)]
    w = jnp.concatenate(parts, axis=1)
    assert w.shape[1] == _P_END
    return w.astype(MXU_DTYPE)


def _compress_weights(pe, w1, w2):
    r = NSA_CMP_LEN // NSA_CMP_STRIDE
    hid = NSA_CMP_HIDDEN
    w1r = w1.reshape(2, r, NSA_CMP_STRIDE, HEAD_DIM, hid)
    eye = jnp.eye(NSA_GROUPS, dtype=w1.dtype)
    w1e = jnp.einsum('kmldj,gh->kmlgdhj', w1r, eye)
    w1e = w1e.reshape(2, r, NSA_CMP_STRIDE * NSA_KVW, NSA_GROUPS * hid)
    w2e = jnp.einsum('kjd,gh->kgjhd', w2, eye).reshape(2, NSA_GROUPS * hid, NSA_KVW)
    pe_e = jnp.broadcast_to(pe.reshape(2, r, NSA_CMP_STRIDE, 1, HEAD_DIM),
                            (2, r, NSA_CMP_STRIDE, NSA_GROUPS, HEAD_DIM))
    return pe_e.reshape(2, r, NSA_CMP_STRIDE * NSA_KVW), w1e.astype(MXU_DTYPE), w2e.astype(MXU_DTYPE)


def _overlap_matrix_t(ncp, seq):
    c = np.arange(ncp)[None, :] * NSA_CMP_STRIDE
    j = np.arange(LANES)[:, None] * NSA_SEL_LEN
    ov = (c < j + NSA_SEL_LEN) & (c + NSA_CMP_LEN - 1 >= j) & (j < seq)
    ov &= (np.arange(ncp)[None, :] < ncp - (NSA_CMP_LEN // NSA_CMP_STRIDE - 1))
    return jnp.asarray(ov, dtype=MXU_DTYPE)


def _block_onehot(seq):
    return jnp.asarray(np.arange(seq)[:, None] // NSA_SEL_LEN == np.arange(LANES)[None, :], dtype=MXU_DTYPE)


def _key_tiles_t(v, tile):
    b, s, w = v.shape
    return jnp.swapaxes(v.reshape(b, s // tile, tile, w), 2, 3)


def _group_values_t(v, tile):
    b, s, w = v.shape
    groups = w // HEAD_DIM
    vt = _key_tiles_t(v, tile).reshape(b, s // tile, groups, HEAD_DIM, tile)
    vt = jnp.swapaxes(vt, 1, 2)
    pad = jnp.zeros((b, groups, s // tile, V_ROWS - HEAD_DIM, tile), v.dtype).at[:, :, :, 0, :].set(1)
    return jnp.concatenate([vt, pad], axis=3)


def kernel(x, ffn1_norm, ffn1_w_gu, ffn1_w_down, mix_norm, w_in, w_gate, nsa_q_norm, nsa_k_norm,
           nsa_cmp_pe, nsa_cmp_w1, nsa_cmp_w2, dsa_q_norm, dsa_k_norm, w_br_a, w_br_b, w_br_c, w_out,
           ffn2_norm, ffn2_w_gu, ffn2_w_down):
    b, s, d = x.shape
    depth = w_in.shape[0]
    n = b * s
    ncp = s // NSA_CMP_STRIDE
    assert s // NSA_SEL_LEN <= LANES and s % 1024 == 0
    top_k = min(DSA_TOPK, s // 4)
    ffn_tf = ffn1_w_down.shape[1]
    cast = lambda w: w.astype(MXU_DTYPE)
    tile2 = lambda v, reps: jnp.tile(v, reps)[None, :]

    pos = jnp.arange(s)
    rope_main = _rope_tables(pos, ROT_DIM, HEAD_DIM)
    rope_idx = _rope_tables(pos, DSA_IDX_ROT, DSA_IDX_DIM)
    rope_cmp = _rope_tables(jnp.arange(ncp) * NSA_CMP_STRIDE + NSA_CMP_LEN - 1, ROT_DIM, HEAD_DIM)
    overlap_t = _overlap_matrix_t(ncp, s)
    onehot = _block_onehot(s)

    x2 = x.reshape(n, d)
    for l in range(depth):
        x2 = _ffn(x2, ffn1_norm[l][None, :], cast(ffn1_w_gu[l]), cast(ffn1_w_down[l]), tm=FFN_TM, tf=ffn_tf)

        (sb, nq, nkc, nvc, nkv, ng, dq, dkv, iq, ik, iw) = _proj(
            x2, mix_norm[l][None, :], _relayout_w_in(w_in[l]),
            tile2(nsa_q_norm[l], NSA_HEADS),
            jnp.stack([jnp.tile(nsa_k_norm[l, 1], NSA_GROUPS), jnp.tile(nsa_k_norm[l, 2], NSA_GROUPS)]),
            tile2(dsa_q_norm[l], DSA_HEADS), tile2(dsa_k_norm[l], 2),
            rope_main, rope_idx, tm=ROW_TM, seq=s)
        r3 = lambda t: t.reshape(b, s, t.shape[-1])

        pe_e, w1e, w2e = _compress_weights(nsa_cmp_pe[l], nsa_cmp_w1[l], nsa_cmp_w2[l])
        kc, vc = _compress(nkc.reshape(b, ncp, NSA_CMP_STRIDE * NSA_KVW),
                           nvc.reshape(b, ncp, NSA_CMP_STRIDE * NSA_KVW),
                           pe_e, w1e, w2e, tile2(nsa_k_norm[l, 0], NSA_GROUPS), rope_cmp)

        o_a = _sb_attention(r3(sb), tq=SB_TQ)

        nkv = r3(nkv)
        ks, vs, kw, vw = (nkv[:, :, c * LANES:(c + 1) * LANES] for c in range(4))
        o_b = _nsa_attention(r3(nq), r3(ng), kc, _group_values_t(vc, ncp)[:, :, 0], ks,
                             _group_values_t(vs, TKS), kw, _group_values_t(vw, TQ_NSA), overlap_t, onehot,
                             tq=TQ_NSA, tks=TKS)

        dkv = r3(dkv)
        iwt = jnp.swapaxes(r3(iw)[:, :, :DSA_IDX_HEADS], 1, 2)
        dkvt = _group_values_t(dkv[:, :, HEAD_DIM:], TKS)[:, 0]
        o_c = _dsa_attention(r3(iq), iwt, r3(ik), r3(dq), dkv, dkvt, tq=TQ, tk=TKS, top_k=top_k)

        x2 = _merge(x2, mix_norm[l][None, :], o_a.reshape(n, SB_W), o_b.reshape(n, NSA_QW),
                    o_c.reshape(n, DSA_QW), cast(w_gate[l]), cast(w_br_a[l]), cast(w_br_b[l]),
                    cast(w_br_c[l]), cast(w_out[l]), tm=MERGE_TM)

        x2 = _ffn(x2, ffn2_norm[l][None, :], cast(ffn2_w_gu[l]), cast(ffn2_w_down[l]), tm=FFN_TM, tf=ffn_tf)
    return x2.reshape(b, s, d)
```
